```python
import jax, jax.numpy as jnp
from jax import lax
import numpy as np

D_MODEL = 1024
BATCH = 8
SEQ = 2048
DEPTH = 2

N_META = 16
GRID_W = 64
NORM_EPS = 1e-6
SC_WIDTH = D_MODEL // 2
SC_KERNEL = 3
GLA_HEADS = 4
GLA_DV = (D_MODEL // 2) // GLA_HEADS
GLA_DK = GLA_DV // 2
GLA_GATE_RANK = 16
GLA_GATE_TAU = 16.0
GLA_CHUNK = 64
MIX_SPLIT_SIZES = [SC_WIDTH] * 3 + [GLA_HEADS * GLA_DK] * 2 + [GLA_HEADS * GLA_DV] * 2 + [GLA_GATE_RANK] * 2
MIX_IN = sum(MIX_SPLIT_SIZES)
ATT_HEAD_DIM = 64
ATT_HEADS = D_MODEL // ATT_HEAD_DIM
ATT_KV_HEADS = 4
ATT_GROUP = ATT_HEADS // ATT_KV_HEADS
ATT_BLOCK = 128
ROPE_THETA = 10000.0
ATT_IN = ATT_HEADS * ATT_HEAD_DIM + 2 * ATT_KV_HEADS * ATT_HEAD_DIM
MOE_GROUPS = 4
MOE_EXPERTS_PER_GROUP = 8
N_EXPERTS = MOE_GROUPS * MOE_EXPERTS_PER_GROUP
MOE_TOP_K = 2
MOE_D_FF = 256
N_EVEN = (DEPTH + 1) // 2
N_ODD = DEPTH // 2

kernel_name = "hybrid_conv_gla_axialgqa_hmoe_encoder"


def rms_norm(x, g):
    xf = x.astype(jnp.float32)
    y = xf * lax.rsqrt(jnp.mean(xf * xf, axis=-1, keepdims=True) + NORM_EPS)
    return (y * g.astype(jnp.float32)).astype(x.dtype)


def centred_depthwise_conv3(z, w):
    zp = jnp.pad(z, ((0, 0), (1, 1), (0, 0)))
    return w[0] * zp[:, :-2] + w[1] * zp[:, 1:-1] + w[2] * zp[:, 2:]


def gla_chunked(q, k, v, log_a, inclusive):
    b_, L, H, DK = q.shape
    DV = v.shape[-1]
    n = L // GLA_CHUNK

    def chunks(t):
        return t.astype(jnp.float32).reshape(b_, n, GLA_CHUNK, H, t.shape[-1]).transpose(1, 0, 3, 2, 4)

    qc, kc, vc, gc = chunks(q), chunks(k), chunks(v), chunks(log_a)
    cum = jnp.cumsum(gc, axis=3)
    cum_last = cum[:, :, :, -1:, :]
    q_dec = qc * jnp.exp(cum)
    k_inv = kc * jnp.exp(-cum)
    k_end = kc * jnp.exp(cum_last - cum)
    scores = jnp.einsum('nbhcd,nbhsd->nbhcs', q_dec, k_inv)
    mask = jnp.tril(jnp.ones((GLA_CHUNK, GLA_CHUNK), dtype=bool), 0 if inclusive else -1)
    o_intra = jnp.einsum('nbhcs,nbhsv->nbhcv', jnp.where(mask, scores, 0.0), vc)
    kv_chunk = jnp.einsum('nbhcd,nbhcv->nbhdv', k_end, vc)
    chunk_decay = jnp.exp(cum_last[:, :, :, 0, :])

    def step(state, inp):
        q_n, kv_n, dec_n = inp
        o_n = jnp.einsum('bhcd,bhdv->bhcv', q_n, state)
        return dec_n[..., None] * state + kv_n, o_n

    s0 = jnp.zeros((b_, H, DK, DV), jnp.float32)
    _, o_inter = lax.scan(step, s0, (q_dec, kv_chunk, chunk_decay))
    o = o_intra + o_inter
    return o.transpose(1, 0, 3, 2, 4).reshape(b_, L, H, DV)


def pad_meta_chunk(t):
    pad = jnp.zeros((t.shape[0], GLA_CHUNK - N_META) + t.shape[2:], t.dtype)
    return jnp.concatenate([t[:, :N_META], pad, t[:, N_META:]], axis=1)


def conv_gla_mixer(h, w_in, conv_w, wg_f, bg_f, wg_b, bg_b, gla_norm_w, w_out):
    bsz, T, _ = h.shape
    u = h @ w_in
    splits = np.cumsum(MIX_SPLIT_SIZES)[:-1].tolist()
    a_b, a_c, a_x, q, k, v, r, g_f, g_b = jnp.split(u, splits, axis=-1)
    y_a = a_b * centred_depthwise_conv3(a_c * a_x, conv_w)
    log_a_f = jax.nn.log_sigmoid((g_f @ wg_f + bg_f).astype(jnp.float32)) / GLA_GATE_TAU
    log_a_b = jax.nn.log_sigmoid((g_b @ wg_b + bg_b).astype(jnp.float32)) / GLA_GATE_TAU
    heads = lambda t, d: t.reshape(bsz, T, GLA_HEADS, d)
    qh = pad_meta_chunk(heads(q, GLA_DK) * (GLA_DK ** -0.5))
    kh = pad_meta_chunk(heads(k, GLA_DK))
    vh = pad_meta_chunk(heads(v, GLA_DV))
    laf = pad_meta_chunk(heads(log_a_f, GLA_DK))
    lab = pad_meta_chunk(heads(log_a_b, GLA_DK))
    o_fwd = gla_chunked(qh, kh, vh, laf, True)
    flip = lambda t: jnp.flip(t, axis=1)
    o_bwd = flip(gla_chunked(flip(qh), flip(kh), flip(vh), flip(lab), False))
    o = o_fwd + o_bwd
    o = jnp.concatenate([o[:, :N_META], o[:, GLA_CHUNK:]], axis=1)
    o = o * lax.rsqrt(jnp.mean(o * o, axis=-1, keepdims=True) + NORM_EPS) * gla_norm_w.astype(jnp.float32)
    y_b = (o * jax.nn.silu(heads(r, GLA_DV).astype(jnp.float32))).reshape(bsz, T, GLA_HEADS * GLA_DV)
    return jnp.concatenate([y_a, y_b.astype(h.dtype)], axis=-1) @ w_out


def axial_rope_tables(n_tokens):
    rows_n = n_tokens // GRID_W
    rows = jnp.repeat(jnp.arange(rows_n), GRID_W).astype(jnp.float32)
    cols = jnp.tile(jnp.arange(GRID_W), rows_n).astype(jnp.float32)
    axis_dims = ATT_HEAD_DIM // 2
    freqs = ROPE_THETA ** (-jnp.arange(0, axis_dims, 2, dtype=jnp.float32) / axis_dims)
    ang = jnp.concatenate([rows[:, None] * freqs, cols[:, None] * freqs], axis=-1)
    ang = jnp.concatenate([jnp.zeros((N_META, ang.shape[1]), jnp.float32), ang], axis=0)
    return jnp.cos(ang), jnp.sin(ang)


def apply_rope(x, cos, sin):
    xr = x.reshape(x.shape[:-1] + (ATT_HEAD_DIM // 2, 2))
    x0, x1 = xr[..., 0], xr[..., 1]
    c, s = cos[None, :, None, :], sin[None, :, None, :]
    return jnp.stack([x0 * c - x1 * s, x0 * s + x1 * c], axis=-1).reshape(x.shape)


def attend_block(q_blk, k, v):
    s = jnp.einsum('bqhgd,bkhd->bhgqk', q_blk, k) * (ATT_HEAD_DIM ** -0.5)
    p = jax.nn.softmax(s, axis=-1)
    return jnp.einsum('bhgqk,bkhd->bqhgd', p, v)


def gqa_axial_mixer(h, w_in, q_norm, k_norm, w_out, cos, sin, with_meta_queries):
    bsz, T, _ = h.shape
    u = h @ w_in
    nq = ATT_HEADS * ATT_HEAD_DIM
    nkv = ATT_KV_HEADS * ATT_HEAD_DIM
    q = u[..., :nq].reshape(bsz, T, ATT_HEADS, ATT_HEAD_DIM)
    k = u[..., nq:nq + nkv].reshape(bsz, T, ATT_KV_HEADS, ATT_HEAD_DIM)
    v = u[..., nq + nkv:].reshape(bsz, T, ATT_KV_HEADS, ATT_HEAD_DIM).astype(jnp.float32)
    q = apply_rope(rms_norm(q, q_norm).astype(jnp.float32), cos, sin)
    k = apply_rope(rms_norm(k, k_norm).astype(jnp.float32), cos, sin)
    q = q.reshape(bsz, T, ATT_KV_HEADS, ATT_GROUP, ATT_HEAD_DIM)
    n_real = T - N_META
    nb = n_real // ATT_BLOCK
    q_blocks = jnp.moveaxis(q[:, N_META:].reshape(bsz, nb, ATT_BLOCK, ATT_KV_HEADS, ATT_GROUP, ATT_HEAD_DIM), 1, 0)
    o = lax.map(lambda qb: attend_block(qb, k, v), q_blocks)
    o = jnp.moveaxis(o, 0, 1).reshape(bsz, n_real, nq)
    if with_meta_queries:
        o_meta = attend_block(q[:, :N_META], k, v).reshape(bsz, N_META, nq)
        o = jnp.concatenate([o_meta, o], axis=1)
    return o.astype(h.dtype) @ w_out


def hierarchical_moe(h, w_group, b_group, w_expert, b_expert, w_gate, w_up, w_down):
    bsz, T, D = h.shape
    xt = h.reshape(-1, D)
    g_logits = (xt @ w_group + b_group).astype(jnp.float32)
    g_prob = jax.nn.softmax(g_logits, axis=-1)
    g_idx = jnp.argmax(g_logits, axis=-1)
    g_p = jnp.take_along_axis(g_prob, g_idx[:, None], axis=-1)
    e_logits = (xt @ w_expert + b_expert).astype(jnp.float32).reshape(-1, MOE_GROUPS, MOE_EXPERTS_PER_GROUP)
    e_sel = jnp.take_along_axis(e_logits, g_idx[:, None, None], axis=1)[:, 0]
    top_v, top_i = lax.top_k(e_sel, MOE_TOP_K)
    top_w = jax.nn.softmax(top_v, axis=-1) * g_p
    expert_id = g_idx[:, None] * MOE_EXPERTS_PER_GROUP + top_i
    gates = jnp.sum(jax.nn.one_hot(expert_id, N_EXPERTS, dtype=jnp.float32) * top_w[..., None], axis=1)
    out = jnp.zeros(xt.shape, jnp.float32)
    for e in range(N_EXPERTS):
        he = jax.nn.silu(xt @ w_gate[e]) * (xt @ w_up[e])
        out = out + gates[:, e:e + 1] * (he @ w_down[e]).astype(jnp.float32)
    return out.astype(h.dtype).reshape(bsz, T, D)


def setup_inputs(seed: int = 0) -> dict:
    key = jax.random.key(seed)
    ks = jax.random.split(key, 24)
    nrm = lambda k, shape, scale: jax.random.normal(k, shape, jnp.float32) * scale
    gain = lambda k, shape: 1.0 + 0.02 * jax.random.normal(k, shape, jnp.float32)
    D = D_MODEL
    return {
        "x": nrm(ks[0], (BATCH, SEQ, D), 1.0),
        "meta_tokens": nrm(ks[1], (N_META, D), 1.0),
        "norm_mix": gain(ks[2], (DEPTH, D)),
        "norm_ffn": gain(ks[3], (DEPTH, D)),
        "norm_final": gain(ks[4], (D,)),
        "sc_w_in": nrm(ks[5], (N_EVEN, D, MIX_IN), D ** -0.5),
        "sc_conv_w": nrm(ks[6], (N_EVEN, SC_KERNEL, SC_WIDTH), SC_KERNEL ** -0.5),
        "gla_w_gate_fwd": nrm(ks[7], (N_EVEN, GLA_GATE_RANK, GLA_HEADS * GLA_DK), GLA_GATE_RANK ** -0.5),
        "gla_b_gate_fwd": nrm(ks[8], (N_EVEN, GLA_HEADS * GLA_DK), 0.1),
        "gla_w_gate_bwd": nrm(ks[9], (N_EVEN, GLA_GATE_RANK, GLA_HEADS * GLA_DK), GLA_GATE_RANK ** -0.5),
        "gla_b_gate_bwd": nrm(ks[10], (N_EVEN, GLA_HEADS * GLA_DK), 0.1),
        "gla_norm_w": gain(ks[11], (N_EVEN, GLA_DV)),
        "sc_w_out": nrm(ks[12], (N_EVEN, SC_WIDTH + GLA_HEADS * GLA_DV, D), (SC_WIDTH + GLA_HEADS * GLA_DV) ** -0.5),
        "att_w_in": nrm(ks[13], (N_ODD, D, ATT_IN), D ** -0.5),
        "att_q_norm": gain(ks[14], (N_ODD, ATT_HEAD_DIM)),
        "att_k_norm": gain(ks[15], (N_ODD, ATT_HEAD_DIM)),
        "att_w_out": nrm(ks[16], (N_ODD, ATT_HEADS * ATT_HEAD_DIM, D), (ATT_HEADS * ATT_HEAD_DIM) ** -0.5),
        "moe_w_group": nrm(ks[17], (DEPTH, D, MOE_GROUPS), D ** -0.5),
        "moe_b_group": nrm(ks[18], (DEPTH, MOE_GROUPS), 0.01),
        "moe_w_expert": nrm(ks[19], (DEPTH, D, N_EXPERTS), D ** -0.5),
        "moe_b_expert": nrm(ks[20], (DEPTH, N_EXPERTS), 0.01),
        "moe_w_gate": nrm(ks[21], (DEPTH, N_EXPERTS, D, MOE_D_FF), D ** -0.5),
        "moe_w_up": nrm(ks[22], (DEPTH, N_EXPERTS, D, MOE_D_FF), D ** -0.5),
        "moe_w_down": nrm(ks[23], (DEPTH, N_EXPERTS, MOE_D_FF, D), MOE_D_FF ** -0.5),
    }


def reference(x, meta_tokens, norm_mix, norm_ffn, norm_final, sc_w_in, sc_conv_w, gla_w_gate_fwd, gla_b_gate_fwd, gla_w_gate_bwd, gla_b_gate_bwd, gla_norm_w, sc_w_out, att_w_in, att_q_norm, att_k_norm, att_w_out, moe_w_group, moe_b_group, moe_w_expert, moe_b_expert, moe_w_gate, moe_w_up, moe_w_down):
    bsz, n_tokens, D = x.shape
    meta = jnp.broadcast_to(meta_tokens.astype(x.dtype)[None], (bsz, N_META, D))
    h = jnp.concatenate([meta, x], axis=1)
    cos, sin = axial_rope_tables(n_tokens)
    for i in range(DEPTH):
        last = i == DEPTH - 1
        j = i // 2
        hn = rms_norm(h, norm_mix[i])
        if i % 2 == 0:
            h = h + conv_gla_mixer(hn, sc_w_in[j], sc_conv_w[j], gla_w_gate_fwd[j], gla_b_gate_fwd[j],
                                   gla_w_gate_bwd[j], gla_b_gate_bwd[j], gla_norm_w[j], sc_w_out[j])
            if last:
                h = h[:, N_META:]
        else:
            mix = gqa_axial_mixer(hn, att_w_in[j], att_q_norm[j], att_k_norm[j], att_w_out[j], cos, sin,
                                  not last)
            h = (h[:, N_META:] if last else h) + mix
        h = h + hierarchical_moe(rms_norm(h, norm_ffn[i]), moe_w_group[i], moe_b_group[i], moe_w_expert[i],
                                 moe_b_expert[i], moe_w_gate[i], moe_w_up[i], moe_w_down[i])
    return rms_norm(h, norm_final)
```

```python
import functools

import jax
import jax.numpy as jnp
from jax import lax
from jax.experimental import pallas as pl
from jax.experimental.pallas import tpu as pltpu

F32 = jnp.float32
BF16 = jnp.bfloat16

NORM_EPS = 1e-6
N_META = 16
GRID_W = 64
ROPE_THETA = 10000.0
SC_WIDTH = 512
GLA_HEADS = 4
GLA_DK = 64
GLA_DV = 128
GLA_GATE_RANK = 16
GLA_GATE_TAU = 16.0
GLA_CHUNK = 64
ATT_HEAD_DIM = 64
ATT_HEADS = 16
ATT_KV_HEADS = 4
ATT_GROUP = ATT_HEADS // ATT_KV_HEADS
MOE_GROUPS = 4
MOE_EXPERTS_PER_GROUP = 8
N_EXPERTS = 32
MOE_D_FF = 256

LANES = 128
MIB = 2 ** 20
VMEM_LIMIT = 56 * MIB

_NT = (((1,), (1,)), ((), ()))
_TN = (((0,), (0,)), ((), ()))


def _params(*semantics):
    return pltpu.CompilerParams(dimension_semantics=semantics, vmem_limit_bytes=VMEM_LIMIT)


def _dot(a, b):
    return jnp.dot(a, b, preferred_element_type=F32)


def _silu(x):
    return x * (1.0 / (1.0 + jnp.exp(-x)))


def _norm_matmul_kernel(h_ref, g_ref, w_ref, o_ref, xn_ref):
    @pl.when(pl.program_id(1) == 0)
    def _():
        x = h_ref[...]
        ms = jnp.mean(x * x, axis=-1, keepdims=True)
        xn_ref[...] = (x * lax.rsqrt(ms + NORM_EPS) * g_ref[...]).astype(BF16)

    o_ref[...] = _dot(xn_ref[...], w_ref[...]).astype(o_ref.dtype)


def _norm_matmul(h, g, w, tm, tn):
    n, d = h.shape
    nout = w.shape[1]
    return pl.pallas_call(
        _norm_matmul_kernel,
        grid=(n // tm, nout // tn),
        in_specs=[pl.BlockSpec((tm, d), lambda i, j: (i, 0)),
                  pl.BlockSpec((1, d), lambda i, j: (0, 0)),
                  pl.BlockSpec((d, tn), lambda i, j: (0, j))],
        out_specs=pl.BlockSpec((tm, tn), lambda i, j: (i, j)),
        out_shape=jax.ShapeDtypeStruct((n, nout), BF16),
        scratch_shapes=[pltpu.VMEM((tm, d), BF16)],
        compiler_params=_params("parallel", "arbitrary"),
        name="norm_matmul",
    )(h, g.reshape(1, d), w)


def _conv_kernel(ab_ref, ac_ref, ax_ref, w_ref, o_ref, z_ref):
    t, c = ab_ref.shape
    z = ac_ref[...].astype(F32) * ax_ref[...].astype(F32)
    z_ref[0:8, :] = jnp.zeros((8, c), F32)
    z_ref[8:8 + t, :] = z
    z_ref[8 + t:16 + t, :] = jnp.zeros((8, c), F32)
    w = w_ref[...]
    conv = w[0:1, :] * z_ref[7:7 + t, :] + w[1:2, :] * z + w[2:3, :] * z_ref[9:9 + t, :]
    o_ref[...] = (ab_ref[...].astype(F32) * conv).astype(o_ref.dtype)


def _gated_conv(u3, conv_w):
    b, t, _ = u3.shape
    nblk = SC_WIDTH // LANES

    def col(off):
        return pl.BlockSpec((None, t, LANES), lambda i, j: (i, 0, off + j))

    return pl.pallas_call(
        _conv_kernel,
        grid=(b, nblk),
        in_specs=[col(0), col(nblk), col(2 * nblk),
                  pl.BlockSpec((3, LANES), lambda i, j: (0, j))],
        out_specs=pl.BlockSpec((None, t, LANES), lambda i, j: (i, 0, j)),
        out_shape=jax.ShapeDtypeStruct((b, t, SC_WIDTH), BF16),
        scratch_shapes=[pltpu.VMEM((t + 16, LANES), F32)],
        compiler_params=_params("parallel", "parallel"),
        name="gated_conv",
    )(u3, u3, u3, conv_w)


GLA_ROW_TILE = 688
GLA_EPI_TILE = 48


def _gla_kernel(q_ref, k_ref, v_ref, r_ref, g_ref, wc_ref, bc_ref, nw_ref, y_ref,
                la_ref, of_ref, ob_ref, sf_ref, sb_ref):
    t, nk = q_ref.shape
    nv = v_ref.shape[1]
    c = GLA_CHUNK
    n_real = (t - N_META) // c

    for i in range(t // GLA_ROW_TILE):
        rows = pl.ds(i * GLA_ROW_TILE, GLA_ROW_TILE)
        pre = _dot(g_ref[rows, :], wc_ref[...]) + bc_ref[...]
        la_ref[rows, :] = (jnp.minimum(pre, 0.0) - jnp.log1p(jnp.exp(-jnp.abs(pre)))) * (1.0 / GLA_GATE_TAU)

    def iota(shape, dim):
        return lax.broadcasted_iota(jnp.int32, shape, dim)

    tri_f = jnp.where(iota((c, c), 0) >= iota((c, c), 1), 1.0, 0.0).astype(BF16)
    tri_b = jnp.where(iota((c, c), 0) <= iota((c, c), 1), 1.0, 0.0).astype(BF16)
    keep_f = (iota((c, nk), 1) & (c - 1)) <= iota((c, nk), 0)
    keep_b = (iota((c, nk), 1) & (c - 1)) > iota((c, nk), 0)
    bd_k = (iota((nk, nk), 0) >> 6) == (iota((nk, nk), 1) >> 6)
    bd_v = (iota((nk, nv), 0) >> 6) == (iota((nk, nv), 1) >> 7)
    bd_s = (iota((nv, nk), 0) >> 7) == (iota((nv, nk), 1) >> 6)
    valid0 = iota((c, 1), 0) < N_META

    def chunk(rows, valid, tri, keep, tot_row, la_col, s_ref, o_ref, out_rows, n_out):
        q = q_ref[rows, :].astype(F32) * (GLA_DK ** -0.5)
        k = k_ref[rows, :].astype(F32)
        v = v_ref[rows, :].astype(F32)
        la = la_ref[rows, la_col:la_col + nk]
        if valid is not None:
            q = jnp.where(valid, q, 0.0)
            k = jnp.where(valid, k, 0.0)
            v = jnp.where(valid, v, 0.0)
            la = jnp.where(valid, la, 0.0)
        hi = la.astype(BF16)
        rem = la - hi.astype(F32)
        mid = rem.astype(BF16)
        lo = (rem - mid.astype(F32)).astype(BF16)
        cum = _dot(tri, hi) + _dot(tri, mid) + _dot(tri, lo)
        tot = cum[tot_row:tot_row + 1, :]
        q_dec = (q * jnp.exp(cum)).astype(BF16)
        k_inv = k * jnp.exp(-cum)
        k_end = (k * jnp.exp(tot - cum)).astype(BF16)
        kbd = jnp.where(bd_k, jnp.concatenate([k_inv] * GLA_HEADS, axis=0), 0.0).astype(BF16)
        p = lax.dot_general(q_dec, kbd, _NT, preferred_element_type=F32)
        p = jnp.where(keep, p, 0.0).astype(BF16)
        vbd = jnp.where(bd_v, jnp.concatenate([v] * GLA_HEADS, axis=0), 0.0).astype(BF16)
        s_old = s_ref[...]
        o = _dot(p, vbd) + lax.dot_general(q_dec, s_old.astype(BF16), _NT, preferred_element_type=F32)
        kvt = lax.dot_general(v.astype(BF16), k_end, _TN, preferred_element_type=F32)
        s_ref[...] = jnp.exp(tot) * s_old + jnp.where(bd_s, kvt, 0.0)
        o_ref[out_rows, :] = o[:n_out]

    sf_ref[...] = jnp.zeros(sf_ref.shape, F32)
    sb_ref[...] = jnp.zeros(sb_ref.shape, F32)
    meta_rows = pl.ds(0, c)
    chunk(meta_rows, valid0, tri_f, keep_f, c - 1, 0, sf_ref, of_ref, pl.ds(0, N_META), N_META)

    def body(i, carry):
        rf = pl.multiple_of(N_META + i * c, 16)
        chunk(pl.ds(rf, c), None, tri_f, keep_f, c - 1, 0, sf_ref, of_ref, pl.ds(rf, c), c)
        rb = pl.multiple_of(N_META + (n_real - 1 - i) * c, 16)
        chunk(pl.ds(rb, c), None, tri_b, keep_b, 0, nk, sb_ref, ob_ref, pl.ds(rb, c), c)
        return carry

    lax.fori_loop(0, n_real, body, 0)
    chunk(meta_rows, valid0, tri_b, keep_b, 0, nk, sb_ref, ob_ref, pl.ds(0, N_META), N_META)

    def epilogue(i, carry):
        rows = pl.ds(pl.multiple_of(i * GLA_EPI_TILE, 16), GLA_EPI_TILE)
        o = of_ref[rows, :] + ob_ref[rows, :]
        r = r_ref[rows, :].astype(F32)
        for h in range(GLA_HEADS):
            cols = slice(h * GLA_DV, (h + 1) * GLA_DV)
            oh = o[:, cols]
            ms = jnp.mean(oh * oh, axis=-1, keepdims=True)
            yh = oh * lax.rsqrt(ms + NORM_EPS) * nw_ref[...] * _silu(r[:, cols])
            y_ref[rows, cols] = yh.astype(y_ref.dtype)
        return carry

    lax.fori_loop(0, t // GLA_EPI_TILE, epilogue, 0)


def _gla(u3, wc, bc, norm_w):
    b, t, _ = u3.shape
    nk = GLA_HEADS * GLA_DK
    nv = GLA_HEADS * GLA_DV
    q_blk = 3 * SC_WIDTH // nk
    v_blk = (3 * SC_WIDTH + 2 * nk) // nv
    g_blk = (3 * SC_WIDTH + 2 * nk + 2 * nv) // LANES
    full = lambda shape: pl.BlockSpec(shape, lambda i: (0, 0))
    return pl.pallas_call(
        _gla_kernel,
        grid=(b,),
        in_specs=[pl.BlockSpec((None, t, nk), lambda i: (i, 0, q_blk)),
                  pl.BlockSpec((None, t, nk), lambda i: (i, 0, q_blk + 1)),
                  pl.BlockSpec((None, t, nv), lambda i: (i, 0, v_blk)),
                  pl.BlockSpec((None, t, nv), lambda i: (i, 0, v_blk + 1)),
                  pl.BlockSpec((None, t, LANES), lambda i: (i, 0, g_blk)),
                  full((LANES, 2 * nk)), full((1, 2 * nk)), full((1, GLA_DV))],
        out_specs=pl.BlockSpec((None, t, nv), lambda i: (i, 0, 0)),
        out_shape=jax.ShapeDtypeStruct((b, t, nv), BF16),
        scratch_shapes=[pltpu.VMEM((t, 2 * nk), F32),
                        pltpu.VMEM((t, nv), F32), pltpu.VMEM((t, nv), F32),
                        pltpu.VMEM((nv, nk), F32), pltpu.VMEM((nv, nk), F32)],
        compiler_params=_params("parallel"),
        name="gla",
    )(u3, u3, u3, u3, u3, wc, bc, norm_w)


def _proj2_res_kernel(a_ref, b_ref, wa_ref, wb_ref, res_ref, o_ref):
    o_ref[...] = res_ref[...] + _dot(a_ref[...], wa_ref[...]) + _dot(b_ref[...], wb_ref[...])


def _proj2_res(a, b, wa, wb, res, tm):
    n, d = res.shape
    ka, kb = a.shape[1], b.shape[1]
    return pl.pallas_call(
        _proj2_res_kernel,
        grid=(n // tm,),
        in_specs=[pl.BlockSpec((tm, ka), lambda i: (i, 0)),
                  pl.BlockSpec((tm, kb), lambda i: (i, 0)),
                  pl.BlockSpec((ka, d), lambda i: (0, 0)),
                  pl.BlockSpec((kb, d), lambda i: (0, 0)),
                  pl.BlockSpec((tm, d), lambda i: (i, 0))],
        out_specs=pl.BlockSpec((tm, d), lambda i: (i, 0)),
        out_shape=jax.ShapeDtypeStruct((n, d), F32),
        compiler_params=_params("parallel"),
        name="proj2_res",
    )(a, b, wa, wb, res)


GROUP_LANE0 = N_EXPERTS


def _router_kernel(h_ref, g_ref, whi_ref, wlo_ref, b_ref, xn_ref, gates_ref):
    x = h_ref[...]
    ms = jnp.mean(x * x, axis=-1, keepdims=True)
    xn = x * lax.rsqrt(ms + NORM_EPS) * g_ref[...]
    xhi = xn.astype(BF16)
    xn_ref[...] = xhi
    xlo = (xn - xhi.astype(F32)).astype(BF16)
    logits = _dot(xhi, whi_ref[...]) + _dot(xhi, wlo_ref[...]) + _dot(xlo, whi_ref[...]) + b_ref[...]

    lane = lax.broadcasted_iota(jnp.int32, logits.shape, 1)
    neg = -jnp.inf
    big = jnp.int32(LANES)
    is_group = (lane >> 2) == (GROUP_LANE0 >> 2)
    gl = jnp.where(is_group, logits, neg)
    gmax = jnp.max(gl, axis=-1, keepdims=True)
    g_p = 1.0 / jnp.sum(jnp.exp(gl - gmax), axis=-1, keepdims=True)
    g_idx = jnp.min(jnp.where(gl == gmax, lane, big), axis=-1, keepdims=True) - GROUP_LANE0
    el = jnp.where((lane >> 3) == g_idx, logits, neg)
    m1 = jnp.max(el, axis=-1, keepdims=True)
    i1 = jnp.min(jnp.where(el == m1, lane, big), axis=-1, keepdims=True)
    el2 = jnp.where(lane == i1, neg, el)
    m2 = jnp.max(el2, axis=-1, keepdims=True)
    i2 = jnp.min(jnp.where(el2 == m2, lane, big), axis=-1, keepdims=True)
    e2 = jnp.exp(m2 - m1)
    w1 = g_p / (1.0 + e2)
    w2 = g_p * e2 / (1.0 + e2)
    gates_ref[...] = jnp.where(lane == i1, w1, 0.0) + jnp.where(lane == i2, w2, 0.0)


def _router(h, g, whi, wlo, bias, tm):
    n, d = h.shape
    return pl.pallas_call(
        _router_kernel,
        grid=(n // tm,),
        in_specs=[pl.BlockSpec((tm, d), lambda i: (i, 0)),
                  pl.BlockSpec((1, d), lambda i: (0, 0)),
                  pl.BlockSpec((d, LANES), lambda i: (0, 0)),
                  pl.BlockSpec((d, LANES), lambda i: (0, 0)),
                  pl.BlockSpec((1, LANES), lambda i: (0, 0))],
        out_specs=[pl.BlockSpec((tm, d), lambda i: (i, 0)),
                   pl.BlockSpec((tm, LANES), lambda i: (i, 0))],
        out_shape=[jax.ShapeDtypeStruct((n, d), BF16),
                   jax.ShapeDtypeStruct((n, LANES), F32)],
        compiler_params=_params("parallel"),
        name="moe_router",
    )(h, g.reshape(1, d), whi, wlo, bias)


def _moe_kernel(xn_ref, gates_ref, wg_ref, wu_ref, wd_ref, res_ref, fg_ref, o_ref, acc_ref, *, final_norm):
    e = pl.program_id(1)

    @pl.when(e == 0)
    def _():
        acc_ref[...] = jnp.zeros(acc_ref.shape, F32)

    x = xn_ref[...]
    he = _silu(_dot(x, wg_ref[...])) * _dot(x, wu_ref[...])
    gates = gates_ref[...]
    lane = lax.broadcasted_iota(jnp.int32, gates.shape, 1)
    gate = jnp.sum(jnp.where(lane == e, gates, 0.0), axis=-1, keepdims=True)
    acc_ref[...] += _dot((he * gate).astype(BF16), wd_ref[...])

    @pl.when(e == pl.num_programs(1) - 1)
    def _():
        y = res_ref[...] + acc_ref[...]
        if final_norm:
            ms = jnp.mean(y * y, axis=-1, keepdims=True)
            y = y * lax.rsqrt(ms + NORM_EPS) * fg_ref[...]
        o_ref[...] = y


def _moe_dense(xn, gates, wg, wu, wd, res, final_g, tm, final_norm):
    n, d = xn.shape
    return pl.pallas_call(
        functools.partial(_moe_kernel, final_norm=final_norm),
        grid=(n // tm, N_EXPERTS),
        in_specs=[pl.BlockSpec((tm, d), lambda i, e: (i, 0)),
                  pl.BlockSpec((tm, LANES), lambda i, e: (i, 0)),
                  pl.BlockSpec((None, d, MOE_D_FF), lambda i, e: (e, 0, 0)),
                  pl.BlockSpec((None, d, MOE_D_FF), lambda i, e: (e, 0, 0)),
                  pl.BlockSpec((None, MOE_D_FF, d), lambda i, e: (e, 0, 0)),
                  pl.BlockSpec((tm, d), lambda i, e: (i, 0)),
                  pl.BlockSpec((1, d), lambda i, e: (0, 0))],
        out_specs=pl.BlockSpec((tm, d), lambda i, e: (i, 0)),
        out_shape=jax.ShapeDtypeStruct((n, d), F32),
        scratch_shapes=[pltpu.VMEM((tm, d), F32)],
        compiler_params=_params("parallel", "arbitrary"),
        name="moe_experts",
    )(xn, gates, wg, wu, wd, res, final_g.reshape(1, d))


def _moe_layer(h, norm_g, w_group, b_group, w_expert, b_expert, w_gate, w_up, w_down,
               final_g, tm_router, tm_moe, final_norm):
    d = h.shape[1]
    pad = LANES - N_EXPERTS - MOE_GROUPS
    wr = jnp.concatenate([w_expert, w_group, jnp.zeros((d, pad), F32)], axis=1)
    whi = wr.astype(BF16)
    wlo = (wr - whi.astype(F32)).astype(BF16)
    bias = jnp.concatenate([b_expert, b_group, jnp.zeros((pad,), F32)]).reshape(1, LANES)
    xn, gates = _router(h, norm_g, whi, wlo, bias, tm_router)
    return _moe_dense(xn, gates, w_gate.astype(BF16), w_up.astype(BF16), w_down.astype(BF16),
                      h, final_g, tm_moe, final_norm)


ROPE_ROW_TILE = 688


def _rope_kernel(q_ref, k_ref, cos_ref, sin_ref, qn_ref, kn_ref, qo_ref, ko_ref):
    t = q_ref.shape[0]
    rt = ROPE_ROW_TILE
    even = (lax.broadcasted_iota(jnp.int32, (rt, LANES), 1) & 1) == 0
    gi = lax.broadcasted_iota(jnp.int32, (LANES, LANES), 0) >> 6
    gj = lax.broadcasted_iota(jnp.int32, (LANES, LANES), 1) >> 6
    gmat = jnp.where(gi == gj, 1.0, 0.0).astype(BF16)

    def norm_rope(x, gain, cos, sin):
        sq = x * x
        hi = sq.astype(BF16)
        lo = (sq - hi.astype(F32)).astype(BF16)
        ms = (_dot(hi, gmat) + _dot(lo, gmat)) * (1.0 / ATT_HEAD_DIM)
        xn = x * lax.rsqrt(ms + NORM_EPS) * gain
        partner = jnp.where(even, pltpu.roll(xn, LANES - 1, 1), pltpu.roll(xn, 1, 1))
        return xn * cos + partner * sin

    for i in range(t // rt):
        lo_row = i * rt
        cos = cos_ref[lo_row:lo_row + rt, :]
        sin = sin_ref[lo_row:lo_row + rt, :]
        skip = N_META if i == 0 else 0
        for j in range(q_ref.shape[1] // LANES):
            cols = slice(j * LANES, (j + 1) * LANES)
            y = norm_rope(q_ref[lo_row:lo_row + rt, cols].astype(F32), qn_ref[...], cos, sin)
            y = y * (ATT_HEAD_DIM ** -0.5)
            qo_ref[lo_row + skip - N_META:lo_row + rt - N_META, cols] = y[skip:].astype(qo_ref.dtype)
        for j in range(k_ref.shape[1] // LANES):
            cols = slice(j * LANES, (j + 1) * LANES)
            y = norm_rope(k_ref[lo_row:lo_row + rt, cols].astype(F32), kn_ref[...], cos, sin)
            ko_ref[lo_row:lo_row + rt, cols] = y.astype(ko_ref.dtype)


def _rope(u3, cos_e, sin_e, qn, kn):
    b, t, _ = u3.shape
    nq = ATT_HEADS * ATT_HEAD_DIM
    nkv = ATT_KV_HEADS * ATT_HEAD_DIM
    full = lambda shape: pl.BlockSpec(shape, lambda i: (0, 0))
    return pl.pallas_call(
        _rope_kernel,
        grid=(b,),
        in_specs=[pl.BlockSpec((None, t, nq), lambda i: (i, 0, 0)),
                  pl.BlockSpec((None, t, nkv), lambda i: (i, 0, nq // nkv)),
                  full((t, LANES)), full((t, LANES)), full((1, LANES)), full((1, LANES))],
        out_specs=[pl.BlockSpec((None, t - N_META, nq), lambda i: (i, 0, 0)),
                   pl.BlockSpec((None, t, nkv), lambda i: (i, 0, 0))],
        out_shape=[jax.ShapeDtypeStruct((b, t - N_META, nq), BF16),
                   jax.ShapeDtypeStruct((b, t, nkv), BF16)],
        compiler_params=_params("parallel"),
        name="qk_norm_rope",
    )(u3, u3, cos_e, sin_e, qn, kn)


ATT_Q_TILE = 256


def _attn_kernel(q_ref, k_ref, v_ref, o_ref, kg_ref, vg_ref):
    s_len, w = q_ref.shape
    hk = pl.program_id(1)
    row = lax.broadcasted_iota(jnp.int32, (w, w), 0)
    col = lax.broadcasted_iota(jnp.int32, (w, w), 1)
    for g in range(ATT_GROUP):
        src = jnp.where((col >> 6) == g, hk * ATT_HEAD_DIM + (col & (ATT_HEAD_DIM - 1)), -1)
        sel = jnp.where(row == src, 1.0, 0.0).astype(BF16)
        kg_ref[g] = _dot(k_ref[...], sel).astype(BF16)
        vg_ref[g] = _dot(v_ref[...], sel).astype(BF16)

    def q_tile(i, carry):
        rows = pl.ds(pl.multiple_of(i * ATT_Q_TILE, ATT_Q_TILE), ATT_Q_TILE)
        q = q_ref[rows, :]
        acc = jnp.zeros((ATT_Q_TILE, w), F32)
        for g in range(ATT_GROUP):
            s = lax.dot_general(q, kg_ref[g], _NT, preferred_element_type=F32)
            p = jnp.exp(s - jnp.max(s, axis=-1, keepdims=True))
            inv = 1.0 / jnp.sum(p, axis=-1, keepdims=True)
            acc = acc + _dot(p.astype(BF16), vg_ref[g]) * inv
        o_ref[rows, :] = acc.astype(o_ref.dtype)
        return carry

    lax.fori_loop(0, s_len // ATT_Q_TILE, q_tile, 0)


def _attention(q, k, u3):
    b, s_len, nq = q.shape
    t = k.shape[1]
    nkv = ATT_KV_HEADS * ATT_HEAD_DIM
    w = ATT_GROUP * ATT_HEAD_DIM
    v_blk = (nq + nkv) // nkv
    return pl.pallas_call(
        _attn_kernel,
        grid=(b, ATT_KV_HEADS),
        in_specs=[pl.BlockSpec((None, s_len, w), lambda i, h: (i, 0, h)),
                  pl.BlockSpec((None, t, nkv), lambda i, h: (i, 0, 0)),
                  pl.BlockSpec((None, t, nkv), lambda i, h: (i, 0, v_blk))],
        out_specs=pl.BlockSpec((None, s_len, w), lambda i, h: (i, 0, h)),
        out_shape=jax.ShapeDtypeStruct((b, s_len, nq), BF16),
        scratch_shapes=[pltpu.VMEM((ATT_GROUP, t, w), BF16), pltpu.VMEM((ATT_GROUP, t, w), BF16)],
        compiler_params=_params("parallel", "parallel"),
        name="gqa_attention",
    )(q, k, u3)


def _att_out_kernel(o_ref, w_ref, h_ref, out_ref):
    out_ref[...] = h_ref[N_META:, :] + _dot(o_ref[...], w_ref[...])


def _att_out(o, w, h3, tn):
    b, s_len, nq = o.shape
    t, d = h3.shape[1], h3.shape[2]
    return pl.pallas_call(
        _att_out_kernel,
        grid=(b, d // tn),
        in_specs=[pl.BlockSpec((None, s_len, nq), lambda i, j: (i, 0, 0)),
                  pl.BlockSpec((nq, tn), lambda i, j: (0, j)),
                  pl.BlockSpec((None, t, tn), lambda i, j: (i, 0, j))],
        out_specs=pl.BlockSpec((None, s_len, tn), lambda i, j: (i, 0, j)),
        out_shape=jax.ShapeDtypeStruct((b, s_len, d), F32),
        compiler_params=_params("parallel", "parallel"),
        name="att_out_res",
    )(o, w, h3)


def _rope_tables(n_tokens):
    rows_n = n_tokens // GRID_W
    rows = jnp.repeat(jnp.arange(rows_n), GRID_W).astype(F32)
    cols = jnp.tile(jnp.arange(GRID_W), rows_n).astype(F32)
    axis_dims = ATT_HEAD_DIM // 2
    freqs = ROPE_THETA ** (-jnp.arange(0, axis_dims, 2, dtype=F32) / axis_dims)
    ang = jnp.concatenate([rows[:, None] * freqs, cols[:, None] * freqs], axis=-1)
    ang = jnp.concatenate([jnp.zeros((N_META, ang.shape[1]), F32), ang], axis=0)
    cos_e = jnp.tile(jnp.repeat(jnp.cos(ang), 2, axis=1), (1, LANES // ATT_HEAD_DIM))
    sign = jnp.tile(jnp.array([-1.0, 1.0], F32), LANES // 2)
    sin_e = jnp.tile(jnp.repeat(jnp.sin(ang), 2, axis=1), (1, LANES // ATT_HEAD_DIM)) * sign
    return cos_e, sin_e


def kernel(x, meta_tokens, norm_mix, norm_ffn, norm_final, sc_w_in, sc_conv_w, gla_w_gate_fwd, gla_b_gate_fwd, gla_w_gate_bwd, gla_b_gate_bwd, gla_norm_w, sc_w_out, att_w_in, att_q_norm, att_k_norm, att_w_out, moe_w_group, moe_b_group, moe_w_expert, moe_b_expert, moe_w_gate, moe_w_up, moe_w_down):
    bsz, s_len, d = x.shape
    t = s_len + N_META
    n0, n1 = bsz * t, bsz * s_len
    tm0 = t // 3
    tm1 = 512

    meta = jnp.broadcast_to(meta_tokens.astype(x.dtype)[None], (bsz, N_META, d))
    h = jnp.concatenate([meta, x], axis=1).reshape(n0, d)

    nk = GLA_HEADS * GLA_DK
    mix_in = sc_w_in.shape[2]
    mix_pad = -mix_in % LANES
    w_in = jnp.pad(sc_w_in[0], ((0, 0), (0, mix_pad))).astype(BF16)
    u = _norm_matmul(h, norm_mix[0], w_in, tm0, 5 * LANES)
    u3 = u.reshape(bsz, t, mix_in + mix_pad)
    y_a = _gated_conv(u3, sc_conv_w[0])
    wc = jnp.zeros((LANES, 2 * nk), F32)
    wc = wc.at[:GLA_GATE_RANK, :nk].set(gla_w_gate_fwd[0])
    wc = wc.at[GLA_GATE_RANK:2 * GLA_GATE_RANK, nk:].set(gla_w_gate_bwd[0])
    bc = jnp.concatenate([gla_b_gate_fwd[0], gla_b_gate_bwd[0]]).reshape(1, 2 * nk)
    y_b = _gla(u3, wc.astype(BF16), bc, gla_norm_w[0].reshape(1, GLA_DV))
    w_out = sc_w_out[0].astype(BF16)
    h = _proj2_res(y_a.reshape(n0, SC_WIDTH), y_b.reshape(n0, -1), w_out[:SC_WIDTH], w_out[SC_WIDTH:], h, tm0)
    h = _moe_layer(h, norm_ffn[0], moe_w_group[0], moe_b_group[0], moe_w_expert[0], moe_b_expert[0],
                   moe_w_gate[0], moe_w_up[0], moe_w_down[0], norm_final, tm0, 2 * tm0, False)

    u = _norm_matmul(h, norm_mix[1], att_w_in[0].astype(BF16), tm0, 4 * LANES)
    u3 = u.reshape(bsz, t, -1)
    cos_e, sin_e = _rope_tables(s_len)
    qn = jnp.tile(att_q_norm[0], LANES // ATT_HEAD_DIM).reshape(1, LANES)
    kn = jnp.tile(att_k_norm[0], LANES // ATT_HEAD_DIM).reshape(1, LANES)
    q, k = _rope(u3, cos_e, sin_e, qn, kn)
    o = _attention(q, k, u3)
    h = _att_out(o, att_w_out[0].astype(BF16), h.reshape(bsz, t, d), 4 * LANES).reshape(n1, d)
    h = _moe_layer(h, norm_ffn[1], moe_w_group[1], moe_b_group[1], moe_w_expert[1], moe_b_expert[1],
                   moe_w_gate[1], moe_w_up[1], moe_w_down[1], norm_final, tm1, 2 * tm1, True)
    return h.reshape(bsz, s_len, d)
```

```python
import functools

import jax
import jax.numpy as jnp
from jax import lax
from jax.experimental import pallas as pl
from jax.experimental.pallas import tpu as pltpu

F32 = jnp.float32
BF16 = jnp.bfloat16

NORM_EPS = 1e-6
N_META = 16
GRID_W = 64
ROPE_THETA = 10000.0
SC_WIDTH = 512
GLA_HEADS = 4
GLA_DK = 64
GLA_DV = 128
GLA_GATE_RANK = 16
GLA_GATE_TAU = 16.0
GLA_CHUNK = 64
ATT_HEAD_DIM = 64
ATT_HEADS = 16
ATT_KV_HEADS = 4
ATT_GROUP = ATT_HEADS // ATT_KV_HEADS
MOE_GROUPS = 4
MOE_EXPERTS_PER_GROUP = 8
N_EXPERTS = 32
MOE_D_FF = 256

LANES = 128
MIB = 2 ** 20
VMEM_LIMIT = 56 * MIB

_NT = (((1,), (1,)), ((), ()))
_TN = (((0,), (0,)), ((), ()))


def _params(*semantics):
    return pltpu.CompilerParams(dimension_semantics=semantics, vmem_limit_bytes=VMEM_LIMIT)


def _dot(a, b):
    return jnp.dot(a, b, preferred_element_type=F32)


def _silu(x):
    return x * (1.0 / (1.0 + jnp.exp(-x)))


def _norm_matmul_kernel(h_ref, g_ref, w_ref, o_ref, xn_ref):
    @pl.when(pl.program_id(1) == 0)
    def _():
        x = h_ref[...]
        ms = jnp.mean(x * x, axis=-1, keepdims=True)
        xn_ref[...] = (x * lax.rsqrt(ms + NORM_EPS) * g_ref[...]).astype(BF16)

    o_ref[...] = _dot(xn_ref[...], w_ref[...]).astype(o_ref.dtype)


def _norm_matmul(h, g, w, tm, tn):
    n, d = h.shape
    nout = w.shape[1]
    return pl.pallas_call(
        _norm_matmul_kernel,
        grid=(n // tm, nout // tn),
        in_specs=[pl.BlockSpec((tm, d), lambda i, j: (i, 0)),
                  pl.BlockSpec((1, d), lambda i, j: (0, 0)),
                  pl.BlockSpec((d, tn), lambda i, j: (0, j))],
        out_specs=pl.BlockSpec((tm, tn), lambda i, j: (i, j)),
        out_shape=jax.ShapeDtypeStruct((n, nout), BF16),
        scratch_shapes=[pltpu.VMEM((tm, d), BF16)],
        compiler_params=_params("parallel", "arbitrary"),
        name="norm_matmul",
    )(h, g.reshape(1, d), w)


def _conv_kernel(ab_ref, ac_ref, ax_ref, w_ref, o_ref, z_ref):
    t, c = ab_ref.shape
    z = ac_ref[...].astype(F32) * ax_ref[...].astype(F32)
    z_ref[0:8, :] = jnp.zeros((8, c), F32)
    z_ref[8:8 + t, :] = z
    z_ref[8 + t:16 + t, :] = jnp.zeros((8, c), F32)
    w = w_ref[...]
    conv = w[0:1, :] * z_ref[7:7 + t, :] + w[1:2, :] * z + w[2:3, :] * z_ref[9:9 + t, :]
    o_ref[...] = (ab_ref[...].astype(F32) * conv).astype(o_ref.dtype)


def _gated_conv(u3, conv_w):
    b, t, _ = u3.shape
    nblk = SC_WIDTH // LANES

    def col(off):
        return pl.BlockSpec((None, t, LANES), lambda i, j: (i, 0, off + j))

    return pl.pallas_call(
        _conv_kernel,
        grid=(b, nblk),
        in_specs=[col(0), col(nblk), col(2 * nblk),
                  pl.BlockSpec((3, LANES), lambda i, j: (0, j))],
        out_specs=pl.BlockSpec((None, t, LANES), lambda i, j: (i, 0, j)),
        out_shape=jax.ShapeDtypeStruct((b, t, SC_WIDTH), BF16),
        scratch_shapes=[pltpu.VMEM((t + 16, LANES), F32)],
        compiler_params=_params("parallel", "parallel"),
        name="gated_conv",
    )(u3, u3, u3, conv_w)


GLA_ROW_TILE = 688
GLA_EPI_TILE = 48


def _gla_kernel(q_ref, k_ref, v_ref, r_ref, g_ref, wc_ref, bc_ref, nw_ref, y_ref,
                la_ref, of_ref, ob_ref, sf_ref, sb_ref):
    t, nk = q_ref.shape
    nv = v_ref.shape[1]
    c = GLA_CHUNK
    n_real = (t - N_META) // c

    for i in range(t // GLA_ROW_TILE):
        rows = pl.ds(i * GLA_ROW_TILE, GLA_ROW_TILE)
        pre = _dot(g_ref[rows, :], wc_ref[...]) + bc_ref[...]
        la_ref[rows, :] = (jnp.minimum(pre, 0.0) - jnp.log1p(jnp.exp(-jnp.abs(pre)))) * (1.0 / GLA_GATE_TAU)

    def iota(shape, dim):
        return lax.broadcasted_iota(jnp.int32, shape, dim)

    tri_f = jnp.where(iota((c, c), 0) >= iota((c, c), 1), 1.0, 0.0).astype(BF16)
    tri_b = jnp.where(iota((c, c), 0) <= iota((c, c), 1), 1.0, 0.0).astype(BF16)
    keep_f = (iota((c, nk), 1) & (c - 1)) <= iota((c, nk), 0)
    keep_b = (iota((c, nk), 1) & (c - 1)) > iota((c, nk), 0)
    bd_k = (iota((nk, nk), 0) >> 6) == (iota((nk, nk), 1) >> 6)
    bd_v = (iota((nk, nv), 0) >> 6) == (iota((nk, nv), 1) >> 7)
    bd_s = (iota((nv, nk), 0) >> 7) == (iota((nv, nk), 1) >> 6)
    valid0 = iota((c, 1), 0) < N_META

    def chunk(rows, valid, tri, keep, tot_row, la_col, s_ref, o_ref, out_rows, n_out):
        q = q_ref[rows, :].astype(F32) * (GLA_DK ** -0.5)
        k = k_ref[rows, :].astype(F32)
        v = v_ref[rows, :].astype(F32)
        la = la_ref[rows, la_col:la_col + nk]
        if valid is not None:
            q = jnp.where(valid, q, 0.0)
            k = jnp.where(valid, k, 0.0)
            v = jnp.where(valid, v, 0.0)
            la = jnp.where(valid, la, 0.0)
        hi = la.astype(BF16)
        rem = la - hi.astype(F32)
        mid = rem.astype(BF16)
        lo = (rem - mid.astype(F32)).astype(BF16)
        cum = _dot(tri, hi) + _dot(tri, mid) + _dot(tri, lo)
        tot = cum[tot_row:tot_row + 1, :]
        q_dec = (q * jnp.exp(cum)).astype(BF16)
        k_inv = k * jnp.exp(-cum)
        k_end = (k * jnp.exp(tot - cum)).astype(BF16)
        kbd = jnp.where(bd_k, jnp.concatenate([k_inv] * GLA_HEADS, axis=0), 0.0).astype(BF16)
        p = lax.dot_general(q_dec, kbd, _NT, preferred_element_type=F32)
        p = jnp.where(keep, p, 0.0).astype(BF16)
        vbd = jnp.where(bd_v, jnp.concatenate([v] * GLA_HEADS, axis=0), 0.0).astype(BF16)
        s_old = s_ref[...]
        o = _dot(p, vbd) + lax.dot_general(q_dec, s_old.astype(BF16), _NT, preferred_element_type=F32)
        kvt = lax.dot_general(v.astype(BF16), k_end, _TN, preferred_element_type=F32)
        s_ref[...] = jnp.exp(tot) * s_old + jnp.where(bd_s, kvt, 0.0)
        o_ref[out_rows, :] = o[:n_out]

    sf_ref[...] = jnp.zeros(sf_ref.shape, F32)
    sb_ref[...] = jnp.zeros(sb_ref.shape, F32)
    meta_rows = pl.ds(0, c)
    chunk(meta_rows, valid0, tri_f, keep_f, c - 1, 0, sf_ref, of_ref, pl.ds(0, N_META), N_META)

    def body(i, carry):
        rf = pl.multiple_of(N_META + i * c, 16)
        chunk(pl.ds(rf, c), None, tri_f, keep_f, c - 1, 0, sf_ref, of_ref, pl.ds(rf, c), c)
        rb = pl.multiple_of(N_META + (n_real - 1 - i) * c, 16)
        chunk(pl.ds(rb, c), None, tri_b, keep_b, 0, nk, sb_ref, ob_ref, pl.ds(rb, c), c)
        return carry

    lax.fori_loop(0, n_real, body, 0)
    chunk(meta_rows, valid0, tri_b, keep_b, 0, nk, sb_ref, ob_ref, pl.ds(0, N_META), N_META)

    def epilogue(i, carry):
        rows = pl.ds(pl.multiple_of(i * GLA_EPI_TILE, 16), GLA_EPI_TILE)
        o = of_ref[rows, :] + ob_ref[rows, :]
        r = r_ref[rows, :].astype(F32)
        for h in range(GLA_HEADS):
            cols = slice(h * GLA_DV, (h + 1) * GLA_DV)
            oh = o[:, cols]
            ms = jnp.mean(oh * oh, axis=-1, keepdims=True)
            yh = oh * lax.rsqrt(ms + NORM_EPS) * nw_ref[...] * _silu(r[:, cols])
            y_ref[rows, cols] = yh.astype(y_ref.dtype)
        return carry

    lax.fori_loop(0, t // GLA_EPI_TILE, epilogue, 0)


def _gla(u3, wc, bc, norm_w):
    b, t, _ = u3.shape
    nk = GLA_HEADS * GLA_DK
    nv = GLA_HEADS * GLA_DV
    q_blk = 3 * SC_WIDTH // nk
    v_blk = (3 * SC_WIDTH + 2 * nk) // nv
    g_blk = (3 * SC_WIDTH + 2 * nk + 2 * nv) // LANES
    full = lambda shape: pl.BlockSpec(shape, lambda i: (0, 0))
    return pl.pallas_call(
        _gla_kernel,
        grid=(b,),
        in_specs=[pl.BlockSpec((None, t, nk), lambda i: (i, 0, q_blk)),
                  pl.BlockSpec((None, t, nk), lambda i: (i, 0, q_blk + 1)),
                  pl.BlockSpec((None, t, nv), lambda i: (i, 0, v_blk)),
                  pl.BlockSpec((None, t, nv), lambda i: (i, 0, v_blk + 1)),
                  pl.BlockSpec((None, t, LANES), lambda i: (i, 0, g_blk)),
                  full((LANES, 2 * nk)), full((1, 2 * nk)), full((1, GLA_DV))],
        out_specs=pl.BlockSpec((None, t, nv), lambda i: (i, 0, 0)),
        out_shape=jax.ShapeDtypeStruct((b, t, nv), BF16),
        scratch_shapes=[pltpu.VMEM((t, 2 * nk), F32),
                        pltpu.VMEM((t, nv), F32), pltpu.VMEM((t, nv), F32),
                        pltpu.VMEM((nv, nk), F32), pltpu.VMEM((nv, nk), F32)],
        compiler_params=_params("parallel"),
        name="gla",
    )(u3, u3, u3, u3, u3, wc, bc, norm_w)


def _proj2_res_kernel(a_ref, b_ref, wa_ref, wb_ref, res_ref, o_ref):
    o_ref[...] = res_ref[...] + _dot(a_ref[...], wa_ref[...]) + _dot(b_ref[...], wb_ref[...])


def _proj2_res(a, b, wa, wb, res, tm):
    n, d = res.shape
    ka, kb = a.shape[1], b.shape[1]
    return pl.pallas_call(
        _proj2_res_kernel,
        grid=(n // tm,),
        in_specs=[pl.BlockSpec((tm, ka), lambda i: (i, 0)),
                  pl.BlockSpec((tm, kb), lambda i: (i, 0)),
                  pl.BlockSpec((ka, d), lambda i: (0, 0)),
                  pl.BlockSpec((kb, d), lambda i: (0, 0)),
                  pl.BlockSpec((tm, d), lambda i: (i, 0))],
        out_specs=pl.BlockSpec((tm, d), lambda i: (i, 0)),
        out_shape=jax.ShapeDtypeStruct((n, d), F32),
        compiler_params=_params("parallel"),
        name="proj2_res",
    )(a, b, wa, wb, res)


GROUP_LANE0 = N_EXPERTS


def _router_kernel(h_ref, g_ref, whi_ref, wlo_ref, b_ref, xn_ref, gates_ref, info_ref, inforow_ref, count_ref):
    x = h_ref[...]
    ms = jnp.mean(x * x, axis=-1, keepdims=True)
    xn = x * lax.rsqrt(ms + NORM_EPS) * g_ref[...]
    xhi = xn.astype(BF16)
    xn_ref[...] = xhi
    xlo = (xn - xhi.astype(F32)).astype(BF16)
    logits = _dot(xhi, whi_ref[...]) + _dot(xhi, wlo_ref[...]) + _dot(xlo, whi_ref[...]) + b_ref[...]

    lane = lax.broadcasted_iota(jnp.int32, logits.shape, 1)
    neg = -jnp.inf
    big = jnp.int32(LANES)
    is_group = (lane >> 2) == (GROUP_LANE0 >> 2)
    gl = jnp.where(is_group, logits, neg)
    gmax = jnp.max(gl, axis=-1, keepdims=True)
    g_p = 1.0 / jnp.sum(jnp.exp(gl - gmax), axis=-1, keepdims=True)
    g_idx = jnp.min(jnp.where(gl == gmax, lane, big), axis=-1, keepdims=True) - GROUP_LANE0
    el = jnp.where((lane >> 3) == g_idx, logits, neg)
    m1 = jnp.max(el, axis=-1, keepdims=True)
    i1 = jnp.min(jnp.where(el == m1, lane, big), axis=-1, keepdims=True)
    el2 = jnp.where(lane == i1, neg, el)
    m2 = jnp.max(el2, axis=-1, keepdims=True)
    i2 = jnp.min(jnp.where(el2 == m2, lane, big), axis=-1, keepdims=True)
    e2 = jnp.exp(m2 - m1)
    w1 = g_p / (1.0 + e2)
    w2 = g_p * e2 / (1.0 + e2)
    gates_ref[...] = jnp.where(lane == i1, w1, 0.0) + jnp.where(lane == i2, w2, 0.0)

    tm = x.shape[0]
    onehot = jnp.where(lane == g_idx, 1.0, 0.0)
    r_i = lax.broadcasted_iota(jnp.int32, (tm, tm), 0)
    c_i = lax.broadcasted_iota(jnp.int32, (tm, tm), 1)
    before = jnp.where(r_i > c_i, 1.0, 0.0).astype(BF16)
    prefix = _dot(before, onehot.astype(BF16))
    rank = jnp.sum(jnp.where(lane == g_idx, prefix, 0.0), axis=-1, keepdims=True).astype(jnp.int32)
    info_ref[...] = jnp.where(lane == 0, g_idx, jnp.where(lane == 1, rank, 0))
    eye = r_i == c_i
    g_row = jnp.sum(jnp.where(eye, g_idx, 0), axis=0, keepdims=True)
    rank_row = jnp.sum(jnp.where(eye, rank, 0), axis=0, keepdims=True)
    sub = lax.broadcasted_iota(jnp.int32, inforow_ref.shape, 0)
    inforow_ref[...] = jnp.where(sub == 0, g_row, jnp.where(sub == 1, rank_row, 0))
    counts = jnp.sum(onehot, axis=0, keepdims=True).astype(jnp.int32)
    count_ref[...] = jnp.broadcast_to(counts, count_ref.shape)


def _router(h, g, whi, wlo, bias, tm):
    n, d = h.shape
    nb = n // tm
    return pl.pallas_call(
        _router_kernel,
        grid=(nb,),
        in_specs=[pl.BlockSpec((tm, d), lambda i: (i, 0)),
                  pl.BlockSpec((1, d), lambda i: (0, 0)),
                  pl.BlockSpec((d, LANES), lambda i: (0, 0)),
                  pl.BlockSpec((d, LANES), lambda i: (0, 0)),
                  pl.BlockSpec((1, LANES), lambda i: (0, 0))],
        out_specs=[pl.BlockSpec((tm, d), lambda i: (i, 0)),
                   pl.BlockSpec((tm, LANES), lambda i: (i, 0)),
                   pl.BlockSpec((tm, LANES), lambda i: (i, 0)),
                   pl.BlockSpec((None, 8, tm), lambda i: (i, 0, 0)),
                   pl.BlockSpec((None, 8, LANES), lambda i: (i, 0, 0))],
        out_shape=[jax.ShapeDtypeStruct((n, d), BF16),
                   jax.ShapeDtypeStruct((n, LANES), F32),
                   jax.ShapeDtypeStruct((n, LANES), jnp.int32),
                   jax.ShapeDtypeStruct((nb, 8, tm), jnp.int32),
                   jax.ShapeDtypeStruct((nb, 8, LANES), jnp.int32)],
        compiler_params=_params("parallel"),
        name="moe_router",
    )(h, g.reshape(1, d), whi, wlo, bias)


MOE_TILE = 512
SEG_ALIGN = 16
STRIP = LANES


def _pow2_sizes(max_rows, min_rows):
    sizes, s = [], min_rows
    while s <= max_rows:
        sizes.append(s)
        s *= 2
    return sizes[::-1]


def _strip_copies(n_rows, sizes, make_copy):
    pos = 0
    for sz in sizes:
        bit = n_rows & sz
        yield bit, make_copy(pos, sz)
        pos = pos + bit


def _slot_key(group, rank, boff_ref, b):
    key = rank
    for g in range(MOE_GROUPS):
        key = key + jnp.where(group == g, boff_ref[MOE_GROUPS * b + g], 0)
    return key


def _dispatch_kernel(off_ref, cpad_ref, boff_ref, tail_ref, x_ref, gates_ref, inforow_ref,
                     xs_hbm, gs_hbm, xstg, gstg, sem):
    b = pl.program_id(0)
    tb = x_ref.shape[0]
    ks = xstg.shape[0]
    key = _slot_key(inforow_ref[0:1, :], inforow_ref[1:2, :], boff_ref, b)
    slot = lax.broadcasted_iota(jnp.int32, (ks, tb), 0)
    perm = jnp.where(slot == key, 1.0, 0.0).astype(BF16)
    xstg[...] = _dot(perm, x_ref[...]).astype(BF16)
    gt = gates_ref[...]
    hi = gt.astype(BF16)
    rem = gt - hi.astype(F32)
    mid = rem.astype(BF16)
    lo = (rem - mid.astype(F32)).astype(BF16)
    gstg[...] = _dot(perm, hi) + _dot(perm, mid) + _dot(perm, lo)

    sizes = _pow2_sizes(tb, SEG_ALIGN)

    def copies():
        for g in range(MOE_GROUPS):
            src0 = boff_ref[MOE_GROUPS * b + g]
            dst0 = off_ref[MOE_GROUPS * b + g]

            def make(pos, sz, src0=src0, dst0=dst0):
                src = pl.ds(pl.multiple_of(src0 + pos, SEG_ALIGN), sz)
                dst = pl.ds(pl.multiple_of(dst0 + pos, SEG_ALIGN), sz)
                return (pltpu.make_async_copy(xstg.at[src, :], xs_hbm.at[dst, :], sem.at[0]),
                        pltpu.make_async_copy(gstg.at[src, :], gs_hbm.at[dst, :], sem.at[1]))

            yield from _strip_copies(cpad_ref[MOE_GROUPS * b + g], sizes, make)

    for bit, (cx, cg) in copies():
        @pl.when(bit != 0)
        def _():
            cx.start()
            cg.start()
    for bit, (cx, cg) in copies():
        @pl.when(bit != 0)
        def _():
            cx.wait()
            cg.wait()

    @pl.when(b == pl.num_programs(0) - 1)
    def _():
        xstg[...] = jnp.zeros(xstg.shape, BF16)
        gstg[...] = jnp.zeros(gstg.shape, F32)
        tail_sizes = _pow2_sizes(MOE_TILE // 2, SEG_ALIGN)

        def tails():
            for g in range(MOE_GROUPS):
                dst0 = tail_ref[g]

                def make(pos, sz, dst0=dst0):
                    dst = pl.ds(pl.multiple_of(dst0 + pos, SEG_ALIGN), sz)
                    return (pltpu.make_async_copy(xstg.at[pl.ds(0, sz), :], xs_hbm.at[dst, :], sem.at[0]),
                            pltpu.make_async_copy(gstg.at[pl.ds(0, sz), :], gs_hbm.at[dst, :], sem.at[1]))

                yield from _strip_copies(tail_ref[MOE_GROUPS + g], tail_sizes, make)

        for bit, (cx, cg) in tails():
            @pl.when(bit != 0)
            def _():
                cx.start()
                cg.start()
        for bit, (cx, cg) in tails():
            @pl.when(bit != 0)
            def _():
                cx.wait()
                cg.wait()

        def spare(i):
            dst = pl.ds(pl.multiple_of(i * MOE_TILE, MOE_TILE), MOE_TILE)
            return (pltpu.make_async_copy(xstg.at[pl.ds(0, MOE_TILE), :], xs_hbm.at[dst, :], sem.at[0]),
                    pltpu.make_async_copy(gstg.at[pl.ds(0, MOE_TILE), :], gs_hbm.at[dst, :], sem.at[1]))

        first_spare = tail_ref[2 * MOE_GROUPS]
        n_tiles = xs_hbm.shape[0] // MOE_TILE

        @pl.loop(first_spare, n_tiles)
        def _(i):
            cx, cg = spare(i)
            cx.start()
            cg.start()

        @pl.loop(first_spare, n_tiles)
        def _(i):
            cx, cg = spare(i)
            cx.wait()
            cg.wait()


def _dispatch(xn, gates, inforow, off, cpad, boff, tail, n_sorted, tb, ks):
    n, d = xn.shape
    nb = n // tb
    grid_spec = pltpu.PrefetchScalarGridSpec(
        num_scalar_prefetch=4,
        grid=(nb,),
        in_specs=[pl.BlockSpec((tb, d), lambda i, *_: (i, 0)),
                  pl.BlockSpec((tb, LANES), lambda i, *_: (i, 0)),
                  pl.BlockSpec((None, 8, tb), lambda i, *_: (i, 0, 0))],
        out_specs=[pl.BlockSpec(memory_space=pl.ANY), pl.BlockSpec(memory_space=pl.ANY)],
        scratch_shapes=[pltpu.VMEM((ks, d), BF16), pltpu.VMEM((ks, LANES), F32),
                        pltpu.SemaphoreType.DMA((2,))])
    return pl.pallas_call(
        _dispatch_kernel,
        grid_spec=grid_spec,
        out_shape=[jax.ShapeDtypeStruct((n_sorted, d), BF16),
                   jax.ShapeDtypeStruct((n_sorted, LANES), F32)],
        compiler_params=_params("arbitrary"),
        name="moe_dispatch",
    )(off, cpad, boff, tail, xn, gates, inforow)


def _group_mlp_kernel(tg_ref, nused_ref, x_ref, gs_ref, wg_ref, wu_ref, wd_ref, y_ref):
    i = pl.program_id(0)

    @pl.when(i < nused_ref[0])
    def _():
        x = x_ref[...]
        gs = gs_ref[...]
        lane = lax.broadcasted_iota(jnp.int32, gs.shape, 1)
        lane0 = tg_ref[i] * MOE_EXPERTS_PER_GROUP
        acc = jnp.zeros(y_ref.shape, F32)
        for e in range(MOE_EXPERTS_PER_GROUP):
            he = _silu(_dot(x, wg_ref[e])) * _dot(x, wu_ref[e])
            gate = jnp.sum(jnp.where(lane == lane0 + e, gs, 0.0), axis=-1, keepdims=True)
            acc = acc + _dot((he * gate).astype(BF16), wd_ref[e])
        y_ref[...] = acc.astype(y_ref.dtype)

    @pl.when(i >= nused_ref[0])
    def _():
        y_ref[...] = jnp.zeros(y_ref.shape, y_ref.dtype)


def _group_mlp(xs, gs, wg, wu, wd, tile_group, nused):
    n_sorted, d = xs.shape
    ne = MOE_EXPERTS_PER_GROUP

    def rows(i, tg, nu):
        return (jnp.minimum(i, nu[0] - 1), 0)

    def group(i, tg, nu):
        return (tg[i], 0, 0)

    grid_spec = pltpu.PrefetchScalarGridSpec(
        num_scalar_prefetch=2,
        grid=(n_sorted // MOE_TILE,),
        in_specs=[pl.BlockSpec((MOE_TILE, d), rows),
                  pl.BlockSpec((MOE_TILE, LANES), rows),
                  pl.BlockSpec((ne, d, MOE_D_FF), group),
                  pl.BlockSpec((ne, d, MOE_D_FF), group),
                  pl.BlockSpec((ne, MOE_D_FF, d), group)],
        out_specs=pl.BlockSpec((MOE_TILE, d), lambda i, tg, nu: (i, 0)))
    return pl.pallas_call(
        _group_mlp_kernel,
        grid_spec=grid_spec,
        out_shape=jax.ShapeDtypeStruct((n_sorted, d), BF16),
        compiler_params=_params("arbitrary"),
        name="moe_experts",
    )(tile_group, nused, xs, gs, wg, wu, wd)


def _combine_kernel(off_ref, cpad_ref, boff_ref, info_ref, res_ref, fg_ref, y_hbm, o_ref, ybuf, sem,
                    *, final_norm):
    b = pl.program_id(0)
    tb = res_ref.shape[0]
    ks = ybuf.shape[0]

    @pl.when(b == 0)
    def _():
        ybuf[...] = jnp.zeros(ybuf.shape, ybuf.dtype)

    sizes = _pow2_sizes(tb, STRIP)

    def copies():
        for g in range(MOE_GROUPS):
            src0 = off_ref[MOE_GROUPS * b + g]
            dst0 = boff_ref[MOE_GROUPS * b + g]

            def make(pos, sz, src0=src0, dst0=dst0):
                src = pl.ds(pl.multiple_of(src0 + pos, SEG_ALIGN), sz)
                dst = pl.ds(pl.multiple_of(dst0 + pos, STRIP), sz)
                return pltpu.make_async_copy(y_hbm.at[src, :], ybuf.at[dst, :], sem.at[0])

            n_rows = (cpad_ref[MOE_GROUPS * b + g] + (STRIP - 1)) & (-STRIP)
            yield from _strip_copies(n_rows, sizes, make)

    for bit, c in copies():
        @pl.when(bit != 0)
        def _():
            c.start()
    for bit, c in copies():
        @pl.when(bit != 0)
        def _():
            c.wait()

    key = _slot_key(info_ref[:, 0:1], info_ref[:, 1:2], boff_ref, b)
    slot = lax.broadcasted_iota(jnp.int32, (tb, ks), 1)
    perm = jnp.where(slot == key, 1.0, 0.0).astype(BF16)
    y = res_ref[...] + _dot(perm, ybuf[...])
    if final_norm:
        ms = jnp.mean(y * y, axis=-1, keepdims=True)
        y = y * lax.rsqrt(ms + NORM_EPS) * fg_ref[...]
    o_ref[...] = y


def _combine(ys, info, res, final_g, off, cpad, boff, tb, ks, final_norm):
    n, d = res.shape
    grid_spec = pltpu.PrefetchScalarGridSpec(
        num_scalar_prefetch=3,
        grid=(n // tb,),
        in_specs=[pl.BlockSpec((tb, LANES), lambda i, *_: (i, 0)),
                  pl.BlockSpec((tb, d), lambda i, *_: (i, 0)),
                  pl.BlockSpec((1, d), lambda i, *_: (0, 0)),
                  pl.BlockSpec(memory_space=pl.ANY)],
        out_specs=pl.BlockSpec((tb, d), lambda i, *_: (i, 0)),
        scratch_shapes=[pltpu.VMEM((ks, d), BF16), pltpu.SemaphoreType.DMA((1,))])
    return pl.pallas_call(
        functools.partial(_combine_kernel, final_norm=final_norm),
        grid_spec=grid_spec,
        out_shape=jax.ShapeDtypeStruct((n, d), F32),
        compiler_params=_params("arbitrary"),
        name="moe_combine",
    )(off, cpad, boff, info, res, final_g.reshape(1, d), ys)


def _round_up(x, m):
    return (x + m - 1) // m * m


def _moe_layer(h, norm_g, w_group, b_group, w_expert, b_expert, w_gate, w_up, w_down,
               final_g, tb, final_norm):
    n, d = h.shape
    nb = n // tb
    pad = LANES - N_EXPERTS - MOE_GROUPS
    wr = jnp.concatenate([w_expert, w_group, jnp.zeros((d, pad), F32)], axis=1)
    whi = wr.astype(BF16)
    wlo = (wr - whi.astype(F32)).astype(BF16)
    bias = jnp.concatenate([b_expert, b_group, jnp.zeros((pad,), F32)]).reshape(1, LANES)
    xn, gates, info, inforow, counts = _router(h, norm_g, whi, wlo, bias, tb)

    cnt = counts[:, 0, :MOE_GROUPS]
    cpad = _round_up(cnt, SEG_ALIGN)
    strip = _round_up(cnt, STRIP)
    used = jnp.sum(cpad, axis=0)
    region = _round_up(used, MOE_TILE)
    region_start = jnp.cumsum(region) - region
    off = region_start[None, :] + jnp.cumsum(cpad, axis=0) - cpad
    boff = jnp.cumsum(strip, axis=1) - strip
    nused = (jnp.sum(region) // MOE_TILE).reshape(1)
    tail = jnp.concatenate([region_start + used, region - used, nused]).astype(jnp.int32)
    n_tiles = -(-(n + nb * MOE_GROUPS * (SEG_ALIGN - 1) + MOE_GROUPS * (MOE_TILE - 1)) // MOE_TILE) + 1
    tile_start = jnp.arange(n_tiles, dtype=jnp.int32) * MOE_TILE
    region_end = region_start + region
    tile_group = jnp.minimum(jnp.sum(tile_start[:, None] >= region_end[None, :], axis=1), MOE_GROUPS - 1)
    ks =tb + MOE_GROUPS * STRIP

    flat = lambda a: a.reshape(-1).astype(jnp.int32)
    xs, gs = _dispatch(xn, gates, inforow, flat(off), flat(cpad), flat(boff), tail,
                       n_tiles * MOE_TILE, tb, ks)
    ys = _group_mlp(xs, gs, w_gate.astype(BF16), w_up.astype(BF16), w_down.astype(BF16),
                    tile_group.astype(jnp.int32), nused.astype(jnp.int32))
    return _combine(ys, info, h, final_g, flat(off), flat(cpad), flat(boff), tb, ks, final_norm)


ROPE_ROW_TILE = 688


def _rope_kernel(q_ref, k_ref, cos_ref, sin_ref, qn_ref, kn_ref, qo_ref, ko_ref):
    t = q_ref.shape[0]
    rt = ROPE_ROW_TILE
    even = (lax.broadcasted_iota(jnp.int32, (rt, LANES), 1) & 1) == 0
    gi = lax.broadcasted_iota(jnp.int32, (LANES, LANES), 0) >> 6
    gj = lax.broadcasted_iota(jnp.int32, (LANES, LANES), 1) >> 6
    gmat = jnp.where(gi == gj, 1.0, 0.0).astype(BF16)

    def norm_rope(x, gain, cos, sin):
        sq = x * x
        hi = sq.astype(BF16)
        lo = (sq - hi.astype(F32)).astype(BF16)
        ms = (_dot(hi, gmat) + _dot(lo, gmat)) * (1.0 / ATT_HEAD_DIM)
        xn = x * lax.rsqrt(ms + NORM_EPS) * gain
        partner = jnp.where(even, pltpu.roll(xn, LANES - 1, 1), pltpu.roll(xn, 1, 1))
        return xn * cos + partner * sin

    for i in range(t // rt):
        lo_row = i * rt
        cos = cos_ref[lo_row:lo_row + rt, :]
        sin = sin_ref[lo_row:lo_row + rt, :]
        skip = N_META if i == 0 else 0
        for j in range(q_ref.shape[1] // LANES):
            cols = slice(j * LANES, (j + 1) * LANES)
            y = norm_rope(q_ref[lo_row:lo_row + rt, cols].astype(F32), qn_ref[...], cos, sin)
            y = y * (ATT_HEAD_DIM ** -0.5)
            qo_ref[lo_row + skip - N_META:lo_row + rt - N_META, cols] = y[skip:].astype(qo_ref.dtype)
        for j in range(k_ref.shape[1] // LANES):
            cols = slice(j * LANES, (j + 1) * LANES)
            y = norm_rope(k_ref[lo_row:lo_row + rt, cols].astype(F32), kn_ref[...], cos, sin)
            ko_ref[lo_row:lo_row + rt, cols] = y.astype(ko_ref.dtype)


def _rope(u3, cos_e, sin_e, qn, kn):
    b, t, _ = u3.shape
    nq = ATT_HEADS * ATT_HEAD_DIM
    nkv = ATT_KV_HEADS * ATT_HEAD_DIM
    full = lambda shape: pl.BlockSpec(shape, lambda i: (0, 0))
    return pl.pallas_call(
        _rope_kernel,
        grid=(b,),
        in_specs=[pl.BlockSpec((None, t, nq), lambda i: (i, 0, 0)),
                  pl.BlockSpec((None, t, nkv), lambda i: (i, 0, nq // nkv)),
                  full((t, LANES)), full((t, LANES)), full((1, LANES)), full((1, LANES))],
        out_specs=[pl.BlockSpec((None, t - N_META, nq), lambda i: (i, 0, 0)),
                   pl.BlockSpec((None, t, nkv), lambda i: (i, 0, 0))],
        out_shape=[jax.ShapeDtypeStruct((b, t - N_META, nq), BF16),
                   jax.ShapeDtypeStruct((b, t, nkv), BF16)],
        compiler_params=_params("parallel"),
        name="qk_norm_rope",
    )(u3, u3, cos_e, sin_e, qn, kn)


ATT_Q_TILE = 256


def _attn_kernel(q_ref, k_ref, v_ref, o_ref, kg_ref, vg_ref):
    s_len, w = q_ref.shape
    hk = pl.program_id(1)
    row = lax.broadcasted_iota(jnp.int32, (w, w), 0)
    col = lax.broadcasted_iota(jnp.int32, (w, w), 1)
    for g in range(ATT_GROUP):
        src = jnp.where((col >> 6) == g, hk * ATT_HEAD_DIM + (col & (ATT_HEAD_DIM - 1)), -1)
        sel = jnp.where(row == src, 1.0, 0.0).astype(BF16)
        kg_ref[g] = _dot(k_ref[...], sel).astype(BF16)
        vg_ref[g] = _dot(v_ref[...], sel).astype(BF16)

    def q_tile(i, carry):
        rows = pl.ds(pl.multiple_of(i * ATT_Q_TILE, ATT_Q_TILE), ATT_Q_TILE)
        q = q_ref[rows, :]
        acc = jnp.zeros((ATT_Q_TILE, w), F32)
        for g in range(ATT_GROUP):
            s = lax.dot_general(q, kg_ref[g], _NT, preferred_element_type=F32)
            p = jnp.exp(s - jnp.max(s, axis=-1, keepdims=True))
            inv = 1.0 / jnp.sum(p, axis=-1, keepdims=True)
            acc = acc + _dot(p.astype(BF16), vg_ref[g]) * inv
        o_ref[rows, :] = acc.astype(o_ref.dtype)
        return carry

    lax.fori_loop(0, s_len // ATT_Q_TILE, q_tile, 0)


def _attention(q, k, u3):
    b, s_len, nq = q.shape
    t = k.shape[1]
    nkv = ATT_KV_HEADS * ATT_HEAD_DIM
    w = ATT_GROUP * ATT_HEAD_DIM
    v_blk = (nq + nkv) // nkv
    return pl.pallas_call(
        _attn_kernel,
        grid=(b, ATT_KV_HEADS),
        in_specs=[pl.BlockSpec((None, s_len, w), lambda i, h: (i, 0, h)),
                  pl.BlockSpec((None, t, nkv), lambda i, h: (i, 0, 0)),
                  pl.BlockSpec((None, t, nkv), lambda i, h: (i, 0, v_blk))],
        out_specs=pl.BlockSpec((None, s_len, w), lambda i, h: (i, 0, h)),
        out_shape=jax.ShapeDtypeStruct((b, s_len, nq), BF16),
        scratch_shapes=[pltpu.VMEM((ATT_GROUP, t, w), BF16), pltpu.VMEM((ATT_GROUP, t, w), BF16)],
        compiler_params=_params("parallel", "parallel"),
        name="gqa_attention",
    )(q, k, u3)


def _att_out_kernel(o_ref, w_ref, h_ref, out_ref):
    out_ref[...] = h_ref[N_META:, :] + _dot(o_ref[...], w_ref[...])


def _att_out(o, w, h3, tn):
    b, s_len, nq = o.shape
    t, d = h3.shape[1], h3.shape[2]
    return pl.pallas_call(
        _att_out_kernel,
        grid=(b, d // tn),
        in_specs=[pl.BlockSpec((None, s_len, nq), lambda i, j: (i, 0, 0)),
                  pl.BlockSpec((nq, tn), lambda i, j: (0, j)),
                  pl.BlockSpec((None, t, tn), lambda i, j: (i, 0, j))],
        out_specs=pl.BlockSpec((None, s_len, tn), lambda i, j: (i, 0, j)),
        out_shape=jax.ShapeDtypeStruct((b, s_len, d), F32),
        compiler_params=_params("parallel", "parallel"),
        name="att_out_res",
    )(o, w, h3)


def _rope_tables(n_tokens):
    rows_n = n_tokens // GRID_W
    rows = jnp.repeat(jnp.arange(rows_n), GRID_W).astype(F32)
    cols = jnp.tile(jnp.arange(GRID_W), rows_n).astype(F32)
    axis_dims = ATT_HEAD_DIM // 2
    freqs = ROPE_THETA ** (-jnp.arange(0, axis_dims, 2, dtype=F32) / axis_dims)
    ang = jnp.concatenate([rows[:, None] * freqs, cols[:, None] * freqs], axis=-1)
    ang = jnp.concatenate([jnp.zeros((N_META, ang.shape[1]), F32), ang], axis=0)
    cos_e = jnp.tile(jnp.repeat(jnp.cos(ang), 2, axis=1), (1, LANES // ATT_HEAD_DIM))
    sign = jnp.tile(jnp.array([-1.0, 1.0], F32), LANES // 2)
    sin_e = jnp.tile(jnp.repeat(jnp.sin(ang), 2, axis=1), (1, LANES // ATT_HEAD_DIM)) * sign
    return cos_e, sin_e


def kernel(x, meta_tokens, norm_mix, norm_ffn, norm_final, sc_w_in, sc_conv_w, gla_w_gate_fwd, gla_b_gate_fwd, gla_w_gate_bwd, gla_b_gate_bwd, gla_norm_w, sc_w_out, att_w_in, att_q_norm, att_k_norm, att_w_out, moe_w_group, moe_b_group, moe_w_expert, moe_b_expert, moe_w_gate, moe_w_up, moe_w_down):
    bsz, s_len, d = x.shape
    t = s_len + N_META
    n0, n1 = bsz * t, bsz * s_len
    tm0 = t // 3
    tm1 = 512

    meta = jnp.broadcast_to(meta_tokens.astype(x.dtype)[None], (bsz, N_META, d))
    h = jnp.concatenate([meta, x], axis=1).reshape(n0, d)

    nk = GLA_HEADS * GLA_DK
    mix_in = sc_w_in.shape[2]
    mix_pad = -mix_in % LANES
    w_in = jnp.pad(sc_w_in[0], ((0, 0), (0, mix_pad))).astype(BF16)
    u = _norm_matmul(h, norm_mix[0], w_in, tm0, 5 * LANES)
    u3 = u.reshape(bsz, t, mix_in + mix_pad)
    y_a = _gated_conv(u3, sc_conv_w[0])
    wc = jnp.zeros((LANES, 2 * nk), F32)
    wc = wc.at[:GLA_GATE_RANK, :nk].set(gla_w_gate_fwd[0])
    wc = wc.at[GLA_GATE_RANK:2 * GLA_GATE_RANK, nk:].set(gla_w_gate_bwd[0])
    bc = jnp.concatenate([gla_b_gate_fwd[0], gla_b_gate_bwd[0]]).reshape(1, 2 * nk)
    y_b = _gla(u3, wc.astype(BF16), bc, gla_norm_w[0].reshape(1, GLA_DV))
    w_out = sc_w_out[0].astype(BF16)
    h = _proj2_res(y_a.reshape(n0, SC_WIDTH), y_b.reshape(n0, -1), w_out[:SC_WIDTH], w_out[SC_WIDTH:], h, tm0)
    h = _moe_layer(h, norm_ffn[0], moe_w_group[0], moe_b_group[0], moe_w_expert[0], moe_b_expert[0],
                   moe_w_gate[0], moe_w_up[0], moe_w_down[0], norm_final, 3 * LANES, False)

    u = _norm_matmul(h, norm_mix[1], att_w_in[0].astype(BF16), tm0, 4 * LANES)
    u3 = u.reshape(bsz, t, -1)
    cos_e, sin_e = _rope_tables(s_len)
    qn = jnp.tile(att_q_norm[0], LANES // ATT_HEAD_DIM).reshape(1, LANES)
    kn = jnp.tile(att_k_norm[0], LANES // ATT_HEAD_DIM).reshape(1, LANES)
    q, k = _rope(u3, cos_e, sin_e, qn, kn)
    o = _attention(q, k, u3)
    h = _att_out(o, att_w_out[0].astype(BF16), h.reshape(bsz, t, d), 4 * LANES).reshape(n1, d)
    h = _moe_layer(h, norm_ffn[1], moe_w_group[1], moe_b_group[1], moe_w_expert[1], moe_b_expert[1],
                   moe_w_gate[1], moe_w_up[1], moe_w_down[1], norm_final, tm1, True)
    return h.reshape(bsz, s_len, d)
```

```python
import functools

import jax
import jax.numpy as jnp
from jax import lax
from jax.experimental import pallas as pl
from jax.experimental.pallas import tpu as pltpu

F32 = jnp.float32
BF16 = jnp.bfloat16

NORM_EPS = 1e-6
N_META = 16
GRID_W = 64
ROPE_THETA = 10000.0
SC_WIDTH = 512
GLA_HEADS = 4
GLA_DK = 64
GLA_DV = 128
GLA_GATE_RANK = 16
GLA_GATE_TAU = 16.0
GLA_CHUNK = 64
ATT_HEAD_DIM = 64
ATT_HEADS = 16
ATT_KV_HEADS = 4
ATT_GROUP = ATT_HEADS // ATT_KV_HEADS
MOE_GROUPS = 4
MOE_EXPERTS_PER_GROUP = 8
N_EXPERTS = 32
MOE_D_FF = 256

LANES = 128
MIB = 2 ** 20
VMEM_LIMIT = 56 * MIB

_NT = (((1,), (1,)), ((), ()))
_TN = (((0,), (0,)), ((), ()))


def _params(*semantics):
    return pltpu.CompilerParams(dimension_semantics=semantics, vmem_limit_bytes=VMEM_LIMIT)


def _dot(a, b):
    return jnp.dot(a, b, preferred_element_type=F32)


def _silu(x):
    return x * (1.0 / (1.0 + jnp.exp(-x)))


def _norm_matmul_kernel(h_ref, g_ref, w_ref, o_ref, xn_ref):
    @pl.when(pl.program_id(1) == 0)
    def _():
        x = h_ref[...]
        ms = jnp.mean(x * x, axis=-1, keepdims=True)
        xn_ref[...] = (x * lax.rsqrt(ms + NORM_EPS) * g_ref[...]).astype(BF16)

    o_ref[...] = _dot(xn_ref[...], w_ref[...]).astype(o_ref.dtype)


def _norm_matmul(h, g, w, tm, tn):
    n, d = h.shape
    nout = w.shape[1]
    return pl.pallas_call(
        _norm_matmul_kernel,
        grid=(n // tm, nout // tn),
        in_specs=[pl.BlockSpec((tm, d), lambda i, j: (i, 0)),
                  pl.BlockSpec((1, d), lambda i, j: (0, 0)),
                  pl.BlockSpec((d, tn), lambda i, j: (0, j))],
        out_specs=pl.BlockSpec((tm, tn), lambda i, j: (i, j)),
        out_shape=jax.ShapeDtypeStruct((n, nout), BF16),
        scratch_shapes=[pltpu.VMEM((tm, d), BF16)],
        compiler_params=_params("parallel", "arbitrary"),
        name="norm_matmul",
    )(h, g.reshape(1, d), w)


def _conv_kernel(ab_ref, ac_ref, ax_ref, w_ref, o_ref, z_ref):
    t, c = ab_ref.shape
    z = ac_ref[...].astype(F32) * ax_ref[...].astype(F32)
    z_ref[0:8, :] = jnp.zeros((8, c), F32)
    z_ref[8:8 + t, :] = z
    z_ref[8 + t:16 + t, :] = jnp.zeros((8, c), F32)
    w = w_ref[...]
    conv = w[0:1, :] * z_ref[7:7 + t, :] + w[1:2, :] * z + w[2:3, :] * z_ref[9:9 + t, :]
    o_ref[...] = (ab_ref[...].astype(F32) * conv).astype(o_ref.dtype)


def _gated_conv(u3, conv_w):
    b, t, _ = u3.shape
    nblk = SC_WIDTH // LANES

    def col(off):
        return pl.BlockSpec((None, t, LANES), lambda i, j: (i, 0, off + j))

    return pl.pallas_call(
        _conv_kernel,
        grid=(b, nblk),
        in_specs=[col(0), col(nblk), col(2 * nblk),
                  pl.BlockSpec((3, LANES), lambda i, j: (0, j))],
        out_specs=pl.BlockSpec((None, t, LANES), lambda i, j: (i, 0, j)),
        out_shape=jax.ShapeDtypeStruct((b, t, SC_WIDTH), BF16),
        scratch_shapes=[pltpu.VMEM((t + 16, LANES), F32)],
        compiler_params=_params("parallel", "parallel"),
        name="gated_conv",
    )(u3, u3, u3, conv_w)


GLA_ROW_TILE = 688
GLA_EPI_TILE = 48


def _gla_kernel(q_ref, k_ref, v_ref, r_ref, g_ref, wc_ref, bc_ref, nw_ref, y_ref,
                la_ref, of_ref, ob_ref, sf_ref, sb_ref):
    t, nk = q_ref.shape
    nv = v_ref.shape[1]
    c = GLA_CHUNK
    n_real = (t - N_META) // c

    for i in range(t // GLA_ROW_TILE):
        rows = pl.ds(i * GLA_ROW_TILE, GLA_ROW_TILE)
        pre = _dot(g_ref[rows, :], wc_ref[...]) + bc_ref[...]
        la_ref[rows, :] = (jnp.minimum(pre, 0.0) - jnp.log1p(jnp.exp(-jnp.abs(pre)))) * (1.0 / GLA_GATE_TAU)

    def iota(shape, dim):
        return lax.broadcasted_iota(jnp.int32, shape, dim)

    tri_f = jnp.where(iota((c, c), 0) >= iota((c, c), 1), 1.0, 0.0).astype(BF16)
    tri_b = jnp.where(iota((c, c), 0) <= iota((c, c), 1), 1.0, 0.0).astype(BF16)
    keep_f = (iota((c, nk), 1) & (c - 1)) <= iota((c, nk), 0)
    keep_b = (iota((c, nk), 1) & (c - 1)) > iota((c, nk), 0)
    bd_k = (iota((nk, nk), 0) >> 6) == (iota((nk, nk), 1) >> 6)
    bd_v = (iota((nk, nv), 0) >> 6) == (iota((nk, nv), 1) >> 7)
    bd_s = (iota((nv, nk), 0) >> 7) == (iota((nv, nk), 1) >> 6)
    valid0 = iota((c, 1), 0) < N_META

    def chunk(rows, valid, tri, keep, tot_row, la_col, s_ref, o_ref, out_rows, n_out):
        q = q_ref[rows, :].astype(F32) * (GLA_DK ** -0.5)
        k = k_ref[rows, :].astype(F32)
        v = v_ref[rows, :].astype(F32)
        la = la_ref[rows, la_col:la_col + nk]
        if valid is not None:
            q = jnp.where(valid, q, 0.0)
            k = jnp.where(valid, k, 0.0)
            v = jnp.where(valid, v, 0.0)
            la = jnp.where(valid, la, 0.0)
        hi = la.astype(BF16)
        rem = la - hi.astype(F32)
        mid = rem.astype(BF16)
        lo = (rem - mid.astype(F32)).astype(BF16)
        cum = _dot(tri, hi) + _dot(tri, mid) + _dot(tri, lo)
        tot = cum[tot_row:tot_row + 1, :]
        q_dec = (q * jnp.exp(cum)).astype(BF16)
        k_inv = k * jnp.exp(-cum)
        k_end = (k * jnp.exp(tot - cum)).astype(BF16)
        kbd = jnp.where(bd_k, jnp.concatenate([k_inv] * GLA_HEADS, axis=0), 0.0).astype(BF16)
        p = lax.dot_general(q_dec, kbd, _NT, preferred_element_type=F32)
        p = jnp.where(keep, p, 0.0).astype(BF16)
        vbd = jnp.where(bd_v, jnp.concatenate([v] * GLA_HEADS, axis=0), 0.0).astype(BF16)
        s_old = s_ref[...]
        o = _dot(p, vbd) + lax.dot_general(q_dec, s_old.astype(BF16), _NT, preferred_element_type=F32)
        kvt = lax.dot_general(v.astype(BF16), k_end, _TN, preferred_element_type=F32)
        s_ref[...] = jnp.exp(tot) * s_old + jnp.where(bd_s, kvt, 0.0)
        o_ref[out_rows, :] = o[:n_out]

    sf_ref[...] = jnp.zeros(sf_ref.shape, F32)
    sb_ref[...] = jnp.zeros(sb_ref.shape, F32)
    meta_rows = pl.ds(0, c)
    chunk(meta_rows, valid0, tri_f, keep_f, c - 1, 0, sf_ref, of_ref, pl.ds(0, N_META), N_META)

    def body(i, carry):
        rf = pl.multiple_of(N_META + i * c, 16)
        chunk(pl.ds(rf, c), None, tri_f, keep_f, c - 1, 0, sf_ref, of_ref, pl.ds(rf, c), c)
        rb = pl.multiple_of(N_META + (n_real - 1 - i) * c, 16)
        chunk(pl.ds(rb, c), None, tri_b, keep_b, 0, nk, sb_ref, ob_ref, pl.ds(rb, c), c)
        return carry

    lax.fori_loop(0, n_real, body, 0)
    chunk(meta_rows, valid0, tri_b, keep_b, 0, nk, sb_ref, ob_ref, pl.ds(0, N_META), N_META)

    def epilogue(i, carry):
        rows = pl.ds(pl.multiple_of(i * GLA_EPI_TILE, 16), GLA_EPI_TILE)
        o = of_ref[rows, :] + ob_ref[rows, :]
        r = r_ref[rows, :].astype(F32)
        for h in range(GLA_HEADS):
            cols = slice(h * GLA_DV, (h + 1) * GLA_DV)
            oh = o[:, cols]
            ms = jnp.mean(oh * oh, axis=-1, keepdims=True)
            yh = oh * lax.rsqrt(ms + NORM_EPS) * nw_ref[...] * _silu(r[:, cols])
            y_ref[rows, cols] = yh.astype(y_ref.dtype)
        return carry

    lax.fori_loop(0, t // GLA_EPI_TILE, epilogue, 0)


def _gla(u3, wc, bc, norm_w):
    b, t, _ = u3.shape
    nk = GLA_HEADS * GLA_DK
    nv = GLA_HEADS * GLA_DV
    q_blk = 3 * SC_WIDTH // nk
    v_blk = (3 * SC_WIDTH + 2 * nk) // nv
    g_blk = (3 * SC_WIDTH + 2 * nk + 2 * nv) // LANES
    full = lambda shape: pl.BlockSpec(shape, lambda i: (0, 0))
    return pl.pallas_call(
        _gla_kernel,
        grid=(b,),
        in_specs=[pl.BlockSpec((None, t, nk), lambda i: (i, 0, q_blk)),
                  pl.BlockSpec((None, t, nk), lambda i: (i, 0, q_blk + 1)),
                  pl.BlockSpec((None, t, nv), lambda i: (i, 0, v_blk)),
                  pl.BlockSpec((None, t, nv), lambda i: (i, 0, v_blk + 1)),
                  pl.BlockSpec((None, t, LANES), lambda i: (i, 0, g_blk)),
                  full((LANES, 2 * nk)), full((1, 2 * nk)), full((1, GLA_DV))],
        out_specs=pl.BlockSpec((None, t, nv), lambda i: (i, 0, 0)),
        out_shape=jax.ShapeDtypeStruct((b, t, nv), BF16),
        scratch_shapes=[pltpu.VMEM((t, 2 * nk), F32),
                        pltpu.VMEM((t, nv), F32), pltpu.VMEM((t, nv), F32),
                        pltpu.VMEM((nv, nk), F32), pltpu.VMEM((nv, nk), F32)],
        compiler_params=_params("parallel"),
        name="gla",
    )(u3, u3, u3, u3, u3, wc, bc, norm_w)


def _proj2_res_kernel(a_ref, b_ref, wa_ref, wb_ref, res_ref, o_ref):
    o_ref[...] = res_ref[...] + _dot(a_ref[...], wa_ref[...]) + _dot(b_ref[...], wb_ref[...])


def _proj2_res(a, b, wa, wb, res, tm):
    n, d = res.shape
    ka, kb = a.shape[1], b.shape[1]
    return pl.pallas_call(
        _proj2_res_kernel,
        grid=(n // tm,),
        in_specs=[pl.BlockSpec((tm, ka), lambda i: (i, 0)),
                  pl.BlockSpec((tm, kb), lambda i: (i, 0)),
                  pl.BlockSpec((ka, d), lambda i: (0, 0)),
                  pl.BlockSpec((kb, d), lambda i: (0, 0)),
                  pl.BlockSpec((tm, d), lambda i: (i, 0))],
        out_specs=pl.BlockSpec((tm, d), lambda i: (i, 0)),
        out_shape=jax.ShapeDtypeStruct((n, d), F32),
        compiler_params=_params("parallel"),
        name="proj2_res",
    )(a, b, wa, wb, res)


GROUP_LANE0 = N_EXPERTS


def _router_kernel(h_ref, g_ref, whi_ref, wlo_ref, b_ref, xn_ref, gates_ref, info_ref, inforow_ref, count_ref):
    x = h_ref[...]
    ms = jnp.mean(x * x, axis=-1, keepdims=True)
    xn = x * lax.rsqrt(ms + NORM_EPS) * g_ref[...]
    xhi = xn.astype(BF16)
    xn_ref[...] = xhi
    xlo = (xn - xhi.astype(F32)).astype(BF16)
    logits = _dot(xhi, whi_ref[...]) + _dot(xhi, wlo_ref[...]) + _dot(xlo, whi_ref[...]) + b_ref[...]

    lane = lax.broadcasted_iota(jnp.int32, logits.shape, 1)
    neg = -jnp.inf
    big = jnp.int32(LANES)
    is_group = (lane >> 2) == (GROUP_LANE0 >> 2)
    gl = jnp.where(is_group, logits, neg)
    gmax = jnp.max(gl, axis=-1, keepdims=True)
    g_p = 1.0 / jnp.sum(jnp.exp(gl - gmax), axis=-1, keepdims=True)
    g_idx = jnp.min(jnp.where(gl == gmax, lane, big), axis=-1, keepdims=True) - GROUP_LANE0
    el = jnp.where((lane >> 3) == g_idx, logits, neg)
    m1 = jnp.max(el, axis=-1, keepdims=True)
    i1 = jnp.min(jnp.where(el == m1, lane, big), axis=-1, keepdims=True)
    el2 = jnp.where(lane == i1, neg, el)
    m2 = jnp.max(el2, axis=-1, keepdims=True)
    i2 = jnp.min(jnp.where(el2 == m2, lane, big), axis=-1, keepdims=True)
    e2 = jnp.exp(m2 - m1)
    w1 = g_p / (1.0 + e2)
    w2 = g_p * e2 / (1.0 + e2)
    gates_ref[...] = jnp.where(lane == i1, w1, 0.0) + jnp.where(lane == i2, w2, 0.0)

    tm = x.shape[0]
    onehot = jnp.where(lane == g_idx, 1.0, 0.0)
    r_i = lax.broadcasted_iota(jnp.int32, (tm, tm), 0)
    c_i = lax.broadcasted_iota(jnp.int32, (tm, tm), 1)
    before = jnp.where(r_i > c_i, 1.0, 0.0).astype(BF16)
    prefix = _dot(before, onehot.astype(BF16))
    rank = jnp.sum(jnp.where(lane == g_idx, prefix, 0.0), axis=-1, keepdims=True).astype(jnp.int32)
    info_ref[...] = jnp.where(lane == 0, g_idx, jnp.where(lane == 1, rank, 0))
    eye = r_i == c_i
    g_row = jnp.sum(jnp.where(eye, g_idx, 0), axis=0, keepdims=True)
    rank_row = jnp.sum(jnp.where(eye, rank, 0), axis=0, keepdims=True)
    sub = lax.broadcasted_iota(jnp.int32, inforow_ref.shape, 0)
    inforow_ref[...] = jnp.where(sub == 0, g_row, jnp.where(sub == 1, rank_row, 0))
    counts = jnp.sum(onehot, axis=0, keepdims=True).astype(jnp.int32)
    count_ref[...] = jnp.broadcast_to(counts, count_ref.shape)


def _router(h, g, whi, wlo, bias, tm):
    n, d = h.shape
    nb = n // tm
    return pl.pallas_call(
        _router_kernel,
        grid=(nb,),
        in_specs=[pl.BlockSpec((tm, d), lambda i: (i, 0)),
                  pl.BlockSpec((1, d), lambda i: (0, 0)),
                  pl.BlockSpec((d, LANES), lambda i: (0, 0)),
                  pl.BlockSpec((d, LANES), lambda i: (0, 0)),
                  pl.BlockSpec((1, LANES), lambda i: (0, 0))],
        out_specs=[pl.BlockSpec((tm, d), lambda i: (i, 0)),
                   pl.BlockSpec((tm, LANES), lambda i: (i, 0)),
                   pl.BlockSpec((tm, LANES), lambda i: (i, 0)),
                   pl.BlockSpec((None, 8, tm), lambda i: (i, 0, 0)),
                   pl.BlockSpec((None, 8, LANES), lambda i: (i, 0, 0))],
        out_shape=[jax.ShapeDtypeStruct((n, d), BF16),
                   jax.ShapeDtypeStruct((n, LANES), F32),
                   jax.ShapeDtypeStruct((n, LANES), jnp.int32),
                   jax.ShapeDtypeStruct((nb, 8, tm), jnp.int32),
                   jax.ShapeDtypeStruct((nb, 8, LANES), jnp.int32)],
        compiler_params=_params("parallel"),
        name="moe_router",
    )(h, g.reshape(1, d), whi, wlo, bias)


MOE_TILE = 512
SEG_ALIGN = 16
STRIP = LANES


def _pow2_sizes(max_rows, min_rows):
    sizes, s = [], min_rows
    while s <= max_rows:
        sizes.append(s)
        s *= 2
    return sizes[::-1]


def _strip_copies(n_rows, sizes, make_copy):
    pos = 0
    for sz in sizes:
        bit = n_rows & sz
        yield bit, make_copy(pos, sz)
        pos = pos + bit


def _slot_key(group, rank, boff_ref, b):
    key = rank
    for g in range(MOE_GROUPS):
        key = key + jnp.where(group == g, boff_ref[MOE_GROUPS * b + g], 0)
    return key


def _dispatch_kernel(off_ref, cpad_ref, boff_ref, tail_ref, x_ref, gates_ref, inforow_ref,
                     xs_hbm, gs_hbm, xstg, gstg, sem):
    b = pl.program_id(0)
    tb = x_ref.shape[0]
    ks = xstg.shape[0]
    key = _slot_key(inforow_ref[0:1, :], inforow_ref[1:2, :], boff_ref, b)
    slot = lax.broadcasted_iota(jnp.int32, (ks, tb), 0)
    perm = jnp.where(slot == key, 1.0, 0.0).astype(BF16)
    xstg[...] = _dot(perm, x_ref[...]).astype(BF16)
    gt = gates_ref[...]
    hi = gt.astype(BF16)
    rem = gt - hi.astype(F32)
    mid = rem.astype(BF16)
    lo = (rem - mid.astype(F32)).astype(BF16)
    gstg[...] = _dot(perm, hi) + _dot(perm, mid) + _dot(perm, lo)

    sizes = _pow2_sizes(tb, SEG_ALIGN)

    def copies():
        for g in range(MOE_GROUPS):
            src0 = boff_ref[MOE_GROUPS * b + g]
            dst0 = off_ref[MOE_GROUPS * b + g]

            def make(pos, sz, src0=src0, dst0=dst0):
                src = pl.ds(pl.multiple_of(src0 + pos, SEG_ALIGN), sz)
                dst = pl.ds(pl.multiple_of(dst0 + pos, SEG_ALIGN), sz)
                return (pltpu.make_async_copy(xstg.at[src, :], xs_hbm.at[dst, :], sem.at[0]),
                        pltpu.make_async_copy(gstg.at[src, :], gs_hbm.at[dst, :], sem.at[1]))

            yield from _strip_copies(cpad_ref[MOE_GROUPS * b + g], sizes, make)

    for bit, (cx, cg) in copies():
        @pl.when(bit != 0)
        def _():
            cx.start()
            cg.start()
    for bit, (cx, cg) in copies():
        @pl.when(bit != 0)
        def _():
            cx.wait()
            cg.wait()

    @pl.when(b == pl.num_programs(0) - 1)
    def _():
        xstg[...] = jnp.zeros(xstg.shape, BF16)
        gstg[...] = jnp.zeros(gstg.shape, F32)
        tail_sizes = _pow2_sizes(MOE_TILE // 2, SEG_ALIGN)

        def tails():
            for g in range(MOE_GROUPS):
                dst0 = tail_ref[g]

                def make(pos, sz, dst0=dst0):
                    dst = pl.ds(pl.multiple_of(dst0 + pos, SEG_ALIGN), sz)
                    return (pltpu.make_async_copy(xstg.at[pl.ds(0, sz), :], xs_hbm.at[dst, :], sem.at[0]),
                            pltpu.make_async_copy(gstg.at[pl.ds(0, sz), :], gs_hbm.at[dst, :], sem.at[1]))

                yield from _strip_copies(tail_ref[MOE_GROUPS + g], tail_sizes, make)

        for bit, (cx, cg) in tails():
            @pl.when(bit != 0)
            def _():
                cx.start()
                cg.start()
        for bit, (cx, cg) in tails():
            @pl.when(bit != 0)
            def _():
                cx.wait()
                cg.wait()

        def spare(i):
            dst = pl.ds(pl.multiple_of(i * MOE_TILE, MOE_TILE), MOE_TILE)
            return (pltpu.make_async_copy(xstg.at[pl.ds(0, MOE_TILE), :], xs_hbm.at[dst, :], sem.at[0]),
                    pltpu.make_async_copy(gstg.at[pl.ds(0, MOE_TILE), :], gs_hbm.at[dst, :], sem.at[1]))

        first_spare = tail_ref[2 * MOE_GROUPS]
        n_tiles = xs_hbm.shape[0] // MOE_TILE

        @pl.loop(first_spare, n_tiles)
        def _(i):
            cx, cg = spare(i)
            cx.start()
            cg.start()

        @pl.loop(first_spare, n_tiles)
        def _(i):
            cx, cg = spare(i)
            cx.wait()
            cg.wait()


def _dispatch(xn, gates, inforow, off, cpad, boff, tail, n_sorted, tb, ks):
    n, d = xn.shape
    nb = n // tb
    grid_spec = pltpu.PrefetchScalarGridSpec(
        num_scalar_prefetch=4,
        grid=(nb,),
        in_specs=[pl.BlockSpec((tb, d), lambda i, *_: (i, 0)),
                  pl.BlockSpec((tb, LANES), lambda i, *_: (i, 0)),
                  pl.BlockSpec((None, 8, tb), lambda i, *_: (i, 0, 0))],
        out_specs=[pl.BlockSpec(memory_space=pl.ANY), pl.BlockSpec(memory_space=pl.ANY)],
        scratch_shapes=[pltpu.VMEM((ks, d), BF16), pltpu.VMEM((ks, LANES), F32),
                        pltpu.SemaphoreType.DMA((2,))])
    return pl.pallas_call(
        _dispatch_kernel,
        grid_spec=grid_spec,
        out_shape=[jax.ShapeDtypeStruct((n_sorted, d), BF16),
                   jax.ShapeDtypeStruct((n_sorted, LANES), F32)],
        compiler_params=_params("arbitrary"),
        name="moe_dispatch",
    )(off, cpad, boff, tail, xn, gates, inforow)


def _group_mlp_kernel(tg_ref, nused_ref, x_ref, gs_ref, wg_ref, wu_ref, wd_ref, y_ref):
    i = pl.program_id(0)

    @pl.when(i < nused_ref[0])
    def _():
        x = x_ref[...]
        gs = gs_ref[...]
        lane = lax.broadcasted_iota(jnp.int32, gs.shape, 1)
        lane0 = tg_ref[i] * MOE_EXPERTS_PER_GROUP
        acc = jnp.zeros(y_ref.shape, F32)
        for e in range(MOE_EXPERTS_PER_GROUP):
            he = _silu(_dot(x, wg_ref[e])) * _dot(x, wu_ref[e])
            gate = jnp.sum(jnp.where(lane == lane0 + e, gs, 0.0), axis=-1, keepdims=True)
            acc = acc + _dot((he * gate).astype(BF16), wd_ref[e])
        y_ref[...] = acc.astype(y_ref.dtype)

    @pl.when(i >= nused_ref[0])
    def _():
        y_ref[...] = jnp.zeros(y_ref.shape, y_ref.dtype)


def _group_mlp(xs, gs, wg, wu, wd, tile_group, nused):
    n_sorted, d = xs.shape
    ne = MOE_EXPERTS_PER_GROUP

    def rows(i, tg, nu):
        return (jnp.minimum(i, nu[0] - 1), 0)

    def group(i, tg, nu):
        return (tg[i], 0, 0)

    grid_spec = pltpu.PrefetchScalarGridSpec(
        num_scalar_prefetch=2,
        grid=(n_sorted // MOE_TILE,),
        in_specs=[pl.BlockSpec((MOE_TILE, d), rows),
                  pl.BlockSpec((MOE_TILE, LANES), rows),
                  pl.BlockSpec((ne, d, MOE_D_FF), group),
                  pl.BlockSpec((ne, d, MOE_D_FF), group),
                  pl.BlockSpec((ne, MOE_D_FF, d), group)],
        out_specs=pl.BlockSpec((MOE_TILE, d), lambda i, tg, nu: (i, 0)))
    return pl.pallas_call(
        _group_mlp_kernel,
        grid_spec=grid_spec,
        out_shape=jax.ShapeDtypeStruct((n_sorted, d), BF16),
        compiler_params=_params("arbitrary"),
        name="moe_experts",
    )(tile_group, nused, xs, gs, wg, wu, wd)


def _combine_kernel(off_ref, cpad_ref, boff_ref, info_ref, res_ref, fg_ref, y_hbm, o_ref, ybuf, sem,
                    *, final_norm):
    b = pl.program_id(0)
    tb = res_ref.shape[0]
    ks = ybuf.shape[0]

    @pl.when(b == 0)
    def _():
        ybuf[...] = jnp.zeros(ybuf.shape, ybuf.dtype)

    sizes = _pow2_sizes(tb, STRIP)

    def copies():
        for g in range(MOE_GROUPS):
            src0 = off_ref[MOE_GROUPS * b + g]
            dst0 = boff_ref[MOE_GROUPS * b + g]

            def make(pos, sz, src0=src0, dst0=dst0):
                src = pl.ds(pl.multiple_of(src0 + pos, SEG_ALIGN), sz)
                dst = pl.ds(pl.multiple_of(dst0 + pos, STRIP), sz)
                return pltpu.make_async_copy(y_hbm.at[src, :], ybuf.at[dst, :], sem.at[0])

            n_rows = (cpad_ref[MOE_GROUPS * b + g] + (STRIP - 1)) & (-STRIP)
            yield from _strip_copies(n_rows, sizes, make)

    for bit, c in copies():
        @pl.when(bit != 0)
        def _():
            c.start()
    for bit, c in copies():
        @pl.when(bit != 0)
        def _():
            c.wait()

    key = _slot_key(info_ref[:, 0:1], info_ref[:, 1:2], boff_ref, b)
    slot = lax.broadcasted_iota(jnp.int32, (tb, ks), 1)
    perm = jnp.where(slot == key, 1.0, 0.0).astype(BF16)
    y = res_ref[...] + _dot(perm, ybuf[...])
    if final_norm:
        ms = jnp.mean(y * y, axis=-1, keepdims=True)
        y = y * lax.rsqrt(ms + NORM_EPS) * fg_ref[...]
    o_ref[...] = y


def _combine(ys, info, res, final_g, off, cpad, boff, tb, ks, final_norm):
    n, d = res.shape
    grid_spec = pltpu.PrefetchScalarGridSpec(
        num_scalar_prefetch=3,
        grid=(n // tb,),
        in_specs=[pl.BlockSpec((tb, LANES), lambda i, *_: (i, 0)),
                  pl.BlockSpec((tb, d), lambda i, *_: (i, 0)),
                  pl.BlockSpec((1, d), lambda i, *_: (0, 0)),
                  pl.BlockSpec(memory_space=pl.ANY)],
        out_specs=pl.BlockSpec((tb, d), lambda i, *_: (i, 0)),
        scratch_shapes=[pltpu.VMEM((ks, d), BF16), pltpu.SemaphoreType.DMA((1,))])
    return pl.pallas_call(
        functools.partial(_combine_kernel, final_norm=final_norm),
        grid_spec=grid_spec,
        out_shape=jax.ShapeDtypeStruct((n, d), F32),
        compiler_params=_params("arbitrary"),
        name="moe_combine",
    )(off, cpad, boff, info, res, final_g.reshape(1, d), ys)


def _round_up(x, m):
    return (x + m - 1) // m * m


def _moe_layer(h, norm_g, w_group, b_group, w_expert, b_expert, w_gate, w_up, w_down,
               final_g, tb, final_norm):
    n, d = h.shape
    nb = n // tb
    pad = LANES - N_EXPERTS - MOE_GROUPS
    wr = jnp.concatenate([w_expert, w_group, jnp.zeros((d, pad), F32)], axis=1)
    whi = wr.astype(BF16)
    wlo = (wr - whi.astype(F32)).astype(BF16)
    bias = jnp.concatenate([b_expert, b_group, jnp.zeros((pad,), F32)]).reshape(1, LANES)
    xn, gates, info, inforow, counts = _router(h, norm_g, whi, wlo, bias, tb)

    cnt = counts[:, 0, :MOE_GROUPS]
    cpad = _round_up(cnt, SEG_ALIGN)
    strip = _round_up(cnt, STRIP)
    used = jnp.sum(cpad, axis=0)
    region = _round_up(used, MOE_TILE)
    region_start = jnp.cumsum(region) - region
    off = region_start[None, :] + jnp.cumsum(cpad, axis=0) - cpad
    boff = jnp.cumsum(strip, axis=1) - strip
    nused = (jnp.sum(region) // MOE_TILE).reshape(1)
    tail = jnp.concatenate([region_start + used, region - used, nused]).astype(jnp.int32)
    n_tiles = -(-(n + nb * MOE_GROUPS * (SEG_ALIGN - 1) + MOE_GROUPS * (MOE_TILE - 1)) // MOE_TILE) + 1
    tile_start = jnp.arange(n_tiles, dtype=jnp.int32) * MOE_TILE
    region_end = region_start + region
    tile_group = jnp.minimum(jnp.sum(tile_start[:, None] >= region_end[None, :], axis=1), MOE_GROUPS - 1)
    ks =tb + MOE_GROUPS * STRIP

    flat = lambda a: a.reshape(-1).astype(jnp.int32)
    xs, gs = _dispatch(xn, gates, inforow, flat(off), flat(cpad), flat(boff), tail,
                       n_tiles * MOE_TILE, tb, ks)
    ys = _group_mlp(xs, gs, w_gate.astype(BF16), w_up.astype(BF16), w_down.astype(BF16),
                    tile_group.astype(jnp.int32), nused.astype(jnp.int32))
    return _combine(ys, info, h, final_g, flat(off), flat(cpad), flat(boff), tb, ks, final_norm)


ROPE_ROW_TILE = 688
ATT_Q_TILE = 256
ATT_TILES_PER_ITER = 2
ATT_KEY_CHUNK = 512
ATT_VT_ROWS = ATT_HEAD_DIM + 16
ATT_Q_SCALE = ATT_HEAD_DIM ** -0.5 * 1.4426950408889634


def _eye(n):
    return jnp.where(lax.broadcasted_iota(jnp.int32, (n, n), 0) == lax.broadcasted_iota(jnp.int32, (n, n), 1),
                     1.0, 0.0).astype(BF16)


def _transpose_bf16(x, eye):
    return lax.dot_general(eye, x, _NT, preferred_element_type=F32).astype(BF16)


def _rope_kernel(q_ref, k_ref, v_ref, cos_ref, sin_ref, qn_ref, kn_ref, qt_ref, kp_ref, vt_ref):
    t = q_ref.shape[0]
    tq = ATT_Q_TILE
    dh = ATT_HEAD_DIM
    gi = lax.broadcasted_iota(jnp.int32, (LANES, LANES), 0) >> 6
    gj = lax.broadcasted_iota(jnp.int32, (LANES, LANES), 1) >> 6
    gmat = jnp.where(gi == gj, 1.0, 0.0).astype(BF16)
    eye_q = _eye(LANES)

    def norm_rope(x, gain, cos, sin):
        lane = lax.broadcasted_iota(jnp.int32, x.shape, 1)
        sq = x * x
        hi = sq.astype(BF16)
        lo = (sq - hi.astype(F32)).astype(BF16)
        ms = (_dot(hi, gmat) + _dot(lo, gmat)) * (1.0 / dh)
        xn = x * lax.rsqrt(ms + NORM_EPS) * gain
        partner = jnp.where((lane & 1) == 0, pltpu.roll(xn, LANES - 1, 1), pltpu.roll(xn, 1, 1))
        return xn * cos + partner * sin

    for j in range(q_ref.shape[1] // LANES):
        cols = slice(j * LANES, (j + 1) * LANES)
        y = norm_rope(q_ref[N_META:, cols].astype(F32), qn_ref[...], cos_ref[N_META:, :], sin_ref[N_META:, :])
        yt = _transpose_bf16((y * ATT_Q_SCALE).astype(BF16), eye_q)
        for i in range((t - N_META) // tq):
            qt_ref[i, cols, :] = yt[:, i * tq:(i + 1) * tq]

    low = lax.broadcasted_iota(jnp.int32, (t, LANES), 1) < dh
    for j in range(k_ref.shape[1] // LANES):
        cols = slice(j * LANES, (j + 1) * LANES)
        y = norm_rope(k_ref[:, cols].astype(F32), kn_ref[...], cos_ref[...], sin_ref[...])
        kp_ref[2 * j] = jnp.where(low, y, 0.0).astype(BF16)
        kp_ref[2 * j + 1] = jnp.where(low, pltpu.roll(y, dh, 1), 0.0).astype(BF16)

    n_real = t - N_META
    eye_v = _eye(v_ref.shape[1])
    vt_real = _transpose_bf16(v_ref[N_META:, :], eye_v)
    vt_meta = _transpose_bf16(v_ref[0:N_META, :], eye_v)
    for h in range(ATT_KV_HEADS):
        vt_ref[h, 0:dh, 0:n_real] = vt_real[h * dh:(h + 1) * dh, :]
        vt_ref[h, 0:dh, n_real:t] = vt_meta[h * dh:(h + 1) * dh, :]
        extra = lax.broadcasted_iota(jnp.int32, (ATT_VT_ROWS - dh, t), 0)
        vt_ref[h, dh:ATT_VT_ROWS, :] = jnp.where(extra == 0, 1.0, 0.0).astype(BF16)


def _rope(u3, cos_e, sin_e, qn, kn):
    b, t, _ = u3.shape
    nq = ATT_HEADS * ATT_HEAD_DIM
    nkv = ATT_KV_HEADS * ATT_HEAD_DIM
    n_qt = (t - N_META) // ATT_Q_TILE
    full = lambda shape: pl.BlockSpec(shape, lambda i: (0, 0))
    return pl.pallas_call(
        _rope_kernel,
        grid=(b,),
        in_specs=[pl.BlockSpec((None, t, nq), lambda i: (i, 0, 0)),
                  pl.BlockSpec((None, t, nkv), lambda i: (i, 0, nq // nkv)),
                  pl.BlockSpec((None, t, nkv), lambda i: (i, 0, nq // nkv + 1)),
                  full((t, LANES)), full((t, LANES)), full((1, LANES)), full((1, LANES))],
        out_specs=[pl.BlockSpec((None, n_qt, nq, ATT_Q_TILE), lambda i: (i, 0, 0, 0)),
                   pl.BlockSpec((None, ATT_KV_HEADS, t, LANES), lambda i: (i, 0, 0, 0)),
                   pl.BlockSpec((None, ATT_KV_HEADS, ATT_VT_ROWS, t), lambda i: (i, 0, 0, 0))],
        out_shape=[jax.ShapeDtypeStruct((b, n_qt, nq, ATT_Q_TILE), BF16),
                   jax.ShapeDtypeStruct((b, ATT_KV_HEADS, t, LANES), BF16),
                   jax.ShapeDtypeStruct((b, ATT_KV_HEADS, ATT_VT_ROWS, t), BF16)],
        compiler_params=_params("parallel"),
        name="qk_norm_rope",
    )(u3, u3, u3, cos_e, sin_e, qn, kn)


def _attn_kernel(qt_ref, k_ref, vt_ref, o_ref):
    n_qt, w, tq = qt_ref.shape
    dh = ATT_HEAD_DIM
    n_real = k_ref.shape[0] - N_META
    eye = _eye(tq)
    zpad = jnp.zeros((LANES - dh, tq), BF16)

    kc = ATT_KEY_CHUNK
    chunks = [(slice(N_META + c * kc, N_META + (c + 1) * kc), slice(c * kc, (c + 1) * kc))
              for c in range(n_real // kc)]
    chunks.append((slice(0, N_META), slice(n_real, n_real + N_META)))

    def q_operand(i, g):
        return jnp.concatenate([qt_ref[i, g * dh:(g + 1) * dh, :], zpad], axis=0)

    steps = [(u, g) for u in range(ATT_TILES_PER_ITER) for g in range(ATT_GROUP)]

    def q_tiles(it, carry):
        i0 = it * ATT_TILES_PER_ITER
        outs = []
        qp = q_operand(i0, 0)
        cur = [_dot(k_ref[k_rows, :], qp) for k_rows, _ in chunks]
        for n, (u, g) in enumerate(steps):
            more = n + 1 < len(steps)
            if more:
                qp = q_operand(i0 + steps[n + 1][0], steps[n + 1][1])
            m = functools.reduce(jnp.maximum, [jnp.max(s, axis=0, keepdims=True) for s in cur])
            nxt = []
            ot = jnp.zeros((ATT_VT_ROWS, tq), F32)
            for (k_rows, v_cols), s in zip(chunks, cur):
                if more:
                    nxt.append(_dot(k_ref[k_rows, :], qp))
                ot = ot + _dot(vt_ref[:, v_cols], jnp.exp2(s - m).astype(BF16))
            cur = nxt
            outs.append((ot[0:dh] * (1.0 / ot[dh:dh + 1])).astype(BF16))
            if g == ATT_GROUP - 1:
                rows = pl.ds(pl.multiple_of((i0 + u) * tq, tq), tq)
                ot_all = jnp.concatenate(outs, axis=0)
                o_ref[rows, :] = lax.dot_general(eye, ot_all, _NT, preferred_element_type=F32).astype(o_ref.dtype)
                outs = []
        return carry

    lax.fori_loop(0, n_qt // ATT_TILES_PER_ITER, q_tiles, 0)


def _attention(qt, kp, vt):
    b, n_qt, nq, tq = qt.shape
    t = kp.shape[2]
    w = ATT_GROUP * ATT_HEAD_DIM
    return pl.pallas_call(
        _attn_kernel,
        grid=(b, ATT_KV_HEADS),
        in_specs=[pl.BlockSpec((None, n_qt, w, tq), lambda i, h: (i, 0, h, 0)),
                  pl.BlockSpec((None, None, t, LANES), lambda i, h: (i, h, 0, 0)),
                  pl.BlockSpec((None, None, ATT_VT_ROWS, t), lambda i, h: (i, h, 0, 0))],
        out_specs=pl.BlockSpec((None, n_qt * tq, w), lambda i, h: (i, 0, h)),
        out_shape=jax.ShapeDtypeStruct((b, n_qt * tq, nq), BF16),
        compiler_params=_params("parallel", "parallel"),
        name="gqa_attention",
    )(qt, kp, vt)


def _att_out_kernel(o_ref, w_ref, h_ref, out_ref):
    out_ref[...] = h_ref[N_META:, :] + _dot(o_ref[...], w_ref[...])


def _att_out(o, w, h3, tn):
    b, s_len, nq = o.shape
    t, d = h3.shape[1], h3.shape[2]
    return pl.pallas_call(
        _att_out_kernel,
        grid=(b, d // tn),
        in_specs=[pl.BlockSpec((None, s_len, nq), lambda i, j: (i, 0, 0)),
                  pl.BlockSpec((nq, tn), lambda i, j: (0, j)),
                  pl.BlockSpec((None, t, tn), lambda i, j: (i, 0, j))],
        out_specs=pl.BlockSpec((None, s_len, tn), lambda i, j: (i, 0, j)),
        out_shape=jax.ShapeDtypeStruct((b, s_len, d), F32),
        compiler_params=_params("parallel", "parallel"),
        name="att_out_res",
    )(o, w, h3)


def _rope_tables(n_tokens):
    rows_n = n_tokens // GRID_W
    rows = jnp.repeat(jnp.arange(rows_n), GRID_W).astype(F32)
    cols = jnp.tile(jnp.arange(GRID_W), rows_n).astype(F32)
    axis_dims = ATT_HEAD_DIM // 2
    freqs = ROPE_THETA ** (-jnp.arange(0, axis_dims, 2, dtype=F32) / axis_dims)
    ang = jnp.concatenate([rows[:, None] * freqs, cols[:, None] * freqs], axis=-1)
    ang = jnp.concatenate([jnp.zeros((N_META, ang.shape[1]), F32), ang], axis=0)
    cos_e = jnp.tile(jnp.repeat(jnp.cos(ang), 2, axis=1), (1, LANES // ATT_HEAD_DIM))
    sign = jnp.tile(jnp.array([-1.0, 1.0], F32), LANES // 2)
    sin_e = jnp.tile(jnp.repeat(jnp.sin(ang), 2, axis=1), (1, LANES // ATT_HEAD_DIM)) * sign
    return cos_e, sin_e


def kernel(x, meta_tokens, norm_mix, norm_ffn, norm_final, sc_w_in, sc_conv_w, gla_w_gate_fwd, gla_b_gate_fwd, gla_w_gate_bwd, gla_b_gate_bwd, gla_norm_w, sc_w_out, att_w_in, att_q_norm, att_k_norm, att_w_out, moe_w_group, moe_b_group, moe_w_expert, moe_b_expert, moe_w_gate, moe_w_up, moe_w_down):
    bsz, s_len, d = x.shape
    t = s_len + N_META
    n0, n1 = bsz * t, bsz * s_len
    tm0 = t // 3
    tm1 = 512

    meta = jnp.broadcast_to(meta_tokens.astype(x.dtype)[None], (bsz, N_META, d))
    h = jnp.concatenate([meta, x], axis=1).reshape(n0, d)

    nk = GLA_HEADS * GLA_DK
    mix_in = sc_w_in.shape[2]
    mix_pad = -mix_in % LANES
    w_in = jnp.pad(sc_w_in[0], ((0, 0), (0, mix_pad))).astype(BF16)
    u = _norm_matmul(h, norm_mix[0], w_in, t, 5 * LANES)
    u3 = u.reshape(bsz, t, mix_in + mix_pad)
    y_a = _gated_conv(u3, sc_conv_w[0])
    wc = jnp.zeros((LANES, 2 * nk), F32)
    wc = wc.at[:GLA_GATE_RANK, :nk].set(gla_w_gate_fwd[0])
    wc = wc.at[GLA_GATE_RANK:2 * GLA_GATE_RANK, nk:].set(gla_w_gate_bwd[0])
    bc = jnp.concatenate([gla_b_gate_fwd[0], gla_b_gate_bwd[0]]).reshape(1, 2 * nk)
    y_b = _gla(u3, wc.astype(BF16), bc, gla_norm_w[0].reshape(1, GLA_DV))
    w_out = sc_w_out[0].astype(BF16)
    h = _proj2_res(y_a.reshape(n0, SC_WIDTH), y_b.reshape(n0, -1), w_out[:SC_WIDTH], w_out[SC_WIDTH:], h, tm0)
    h = _moe_layer(h, norm_ffn[0], moe_w_group[0], moe_b_group[0], moe_w_expert[0], moe_b_expert[0],
                   moe_w_gate[0], moe_w_up[0], moe_w_down[0], norm_final, 3 * LANES, False)

    u = _norm_matmul(h, norm_mix[1], att_w_in[0].astype(BF16), t, 4 * LANES)
    u3 = u.reshape(bsz, t, -1)
    cos_e, sin_e = _rope_tables(s_len)
    qn = jnp.tile(att_q_norm[0], LANES // ATT_HEAD_DIM).reshape(1, LANES)
    kn = jnp.tile(att_k_norm[0], LANES // ATT_HEAD_DIM).reshape(1, LANES)
    qt, kp, vt = _rope(u3, cos_e, sin_e, qn, kn)
    o = _attention(qt, kp, vt)
    h = _att_out(o, att_w_out[0].astype(BF16), h.reshape(bsz, t, d), 4 * LANES).reshape(n1, d)
    h = _moe_layer(h, norm_ffn[1], moe_w_group[1], moe_b_group[1], moe_w_expert[1], moe_b_expert[1],
                   moe_w_gate[1], moe_w_up[1], moe_w_down[1], norm_final, tm1, True)
    return h.reshape(bsz, s_len, d)
```

```python
import functools

import jax
import jax.numpy as jnp
from jax import lax
from jax.experimental import pallas as pl
from jax.experimental.pallas import tpu as pltpu

F32 = jnp.float32
BF16 = jnp.bfloat16

NORM_EPS = 1e-6
N_META = 16
GRID_W = 64
ROPE_THETA = 10000.0
SC_WIDTH = 512
GLA_HEADS = 4
GLA_DK = 64
GLA_DV = 128
GLA_GATE_RANK = 16
GLA_GATE_TAU = 16.0
GLA_CHUNK = 64
ATT_HEAD_DIM = 64
ATT_HEADS = 16
ATT_KV_HEADS = 4
ATT_GROUP = ATT_HEADS // ATT_KV_HEADS
MOE_GROUPS = 4
MOE_EXPERTS_PER_GROUP = 8
N_EXPERTS = 32
MOE_D_FF = 256

LANES = 128
MIB = 2 ** 20
VMEM_LIMIT = 56 * MIB

_NT = (((1,), (1,)), ((), ()))
_TN = (((0,), (0,)), ((), ()))


def _params(*semantics):
    return pltpu.CompilerParams(dimension_semantics=semantics, vmem_limit_bytes=VMEM_LIMIT)


def _dot(a, b):
    return jnp.dot(a, b, preferred_element_type=F32)


def _silu(x):
    return x * (1.0 / (1.0 + jnp.exp(-x)))


def _norm_matmul_kernel(h_ref, g_ref, w_ref, o_ref, xn_ref):
    @pl.when(pl.program_id(1) == 0)
    def _():
        x = h_ref[...]
        ms = jnp.mean(x * x, axis=-1, keepdims=True)
        xn_ref[...] = (x * lax.rsqrt(ms + NORM_EPS) * g_ref[...]).astype(BF16)

    o_ref[...] = _dot(xn_ref[...], w_ref[...]).astype(o_ref.dtype)


def _norm_matmul(h, g, w, tm, tn):
    n, d = h.shape
    nout = w.shape[1]
    return pl.pallas_call(
        _norm_matmul_kernel,
        grid=(n // tm, nout // tn),
        in_specs=[pl.BlockSpec((tm, d), lambda i, j: (i, 0)),
                  pl.BlockSpec((1, d), lambda i, j: (0, 0)),
                  pl.BlockSpec((d, tn), lambda i, j: (0, j))],
        out_specs=pl.BlockSpec((tm, tn), lambda i, j: (i, j)),
        out_shape=jax.ShapeDtypeStruct((n, nout), BF16),
        scratch_shapes=[pltpu.VMEM((tm, d), BF16)],
        compiler_params=_params("parallel", "arbitrary"),
        name="norm_matmul",
    )(h, g.reshape(1, d), w)


def _conv_kernel(ab_ref, ac_ref, ax_ref, w_ref, o_ref, z_ref):
    t, c = ab_ref.shape
    z = ac_ref[...].astype(F32) * ax_ref[...].astype(F32)
    z_ref[0:8, :] = jnp.zeros((8, c), F32)
    z_ref[8:8 + t, :] = z
    z_ref[8 + t:16 + t, :] = jnp.zeros((8, c), F32)
    w = w_ref[...]
    conv = w[0:1, :] * z_ref[7:7 + t, :] + w[1:2, :] * z + w[2:3, :] * z_ref[9:9 + t, :]
    o_ref[...] = (ab_ref[...].astype(F32) * conv).astype(o_ref.dtype)


def _gated_conv(u3, conv_w):
    b, t, _ = u3.shape
    nblk = SC_WIDTH // LANES

    def col(off):
        return pl.BlockSpec((None, t, LANES), lambda i, j: (i, 0, off + j))

    return pl.pallas_call(
        _conv_kernel,
        grid=(b, nblk),
        in_specs=[col(0), col(nblk), col(2 * nblk),
                  pl.BlockSpec((3, LANES), lambda i, j: (0, j))],
        out_specs=pl.BlockSpec((None, t, LANES), lambda i, j: (i, 0, j)),
        out_shape=jax.ShapeDtypeStruct((b, t, SC_WIDTH), BF16),
        scratch_shapes=[pltpu.VMEM((t + 16, LANES), F32)],
        compiler_params=_params("parallel", "parallel"),
        name="gated_conv",
    )(u3, u3, u3, conv_w)


GLA_ROW_TILE = 688
GLA_EPI_TILE = 48


def _gla_kernel(q_ref, k_ref, v_ref, r_ref, g_ref, wc_ref, bc_ref, nw_ref, y_ref,
                la_ref, of_ref, ob_ref, sf_ref, sb_ref):
    t, nk = q_ref.shape
    nv = v_ref.shape[1]
    c = GLA_CHUNK
    n_real = (t - N_META) // c

    for i in range(t // GLA_ROW_TILE):
        rows = pl.ds(i * GLA_ROW_TILE, GLA_ROW_TILE)
        pre = _dot(g_ref[rows, :], wc_ref[...]) + bc_ref[...]
        la_ref[rows, :] = (jnp.minimum(pre, 0.0) - jnp.log(1.0 + jnp.exp(-jnp.abs(pre)))) * (1.0 / GLA_GATE_TAU)

    def iota(shape, dim):
        return lax.broadcasted_iota(jnp.int32, shape, dim)

    tri_f = jnp.where(iota((c, c), 0) >= iota((c, c), 1), 1.0, 0.0).astype(BF16)
    tri_b = jnp.where(iota((c, c), 0) <= iota((c, c), 1), 1.0, 0.0).astype(BF16)
    keep_f = (iota((c, nk), 1) & (c - 1)) <= iota((c, nk), 0)
    keep_b = (iota((c, nk), 1) & (c - 1)) > iota((c, nk), 0)
    bd_k = (iota((nk, nk), 0) >> 6) == (iota((nk, nk), 1) >> 6)
    bd_v = (iota((nk, nv), 0) >> 6) == (iota((nk, nv), 1) >> 7)
    bd_s = (iota((nv, nk), 0) >> 7) == (iota((nv, nk), 1) >> 6)
    valid0 = iota((c, 1), 0) < N_META

    def chunk(rows, valid, tri, keep, tot_row, la_col, s_ref, o_ref, out_rows, n_out):
        q = q_ref[rows, :].astype(F32) * (GLA_DK ** -0.5)
        k = k_ref[rows, :].astype(F32)
        v = v_ref[rows, :].astype(F32)
        la = la_ref[rows, la_col:la_col + nk]
        if valid is not None:
            q = jnp.where(valid, q, 0.0)
            k = jnp.where(valid, k, 0.0)
            v = jnp.where(valid, v, 0.0)
            la = jnp.where(valid, la, 0.0)
        hi = la.astype(BF16)
        rem = la - hi.astype(F32)
        mid = rem.astype(BF16)
        lo = (rem - mid.astype(F32)).astype(BF16)
        cum = _dot(tri, hi) + _dot(tri, mid) + _dot(tri, lo)
        yield
        tot = cum[tot_row:tot_row + 1, :]
        q_dec = (q * jnp.exp(cum)).astype(BF16)
        k_inv = k * jnp.exp(-cum)
        k_end = (k * jnp.exp(tot - cum)).astype(BF16)
        kbd = jnp.where(bd_k, jnp.concatenate([k_inv] * GLA_HEADS, axis=0), 0.0).astype(BF16)
        p = lax.dot_general(q_dec, kbd, _NT, preferred_element_type=F32)
        s_old = s_ref[...]
        o_inter = lax.dot_general(q_dec, s_old.astype(BF16), _NT, preferred_element_type=F32)
        kvt = lax.dot_general(v.astype(BF16), k_end, _TN, preferred_element_type=F32)
        yield
        p = jnp.where(keep, p, 0.0).astype(BF16)
        vbd = jnp.where(bd_v, jnp.concatenate([v] * GLA_HEADS, axis=0), 0.0).astype(BF16)
        o = _dot(p, vbd) + o_inter
        s_ref[...] = jnp.exp(tot) * s_old + jnp.where(bd_s, kvt, 0.0)
        o_ref[out_rows, :] = o[:n_out]

    def lockstep(*gens):
        live = list(gens)
        while live:
            nxt = []
            for gen in live:
                try:
                    next(gen)
                    nxt.append(gen)
                except StopIteration:
                    pass
            live = nxt

    sf_ref[...] = jnp.zeros(sf_ref.shape, F32)
    sb_ref[...] = jnp.zeros(sb_ref.shape, F32)
    meta_rows = pl.ds(0, c)
    lockstep(chunk(meta_rows, valid0, tri_f, keep_f, c - 1, 0, sf_ref, of_ref, pl.ds(0, N_META), N_META))

    def body(i, carry):
        rf = pl.multiple_of(N_META + i * c, 16)
        rb = pl.multiple_of(N_META + (n_real - 1 - i) * c, 16)
        lockstep(chunk(pl.ds(rf, c), None, tri_f, keep_f, c - 1, 0, sf_ref, of_ref, pl.ds(rf, c), c),
                 chunk(pl.ds(rb, c), None, tri_b, keep_b, 0, nk, sb_ref, ob_ref, pl.ds(rb, c), c))
        return carry

    lax.fori_loop(0, n_real, body, 0)
    lockstep(chunk(meta_rows, valid0, tri_b, keep_b, 0, nk, sb_ref, ob_ref, pl.ds(0, N_META), N_META))

    def epilogue(i, carry):
        rows = pl.ds(pl.multiple_of(i * GLA_EPI_TILE, 16), GLA_EPI_TILE)
        o = of_ref[rows, :] + ob_ref[rows, :]
        r = r_ref[rows, :].astype(F32)
        for h in range(GLA_HEADS):
            cols = slice(h * GLA_DV, (h + 1) * GLA_DV)
            oh = o[:, cols]
            ms = jnp.mean(oh * oh, axis=-1, keepdims=True)
            yh = oh * lax.rsqrt(ms + NORM_EPS) * nw_ref[...] * _silu(r[:, cols])
            y_ref[rows, cols] = yh.astype(y_ref.dtype)
        return carry

    lax.fori_loop(0, t // GLA_EPI_TILE, epilogue, 0)


def _gla(u3, wc, bc, norm_w):
    b, t, _ = u3.shape
    nk = GLA_HEADS * GLA_DK
    nv = GLA_HEADS * GLA_DV
    q_blk = 3 * SC_WIDTH // nk
    v_blk = (3 * SC_WIDTH + 2 * nk) // nv
    g_blk = (3 * SC_WIDTH + 2 * nk + 2 * nv) // LANES
    full = lambda shape: pl.BlockSpec(shape, lambda i: (0, 0))
    return pl.pallas_call(
        _gla_kernel,
        grid=(b,),
        in_specs=[pl.BlockSpec((None, t, nk), lambda i: (i, 0, q_blk)),
                  pl.BlockSpec((None, t, nk), lambda i: (i, 0, q_blk + 1)),
                  pl.BlockSpec((None, t, nv), lambda i: (i, 0, v_blk)),
                  pl.BlockSpec((None, t, nv), lambda i: (i, 0, v_blk + 1)),
                  pl.BlockSpec((None, t, LANES), lambda i: (i, 0, g_blk)),
                  full((LANES, 2 * nk)), full((1, 2 * nk)), full((1, GLA_DV))],
        out_specs=pl.BlockSpec((None, t, nv), lambda i: (i, 0, 0)),
        out_shape=jax.ShapeDtypeStruct((b, t, nv), BF16),
        scratch_shapes=[pltpu.VMEM((t, 2 * nk), F32),
                        pltpu.VMEM((t, nv), F32), pltpu.VMEM((t, nv), F32),
                        pltpu.VMEM((nv, nk), F32), pltpu.VMEM((nv, nk), F32)],
        compiler_params=_params("parallel"),
        name="gla",
    )(u3, u3, u3, u3, u3, wc, bc, norm_w)


def _proj2_res_kernel(a_ref, b_ref, wa_ref, wb_ref, res_ref, o_ref):
    o_ref[...] = res_ref[...] + _dot(a_ref[...], wa_ref[...]) + _dot(b_ref[...], wb_ref[...])


def _proj2_res(a, b, wa, wb, res, tm):
    n, d = res.shape
    ka, kb = a.shape[1], b.shape[1]
    return pl.pallas_call(
        _proj2_res_kernel,
        grid=(n // tm,),
        in_specs=[pl.BlockSpec((tm, ka), lambda i: (i, 0)),
                  pl.BlockSpec((tm, kb), lambda i: (i, 0)),
                  pl.BlockSpec((ka, d), lambda i: (0, 0)),
                  pl.BlockSpec((kb, d), lambda i: (0, 0)),
                  pl.BlockSpec((tm, d), lambda i: (i, 0))],
        out_specs=pl.BlockSpec((tm, d), lambda i: (i, 0)),
        out_shape=jax.ShapeDtypeStruct((n, d), F32),
        compiler_params=_params("parallel"),
        name="proj2_res",
    )(a, b, wa, wb, res)


GROUP_LANE0 = N_EXPERTS


def _router_kernel(h_ref, g_ref, whi_ref, wlo_ref, b_ref, xn_ref, gates_ref, info_ref, inforow_ref, count_ref):
    x = h_ref[...]
    ms = jnp.mean(x * x, axis=-1, keepdims=True)
    xn = x * lax.rsqrt(ms + NORM_EPS) * g_ref[...]
    xhi = xn.astype(BF16)
    xn_ref[...] = xhi
    xlo = (xn - xhi.astype(F32)).astype(BF16)
    logits = _dot(xhi, whi_ref[...]) + _dot(xhi, wlo_ref[...]) + _dot(xlo, whi_ref[...]) + b_ref[...]

    lane = lax.broadcasted_iota(jnp.int32, logits.shape, 1)
    neg = -jnp.inf
    big = jnp.int32(LANES)
    is_group = (lane >> 2) == (GROUP_LANE0 >> 2)
    gl = jnp.where(is_group, logits, neg)
    gmax = jnp.max(gl, axis=-1, keepdims=True)
    g_p = 1.0 / jnp.sum(jnp.exp(gl - gmax), axis=-1, keepdims=True)
    g_idx = jnp.min(jnp.where(gl == gmax, lane, big), axis=-1, keepdims=True) - GROUP_LANE0
    el = jnp.where((lane >> 3) == g_idx, logits, neg)
    m1 = jnp.max(el, axis=-1, keepdims=True)
    i1 = jnp.min(jnp.where(el == m1, lane, big), axis=-1, keepdims=True)
    el2 = jnp.where(lane == i1, neg, el)
    m2 = jnp.max(el2, axis=-1, keepdims=True)
    i2 = jnp.min(jnp.where(el2 == m2, lane, big), axis=-1, keepdims=True)
    e2 = jnp.exp(m2 - m1)
    w1 = g_p / (1.0 + e2)
    w2 = g_p * e2 / (1.0 + e2)
    gates_ref[...] = jnp.where(lane == i1, w1, 0.0) + jnp.where(lane == i2, w2, 0.0)

    tm = x.shape[0]
    onehot = jnp.where(lane == g_idx, 1.0, 0.0)
    r_i = lax.broadcasted_iota(jnp.int32, (tm, tm), 0)
    c_i = lax.broadcasted_iota(jnp.int32, (tm, tm), 1)
    before = jnp.where(r_i > c_i, 1.0, 0.0).astype(BF16)
    prefix = _dot(before, onehot.astype(BF16))
    rank = jnp.sum(jnp.where(lane == g_idx, prefix, 0.0), axis=-1, keepdims=True).astype(jnp.int32)
    info_ref[...] = jnp.where(lane == 0, g_idx, jnp.where(lane == 1, rank, 0))
    eye = r_i == c_i
    g_row = jnp.sum(jnp.where(eye, g_idx, 0), axis=0, keepdims=True)
    rank_row = jnp.sum(jnp.where(eye, rank, 0), axis=0, keepdims=True)
    sub = lax.broadcasted_iota(jnp.int32, inforow_ref.shape, 0)
    inforow_ref[...] = jnp.where(sub == 0, g_row, jnp.where(sub == 1, rank_row, 0))
    counts = jnp.sum(onehot, axis=0, keepdims=True).astype(jnp.int32)
    count_ref[...] = jnp.broadcast_to(counts, count_ref.shape)


def _router(h, g, whi, wlo, bias, tm):
    n, d = h.shape
    nb = n // tm
    return pl.pallas_call(
        _router_kernel,
        grid=(nb,),
        in_specs=[pl.BlockSpec((tm, d), lambda i: (i, 0)),
                  pl.BlockSpec((1, d), lambda i: (0, 0)),
                  pl.BlockSpec((d, LANES), lambda i: (0, 0)),
                  pl.BlockSpec((d, LANES), lambda i: (0, 0)),
                  pl.BlockSpec((1, LANES), lambda i: (0, 0))],
        out_specs=[pl.BlockSpec((tm, d), lambda i: (i, 0)),
                   pl.BlockSpec((tm, LANES), lambda i: (i, 0)),
                   pl.BlockSpec((tm, LANES), lambda i: (i, 0)),
                   pl.BlockSpec((None, 8, tm), lambda i: (i, 0, 0)),
                   pl.BlockSpec((None, 8, LANES), lambda i: (i, 0, 0))],
        out_shape=[jax.ShapeDtypeStruct((n, d), BF16),
                   jax.ShapeDtypeStruct((n, LANES), F32),
                   jax.ShapeDtypeStruct((n, LANES), jnp.int32),
                   jax.ShapeDtypeStruct((nb, 8, tm), jnp.int32),
                   jax.ShapeDtypeStruct((nb, 8, LANES), jnp.int32)],
        compiler_params=_params("parallel"),
        name="moe_router",
    )(h, g.reshape(1, d), whi, wlo, bias)


MOE_TILE = 512
SEG_ALIGN = 16
STRIP = 32


def _pow2_sizes(max_rows, min_rows):
    sizes, s = [], min_rows
    while s <= max_rows:
        sizes.append(s)
        s *= 2
    return sizes[::-1]


def _strip_copies(n_rows, sizes, make_copy):
    pos = 0
    for sz in sizes:
        bit = n_rows & sz
        yield bit, make_copy(pos, sz)
        pos = pos + bit


def _slot_key(group, rank, boff_ref, b):
    key = rank
    for g in range(MOE_GROUPS):
        key = key + jnp.where(group == g, boff_ref[MOE_GROUPS * b + g], 0)
    return key


def _start_all(copies):
    for bit, cs in copies:
        @pl.when(bit != 0)
        def _():
            for c in cs:
                c.start()


def _wait_all(copies):
    for bit, cs in copies:
        @pl.when(bit != 0)
        def _():
            for c in cs:
                c.wait()


def _dispatch_kernel(off_ref, cpad_ref, boff_ref, tail_ref, x_ref, gates_ref, inforow_ref,
                     xs_hbm, gs_hbm, xstg, gstg, zx, zg, sem):
    b = pl.program_id(0)
    last = pl.num_programs(0) - 1
    tb = x_ref.shape[0]
    ks = xstg.shape[1]
    cur = lax.rem(b, 2)
    key = _slot_key(inforow_ref[0:1, :], inforow_ref[1:2, :], boff_ref, b)
    slot = lax.broadcasted_iota(jnp.int32, (ks, tb), 0)
    perm = jnp.where(slot == key, 1.0, 0.0).astype(BF16)
    xstg[cur] = _dot(perm, x_ref[...]).astype(BF16)
    gt = gates_ref[...]
    hi = gt.astype(BF16)
    lo = (gt - hi.astype(F32)).astype(BF16)
    gstg[cur] = _dot(perm, hi) + _dot(perm, lo)

    sizes = _pow2_sizes(tb, SEG_ALIGN)

    def copies(blk, sl):
        for g in range(MOE_GROUPS):
            src0 = boff_ref[MOE_GROUPS * blk + g]
            dst0 = off_ref[MOE_GROUPS * blk + g]

            def make(pos, sz, src0=src0, dst0=dst0):
                src = pl.ds(pl.multiple_of(src0 + pos, SEG_ALIGN), sz)
                dst = pl.ds(pl.multiple_of(dst0 + pos, SEG_ALIGN), sz)
                return (pltpu.make_async_copy(xstg.at[sl, src, :], xs_hbm.at[dst, :], sem.at[sl, 0]),
                        pltpu.make_async_copy(gstg.at[sl, src, :], gs_hbm.at[dst, :], sem.at[sl, 1]))

            yield from _strip_copies(cpad_ref[MOE_GROUPS * blk + g], sizes, make)

    _start_all(copies(b, cur))

    @pl.when(b > 0)
    def _():
        _wait_all(copies(jnp.maximum(b - 1, 0), 1 - cur))

    @pl.when(b == last)
    def _():
        _wait_all(copies(b, cur))
        zx[...] = jnp.zeros(zx.shape, BF16)
        zg[...] = jnp.zeros(zg.shape, F32)
        tail_sizes = _pow2_sizes(MOE_TILE // 2, SEG_ALIGN)

        def tails():
            for g in range(MOE_GROUPS):
                dst0 = tail_ref[g]

                def make(pos, sz, dst0=dst0):
                    dst = pl.ds(pl.multiple_of(dst0 + pos, SEG_ALIGN), sz)
                    return (pltpu.make_async_copy(zx.at[pl.ds(0, sz), :], xs_hbm.at[dst, :], sem.at[0, 0]),
                            pltpu.make_async_copy(zg.at[pl.ds(0, sz), :], gs_hbm.at[dst, :], sem.at[0, 1]))

                yield from _strip_copies(tail_ref[MOE_GROUPS + g], tail_sizes, make)

        _start_all(tails())
        _wait_all(tails())

        def spare(i):
            dst = pl.ds(pl.multiple_of(i * MOE_TILE, MOE_TILE), MOE_TILE)
            return (pltpu.make_async_copy(zx, xs_hbm.at[dst, :], sem.at[0, 0]),
                    pltpu.make_async_copy(zg, gs_hbm.at[dst, :], sem.at[0, 1]))

        first_spare = tail_ref[2 * MOE_GROUPS]
        n_tiles = xs_hbm.shape[0] // MOE_TILE

        @pl.loop(first_spare, n_tiles)
        def _(i):
            cx, cg = spare(i)
            cx.start()
            cg.start()

        @pl.loop(first_spare, n_tiles)
        def _(i):
            cx, cg = spare(i)
            cx.wait()
            cg.wait()


def _dispatch(xn, gates, inforow, off, cpad, boff, tail, n_sorted, tb, ks):
    n, d = xn.shape
    nb = n // tb
    grid_spec = pltpu.PrefetchScalarGridSpec(
        num_scalar_prefetch=4,
        grid=(nb,),
        in_specs=[pl.BlockSpec((tb, d), lambda i, *_: (i, 0)),
                  pl.BlockSpec((tb, LANES), lambda i, *_: (i, 0)),
                  pl.BlockSpec((None, 8, tb), lambda i, *_: (i, 0, 0))],
        out_specs=[pl.BlockSpec(memory_space=pl.ANY), pl.BlockSpec(memory_space=pl.ANY)],
        scratch_shapes=[pltpu.VMEM((2, ks, d), BF16), pltpu.VMEM((2, ks, LANES), F32),
                        pltpu.VMEM((MOE_TILE, d), BF16), pltpu.VMEM((MOE_TILE, LANES), F32),
                        pltpu.SemaphoreType.DMA((2, 2))])
    return pl.pallas_call(
        _dispatch_kernel,
        grid_spec=grid_spec,
        out_shape=[jax.ShapeDtypeStruct((n_sorted, d), BF16),
                   jax.ShapeDtypeStruct((n_sorted, LANES), F32)],
        compiler_params=_params("arbitrary"),
        name="moe_dispatch",
    )(off, cpad, boff, tail, xn, gates, inforow)


def _group_mlp_kernel(tg_ref, nused_ref, x_ref, gs_ref, wg_ref, wu_ref, wd_ref, y_ref):
    i = pl.program_id(0)

    @pl.when(i < nused_ref[0])
    def _():
        x = x_ref[...]
        gs = gs_ref[...]
        lane = lax.broadcasted_iota(jnp.int32, gs.shape, 1)
        lane0 = tg_ref[i] * MOE_EXPERTS_PER_GROUP
        acc = jnp.zeros(y_ref.shape, F32)
        for e in range(MOE_EXPERTS_PER_GROUP):
            he = _silu(_dot(x, wg_ref[e])) * _dot(x, wu_ref[e])
            gate = jnp.sum(jnp.where(lane == lane0 + e, gs, 0.0), axis=-1, keepdims=True)
            acc = acc + _dot((he * gate).astype(BF16), wd_ref[e])
        y_ref[...] = acc.astype(y_ref.dtype)

    @pl.when(i >= nused_ref[0])
    def _():
        y_ref[...] = jnp.zeros(y_ref.shape, y_ref.dtype)


def _group_mlp(xs, gs, wg, wu, wd, tile_group, nused):
    n_sorted, d = xs.shape
    ne = MOE_EXPERTS_PER_GROUP

    def rows(i, tg, nu):
        return (jnp.minimum(i, nu[0] - 1), 0)

    def group(i, tg, nu):
        return (tg[i], 0, 0)

    grid_spec = pltpu.PrefetchScalarGridSpec(
        num_scalar_prefetch=2,
        grid=(n_sorted // MOE_TILE,),
        in_specs=[pl.BlockSpec((MOE_TILE, d), rows),
                  pl.BlockSpec((MOE_TILE, LANES), rows),
                  pl.BlockSpec((ne, d, MOE_D_FF), group),
                  pl.BlockSpec((ne, d, MOE_D_FF), group),
                  pl.BlockSpec((ne, MOE_D_FF, d), group)],
        out_specs=pl.BlockSpec((MOE_TILE, d), lambda i, tg, nu: (i, 0)))
    return pl.pallas_call(
        _group_mlp_kernel,
        grid_spec=grid_spec,
        out_shape=jax.ShapeDtypeStruct((n_sorted, d), BF16),
        compiler_params=_params("arbitrary"),
        name="moe_experts",
    )(tile_group, nused, xs, gs, wg, wu, wd)


def _combine_kernel(off_ref, cpad_ref, boff_ref, info_ref, res_ref, fg_ref, y_hbm, o_ref, ybuf, sem,
                    *, final_norm):
    b = pl.program_id(0)
    last = pl.num_programs(0) - 1
    tb = res_ref.shape[0]
    ks = ybuf.shape[1]
    cur = lax.rem(b, 2)
    sizes = _pow2_sizes(tb, STRIP)

    def copies(blk, sl):
        for g in range(MOE_GROUPS):
            src0 = off_ref[MOE_GROUPS * blk + g]
            dst0 = boff_ref[MOE_GROUPS * blk + g]

            def make(pos, sz, src0=src0, dst0=dst0):
                src = pl.ds(pl.multiple_of(src0 + pos, SEG_ALIGN), sz)
                dst = pl.ds(pl.multiple_of(dst0 + pos, STRIP), sz)
                return (pltpu.make_async_copy(y_hbm.at[src, :], ybuf.at[sl, dst, :], sem.at[sl]),)

            n_rows = (cpad_ref[MOE_GROUPS * blk + g] + (STRIP - 1)) & (-STRIP)
            yield from _strip_copies(n_rows, sizes, make)

    @pl.when(b == 0)
    def _():
        ybuf[...] = jnp.zeros(ybuf.shape, ybuf.dtype)
        _start_all(copies(b, cur))

    @pl.when(b < last)
    def _():
        _start_all(copies(jnp.minimum(b + 1, last), 1 - cur))

    _wait_all(copies(b, cur))

    key = _slot_key(info_ref[:, 0:1], info_ref[:, 1:2], boff_ref, b)
    slot = lax.broadcasted_iota(jnp.int32, (tb, ks), 1)
    perm = jnp.where(slot == key, 1.0, 0.0).astype(BF16)
    y = res_ref[...] + _dot(perm, ybuf[cur])
    if final_norm:
        ms = jnp.mean(y * y, axis=-1, keepdims=True)
        y = y * lax.rsqrt(ms + NORM_EPS) * fg_ref[...]
    o_ref[...] = y


def _combine(ys, info, res, final_g, off, cpad, boff, tb, ks, final_norm):
    n, d = res.shape
    grid_spec = pltpu.PrefetchScalarGridSpec(
        num_scalar_prefetch=3,
        grid=(n // tb,),
        in_specs=[pl.BlockSpec((tb, LANES), lambda i, *_: (i, 0)),
                  pl.BlockSpec((tb, d), lambda i, *_: (i, 0)),
                  pl.BlockSpec((1, d), lambda i, *_: (0, 0)),
                  pl.BlockSpec(memory_space=pl.ANY)],
        out_specs=pl.BlockSpec((tb, d), lambda i, *_: (i, 0)),
        scratch_shapes=[pltpu.VMEM((2, ks, d), BF16), pltpu.SemaphoreType.DMA((2,))])
    return pl.pallas_call(
        functools.partial(_combine_kernel, final_norm=final_norm),
        grid_spec=grid_spec,
        out_shape=jax.ShapeDtypeStruct((n, d), F32),
        compiler_params=_params("arbitrary"),
        name="moe_combine",
    )(off, cpad, boff, info, res, final_g.reshape(1, d), ys)


def _round_up(x, m):
    return (x + m - 1) // m * m


def _moe_layer(h, norm_g, w_group, b_group, w_expert, b_expert, w_gate, w_up, w_down,
               final_g, tb, final_norm):
    n, d = h.shape
    nb = n // tb
    pad = LANES - N_EXPERTS - MOE_GROUPS
    wr = jnp.concatenate([w_expert, w_group, jnp.zeros((d, pad), F32)], axis=1)
    whi = wr.astype(BF16)
    wlo = (wr - whi.astype(F32)).astype(BF16)
    bias = jnp.concatenate([b_expert, b_group, jnp.zeros((pad,), F32)]).reshape(1, LANES)
    xn, gates, info, inforow, counts = _router(h, norm_g, whi, wlo, bias, tb)

    cnt = counts[:, 0, :MOE_GROUPS]
    cpad = _round_up(cnt, SEG_ALIGN)
    strip = _round_up(cnt, STRIP)
    used = jnp.sum(cpad, axis=0)
    region = _round_up(used, MOE_TILE)
    region_start = jnp.cumsum(region) - region
    off = region_start[None, :] + jnp.cumsum(cpad, axis=0) - cpad
    boff = jnp.cumsum(strip, axis=1) - strip
    nused = (jnp.sum(region) // MOE_TILE).reshape(1)
    tail = jnp.concatenate([region_start + used, region - used, nused]).astype(jnp.int32)
    n_tiles = -(-(n + nb * MOE_GROUPS * (SEG_ALIGN - 1) + MOE_GROUPS * (MOE_TILE - 1)) // MOE_TILE) + 1
    tile_start = jnp.arange(n_tiles, dtype=jnp.int32) * MOE_TILE
    region_end = region_start + region
    tile_group = jnp.minimum(jnp.sum(tile_start[:, None] >= region_end[None, :], axis=1), MOE_GROUPS - 1)
    ks =tb + MOE_GROUPS * STRIP

    flat = lambda a: a.reshape(-1).astype(jnp.int32)
    xs, gs = _dispatch(xn, gates, inforow, flat(off), flat(cpad), flat(boff), tail,
                       n_tiles * MOE_TILE, tb, ks)
    ys = _group_mlp(xs, gs, w_gate.astype(BF16), w_up.astype(BF16), w_down.astype(BF16),
                    tile_group.astype(jnp.int32), nused.astype(jnp.int32))
    return _combine(ys, info, h, final_g, flat(off), flat(cpad), flat(boff), tb, ks, final_norm)


ROPE_ROW_TILE = 688
ATT_Q_TILE = 256
ATT_TILES_PER_ITER = 2
ATT_KEY_CHUNK = 512
ATT_VT_ROWS = ATT_HEAD_DIM + 16
ATT_Q_SCALE = ATT_HEAD_DIM ** -0.5 * 1.4426950408889634


def _eye(n):
    return jnp.where(lax.broadcasted_iota(jnp.int32, (n, n), 0) == lax.broadcasted_iota(jnp.int32, (n, n), 1),
                     1.0, 0.0).astype(BF16)


def _transpose_bf16(x, eye):
    return lax.dot_general(eye, x, _NT, preferred_element_type=F32).astype(BF16)


def _rope_kernel(q_ref, k_ref, v_ref, cos_ref, sin_ref, qn_ref, kn_ref, qt_ref, kp_ref, vt_ref):
    t = q_ref.shape[0]
    tq = ATT_Q_TILE
    dh = ATT_HEAD_DIM
    gi = lax.broadcasted_iota(jnp.int32, (LANES, LANES), 0) >> 6
    gj = lax.broadcasted_iota(jnp.int32, (LANES, LANES), 1) >> 6
    gmat = jnp.where(gi == gj, 1.0, 0.0).astype(BF16)
    eye_q = _eye(LANES)

    def norm_rope(x, gain, cos, sin):
        lane = lax.broadcasted_iota(jnp.int32, x.shape, 1)
        sq = x * x
        hi = sq.astype(BF16)
        lo = (sq - hi.astype(F32)).astype(BF16)
        ms = (_dot(hi, gmat) + _dot(lo, gmat)) * (1.0 / dh)
        xn = x * lax.rsqrt(ms + NORM_EPS) * gain
        partner = jnp.where((lane & 1) == 0, pltpu.roll(xn, LANES - 1, 1), pltpu.roll(xn, 1, 1))
        return xn * cos + partner * sin

    for j in range(q_ref.shape[1] // LANES):
        cols = slice(j * LANES, (j + 1) * LANES)
        y = norm_rope(q_ref[N_META:, cols].astype(F32), qn_ref[...], cos_ref[N_META:, :], sin_ref[N_META:, :])
        yt = _transpose_bf16((y * ATT_Q_SCALE).astype(BF16), eye_q)
        for i in range((t - N_META) // tq):
            qt_ref[i, cols, :] = yt[:, i * tq:(i + 1) * tq]

    low = lax.broadcasted_iota(jnp.int32, (t, LANES), 1) < dh
    for j in range(k_ref.shape[1] // LANES):
        cols = slice(j * LANES, (j + 1) * LANES)
        y = norm_rope(k_ref[:, cols].astype(F32), kn_ref[...], cos_ref[...], sin_ref[...])
        kp_ref[2 * j] = jnp.where(low, y, 0.0).astype(BF16)
        kp_ref[2 * j + 1] = jnp.where(low, pltpu.roll(y, dh, 1), 0.0).astype(BF16)

    n_real = t - N_META
    eye_v = _eye(v_ref.shape[1])
    vt_real = _transpose_bf16(v_ref[N_META:, :], eye_v)
    vt_meta = _transpose_bf16(v_ref[0:N_META, :], eye_v)
    for h in range(ATT_KV_HEADS):
        vt_ref[h, 0:dh, 0:n_real] = vt_real[h * dh:(h + 1) * dh, :]
        vt_ref[h, 0:dh, n_real:t] = vt_meta[h * dh:(h + 1) * dh, :]
        extra = lax.broadcasted_iota(jnp.int32, (ATT_VT_ROWS - dh, t), 0)
        vt_ref[h, dh:ATT_VT_ROWS, :] = jnp.where(extra == 0, 1.0, 0.0).astype(BF16)


def _rope(u3, cos_e, sin_e, qn, kn):
    b, t, _ = u3.shape
    nq = ATT_HEADS * ATT_HEAD_DIM
    nkv = ATT_KV_HEADS * ATT_HEAD_DIM
    n_qt = (t - N_META) // ATT_Q_TILE
    full = lambda shape: pl.BlockSpec(shape, lambda i: (0, 0))
    return pl.pallas_call(
        _rope_kernel,
        grid=(b,),
        in_specs=[pl.BlockSpec((None, t, nq), lambda i: (i, 0, 0)),
                  pl.BlockSpec((None, t, nkv), lambda i: (i, 0, nq // nkv)),
                  pl.BlockSpec((None, t, nkv), lambda i: (i, 0, nq // nkv + 1)),
                  full((t, LANES)), full((t, LANES)), full((1, LANES)), full((1, LANES))],
        out_specs=[pl.BlockSpec((None, n_qt, nq, ATT_Q_TILE), lambda i: (i, 0, 0, 0)),
                   pl.BlockSpec((None, ATT_KV_HEADS, t, LANES), lambda i: (i, 0, 0, 0)),
                   pl.BlockSpec((None, ATT_KV_HEADS, ATT_VT_ROWS, t), lambda i: (i, 0, 0, 0))],
        out_shape=[jax.ShapeDtypeStruct((b, n_qt, nq, ATT_Q_TILE), BF16),
                   jax.ShapeDtypeStruct((b, ATT_KV_HEADS, t, LANES), BF16),
                   jax.ShapeDtypeStruct((b, ATT_KV_HEADS, ATT_VT_ROWS, t), BF16)],
        compiler_params=_params("parallel"),
        name="qk_norm_rope",
    )(u3, u3, u3, cos_e, sin_e, qn, kn)


def _attn_kernel(qt_ref, k_ref, vt_ref, o_ref):
    n_qt, w, tq = qt_ref.shape
    dh = ATT_HEAD_DIM
    n_real = k_ref.shape[0] - N_META
    eye = _eye(tq)
    zpad = jnp.zeros((LANES - dh, tq), BF16)

    kc = ATT_KEY_CHUNK
    chunks = [(slice(N_META + c * kc, N_META + (c + 1) * kc), slice(c * kc, (c + 1) * kc))
              for c in range(n_real // kc)]
    chunks.append((slice(0, N_META), slice(n_real, n_real + N_META)))

    def q_operand(i, g):
        return jnp.concatenate([qt_ref[i, g * dh:(g + 1) * dh, :], zpad], axis=0)

    steps = [(u, g) for u in range(ATT_TILES_PER_ITER) for g in range(ATT_GROUP)]

    def q_tiles(it, carry):
        i0 = it * ATT_TILES_PER_ITER
        outs = []
        qp = q_operand(i0, 0)
        cur = [_dot(k_ref[k_rows, :], qp) for k_rows, _ in chunks]
        for n, (u, g) in enumerate(steps):
            more = n + 1 < len(steps)
            if more:
                qp = q_operand(i0 + steps[n + 1][0], steps[n + 1][1])
            m = functools.reduce(jnp.maximum, [jnp.max(s, axis=0, keepdims=True) for s in cur])
            nxt = []
            ot = jnp.zeros((ATT_VT_ROWS, tq), F32)
            for (k_rows, v_cols), s in zip(chunks, cur):
                if more:
                    nxt.append(_dot(k_ref[k_rows, :], qp))
                ot = ot + _dot(vt_ref[:, v_cols], jnp.exp2(s - m).astype(BF16))
            cur = nxt
            outs.append((ot[0:dh] * (1.0 / ot[dh:dh + 1])).astype(BF16))
            if g == ATT_GROUP - 1:
                rows = pl.ds(pl.multiple_of((i0 + u) * tq, tq), tq)
                ot_all = jnp.concatenate(outs, axis=0)
                o_ref[rows, :] = lax.dot_general(eye, ot_all, _NT, preferred_element_type=F32).astype(o_ref.dtype)
                outs = []
        return carry

    lax.fori_loop(0, n_qt // ATT_TILES_PER_ITER, q_tiles, 0)


def _attention(qt, kp, vt):
    b, n_qt, nq, tq = qt.shape
    t = kp.shape[2]
    w = ATT_GROUP * ATT_HEAD_DIM
    return pl.pallas_call(
        _attn_kernel,
        grid=(b, ATT_KV_HEADS),
        in_specs=[pl.BlockSpec((None, n_qt, w, tq), lambda i, h: (i, 0, h, 0)),
                  pl.BlockSpec((None, None, t, LANES), lambda i, h: (i, h, 0, 0)),
                  pl.BlockSpec((None, None, ATT_VT_ROWS, t), lambda i, h: (i, h, 0, 0))],
        out_specs=pl.BlockSpec((None, n_qt * tq, w), lambda i, h: (i, 0, h)),
        out_shape=jax.ShapeDtypeStruct((b, n_qt * tq, nq), BF16),
        compiler_params=_params("parallel", "parallel"),
        name="gqa_attention",
    )(qt, kp, vt)


def _att_out_kernel(o_ref, w_ref, h_ref, out_ref):
    out_ref[...] = h_ref[N_META:, :] + _dot(o_ref[...], w_ref[...])


def _att_out(o, w, h3, tn):
    b, s_len, nq = o.shape
    t, d = h3.shape[1], h3.shape[2]
    return pl.pallas_call(
        _att_out_kernel,
        grid=(b, d // tn),
        in_specs=[pl.BlockSpec((None, s_len, nq), lambda i, j: (i, 0, 0)),
                  pl.BlockSpec((nq, tn), lambda i, j: (0, j)),
                  pl.BlockSpec((None, t, tn), lambda i, j: (i, 0, j))],
        out_specs=pl.BlockSpec((None, s_len, tn), lambda i, j: (i, 0, j)),
        out_shape=jax.ShapeDtypeStruct((b, s_len, d), F32),
        compiler_params=_params("parallel", "parallel"),
        name="att_out_res",
    )(o, w, h3)


def _rope_tables(n_tokens):
    rows_n = n_tokens // GRID_W
    rows = jnp.repeat(jnp.arange(rows_n), GRID_W).astype(F32)
    cols = jnp.tile(jnp.arange(GRID_W), rows_n).astype(F32)
    axis_dims = ATT_HEAD_DIM // 2
    freqs = ROPE_THETA ** (-jnp.arange(0, axis_dims, 2, dtype=F32) / axis_dims)
    ang = jnp.concatenate([rows[:, None] * freqs, cols[:, None] * freqs], axis=-1)
    ang = jnp.concatenate([jnp.zeros((N_META, ang.shape[1]), F32), ang], axis=0)
    cos_e = jnp.tile(jnp.repeat(jnp.cos(ang), 2, axis=1), (1, LANES // ATT_HEAD_DIM))
    sign = jnp.tile(jnp.array([-1.0, 1.0], F32), LANES // 2)
    sin_e = jnp.tile(jnp.repeat(jnp.sin(ang), 2, axis=1), (1, LANES // ATT_HEAD_DIM)) * sign
    return cos_e, sin_e


def kernel(x, meta_tokens, norm_mix, norm_ffn, norm_final, sc_w_in, sc_conv_w, gla_w_gate_fwd, gla_b_gate_fwd, gla_w_gate_bwd, gla_b_gate_bwd, gla_norm_w, sc_w_out, att_w_in, att_q_norm, att_k_norm, att_w_out, moe_w_group, moe_b_group, moe_w_expert, moe_b_expert, moe_w_gate, moe_w_up, moe_w_down):
    bsz, s_len, d = x.shape
    t = s_len + N_META
    n0, n1 = bsz * t, bsz * s_len
    tm0 = t // 3
    tm1 = 512

    meta = jnp.broadcast_to(meta_tokens.astype(x.dtype)[None], (bsz, N_META, d))
    h = jnp.concatenate([meta, x], axis=1).reshape(n0, d)

    nk = GLA_HEADS * GLA_DK
    mix_in = sc_w_in.shape[2]
    mix_pad = -mix_in % LANES
    w_in = jnp.pad(sc_w_in[0], ((0, 0), (0, mix_pad))).astype(BF16)
    u = _norm_matmul(h, norm_mix[0], w_in, t, 5 * LANES)
    u3 = u.reshape(bsz, t, mix_in + mix_pad)
    y_a = _gated_conv(u3, sc_conv_w[0])
    wc = jnp.zeros((LANES, 2 * nk), F32)
    wc = wc.at[:GLA_GATE_RANK, :nk].set(gla_w_gate_fwd[0])
    wc = wc.at[GLA_GATE_RANK:2 * GLA_GATE_RANK, nk:].set(gla_w_gate_bwd[0])
    bc = jnp.concatenate([gla_b_gate_fwd[0], gla_b_gate_bwd[0]]).reshape(1, 2 * nk)
    y_b = _gla(u3, wc.astype(BF16), bc, gla_norm_w[0].reshape(1, GLA_DV))
    w_out = sc_w_out[0].astype(BF16)
    h = _proj2_res(y_a.reshape(n0, SC_WIDTH), y_b.reshape(n0, -1), w_out[:SC_WIDTH], w_out[SC_WIDTH:], h, tm0)
    h = _moe_layer(h, norm_ffn[0], moe_w_group[0], moe_b_group[0], moe_w_expert[0], moe_b_expert[0],
                   moe_w_gate[0], moe_w_up[0], moe_w_down[0], norm_final, 3 * LANES, False)

    u = _norm_matmul(h, norm_mix[1], att_w_in[0].astype(BF16), t, 4 * LANES)
    u3 = u.reshape(bsz, t, -1)
    cos_e, sin_e = _rope_tables(s_len)
    qn = jnp.tile(att_q_norm[0], LANES // ATT_HEAD_DIM).reshape(1, LANES)
    kn = jnp.tile(att_k_norm[0], LANES // ATT_HEAD_DIM).reshape(1, LANES)
    qt, kp, vt = _rope(u3, cos_e, sin_e, qn, kn)
    o = _attention(qt, kp, vt)
    h = _att_out(o, att_w_out[0].astype(BF16), h.reshape(bsz, t, d), 4 * LANES).reshape(n1, d)
    h = _moe_layer(h, norm_ffn[1], moe_w_group[1], moe_b_group[1], moe_w_expert[1], moe_b_expert[1],
                   moe_w_gate[1], moe_w_up[1], moe_w_down[1], norm_final, tm1, True)
    return h.reshape(bsz, s_len, d)
```

```python
import functools

import jax
import jax.numpy as jnp
from jax import lax
from jax.experimental import pallas as pl
from jax.experimental.pallas import tpu as pltpu

F32 = jnp.float32
BF16 = jnp.bfloat16

NORM_EPS = 1e-6
N_META = 16
GRID_W = 64
ROPE_THETA = 10000.0
SC_WIDTH = 512
GLA_HEADS = 4
GLA_DK = 64
GLA_DV = 128
GLA_GATE_RANK = 16
GLA_GATE_TAU = 16.0
GLA_CHUNK = 64
ATT_HEAD_DIM = 64
ATT_HEADS = 16
ATT_KV_HEADS = 4
ATT_GROUP = ATT_HEADS // ATT_KV_HEADS
MOE_GROUPS = 4
MOE_EXPERTS_PER_GROUP = 8
N_EXPERTS = 32
MOE_D_FF = 256

LANES = 128
MXU_WIDTH = 256
MIB = 2 ** 20
VMEM_LIMIT = 56 * MIB

_NT = (((1,), (1,)), ((), ()))
_TN = (((0,), (0,)), ((), ()))


def _params(*semantics):
    return pltpu.CompilerParams(dimension_semantics=semantics, vmem_limit_bytes=VMEM_LIMIT)


def _dot(a, b):
    return jnp.dot(a, b, preferred_element_type=F32)


def _silu(x):
    return x * (1.0 / (1.0 + jnp.exp(-x)))


def _norm_matmul_kernel(h_ref, g_ref, w_ref, o_ref, xn_ref):
    @pl.when(pl.program_id(1) == 0)
    def _():
        x = h_ref[...]
        ms = jnp.mean(x * x, axis=-1, keepdims=True)
        xn_ref[...] = (x * lax.rsqrt(ms + NORM_EPS) * g_ref[...]).astype(BF16)

    o_ref[...] = _dot(xn_ref[...], w_ref[...]).astype(o_ref.dtype)


def _norm_matmul(h, g, w, tm, tn):
    n, d = h.shape
    nout = w.shape[1]
    return pl.pallas_call(
        _norm_matmul_kernel,
        grid=(n // tm, nout // tn),
        in_specs=[pl.BlockSpec((tm, d), lambda i, j: (i, 0)),
                  pl.BlockSpec((1, d), lambda i, j: (0, 0)),
                  pl.BlockSpec((d, tn), lambda i, j: (0, j))],
        out_specs=pl.BlockSpec((tm, tn), lambda i, j: (i, j)),
        out_shape=jax.ShapeDtypeStruct((n, nout), BF16),
        scratch_shapes=[pltpu.VMEM((tm, d), BF16)],
        compiler_params=_params("parallel", "arbitrary"),
        name="norm_matmul",
    )(h, g.reshape(1, d), w)


def _conv_kernel(ab_ref, ac_ref, ax_ref, w_ref, o_ref, z_ref):
    t, c = ab_ref.shape
    z = ac_ref[...].astype(F32) * ax_ref[...].astype(F32)
    z_ref[0:8, :] = jnp.zeros((8, c), F32)
    z_ref[8:8 + t, :] = z
    z_ref[8 + t:16 + t, :] = jnp.zeros((8, c), F32)
    w = w_ref[...]
    conv = w[0:1, :] * z_ref[7:7 + t, :] + w[1:2, :] * z + w[2:3, :] * z_ref[9:9 + t, :]
    o_ref[...] = (ab_ref[...].astype(F32) * conv).astype(o_ref.dtype)


def _gated_conv(u3, conv_w):
    b, t, _ = u3.shape
    nblk = SC_WIDTH // LANES

    def col(off):
        return pl.BlockSpec((None, t, LANES), lambda i, j: (i, 0, off + j))

    return pl.pallas_call(
        _conv_kernel,
        grid=(b, nblk),
        in_specs=[col(0), col(nblk), col(2 * nblk),
                  pl.BlockSpec((3, LANES), lambda i, j: (0, j))],
        out_specs=pl.BlockSpec((None, t, LANES), lambda i, j: (i, 0, j)),
        out_shape=jax.ShapeDtypeStruct((b, t, SC_WIDTH), BF16),
        scratch_shapes=[pltpu.VMEM((t + 16, LANES), F32)],
        compiler_params=_params("parallel", "parallel"),
        name="gated_conv",
    )(u3, u3, u3, conv_w)


GLA_ROW_TILE = 688
GLA_EPI_TILE = 48


def _gla_kernel(q_ref, k_ref, v_ref, r_ref, g_ref, wc_ref, bc_ref, nw_ref, y_ref,
                la_ref, of_ref, ob_ref, sf_ref, sb_ref):
    t, nk = q_ref.shape
    nv = v_ref.shape[1]
    c = GLA_CHUNK
    n_real = (t - N_META) // c

    for i in range(t // GLA_ROW_TILE):
        rows = pl.ds(i * GLA_ROW_TILE, GLA_ROW_TILE)
        pre = _dot(g_ref[rows, :], wc_ref[...]) + bc_ref[...]
        la_ref[rows, :] = (jnp.minimum(pre, 0.0) - jnp.log(1.0 + jnp.exp(-jnp.abs(pre)))) * (1.0 / GLA_GATE_TAU)

    def iota(shape, dim):
        return lax.broadcasted_iota(jnp.int32, shape, dim)

    tri_f = jnp.where(iota((c, c), 0) >= iota((c, c), 1), 1.0, 0.0).astype(BF16)
    tri_b = jnp.where(iota((c, c), 0) <= iota((c, c), 1), 1.0, 0.0).astype(BF16)
    keep_f = (iota((c, nk), 1) & (c - 1)) <= iota((c, nk), 0)
    keep_b = (iota((c, nk), 1) & (c - 1)) > iota((c, nk), 0)
    bd_k = (iota((nk, nk), 0) >> 6) == (iota((nk, nk), 1) >> 6)
    bd_v = (iota((nk, nv), 0) >> 6) == (iota((nk, nv), 1) >> 7)
    bd_s = (iota((nv, nk), 0) >> 7) == (iota((nv, nk), 1) >> 6)
    valid0 = iota((c, 1), 0) < N_META

    def cum_decay(rows, valid, tri, la_col):
        la = la_ref[rows, la_col:la_col + nk]
        if valid is not None:
            la = jnp.where(valid, la, 0.0)
        hi = la.astype(BF16)
        rem = la - hi.astype(F32)
        mid = rem.astype(BF16)
        lo = (rem - mid.astype(F32)).astype(BF16)
        return _dot(tri, hi) + _dot(tri, mid) + _dot(tri, lo)

    def chunk(rows, valid, cum, keep, tot_row, s_ref, o_ref, out_rows, n_out):
        q = q_ref[rows, :].astype(F32) * (GLA_DK ** -0.5)
        k = k_ref[rows, :].astype(F32)
        v = v_ref[rows, :].astype(F32)
        if valid is not None:
            q = jnp.where(valid, q, 0.0)
            k = jnp.where(valid, k, 0.0)
            v = jnp.where(valid, v, 0.0)
        tot = cum[tot_row:tot_row + 1, :]
        q_dec = (q * jnp.exp(cum)).astype(BF16)
        k_inv = k * jnp.exp(-cum)
        k_end = (k * jnp.exp(tot - cum)).astype(BF16)
        kbd = jnp.where(bd_k, jnp.concatenate([k_inv] * GLA_HEADS, axis=0), 0.0).astype(BF16)
        p = lax.dot_general(q_dec, kbd, _NT, preferred_element_type=F32)
        s_old = s_ref[...]
        o_inter = lax.dot_general(q_dec, s_old.astype(BF16), _NT, preferred_element_type=F32)
        kvt = lax.dot_general(v.astype(BF16), k_end, _TN, preferred_element_type=F32)
        yield
        p = jnp.where(keep, p, 0.0).astype(BF16)
        vbd = jnp.where(bd_v, jnp.concatenate([v] * GLA_HEADS, axis=0), 0.0).astype(BF16)
        o = _dot(p, vbd) + o_inter
        s_ref[...] = jnp.exp(tot) * s_old + jnp.where(bd_s, kvt, 0.0)
        o_ref[out_rows, :] = o[:n_out]

    def lockstep(*gens):
        live = list(gens)
        while live:
            nxt = []
            for gen in live:
                try:
                    next(gen)
                    nxt.append(gen)
                except StopIteration:
                    pass
            live = nxt

    sf_ref[...] = jnp.zeros(sf_ref.shape, F32)
    sb_ref[...] = jnp.zeros(sb_ref.shape, F32)
    meta_rows = pl.ds(0, c)
    lockstep(chunk(meta_rows, valid0, cum_decay(meta_rows, valid0, tri_f, 0), keep_f, c - 1,
                   sf_ref, of_ref, pl.ds(0, N_META), N_META))

    def rows_fwd(i):
        return pl.ds(pl.multiple_of(N_META + i * c, 16), c)

    def rows_bwd(i):
        return pl.ds(pl.multiple_of(N_META + (n_real - 1 - i) * c, 16), c)

    def body(i, cums):
        cum_f, cum_b = cums
        nxt = jnp.minimum(i + 1, n_real - 1)
        cums = (cum_decay(rows_fwd(nxt), None, tri_f, 0), cum_decay(rows_bwd(nxt), None, tri_b, nk))
        lockstep(chunk(rows_fwd(i), None, cum_f, keep_f, c - 1, sf_ref, of_ref, rows_fwd(i), c),
                 chunk(rows_bwd(i), None, cum_b, keep_b, 0, sb_ref, ob_ref, rows_bwd(i), c))
        return cums

    first = (cum_decay(rows_fwd(0), None, tri_f, 0), cum_decay(rows_bwd(0), None, tri_b, nk))
    lax.fori_loop(0, n_real, body, first)
    lockstep(chunk(meta_rows, valid0, cum_decay(meta_rows, valid0, tri_b, nk), keep_b, 0,
                   sb_ref, ob_ref, pl.ds(0, N_META), N_META))

    def epilogue(i, carry):
        rows = pl.ds(pl.multiple_of(i * GLA_EPI_TILE, 16), GLA_EPI_TILE)
        o = of_ref[rows, :] + ob_ref[rows, :]
        r = r_ref[rows, :].astype(F32)
        for h in range(GLA_HEADS):
            cols = slice(h * GLA_DV, (h + 1) * GLA_DV)
            oh = o[:, cols]
            ms = jnp.mean(oh * oh, axis=-1, keepdims=True)
            yh = oh * lax.rsqrt(ms + NORM_EPS) * nw_ref[...] * _silu(r[:, cols])
            y_ref[rows, cols] = yh.astype(y_ref.dtype)
        return carry

    lax.fori_loop(0, t // GLA_EPI_TILE, epilogue, 0)


def _gla(u3, wc, bc, norm_w):
    b, t, _ = u3.shape
    nk = GLA_HEADS * GLA_DK
    nv = GLA_HEADS * GLA_DV
    q_blk = 3 * SC_WIDTH // nk
    v_blk = (3 * SC_WIDTH + 2 * nk) // nv
    g_blk = (3 * SC_WIDTH + 2 * nk + 2 * nv) // LANES
    full = lambda shape: pl.BlockSpec(shape, lambda i: (0, 0))
    return pl.pallas_call(
        _gla_kernel,
        grid=(b,),
        in_specs=[pl.BlockSpec((None, t, nk), lambda i: (i, 0, q_blk)),
                  pl.BlockSpec((None, t, nk), lambda i: (i, 0, q_blk + 1)),
                  pl.BlockSpec((None, t, nv), lambda i: (i, 0, v_blk)),
                  pl.BlockSpec((None, t, nv), lambda i: (i, 0, v_blk + 1)),
                  pl.BlockSpec((None, t, LANES), lambda i: (i, 0, g_blk)),
                  full((LANES, 2 * nk)), full((1, 2 * nk)), full((1, GLA_DV))],
        out_specs=pl.BlockSpec((None, t, nv), lambda i: (i, 0, 0)),
        out_shape=jax.ShapeDtypeStruct((b, t, nv), BF16),
        scratch_shapes=[pltpu.VMEM((t, 2 * nk), F32),
                        pltpu.VMEM((t, nv), F32), pltpu.VMEM((t, nv), F32),
                        pltpu.VMEM((nv, nk), F32), pltpu.VMEM((nv, nk), F32)],
        compiler_params=_params("parallel"),
        name="gla",
    )(u3, u3, u3, u3, u3, wc, bc, norm_w)


def _proj2_res_kernel(a_ref, b_ref, wa_ref, wb_ref, res_ref, o_ref):
    o_ref[...] = res_ref[...] + _dot(a_ref[...], wa_ref[...]) + _dot(b_ref[...], wb_ref[...])


def _proj2_res(a, b, wa, wb, res, tm):
    n, d = res.shape
    ka, kb = a.shape[1], b.shape[1]
    return pl.pallas_call(
        _proj2_res_kernel,
        grid=(n // tm,),
        in_specs=[pl.BlockSpec((tm, ka), lambda i: (i, 0)),
                  pl.BlockSpec((tm, kb), lambda i: (i, 0)),
                  pl.BlockSpec((ka, d), lambda i: (0, 0)),
                  pl.BlockSpec((kb, d), lambda i: (0, 0)),
                  pl.BlockSpec((tm, d), lambda i: (i, 0))],
        out_specs=pl.BlockSpec((tm, d), lambda i: (i, 0)),
        out_shape=jax.ShapeDtypeStruct((n, d), F32),
        compiler_params=_params("parallel"),
        name="proj2_res",
    )(a, b, wa, wb, res)


GROUP_LANE0 = N_EXPERTS


def _router_kernel(h_ref, g_ref, whi_ref, wlo_ref, b_ref, xn_ref, gates_ref, info_ref, inforow_ref, count_ref):
    x = h_ref[...]
    ms = jnp.mean(x * x, axis=-1, keepdims=True)
    xn = x * lax.rsqrt(ms + NORM_EPS) * g_ref[...]
    xhi = xn.astype(BF16)
    xn_ref[...] = xhi
    xlo = (xn - xhi.astype(F32)).astype(BF16)
    logits = _dot(xhi, whi_ref[...]) + _dot(xhi, wlo_ref[...]) + _dot(xlo, whi_ref[...]) + b_ref[...]

    lane = lax.broadcasted_iota(jnp.int32, logits.shape, 1)
    neg = -jnp.inf
    big = jnp.int32(LANES)
    is_group = (lane >> 2) == (GROUP_LANE0 >> 2)
    gl = jnp.where(is_group, logits, neg)
    gmax = jnp.max(gl, axis=-1, keepdims=True)
    g_p = 1.0 / jnp.sum(jnp.exp(gl - gmax), axis=-1, keepdims=True)
    g_idx = jnp.min(jnp.where(gl == gmax, lane, big), axis=-1, keepdims=True) - GROUP_LANE0
    el = jnp.where((lane >> 3) == g_idx, logits, neg)
    m1 = jnp.max(el, axis=-1, keepdims=True)
    i1 = jnp.min(jnp.where(el == m1, lane, big), axis=-1, keepdims=True)
    el2 = jnp.where(lane == i1, neg, el)
    m2 = jnp.max(el2, axis=-1, keepdims=True)
    i2 = jnp.min(jnp.where(el2 == m2, lane, big), axis=-1, keepdims=True)
    e2 = jnp.exp(m2 - m1)
    w1 = g_p / (1.0 + e2)
    w2 = g_p * e2 / (1.0 + e2)
    gates_ref[...] = jnp.where(lane == i1, w1, 0.0) + jnp.where(lane == i2, w2, 0.0)

    tm = x.shape[0]
    onehot = jnp.where(lane == g_idx, 1.0, 0.0)
    r_i = lax.broadcasted_iota(jnp.int32, (tm, tm), 0)
    c_i = lax.broadcasted_iota(jnp.int32, (tm, tm), 1)
    before = jnp.where(r_i > c_i, 1.0, 0.0).astype(BF16)
    prefix = _dot(before, onehot.astype(BF16))
    rank = jnp.sum(jnp.where(lane == g_idx, prefix, 0.0), axis=-1, keepdims=True).astype(jnp.int32)
    info_ref[...] = jnp.where(lane == 0, g_idx, jnp.where(lane == 1, rank, 0))
    eye = r_i == c_i
    g_row = jnp.sum(jnp.where(eye, g_idx, 0), axis=0, keepdims=True)
    rank_row = jnp.sum(jnp.where(eye, rank, 0), axis=0, keepdims=True)
    sub = lax.broadcasted_iota(jnp.int32, inforow_ref.shape, 0)
    inforow_ref[...] = jnp.where(sub == 0, g_row, jnp.where(sub == 1, rank_row, 0))
    counts = jnp.sum(onehot, axis=0, keepdims=True).astype(jnp.int32)
    count_ref[...] = jnp.broadcast_to(counts, count_ref.shape)


def _router(h, g, whi, wlo, bias, tm):
    n, d = h.shape
    nb = n // tm
    return pl.pallas_call(
        _router_kernel,
        grid=(nb,),
        in_specs=[pl.BlockSpec((tm, d), lambda i: (i, 0)),
                  pl.BlockSpec((1, d), lambda i: (0, 0)),
                  pl.BlockSpec((d, LANES), lambda i: (0, 0)),
                  pl.BlockSpec((d, LANES), lambda i: (0, 0)),
                  pl.BlockSpec((1, LANES), lambda i: (0, 0))],
        out_specs=[pl.BlockSpec((tm, d), lambda i: (i, 0)),
                   pl.BlockSpec((tm, LANES), lambda i: (i, 0)),
                   pl.BlockSpec((tm, LANES), lambda i: (i, 0)),
                   pl.BlockSpec((None, 8, tm), lambda i: (i, 0, 0)),
                   pl.BlockSpec((None, 8, LANES), lambda i: (i, 0, 0))],
        out_shape=[jax.ShapeDtypeStruct((n, d), BF16),
                   jax.ShapeDtypeStruct((n, LANES), F32),
                   jax.ShapeDtypeStruct((n, LANES), jnp.int32),
                   jax.ShapeDtypeStruct((nb, 8, tm), jnp.int32),
                   jax.ShapeDtypeStruct((nb, 8, LANES), jnp.int32)],
        compiler_params=_params("parallel"),
        name="moe_router",
    )(h, g.reshape(1, d), whi, wlo, bias)


MOE_TILE = 512
SEG_ALIGN = 16
STRIP = 32


def _pow2_sizes(max_rows, min_rows):
    sizes, s = [], min_rows
    while s <= max_rows:
        sizes.append(s)
        s *= 2
    return sizes[::-1]


def _strip_copies(n_rows, sizes, make_copy):
    pos = 0
    for sz in sizes:
        bit = n_rows & sz
        yield bit, make_copy(pos, sz)
        pos = pos + bit


def _slot_key(group, rank, boff_ref, b):
    key = rank
    for g in range(MOE_GROUPS):
        key = key + jnp.where(group == g, boff_ref[MOE_GROUPS * b + g], 0)
    return key


def _start_all(copies):
    for bit, cs in copies:
        @pl.when(bit != 0)
        def _():
            for c in cs:
                c.start()


def _wait_all(copies):
    for bit, cs in copies:
        @pl.when(bit != 0)
        def _():
            for c in cs:
                c.wait()


def _dispatch_kernel(off_ref, cpad_ref, boff_ref, tail_ref, x_ref, gates_ref, inforow_ref,
                     xs_hbm, gs_hbm, xstg, gstg, zx, zg, sem):
    b = pl.program_id(0)
    last = pl.num_programs(0) - 1
    tb = x_ref.shape[0]
    ks = xstg.shape[1]
    cur = lax.rem(b, 2)
    key = _slot_key(inforow_ref[0:1, :], inforow_ref[1:2, :], boff_ref, b)
    slot = lax.broadcasted_iota(jnp.int32, (ks, tb), 0)
    perm = jnp.where(slot == key, 1.0, 0.0).astype(BF16)
    xstg[cur] = _dot(perm, x_ref[...]).astype(BF16)
    gt = gates_ref[...]
    hi = gt.astype(BF16)
    lo = (gt - hi.astype(F32)).astype(BF16)
    gstg[cur] = _dot(perm, hi) + _dot(perm, lo)

    sizes = _pow2_sizes(tb, SEG_ALIGN)

    def copies(blk, sl):
        for g in range(MOE_GROUPS):
            src0 = boff_ref[MOE_GROUPS * blk + g]
            dst0 = off_ref[MOE_GROUPS * blk + g]

            def make(pos, sz, src0=src0, dst0=dst0):
                src = pl.ds(pl.multiple_of(src0 + pos, SEG_ALIGN), sz)
                dst = pl.ds(pl.multiple_of(dst0 + pos, SEG_ALIGN), sz)
                return (pltpu.make_async_copy(xstg.at[sl, src, :], xs_hbm.at[dst, :], sem.at[sl, 0]),
                        pltpu.make_async_copy(gstg.at[sl, src, :], gs_hbm.at[dst, :], sem.at[sl, 1]))

            yield from _strip_copies(cpad_ref[MOE_GROUPS * blk + g], sizes, make)

    _start_all(copies(b, cur))

    @pl.when(b > 0)
    def _():
        _wait_all(copies(jnp.maximum(b - 1, 0), 1 - cur))

    @pl.when(b == last)
    def _():
        _wait_all(copies(b, cur))
        zx[...] = jnp.zeros(zx.shape, BF16)
        zg[...] = jnp.zeros(zg.shape, F32)
        tail_sizes = _pow2_sizes(MOE_TILE // 2, SEG_ALIGN)

        def tails():
            for g in range(MOE_GROUPS):
                dst0 = tail_ref[g]

                def make(pos, sz, dst0=dst0):
                    dst = pl.ds(pl.multiple_of(dst0 + pos, SEG_ALIGN), sz)
                    return (pltpu.make_async_copy(zx.at[pl.ds(0, sz), :], xs_hbm.at[dst, :], sem.at[0, 0]),
                            pltpu.make_async_copy(zg.at[pl.ds(0, sz), :], gs_hbm.at[dst, :], sem.at[0, 1]))

                yield from _strip_copies(tail_ref[MOE_GROUPS + g], tail_sizes, make)

        _start_all(tails())
        _wait_all(tails())

        def spare(i):
            dst = pl.ds(pl.multiple_of(i * MOE_TILE, MOE_TILE), MOE_TILE)
            return (pltpu.make_async_copy(zx, xs_hbm.at[dst, :], sem.at[0, 0]),
                    pltpu.make_async_copy(zg, gs_hbm.at[dst, :], sem.at[0, 1]))

        first_spare = tail_ref[2 * MOE_GROUPS]
        n_tiles = xs_hbm.shape[0] // MOE_TILE

        @pl.loop(first_spare, n_tiles)
        def _(i):
            cx, cg = spare(i)
            cx.start()
            cg.start()

        @pl.loop(first_spare, n_tiles)
        def _(i):
            cx, cg = spare(i)
            cx.wait()
            cg.wait()


def _dispatch(xn, gates, inforow, off, cpad, boff, tail, n_sorted, tb, ks):
    n, d = xn.shape
    nb = n // tb
    grid_spec = pltpu.PrefetchScalarGridSpec(
        num_scalar_prefetch=4,
        grid=(nb,),
        in_specs=[pl.BlockSpec((tb, d), lambda i, *_: (i, 0)),
                  pl.BlockSpec((tb, LANES), lambda i, *_: (i, 0)),
                  pl.BlockSpec((None, 8, tb), lambda i, *_: (i, 0, 0))],
        out_specs=[pl.BlockSpec(memory_space=pl.ANY), pl.BlockSpec(memory_space=pl.ANY)],
        scratch_shapes=[pltpu.VMEM((2, ks, d), BF16), pltpu.VMEM((2, ks, LANES), F32),
                        pltpu.VMEM((MOE_TILE, d), BF16), pltpu.VMEM((MOE_TILE, LANES), F32),
                        pltpu.SemaphoreType.DMA((2, 2))])
    return pl.pallas_call(
        _dispatch_kernel,
        grid_spec=grid_spec,
        out_shape=[jax.ShapeDtypeStruct((n_sorted, d), BF16),
                   jax.ShapeDtypeStruct((n_sorted, LANES), F32)],
        compiler_params=_params("arbitrary"),
        name="moe_dispatch",
    )(off, cpad, boff, tail, xn, gates, inforow)


def _group_mlp_kernel(tg_ref, nused_ref, x_ref, gs_ref, wg_ref, wu_ref, wd_ref, y_ref):
    i = pl.program_id(0)

    @pl.when(i < nused_ref[0])
    def _():
        x = x_ref[...]
        gs = gs_ref[...]
        lane = lax.broadcasted_iota(jnp.int32, gs.shape, 1)
        lane0 = tg_ref[i] * MOE_EXPERTS_PER_GROUP
        acc = jnp.zeros(y_ref.shape, F32)
        for e in range(MOE_EXPERTS_PER_GROUP):
            he = _silu(_dot(x, wg_ref[e])) * _dot(x, wu_ref[e])
            gate = jnp.sum(jnp.where(lane == lane0 + e, gs, 0.0), axis=-1, keepdims=True)
            acc = acc + _dot((he * gate).astype(BF16), wd_ref[e])
        y_ref[...] = acc.astype(y_ref.dtype)

    @pl.when(i >= nused_ref[0])
    def _():
        y_ref[...] = jnp.zeros(y_ref.shape, y_ref.dtype)


def _cast_kernel(w_ref, o_ref):
    o_ref[...] = w_ref[...].astype(o_ref.dtype)


def _cast_experts(w):
    nl, ne, a, b = w.shape
    blk = (None, MOE_EXPERTS_PER_GROUP, a, b)
    return pl.pallas_call(
        _cast_kernel,
        grid=(nl, ne // MOE_EXPERTS_PER_GROUP),
        in_specs=[pl.BlockSpec(blk, lambda l, g: (l, g, 0, 0))],
        out_specs=pl.BlockSpec(blk, lambda l, g: (l, g, 0, 0)),
        out_shape=jax.ShapeDtypeStruct(w.shape, BF16),
        compiler_params=_params("parallel", "parallel"),
        name="cast_experts",
    )(w)


def _group_mlp(xs, gs, wg, wu, wd, layer, tile_group, nused):
    n_sorted, d = xs.shape
    ne = MOE_EXPERTS_PER_GROUP

    def rows(i, tg, nu):
        return (jnp.minimum(i, nu[0] - 1), 0)

    def group(i, tg, nu):
        return (layer, tg[i], 0, 0)

    grid_spec = pltpu.PrefetchScalarGridSpec(
        num_scalar_prefetch=2,
        grid=(n_sorted // MOE_TILE,),
        in_specs=[pl.BlockSpec((MOE_TILE, d), rows),
                  pl.BlockSpec((MOE_TILE, LANES), rows),
                  pl.BlockSpec((None, ne, d, MOE_D_FF), group),
                  pl.BlockSpec((None, ne, d, MOE_D_FF), group),
                  pl.BlockSpec((None, ne, MOE_D_FF, d), group)],
        out_specs=pl.BlockSpec((MOE_TILE, d), lambda i, tg, nu: (i, 0)))
    return pl.pallas_call(
        _group_mlp_kernel,
        grid_spec=grid_spec,
        out_shape=jax.ShapeDtypeStruct((n_sorted, d), BF16),
        compiler_params=_params("arbitrary"),
        name="moe_experts",
    )(tile_group, nused, xs, gs, wg, wu, wd)


def _combine_kernel(off_ref, cpad_ref, boff_ref, info_ref, res_ref, fg_ref, y_hbm, o_ref, ybuf, sem,
                    *, final_norm):
    b = pl.program_id(0)
    last = pl.num_programs(0) - 1
    tb = res_ref.shape[0]
    ks = ybuf.shape[1]
    cur = lax.rem(b, 2)
    sizes = _pow2_sizes(tb, STRIP)

    def copies(blk, sl):
        for g in range(MOE_GROUPS):
            src0 = off_ref[MOE_GROUPS * blk + g]
            dst0 = boff_ref[MOE_GROUPS * blk + g]

            def make(pos, sz, src0=src0, dst0=dst0):
                src = pl.ds(pl.multiple_of(src0 + pos, SEG_ALIGN), sz)
                dst = pl.ds(pl.multiple_of(dst0 + pos, STRIP), sz)
                return (pltpu.make_async_copy(y_hbm.at[src, :], ybuf.at[sl, dst, :], sem.at[sl]),)

            n_rows = (cpad_ref[MOE_GROUPS * blk + g] + (STRIP - 1)) & (-STRIP)
            yield from _strip_copies(n_rows, sizes, make)

    @pl.when(b == 0)
    def _():
        ybuf[...] = jnp.zeros(ybuf.shape, ybuf.dtype)
        _start_all(copies(b, cur))

    @pl.when(b < last)
    def _():
        _start_all(copies(jnp.minimum(b + 1, last), 1 - cur))

    _wait_all(copies(b, cur))

    key = _slot_key(info_ref[:, 0:1], info_ref[:, 1:2], boff_ref, b)
    slot = lax.broadcasted_iota(jnp.int32, (tb, ks), 1)
    perm = jnp.where(slot == key, 1.0, 0.0).astype(BF16)
    y = res_ref[...] + _dot(perm, ybuf[cur])
    if final_norm:
        ms = jnp.mean(y * y, axis=-1, keepdims=True)
        y = y * lax.rsqrt(ms + NORM_EPS) * fg_ref[...]
    o_ref[...] = y


def _combine(ys, info, res, final_g, off, cpad, boff, tb, ks, final_norm):
    n, d = res.shape
    grid_spec = pltpu.PrefetchScalarGridSpec(
        num_scalar_prefetch=3,
        grid=(n // tb,),
        in_specs=[pl.BlockSpec((tb, LANES), lambda i, *_: (i, 0)),
                  pl.BlockSpec((tb, d), lambda i, *_: (i, 0)),
                  pl.BlockSpec((1, d), lambda i, *_: (0, 0)),
                  pl.BlockSpec(memory_space=pl.ANY)],
        out_specs=pl.BlockSpec((tb, d), lambda i, *_: (i, 0)),
        scratch_shapes=[pltpu.VMEM((2, ks, d), BF16), pltpu.SemaphoreType.DMA((2,))])
    return pl.pallas_call(
        functools.partial(_combine_kernel, final_norm=final_norm),
        grid_spec=grid_spec,
        out_shape=jax.ShapeDtypeStruct((n, d), F32),
        compiler_params=_params("arbitrary"),
        name="moe_combine",
    )(off, cpad, boff, info, res, final_g.reshape(1, d), ys)


def _round_up(x, m):
    return (x + m - 1) // m * m


def _moe_layer(h, norm_g, w_group, b_group, w_expert, b_expert, w_gate, w_up, w_down, layer,
               final_g, tb, final_norm):
    n, d = h.shape
    nb = n // tb
    pad = LANES - N_EXPERTS - MOE_GROUPS
    wr = jnp.concatenate([w_expert, w_group, jnp.zeros((d, pad), F32)], axis=1)
    whi = wr.astype(BF16)
    wlo = (wr - whi.astype(F32)).astype(BF16)
    bias = jnp.concatenate([b_expert, b_group, jnp.zeros((pad,), F32)]).reshape(1, LANES)
    xn, gates, info, inforow, counts = _router(h, norm_g, whi, wlo, bias, tb)

    cnt = counts[:, 0, :MOE_GROUPS]
    cpad = _round_up(cnt, SEG_ALIGN)
    strip = _round_up(cnt, STRIP)
    used = jnp.sum(cpad, axis=0)
    region = _round_up(used, MOE_TILE)
    region_start = jnp.cumsum(region) - region
    off = region_start[None, :] + jnp.cumsum(cpad, axis=0) - cpad
    boff = jnp.cumsum(strip, axis=1) - strip
    nused = (jnp.sum(region) // MOE_TILE).reshape(1)
    tail = jnp.concatenate([region_start + used, region - used, nused]).astype(jnp.int32)
    n_tiles = -(-(n + nb * MOE_GROUPS * (SEG_ALIGN - 1) + MOE_GROUPS * (MOE_TILE - 1)) // MOE_TILE) + 1
    tile_start = jnp.arange(n_tiles, dtype=jnp.int32) * MOE_TILE
    region_end = region_start + region
    tile_group = jnp.minimum(jnp.sum(tile_start[:, None] >= region_end[None, :], axis=1), MOE_GROUPS - 1)
    ks =tb + MOE_GROUPS * STRIP

    flat = lambda a: a.reshape(-1).astype(jnp.int32)
    xs, gs = _dispatch(xn, gates, inforow, flat(off), flat(cpad), flat(boff), tail,
                       n_tiles * MOE_TILE, tb, ks)
    ys = _group_mlp(xs, gs, w_gate, w_up, w_down, layer,
                    tile_group.astype(jnp.int32), nused.astype(jnp.int32))
    return _combine(ys, info, h, final_g, flat(off), flat(cpad), flat(boff), tb, ks, final_norm)


ROPE_ROW_TILE = 688
ATT_Q_TILE = 256
ATT_TILES_PER_ITER = 2
ATT_KEY_CHUNK = 512
ATT_VT_ROWS = ATT_HEAD_DIM + 16
ATT_Q_SCALE = ATT_HEAD_DIM ** -0.5 * 1.4426950408889634


def _eye(n):
    return jnp.where(lax.broadcasted_iota(jnp.int32, (n, n), 0) == lax.broadcasted_iota(jnp.int32, (n, n), 1),
                     1.0, 0.0).astype(BF16)


def _transpose_bf16(x, eye):
    return lax.dot_general(eye, x, _NT, preferred_element_type=F32).astype(BF16)


def _rope_kernel(q_ref, k_ref, v_ref, cos_ref, sin_ref, qn_ref, kn_ref, qt_ref, kp_ref, vt_ref):
    t = q_ref.shape[0]
    tq = ATT_Q_TILE
    dh = ATT_HEAD_DIM
    gi = lax.broadcasted_iota(jnp.int32, (LANES, LANES), 0) >> 6
    gj = lax.broadcasted_iota(jnp.int32, (LANES, LANES), 1) >> 6
    gmat = jnp.where(gi == gj, 1.0, 0.0).astype(BF16)
    eye_q = _eye(LANES)

    def norm_rope(x, gain, cos, sin):
        lane = lax.broadcasted_iota(jnp.int32, x.shape, 1)
        sq = x * x
        hi = sq.astype(BF16)
        lo = (sq - hi.astype(F32)).astype(BF16)
        ms = (_dot(hi, gmat) + _dot(lo, gmat)) * (1.0 / dh)
        xn = x * lax.rsqrt(ms + NORM_EPS) * gain
        partner = jnp.where((lane & 1) == 0, pltpu.roll(xn, LANES - 1, 1), pltpu.roll(xn, 1, 1))
        return xn * cos + partner * sin

    for j in range(q_ref.shape[1] // LANES):
        cols = slice(j * LANES, (j + 1) * LANES)
        y = norm_rope(q_ref[N_META:, cols].astype(F32), qn_ref[...], cos_ref[N_META:, :], sin_ref[N_META:, :])
        yt = _transpose_bf16((y * ATT_Q_SCALE).astype(BF16), eye_q)
        for i in range((t - N_META) // tq):
            qt_ref[i, cols, :] = yt[:, i * tq:(i + 1) * tq]

    low = lax.broadcasted_iota(jnp.int32, (t, LANES), 1) < dh
    for j in range(k_ref.shape[1] // LANES):
        cols = slice(j * LANES, (j + 1) * LANES)
        y = norm_rope(k_ref[:, cols].astype(F32), kn_ref[...], cos_ref[...], sin_ref[...])
        kp_ref[2 * j] = jnp.where(low, y, 0.0).astype(BF16)
        kp_ref[2 * j + 1] = jnp.where(low, pltpu.roll(y, dh, 1), 0.0).astype(BF16)

    n_real = t - N_META
    eye_v = _eye(v_ref.shape[1])
    vt_real = _transpose_bf16(v_ref[N_META:, :], eye_v)
    vt_meta = _transpose_bf16(v_ref[0:N_META, :], eye_v)
    for h in range(ATT_KV_HEADS):
        vt_ref[h, 0:dh, 0:n_real] = vt_real[h * dh:(h + 1) * dh, :]
        vt_ref[h, 0:dh, n_real:t] = vt_meta[h * dh:(h + 1) * dh, :]
        extra = lax.broadcasted_iota(jnp.int32, (ATT_VT_ROWS - dh, t), 0)
        vt_ref[h, dh:ATT_VT_ROWS, :] = jnp.where(extra == 0, 1.0, 0.0).astype(BF16)


def _rope(u3, cos_e, sin_e, qn, kn):
    b, t, _ = u3.shape
    nq = ATT_HEADS * ATT_HEAD_DIM
    nkv = ATT_KV_HEADS * ATT_HEAD_DIM
    n_qt = (t - N_META) // ATT_Q_TILE
    full = lambda shape: pl.BlockSpec(shape, lambda i: (0, 0))
    return pl.pallas_call(
        _rope_kernel,
        grid=(b,),
        in_specs=[pl.BlockSpec((None, t, nq), lambda i: (i, 0, 0)),
                  pl.BlockSpec((None, t, nkv), lambda i: (i, 0, nq // nkv)),
                  pl.BlockSpec((None, t, nkv), lambda i: (i, 0, nq // nkv + 1)),
                  full((t, LANES)), full((t, LANES)), full((1, LANES)), full((1, LANES))],
        out_specs=[pl.BlockSpec((None, n_qt, nq, ATT_Q_TILE), lambda i: (i, 0, 0, 0)),
                   pl.BlockSpec((None, ATT_KV_HEADS, t, LANES), lambda i: (i, 0, 0, 0)),
                   pl.BlockSpec((None, ATT_KV_HEADS, ATT_VT_ROWS, t), lambda i: (i, 0, 0, 0))],
        out_shape=[jax.ShapeDtypeStruct((b, n_qt, nq, ATT_Q_TILE), BF16),
                   jax.ShapeDtypeStruct((b, ATT_KV_HEADS, t, LANES), BF16),
                   jax.ShapeDtypeStruct((b, ATT_KV_HEADS, ATT_VT_ROWS, t), BF16)],
        compiler_params=_params("parallel"),
        name="qk_norm_rope",
    )(u3, u3, u3, cos_e, sin_e, qn, kn)


def _attn_kernel(qt_ref, k_ref, vt_ref, o_ref):
    n_qt, w, tq = qt_ref.shape
    dh = ATT_HEAD_DIM
    n_real = k_ref.shape[0] - N_META
    eye = _eye(tq)
    zpad = jnp.zeros((LANES - dh, tq), BF16)

    kc = ATT_KEY_CHUNK
    chunks = [(slice(N_META + c * kc, N_META + (c + 1) * kc), slice(c * kc, (c + 1) * kc))
              for c in range(n_real // kc)]
    chunks.append((slice(0, N_META), slice(n_real, n_real + N_META)))

    def q_operand(i, g):
        return jnp.concatenate([qt_ref[i, g * dh:(g + 1) * dh, :], zpad], axis=0)

    steps = [(u, g) for u in range(ATT_TILES_PER_ITER) for g in range(ATT_GROUP)]

    def q_tiles(it, carry):
        i0 = it * ATT_TILES_PER_ITER
        outs = []
        qp = q_operand(i0, 0)
        cur = [_dot(k_ref[k_rows, :], qp) for k_rows, _ in chunks]
        for n, (u, g) in enumerate(steps):
            more = n + 1 < len(steps)
            if more:
                qp = q_operand(i0 + steps[n + 1][0], steps[n + 1][1])
            m = functools.reduce(jnp.maximum, [jnp.max(s, axis=0, keepdims=True) for s in cur])
            nxt = []
            ot = jnp.zeros((ATT_VT_ROWS, tq), F32)
            for (k_rows, v_cols), s in zip(chunks, cur):
                if more:
                    nxt.append(_dot(k_ref[k_rows, :], qp))
                ot = ot + _dot(vt_ref[:, v_cols], jnp.exp2(s - m).astype(BF16))
            cur = nxt
            outs.append((ot[0:dh] * (1.0 / ot[dh:dh + 1])).astype(BF16))
            if g == ATT_GROUP - 1:
                rows = pl.ds(pl.multiple_of((i0 + u) * tq, tq), tq)
                ot_all = jnp.concatenate(outs, axis=0)
                o_ref[rows, :] = lax.dot_general(eye, ot_all, _NT, preferred_element_type=F32).astype(o_ref.dtype)
                outs = []
        return carry

    lax.fori_loop(0, n_qt // ATT_TILES_PER_ITER, q_tiles, 0)


def _attention(qt, kp, vt):
    b, n_qt, nq, tq = qt.shape
    t = kp.shape[2]
    w = ATT_GROUP * ATT_HEAD_DIM
    return pl.pallas_call(
        _attn_kernel,
        grid=(b, ATT_KV_HEADS),
        in_specs=[pl.BlockSpec((None, n_qt, w, tq), lambda i, h: (i, 0, h, 0)),
                  pl.BlockSpec((None, None, t, LANES), lambda i, h: (i, h, 0, 0)),
                  pl.BlockSpec((None, None, ATT_VT_ROWS, t), lambda i, h: (i, h, 0, 0))],
        out_specs=pl.BlockSpec((None, n_qt * tq, w), lambda i, h: (i, 0, h)),
        out_shape=jax.ShapeDtypeStruct((b, n_qt * tq, nq), BF16),
        compiler_params=_params("parallel", "parallel"),
        name="gqa_attention",
    )(qt, kp, vt)


def _att_out_kernel(o_ref, w_ref, h_ref, out_ref):
    out_ref[...] = h_ref[N_META:, :] + _dot(o_ref[...], w_ref[...])


def _att_out(o, w, h3, tn):
    b, s_len, nq = o.shape
    t, d = h3.shape[1], h3.shape[2]
    return pl.pallas_call(
        _att_out_kernel,
        grid=(b, d // tn),
        in_specs=[pl.BlockSpec((None, s_len, nq), lambda i, j: (i, 0, 0)),
                  pl.BlockSpec((nq, tn), lambda i, j: (0, j)),
                  pl.BlockSpec((None, t, tn), lambda i, j: (i, 0, j))],
        out_specs=pl.BlockSpec((None, s_len, tn), lambda i, j: (i, 0, j)),
        out_shape=jax.ShapeDtypeStruct((b, s_len, d), F32),
        compiler_params=_params("parallel", "parallel"),
        name="att_out_res",
    )(o, w, h3)


def _rope_tables(n_tokens):
    rows_n = n_tokens // GRID_W
    rows = jnp.repeat(jnp.arange(rows_n), GRID_W).astype(F32)
    cols = jnp.tile(jnp.arange(GRID_W), rows_n).astype(F32)
    axis_dims = ATT_HEAD_DIM // 2
    freqs = ROPE_THETA ** (-jnp.arange(0, axis_dims, 2, dtype=F32) / axis_dims)
    ang = jnp.concatenate([rows[:, None] * freqs, cols[:, None] * freqs], axis=-1)
    ang = jnp.concatenate([jnp.zeros((N_META, ang.shape[1]), F32), ang], axis=0)
    cos_e = jnp.tile(jnp.repeat(jnp.cos(ang), 2, axis=1), (1, LANES // ATT_HEAD_DIM))
    sign = jnp.tile(jnp.array([-1.0, 1.0], F32), LANES // 2)
    sin_e = jnp.tile(jnp.repeat(jnp.sin(ang), 2, axis=1), (1, LANES // ATT_HEAD_DIM)) * sign
    return cos_e, sin_e


def kernel(x, meta_tokens, norm_mix, norm_ffn, norm_final, sc_w_in, sc_conv_w, gla_w_gate_fwd, gla_b_gate_fwd, gla_w_gate_bwd, gla_b_gate_bwd, gla_norm_w, sc_w_out, att_w_in, att_q_norm, att_k_norm, att_w_out, moe_w_group, moe_b_group, moe_w_expert, moe_b_expert, moe_w_gate, moe_w_up, moe_w_down):
    bsz, s_len, d = x.shape
    t = s_len + N_META
    n0, n1 = bsz * t, bsz * s_len
    tm0 = t // 3
    tm1 = 512

    w_gate, w_up, w_down = _cast_experts(moe_w_gate), _cast_experts(moe_w_up), _cast_experts(moe_w_down)
    meta = jnp.broadcast_to(meta_tokens.astype(x.dtype)[None], (bsz, N_META, d))
    h = jnp.concatenate([meta, x], axis=1).reshape(n0, d)

    nk = GLA_HEADS * GLA_DK
    mix_in = sc_w_in.shape[2]
    mix_pad = -mix_in % MXU_WIDTH
    w_in = jnp.pad(sc_w_in[0], ((0, 0), (0, mix_pad))).astype(BF16)
    u = _norm_matmul(h, norm_mix[0], w_in, tm0, mix_in + mix_pad)
    u3 = u.reshape(bsz, t, mix_in + mix_pad)
    y_a = _gated_conv(u3, sc_conv_w[0])
    wc = jnp.zeros((LANES, 2 * nk), F32)
    wc = wc.at[:GLA_GATE_RANK, :nk].set(gla_w_gate_fwd[0])
    wc = wc.at[GLA_GATE_RANK:2 * GLA_GATE_RANK, nk:].set(gla_w_gate_bwd[0])
    bc = jnp.concatenate([gla_b_gate_fwd[0], gla_b_gate_bwd[0]]).reshape(1, 2 * nk)
    y_b = _gla(u3, wc.astype(BF16), bc, gla_norm_w[0].reshape(1, GLA_DV))
    w_out = sc_w_out[0].astype(BF16)
    h = _proj2_res(y_a.reshape(n0, SC_WIDTH), y_b.reshape(n0, -1), w_out[:SC_WIDTH], w_out[SC_WIDTH:], h, tm0)
    h = _moe_layer(h, norm_ffn[0], moe_w_group[0], moe_b_group[0], moe_w_expert[0], moe_b_expert[0],
                   w_gate, w_up, w_down, 0, norm_final, 3 * LANES, False)

    u = _norm_matmul(h, norm_mix[1], att_w_in[0].astype(BF16), tm0, att_w_in.shape[2])
    u3 = u.reshape(bsz, t, -1)
    cos_e, sin_e = _rope_tables(s_len)
    qn = jnp.tile(att_q_norm[0], LANES // ATT_HEAD_DIM).reshape(1, LANES)
    kn = jnp.tile(att_k_norm[0], LANES // ATT_HEAD_DIM).reshape(1, LANES)
    qt, kp, vt = _rope(u3, cos_e, sin_e, qn, kn)
    o = _attention(qt, kp, vt)
    h = _att_out(o, att_w_out[0].astype(BF16), h.reshape(bsz, t, d), 4 * LANES).reshape(n1, d)
    h = _moe_layer(h, norm_ffn[1], moe_w_group[1], moe_b_group[1], moe_w_expert[1], moe_b_expert[1],
                   w_gate, w_up, w_down, 1, norm_final, tm1, True)
    return h.reshape(bsz, s_len, d)
```

```python
import functools

import jax
import jax.numpy as jnp
from jax import lax
from jax.experimental import pallas as pl
from jax.experimental.pallas import tpu as pltpu

F32 = jnp.float32
BF16 = jnp.bfloat16

NORM_EPS = 1e-6
N_META = 16
GRID_W = 64
ROPE_THETA = 10000.0
SC_WIDTH = 512
GLA_HEADS = 4
GLA_DK = 64
GLA_DV = 128
GLA_GATE_RANK = 16
GLA_GATE_TAU = 16.0
GLA_CHUNK = 64
ATT_HEAD_DIM = 64
ATT_HEADS = 16
ATT_KV_HEADS = 4
ATT_GROUP = ATT_HEADS // ATT_KV_HEADS
MOE_GROUPS = 4
MOE_EXPERTS_PER_GROUP = 8
N_EXPERTS = 32
MOE_D_FF = 256

LANES = 128
MXU_WIDTH = 256
MIB = 2 ** 20
VMEM_LIMIT = 56 * MIB

_NT = (((1,), (1,)), ((), ()))
_TN = (((0,), (0,)), ((), ()))


def _params(*semantics):
    return pltpu.CompilerParams(dimension_semantics=semantics, vmem_limit_bytes=VMEM_LIMIT)


def _dot(a, b):
    return jnp.dot(a, b, preferred_element_type=F32)


def _silu(x):
    return x * (1.0 / (1.0 + jnp.exp(-x)))


def _norm_matmul_kernel(h_ref, g_ref, w_ref, o_ref, xn_ref):
    @pl.when(pl.program_id(1) == 0)
    def _():
        x = h_ref[...]
        ms = jnp.mean(x * x, axis=-1, keepdims=True)
        xn_ref[...] = (x * lax.rsqrt(ms + NORM_EPS) * g_ref[...]).astype(BF16)

    o_ref[...] = _dot(xn_ref[...], w_ref[...]).astype(o_ref.dtype)


def _norm_matmul(h, g, w, tm, tn):
    n, d = h.shape
    nout = w.shape[1]
    return pl.pallas_call(
        _norm_matmul_kernel,
        grid=(n // tm, nout // tn),
        in_specs=[pl.BlockSpec((tm, d), lambda i, j: (i, 0)),
                  pl.BlockSpec((1, d), lambda i, j: (0, 0)),
                  pl.BlockSpec((d, tn), lambda i, j: (0, j))],
        out_specs=pl.BlockSpec((tm, tn), lambda i, j: (i, j)),
        out_shape=jax.ShapeDtypeStruct((n, nout), BF16),
        scratch_shapes=[pltpu.VMEM((tm, d), BF16)],
        compiler_params=_params("parallel", "arbitrary"),
        name="norm_matmul",
    )(h, g.reshape(1, d), w)


def _conv_kernel(ab_ref, ac_ref, ax_ref, w_ref, o_ref, z_ref):
    t, c = ab_ref.shape
    z = ac_ref[...].astype(F32) * ax_ref[...].astype(F32)
    z_ref[0:8, :] = jnp.zeros((8, c), F32)
    z_ref[8:8 + t, :] = z
    z_ref[8 + t:16 + t, :] = jnp.zeros((8, c), F32)
    w = w_ref[...]
    conv = w[0:1, :] * z_ref[7:7 + t, :] + w[1:2, :] * z + w[2:3, :] * z_ref[9:9 + t, :]
    o_ref[...] = (ab_ref[...].astype(F32) * conv).astype(o_ref.dtype)


def _gated_conv(u3, conv_w):
    b, t, _ = u3.shape
    nblk = SC_WIDTH // LANES

    def col(off):
        return pl.BlockSpec((None, t, LANES), lambda i, j: (i, 0, off + j))

    return pl.pallas_call(
        _conv_kernel,
        grid=(b, nblk),
        in_specs=[col(0), col(nblk), col(2 * nblk),
                  pl.BlockSpec((3, LANES), lambda i, j: (0, j))],
        out_specs=pl.BlockSpec((None, t, LANES), lambda i, j: (i, 0, j)),
        out_shape=jax.ShapeDtypeStruct((b, t, SC_WIDTH), BF16),
        scratch_shapes=[pltpu.VMEM((t + 16, LANES), F32)],
        compiler_params=_params("parallel", "parallel"),
        name="gated_conv",
    )(u3, u3, u3, conv_w)


GLA_ROW_TILE = 688
GLA_EPI_TILE = 48


def _gla_kernel(q_ref, k_ref, v_ref, r_ref, g_ref, wc_ref, bc_ref, nw_ref, y_ref,
                la_ref, of_ref, ob_ref, sf_ref, sb_ref):
    t, nk = q_ref.shape
    nv = v_ref.shape[1]
    c = GLA_CHUNK
    n_real = (t - N_META) // c

    for i in range(t // GLA_ROW_TILE):
        rows = pl.ds(i * GLA_ROW_TILE, GLA_ROW_TILE)
        pre = _dot(g_ref[rows, :], wc_ref[...]) + bc_ref[...]
        la_ref[rows, :] = (jnp.minimum(pre, 0.0) - jnp.log(1.0 + jnp.exp(-jnp.abs(pre)))) * (1.0 / GLA_GATE_TAU)

    def iota(shape, dim):
        return lax.broadcasted_iota(jnp.int32, shape, dim)

    tri_f = jnp.where(iota((c, c), 0) >= iota((c, c), 1), 1.0, 0.0).astype(BF16)
    tri_b = jnp.where(iota((c, c), 0) <= iota((c, c), 1), 1.0, 0.0).astype(BF16)
    keep_f = (iota((c, nk), 1) & (c - 1)) <= iota((c, nk), 0)
    keep_b = (iota((c, nk), 1) & (c - 1)) > iota((c, nk), 0)
    bd_k = (iota((nk, nk), 0) >> 6) == (iota((nk, nk), 1) >> 6)
    bd_v = (iota((nk, nv), 0) >> 6) == (iota((nk, nv), 1) >> 7)
    bd_s = (iota((nv, nk), 0) >> 7) == (iota((nv, nk), 1) >> 6)
    valid0 = iota((c, 1), 0) < N_META

    def cum_decay(rows, valid, tri, la_col):
        la = la_ref[rows, la_col:la_col + nk]
        if valid is not None:
            la = jnp.where(valid, la, 0.0)
        hi = la.astype(BF16)
        rem = la - hi.astype(F32)
        mid = rem.astype(BF16)
        lo = (rem - mid.astype(F32)).astype(BF16)
        return _dot(tri, hi) + _dot(tri, mid) + _dot(tri, lo)

    def chunk(rows, valid, cum, keep, tot_row, s_ref, o_ref, out_rows, n_out):
        q = q_ref[rows, :].astype(F32) * (GLA_DK ** -0.5)
        k = k_ref[rows, :].astype(F32)
        v = v_ref[rows, :].astype(F32)
        if valid is not None:
            q = jnp.where(valid, q, 0.0)
            k = jnp.where(valid, k, 0.0)
            v = jnp.where(valid, v, 0.0)
        tot = cum[tot_row:tot_row + 1, :]
        q_dec = (q * jnp.exp(cum)).astype(BF16)
        k_inv = k * jnp.exp(-cum)
        k_end = (k * jnp.exp(tot - cum)).astype(BF16)
        kbd = jnp.where(bd_k, jnp.concatenate([k_inv] * GLA_HEADS, axis=0), 0.0).astype(BF16)
        p = lax.dot_general(q_dec, kbd, _NT, preferred_element_type=F32)
        s_old = s_ref[...]
        o_inter = lax.dot_general(q_dec, s_old.astype(BF16), _NT, preferred_element_type=F32)
        kvt = lax.dot_general(v.astype(BF16), k_end, _TN, preferred_element_type=F32)
        yield
        p = jnp.where(keep, p, 0.0).astype(BF16)
        vbd = jnp.where(bd_v, jnp.concatenate([v] * GLA_HEADS, axis=0), 0.0).astype(BF16)
        o = _dot(p, vbd) + o_inter
        s_ref[...] = jnp.exp(tot) * s_old + jnp.where(bd_s, kvt, 0.0)
        o_ref[out_rows, :] = o[:n_out]

    def lockstep(*gens):
        live = list(gens)
        while live:
            nxt = []
            for gen in live:
                try:
                    next(gen)
                    nxt.append(gen)
                except StopIteration:
                    pass
            live = nxt

    sf_ref[...] = jnp.zeros(sf_ref.shape, F32)
    sb_ref[...] = jnp.zeros(sb_ref.shape, F32)
    meta_rows = pl.ds(0, c)
    lockstep(chunk(meta_rows, valid0, cum_decay(meta_rows, valid0, tri_f, 0), keep_f, c - 1,
                   sf_ref, of_ref, pl.ds(0, N_META), N_META))

    def rows_fwd(i):
        return pl.ds(pl.multiple_of(N_META + i * c, 16), c)

    def rows_bwd(i):
        return pl.ds(pl.multiple_of(N_META + (n_real - 1 - i) * c, 16), c)

    def body(i, cums):
        cum_f, cum_b = cums
        nxt = jnp.minimum(i + 1, n_real - 1)
        cums = (cum_decay(rows_fwd(nxt), None, tri_f, 0), cum_decay(rows_bwd(nxt), None, tri_b, nk))
        lockstep(chunk(rows_fwd(i), None, cum_f, keep_f, c - 1, sf_ref, of_ref, rows_fwd(i), c),
                 chunk(rows_bwd(i), None, cum_b, keep_b, 0, sb_ref, ob_ref, rows_bwd(i), c))
        return cums

    first = (cum_decay(rows_fwd(0), None, tri_f, 0), cum_decay(rows_bwd(0), None, tri_b, nk))
    lax.fori_loop(0, n_real, body, first)
    lockstep(chunk(meta_rows, valid0, cum_decay(meta_rows, valid0, tri_b, nk), keep_b, 0,
                   sb_ref, ob_ref, pl.ds(0, N_META), N_META))

    def epilogue(i, carry):
        rows = pl.ds(pl.multiple_of(i * GLA_EPI_TILE, 16), GLA_EPI_TILE)
        o = of_ref[rows, :] + ob_ref[rows, :]
        r = r_ref[rows, :].astype(F32)
        for h in range(GLA_HEADS):
            cols = slice(h * GLA_DV, (h + 1) * GLA_DV)
            oh = o[:, cols]
            ms = jnp.mean(oh * oh, axis=-1, keepdims=True)
            yh = oh * lax.rsqrt(ms + NORM_EPS) * nw_ref[...] * _silu(r[:, cols])
            y_ref[rows, cols] = yh.astype(y_ref.dtype)
        return carry

    lax.fori_loop(0, t // GLA_EPI_TILE, epilogue, 0)


def _gla(u3, wc, bc, norm_w):
    b, t, _ = u3.shape
    nk = GLA_HEADS * GLA_DK
    nv = GLA_HEADS * GLA_DV
    q_blk = 3 * SC_WIDTH // nk
    v_blk = (3 * SC_WIDTH + 2 * nk) // nv
    g_blk = (3 * SC_WIDTH + 2 * nk + 2 * nv) // LANES
    full = lambda shape: pl.BlockSpec(shape, lambda i: (0, 0))
    return pl.pallas_call(
        _gla_kernel,
        grid=(b,),
        in_specs=[pl.BlockSpec((None, t, nk), lambda i: (i, 0, q_blk)),
                  pl.BlockSpec((None, t, nk), lambda i: (i, 0, q_blk + 1)),
                  pl.BlockSpec((None, t, nv), lambda i: (i, 0, v_blk)),
                  pl.BlockSpec((None, t, nv), lambda i: (i, 0, v_blk + 1)),
                  pl.BlockSpec((None, t, LANES), lambda i: (i, 0, g_blk)),
                  full((LANES, 2 * nk)), full((1, 2 * nk)), full((1, GLA_DV))],
        out_specs=pl.BlockSpec((None, t, nv), lambda i: (i, 0, 0)),
        out_shape=jax.ShapeDtypeStruct((b, t, nv), BF16),
        scratch_shapes=[pltpu.VMEM((t, 2 * nk), F32),
                        pltpu.VMEM((t, nv), F32), pltpu.VMEM((t, nv), F32),
                        pltpu.VMEM((nv, nk), F32), pltpu.VMEM((nv, nk), F32)],
        compiler_params=_params("parallel"),
        name="gla",
    )(u3, u3, u3, u3, u3, wc, bc, norm_w)


def _proj2_res_kernel(a_ref, b_ref, wa_ref, wb_ref, res_ref, o_ref):
    o_ref[...] = res_ref[...] + _dot(a_ref[...], wa_ref[...]) + _dot(b_ref[...], wb_ref[...])


def _proj2_res(a, b, wa, wb, res, tm):
    n, d = res.shape
    ka, kb = a.shape[1], b.shape[1]
    return pl.pallas_call(
        _proj2_res_kernel,
        grid=(n // tm,),
        in_specs=[pl.BlockSpec((tm, ka), lambda i: (i, 0)),
                  pl.BlockSpec((tm, kb), lambda i: (i, 0)),
                  pl.BlockSpec((ka, d), lambda i: (0, 0)),
                  pl.BlockSpec((kb, d), lambda i: (0, 0)),
                  pl.BlockSpec((tm, d), lambda i: (i, 0))],
        out_specs=pl.BlockSpec((tm, d), lambda i: (i, 0)),
        out_shape=jax.ShapeDtypeStruct((n, d), F32),
        compiler_params=_params("parallel"),
        name="proj2_res",
    )(a, b, wa, wb, res)


GROUP_LANE0 = N_EXPERTS


def _router_kernel(h_ref, g_ref, whi_ref, wlo_ref, b_ref, xn_ref, gates_ref, info_ref, inforow_ref, count_ref):
    x = h_ref[...]
    ms = jnp.mean(x * x, axis=-1, keepdims=True)
    xn = x * lax.rsqrt(ms + NORM_EPS) * g_ref[...]
    xhi = xn.astype(BF16)
    xn_ref[...] = xhi
    xlo = (xn - xhi.astype(F32)).astype(BF16)
    logits = _dot(xhi, whi_ref[...]) + _dot(xhi, wlo_ref[...]) + _dot(xlo, whi_ref[...]) + b_ref[...]

    lane = lax.broadcasted_iota(jnp.int32, logits.shape, 1)
    neg = -jnp.inf
    big = jnp.int32(LANES)
    is_group = (lane >> 2) == (GROUP_LANE0 >> 2)
    gl = jnp.where(is_group, logits, neg)
    gmax = jnp.max(gl, axis=-1, keepdims=True)
    g_p = 1.0 / jnp.sum(jnp.exp(gl - gmax), axis=-1, keepdims=True)
    g_idx = jnp.min(jnp.where(gl == gmax, lane, big), axis=-1, keepdims=True) - GROUP_LANE0
    el = jnp.where((lane >> 3) == g_idx, logits, neg)
    m1 = jnp.max(el, axis=-1, keepdims=True)
    i1 = jnp.min(jnp.where(el == m1, lane, big), axis=-1, keepdims=True)
    el2 = jnp.where(lane == i1, neg, el)
    m2 = jnp.max(el2, axis=-1, keepdims=True)
    i2 = jnp.min(jnp.where(el2 == m2, lane, big), axis=-1, keepdims=True)
    e2 = jnp.exp(m2 - m1)
    w1 = g_p / (1.0 + e2)
    w2 = g_p * e2 / (1.0 + e2)
    gates_ref[...] = jnp.where(lane == i1, w1, 0.0) + jnp.where(lane == i2, w2, 0.0)

    tm = x.shape[0]
    onehot = jnp.where(lane == g_idx, 1.0, 0.0)
    r_i = lax.broadcasted_iota(jnp.int32, (tm, tm), 0)
    c_i = lax.broadcasted_iota(jnp.int32, (tm, tm), 1)
    before = jnp.where(r_i > c_i, 1.0, 0.0).astype(BF16)
    prefix = _dot(before, onehot.astype(BF16))
    rank = jnp.sum(jnp.where(lane == g_idx, prefix, 0.0), axis=-1, keepdims=True).astype(jnp.int32)
    info_ref[...] = jnp.where(lane == 0, g_idx, jnp.where(lane == 1, rank, 0))
    eye = r_i == c_i
    g_row = jnp.sum(jnp.where(eye, g_idx, 0), axis=0, keepdims=True)
    rank_row = jnp.sum(jnp.where(eye, rank, 0), axis=0, keepdims=True)
    sub = lax.broadcasted_iota(jnp.int32, inforow_ref.shape, 0)
    inforow_ref[...] = jnp.where(sub == 0, g_row, jnp.where(sub == 1, rank_row, 0))
    counts = jnp.sum(onehot, axis=0, keepdims=True).astype(jnp.int32)
    count_ref[...] = jnp.broadcast_to(counts, count_ref.shape)


def _router(h, g, whi, wlo, bias, tm):
    n, d = h.shape
    nb = n // tm
    return pl.pallas_call(
        _router_kernel,
        grid=(nb,),
        in_specs=[pl.BlockSpec((tm, d), lambda i: (i, 0)),
                  pl.BlockSpec((1, d), lambda i: (0, 0)),
                  pl.BlockSpec((d, LANES), lambda i: (0, 0)),
                  pl.BlockSpec((d, LANES), lambda i: (0, 0)),
                  pl.BlockSpec((1, LANES), lambda i: (0, 0))],
        out_specs=[pl.BlockSpec((tm, d), lambda i: (i, 0)),
                   pl.BlockSpec((tm, LANES), lambda i: (i, 0)),
                   pl.BlockSpec((tm, LANES), lambda i: (i, 0)),
                   pl.BlockSpec((None, 8, tm), lambda i: (i, 0, 0)),
                   pl.BlockSpec((None, 8, LANES), lambda i: (i, 0, 0))],
        out_shape=[jax.ShapeDtypeStruct((n, d), BF16),
                   jax.ShapeDtypeStruct((n, LANES), F32),
                   jax.ShapeDtypeStruct((n, LANES), jnp.int32),
                   jax.ShapeDtypeStruct((nb, 8, tm), jnp.int32),
                   jax.ShapeDtypeStruct((nb, 8, LANES), jnp.int32)],
        compiler_params=_params("parallel"),
        name="moe_router",
    )(h, g.reshape(1, d), whi, wlo, bias)


MOE_TILE = 512
SEG_ALIGN = 16
STRIP = 32


def _pow2_sizes(max_rows, min_rows):
    sizes, s = [], min_rows
    while s <= max_rows:
        sizes.append(s)
        s *= 2
    return sizes[::-1]


def _strip_copies(n_rows, sizes, make_copy):
    pos = 0
    for sz in sizes:
        bit = n_rows & sz
        yield bit, make_copy(pos, sz)
        pos = pos + bit


def _slot_key(group, rank, boff_ref, b):
    key = rank
    for g in range(MOE_GROUPS):
        key = key + jnp.where(group == g, boff_ref[MOE_GROUPS * b + g], 0)
    return key


def _start_all(copies):
    for bit, cs in copies:
        @pl.when(bit != 0)
        def _():
            for c in cs:
                c.start()


def _wait_all(copies):
    for bit, cs in copies:
        @pl.when(bit != 0)
        def _():
            for c in cs:
                c.wait()


def _dispatch_kernel(off_ref, cpad_ref, boff_ref, tail_ref, x_ref, gates_ref, inforow_ref,
                     xs_hbm, gs_hbm, xstg, gstg, zx, zg, sem):
    b = pl.program_id(0)
    last = pl.num_programs(0) - 1
    tb = x_ref.shape[0]
    ks = xstg.shape[1]
    cur = lax.rem(b, 2)
    key = _slot_key(inforow_ref[0:1, :], inforow_ref[1:2, :], boff_ref, b)
    slot = lax.broadcasted_iota(jnp.int32, (ks, tb), 0)
    perm = jnp.where(slot == key, 1.0, 0.0).astype(BF16)
    xstg[cur] = _dot(perm, x_ref[...]).astype(BF16)
    gt = gates_ref[...]
    hi = gt.astype(BF16)
    lo = (gt - hi.astype(F32)).astype(BF16)
    gstg[cur] = _dot(perm, hi) + _dot(perm, lo)

    sizes = _pow2_sizes(tb, SEG_ALIGN)

    def copies(blk, sl):
        for g in range(MOE_GROUPS):
            src0 = boff_ref[MOE_GROUPS * blk + g]
            dst0 = off_ref[MOE_GROUPS * blk + g]

            def make(pos, sz, src0=src0, dst0=dst0):
                src = pl.ds(pl.multiple_of(src0 + pos, SEG_ALIGN), sz)
                dst = pl.ds(pl.multiple_of(dst0 + pos, SEG_ALIGN), sz)
                return (pltpu.make_async_copy(xstg.at[sl, src, :], xs_hbm.at[dst, :], sem.at[sl, 0]),
                        pltpu.make_async_copy(gstg.at[sl, src, :], gs_hbm.at[dst, :], sem.at[sl, 1]))

            yield from _strip_copies(cpad_ref[MOE_GROUPS * blk + g], sizes, make)

    _start_all(copies(b, cur))

    @pl.when(b > 0)
    def _():
        _wait_all(copies(jnp.maximum(b - 1, 0), 1 - cur))

    @pl.when(b == last)
    def _():
        _wait_all(copies(b, cur))
        zx[...] = jnp.zeros(zx.shape, BF16)
        zg[...] = jnp.zeros(zg.shape, F32)
        tail_sizes = _pow2_sizes(MOE_TILE // 2, SEG_ALIGN)

        def tails():
            for g in range(MOE_GROUPS):
                dst0 = tail_ref[g]

                def make(pos, sz, dst0=dst0):
                    dst = pl.ds(pl.multiple_of(dst0 + pos, SEG_ALIGN), sz)
                    return (pltpu.make_async_copy(zx.at[pl.ds(0, sz), :], xs_hbm.at[dst, :], sem.at[0, 0]),
                            pltpu.make_async_copy(zg.at[pl.ds(0, sz), :], gs_hbm.at[dst, :], sem.at[0, 1]))

                yield from _strip_copies(tail_ref[MOE_GROUPS + g], tail_sizes, make)

        _start_all(tails())
        _wait_all(tails())

        def spare(i):
            dst = pl.ds(pl.multiple_of(i * MOE_TILE, MOE_TILE), MOE_TILE)
            return (pltpu.make_async_copy(zx, xs_hbm.at[dst, :], sem.at[0, 0]),
                    pltpu.make_async_copy(zg, gs_hbm.at[dst, :], sem.at[0, 1]))

        first_spare = tail_ref[2 * MOE_GROUPS]
        n_tiles = xs_hbm.shape[0] // MOE_TILE

        @pl.loop(first_spare, n_tiles)
        def _(i):
            cx, cg = spare(i)
            cx.start()
            cg.start()

        @pl.loop(first_spare, n_tiles)
        def _(i):
            cx, cg = spare(i)
            cx.wait()
            cg.wait()


def _dispatch(xn, gates, inforow, off, cpad, boff, tail, n_sorted, tb, ks):
    n, d = xn.shape
    nb = n // tb
    grid_spec = pltpu.PrefetchScalarGridSpec(
        num_scalar_prefetch=4,
        grid=(nb,),
        in_specs=[pl.BlockSpec((tb, d), lambda i, *_: (i, 0)),
                  pl.BlockSpec((tb, LANES), lambda i, *_: (i, 0)),
                  pl.BlockSpec((None, 8, tb), lambda i, *_: (i, 0, 0))],
        out_specs=[pl.BlockSpec(memory_space=pl.ANY), pl.BlockSpec(memory_space=pl.ANY)],
        scratch_shapes=[pltpu.VMEM((2, ks, d), BF16), pltpu.VMEM((2, ks, LANES), F32),
                        pltpu.VMEM((MOE_TILE, d), BF16), pltpu.VMEM((MOE_TILE, LANES), F32),
                        pltpu.SemaphoreType.DMA((2, 2))])
    return pl.pallas_call(
        _dispatch_kernel,
        grid_spec=grid_spec,
        out_shape=[jax.ShapeDtypeStruct((n_sorted, d), BF16),
                   jax.ShapeDtypeStruct((n_sorted, LANES), F32)],
        compiler_params=_params("arbitrary"),
        name="moe_dispatch",
    )(off, cpad, boff, tail, xn, gates, inforow)


def _group_mlp_kernel(tg_ref, nused_ref, x_ref, gs_ref, wg_ref, wu_ref, wd_ref, y_ref):
    i = pl.program_id(0)

    @pl.when(i < nused_ref[0])
    def _():
        x = x_ref[...]
        gs = gs_ref[...]
        lane = lax.broadcasted_iota(jnp.int32, gs.shape, 1)
        lane0 = tg_ref[i] * MOE_EXPERTS_PER_GROUP
        acc = jnp.zeros(y_ref.shape, F32)
        for e in range(MOE_EXPERTS_PER_GROUP):
            he = _silu(_dot(x, wg_ref[e])) * _dot(x, wu_ref[e])
            gate = jnp.sum(jnp.where(lane == lane0 + e, gs, 0.0), axis=-1, keepdims=True)
            acc = acc + _dot((he * gate).astype(BF16), wd_ref[e])
        y_ref[...] = acc.astype(y_ref.dtype)

    @pl.when(i >= nused_ref[0])
    def _():
        y_ref[...] = jnp.zeros(y_ref.shape, y_ref.dtype)


def _cast_kernel(w_ref, o_ref):
    o_ref[...] = w_ref[...].astype(o_ref.dtype)


def _cast_experts(w):
    nl, ne, a, b = w.shape
    blk = (None, MOE_EXPERTS_PER_GROUP, a, b)
    return pl.pallas_call(
        _cast_kernel,
        grid=(nl, ne // MOE_EXPERTS_PER_GROUP),
        in_specs=[pl.BlockSpec(blk, lambda l, g: (l, g, 0, 0))],
        out_specs=pl.BlockSpec(blk, lambda l, g: (l, g, 0, 0)),
        out_shape=jax.ShapeDtypeStruct(w.shape, BF16),
        compiler_params=_params("parallel", "parallel"),
        name="cast_experts",
    )(w)


def _group_mlp(xs, gs, wg, wu, wd, layer, tile_group, nused):
    n_sorted, d = xs.shape
    ne = MOE_EXPERTS_PER_GROUP

    def rows(i, tg, nu):
        return (jnp.minimum(i, nu[0] - 1), 0)

    def group(i, tg, nu):
        return (layer, tg[i], 0, 0)

    grid_spec = pltpu.PrefetchScalarGridSpec(
        num_scalar_prefetch=2,
        grid=(n_sorted // MOE_TILE,),
        in_specs=[pl.BlockSpec((MOE_TILE, d), rows),
                  pl.BlockSpec((MOE_TILE, LANES), rows),
                  pl.BlockSpec((None, ne, d, MOE_D_FF), group),
                  pl.BlockSpec((None, ne, d, MOE_D_FF), group),
                  pl.BlockSpec((None, ne, MOE_D_FF, d), group)],
        out_specs=pl.BlockSpec((MOE_TILE, d), lambda i, tg, nu: (i, 0)))
    return pl.pallas_call(
        _group_mlp_kernel,
        grid_spec=grid_spec,
        out_shape=jax.ShapeDtypeStruct((n_sorted, d), BF16),
        compiler_params=_params("arbitrary"),
        name="moe_experts",
    )(tile_group, nused, xs, gs, wg, wu, wd)


def _combine_kernel(off_ref, cpad_ref, boff_ref, info_ref, res_ref, fg_ref, y_hbm, o_ref, ybuf, sem,
                    *, final_norm):
    b = pl.program_id(0)
    last = pl.num_programs(0) - 1
    tb = res_ref.shape[0]
    ks = ybuf.shape[1]
    cur = lax.rem(b, 2)
    sizes = _pow2_sizes(tb, STRIP)

    def copies(blk, sl):
        for g in range(MOE_GROUPS):
            src0 = off_ref[MOE_GROUPS * blk + g]
            dst0 = boff_ref[MOE_GROUPS * blk + g]

            def make(pos, sz, src0=src0, dst0=dst0):
                src = pl.ds(pl.multiple_of(src0 + pos, SEG_ALIGN), sz)
                dst = pl.ds(pl.multiple_of(dst0 + pos, STRIP), sz)
                return (pltpu.make_async_copy(y_hbm.at[src, :], ybuf.at[sl, dst, :], sem.at[sl]),)

            n_rows = (cpad_ref[MOE_GROUPS * blk + g] + (STRIP - 1)) & (-STRIP)
            yield from _strip_copies(n_rows, sizes, make)

    @pl.when(b == 0)
    def _():
        ybuf[...] = jnp.zeros(ybuf.shape, ybuf.dtype)
        _start_all(copies(b, cur))

    @pl.when(b < last)
    def _():
        _start_all(copies(jnp.minimum(b + 1, last), 1 - cur))

    _wait_all(copies(b, cur))

    key = _slot_key(info_ref[:, 0:1], info_ref[:, 1:2], boff_ref, b)
    slot = lax.broadcasted_iota(jnp.int32, (tb, ks), 1)
    perm = jnp.where(slot == key, 1.0, 0.0).astype(BF16)
    y = res_ref[...] + _dot(perm, ybuf[cur])
    if final_norm:
        ms = jnp.mean(y * y, axis=-1, keepdims=True)
        y = y * lax.rsqrt(ms + NORM_EPS) * fg_ref[...]
    o_ref[...] = y


def _combine(ys, info, res, final_g, off, cpad, boff, tb, ks, final_norm):
    n, d = res.shape
    grid_spec = pltpu.PrefetchScalarGridSpec(
        num_scalar_prefetch=3,
        grid=(n // tb,),
        in_specs=[pl.BlockSpec((tb, LANES), lambda i, *_: (i, 0)),
                  pl.BlockSpec((tb, d), lambda i, *_: (i, 0)),
                  pl.BlockSpec((1, d), lambda i, *_: (0, 0)),
                  pl.BlockSpec(memory_space=pl.ANY)],
        out_specs=pl.BlockSpec((tb, d), lambda i, *_: (i, 0)),
        scratch_shapes=[pltpu.VMEM((2, ks, d), BF16), pltpu.SemaphoreType.DMA((2,))])
    return pl.pallas_call(
        functools.partial(_combine_kernel, final_norm=final_norm),
        grid_spec=grid_spec,
        out_shape=jax.ShapeDtypeStruct((n, d), F32),
        compiler_params=_params("arbitrary"),
        name="moe_combine",
    )(off, cpad, boff, info, res, final_g.reshape(1, d), ys)


def _round_up(x, m):
    return (x + m - 1) // m * m


def _moe_layer(h, norm_g, w_group, b_group, w_expert, b_expert, w_gate, w_up, w_down, layer,
               final_g, tb, final_norm):
    n, d = h.shape
    nb = n // tb
    pad = LANES - N_EXPERTS - MOE_GROUPS
    wr = jnp.concatenate([w_expert, w_group, jnp.zeros((d, pad), F32)], axis=1)
    whi = wr.astype(BF16)
    wlo = (wr - whi.astype(F32)).astype(BF16)
    bias = jnp.concatenate([b_expert, b_group, jnp.zeros((pad,), F32)]).reshape(1, LANES)
    xn, gates, info, inforow, counts = _router(h, norm_g, whi, wlo, bias, tb)

    cnt = counts[:, 0, :MOE_GROUPS]
    cpad = _round_up(cnt, SEG_ALIGN)
    strip = _round_up(cnt, STRIP)
    used = jnp.sum(cpad, axis=0)
    region = _round_up(used, MOE_TILE)
    region_start = jnp.cumsum(region) - region
    off = region_start[None, :] + jnp.cumsum(cpad, axis=0) - cpad
    boff = jnp.cumsum(strip, axis=1) - strip
    nused = (jnp.sum(region) // MOE_TILE).reshape(1)
    tail = jnp.concatenate([region_start + used, region - used, nused]).astype(jnp.int32)
    n_tiles = -(-(n + nb * MOE_GROUPS * (SEG_ALIGN - 1) + MOE_GROUPS * (MOE_TILE - 1)) // MOE_TILE) + 1
    tile_start = jnp.arange(n_tiles, dtype=jnp.int32) * MOE_TILE
    region_end = region_start + region
    tile_group = jnp.minimum(jnp.sum(tile_start[:, None] >= region_end[None, :], axis=1), MOE_GROUPS - 1)
    ks =tb + MOE_GROUPS * STRIP

    flat = lambda a: a.reshape(-1).astype(jnp.int32)
    xs, gs = _dispatch(xn, gates, inforow, flat(off), flat(cpad), flat(boff), tail,
                       n_tiles * MOE_TILE, tb, ks)
    ys = _group_mlp(xs, gs, w_gate, w_up, w_down, layer,
                    tile_group.astype(jnp.int32), nused.astype(jnp.int32))
    return _combine(ys, info, h, final_g, flat(off), flat(cpad), flat(boff), tb, ks, final_norm)


ROPE_ROW_TILE = 688
ATT_Q_TILE = 256
ATT_TILES_PER_ITER = 2
ATT_SCORES_AHEAD = 2
ATT_SCORE_SLOTS = 4
ATT_VT_ROWS = ATT_HEAD_DIM + 16
ATT_Q_SCALE = ATT_HEAD_DIM ** -0.5 * 1.4426950408889634


def _eye(n):
    return jnp.where(lax.broadcasted_iota(jnp.int32, (n, n), 0) == lax.broadcasted_iota(jnp.int32, (n, n), 1),
                     1.0, 0.0).astype(BF16)


def _transpose_bf16(x, eye):
    return lax.dot_general(eye, x, _NT, preferred_element_type=F32).astype(BF16)


def _rope_kernel(q_ref, k_ref, v_ref, cos_ref, sin_ref, qn_ref, kn_ref, qt_ref, kp_ref, vt_ref):
    t = q_ref.shape[0]
    tq = ATT_Q_TILE
    dh = ATT_HEAD_DIM
    gi = lax.broadcasted_iota(jnp.int32, (LANES, LANES), 0) >> 6
    gj = lax.broadcasted_iota(jnp.int32, (LANES, LANES), 1) >> 6
    gmat = jnp.where(gi == gj, 1.0, 0.0).astype(BF16)
    eye_q = _eye(LANES)

    def norm_rope(x, gain, cos, sin):
        lane = lax.broadcasted_iota(jnp.int32, x.shape, 1)
        sq = x * x
        hi = sq.astype(BF16)
        lo = (sq - hi.astype(F32)).astype(BF16)
        ms = (_dot(hi, gmat) + _dot(lo, gmat)) * (1.0 / dh)
        xn = x * lax.rsqrt(ms + NORM_EPS) * gain
        partner = jnp.where((lane & 1) == 0, pltpu.roll(xn, LANES - 1, 1), pltpu.roll(xn, 1, 1))
        return xn * cos + partner * sin

    for j in range(q_ref.shape[1] // LANES):
        cols = slice(j * LANES, (j + 1) * LANES)
        y = norm_rope(q_ref[N_META:, cols].astype(F32), qn_ref[...], cos_ref[N_META:, :], sin_ref[N_META:, :])
        yt = _transpose_bf16((y * ATT_Q_SCALE).astype(BF16), eye_q)
        for i in range((t - N_META) // tq):
            qt_ref[i, cols, :] = yt[:, i * tq:(i + 1) * tq]

    low = lax.broadcasted_iota(jnp.int32, (t, LANES), 1) < dh
    for j in range(k_ref.shape[1] // LANES):
        cols = slice(j * LANES, (j + 1) * LANES)
        y = norm_rope(k_ref[:, cols].astype(F32), kn_ref[...], cos_ref[...], sin_ref[...])
        kp_ref[2 * j] = jnp.where(low, y, 0.0).astype(BF16)
        kp_ref[2 * j + 1] = jnp.where(low, pltpu.roll(y, dh, 1), 0.0).astype(BF16)

    n_real = t - N_META
    eye_v = _eye(v_ref.shape[1])
    vt_real = _transpose_bf16(v_ref[N_META:, :], eye_v)
    vt_meta = _transpose_bf16(v_ref[0:N_META, :], eye_v)
    for h in range(ATT_KV_HEADS):
        vt_ref[h, 0:dh, 0:n_real] = vt_real[h * dh:(h + 1) * dh, :]
        vt_ref[h, 0:dh, n_real:t] = vt_meta[h * dh:(h + 1) * dh, :]
        extra = lax.broadcasted_iota(jnp.int32, (ATT_VT_ROWS - dh, t), 0)
        vt_ref[h, dh:ATT_VT_ROWS, :] = jnp.where(extra == 0, 1.0, 0.0).astype(BF16)


def _rope(u3, cos_e, sin_e, qn, kn):
    b, t, _ = u3.shape
    nq = ATT_HEADS * ATT_HEAD_DIM
    nkv = ATT_KV_HEADS * ATT_HEAD_DIM
    n_qt = (t - N_META) // ATT_Q_TILE
    full = lambda shape: pl.BlockSpec(shape, lambda i: (0, 0))
    return pl.pallas_call(
        _rope_kernel,
        grid=(b,),
        in_specs=[pl.BlockSpec((None, t, nq), lambda i: (i, 0, 0)),
                  pl.BlockSpec((None, t, nkv), lambda i: (i, 0, nq // nkv)),
                  pl.BlockSpec((None, t, nkv), lambda i: (i, 0, nq // nkv + 1)),
                  full((t, LANES)), full((t, LANES)), full((1, LANES)), full((1, LANES))],
        out_specs=[pl.BlockSpec((None, n_qt, nq, ATT_Q_TILE), lambda i: (i, 0, 0, 0)),
                   pl.BlockSpec((None, ATT_KV_HEADS, t, LANES), lambda i: (i, 0, 0, 0)),
                   pl.BlockSpec((None, ATT_KV_HEADS, ATT_VT_ROWS, t), lambda i: (i, 0, 0, 0))],
        out_shape=[jax.ShapeDtypeStruct((b, n_qt, nq, ATT_Q_TILE), BF16),
                   jax.ShapeDtypeStruct((b, ATT_KV_HEADS, t, LANES), BF16),
                   jax.ShapeDtypeStruct((b, ATT_KV_HEADS, ATT_VT_ROWS, t), BF16)],
        compiler_params=_params("parallel"),
        name="qk_norm_rope",
    )(u3, u3, u3, cos_e, sin_e, qn, kn)


def _attn_kernel(qt_ref, k_ref, vt_ref, o_ref, s_ref):
    n_qt, w, tq = qt_ref.shape
    dh = ATT_HEAD_DIM
    n_real = k_ref.shape[0] - N_META
    eye = _eye(tq)
    zpad = jnp.zeros((LANES - dh, tq), BF16)

    def scores(i, g, slot):
        qp = jnp.concatenate([qt_ref[i, g * dh:(g + 1) * dh, :], zpad], axis=0)
        s_r = _dot(k_ref[N_META:, :], qp)
        s_m = _dot(k_ref[0:N_META, :], qp)
        s_ref[slot, 0:n_real, :] = s_r
        s_ref[slot, n_real:, :] = s_m
        return jnp.maximum(jnp.max(s_r, axis=0, keepdims=True), jnp.max(s_m, axis=0, keepdims=True))

    steps = [(u, g) for u in range(ATT_TILES_PER_ITER) for g in range(ATT_GROUP)]
    n_steps = len(steps)
    n_iter = n_qt // ATT_TILES_PER_ITER
    ahead = ATT_SCORES_AHEAD
    n_slots = s_ref.shape[0]
    assert n_steps % n_slots == 0 and ahead < n_slots

    def issue_scores(it, n):
        it2 = jnp.minimum(it + n // n_steps, n_iter - 1)
        u, g = steps[n % n_steps]
        return scores(it2 * ATT_TILES_PER_ITER + u, g, n % n_slots)

    def q_tiles(it, pending):
        i0 = it * ATT_TILES_PER_ITER
        outs = []
        pending = list(pending)
        for n, (u, g) in enumerate(steps):
            m, slot = pending.pop(0), n % n_slots
            pending.append(issue_scores(it, n + ahead))
            p_r = jnp.exp2(s_ref[slot, 0:n_real, :] - m).astype(BF16)
            p_m = jnp.exp2(s_ref[slot, n_real:, :] - m).astype(BF16)
            ot = _dot(vt_ref[:, 0:n_real], p_r) + _dot(vt_ref[:, n_real:], p_m)
            outs.append((ot[0:dh] * (1.0 / ot[dh:dh + 1])).astype(BF16))
            if g == ATT_GROUP - 1:
                rows = pl.ds(pl.multiple_of((i0 + u) * tq, tq), tq)
                ot_all = jnp.concatenate(outs, axis=0)
                o_ref[rows, :] = lax.dot_general(eye, ot_all, _NT, preferred_element_type=F32).astype(o_ref.dtype)
                outs = []
        return tuple(pending)

    first = tuple(issue_scores(0, n) for n in range(ahead))
    lax.fori_loop(0, n_iter, q_tiles, first)


def _attention(qt, kp, vt):
    b, n_qt, nq, tq = qt.shape
    t = kp.shape[2]
    w = ATT_GROUP * ATT_HEAD_DIM
    return pl.pallas_call(
        _attn_kernel,
        grid=(b, ATT_KV_HEADS),
        in_specs=[pl.BlockSpec((None, n_qt, w, tq), lambda i, h: (i, 0, h, 0)),
                  pl.BlockSpec((None, None, t, LANES), lambda i, h: (i, h, 0, 0)),
                  pl.BlockSpec((None, None, ATT_VT_ROWS, t), lambda i, h: (i, h, 0, 0))],
        out_specs=pl.BlockSpec((None, n_qt * tq, w), lambda i, h: (i, 0, h)),
        out_shape=jax.ShapeDtypeStruct((b, n_qt * tq, nq), BF16),
        scratch_shapes=[pltpu.VMEM((ATT_SCORE_SLOTS, t, tq), F32)],
        compiler_params=_params("parallel", "parallel"),
        name="gqa_attention",
    )(qt, kp, vt)


def _att_out_kernel(o_ref, w_ref, h_ref, out_ref):
    out_ref[...] = h_ref[N_META:, :] + _dot(o_ref[...], w_ref[...])


def _att_out(o, w, h3, tn):
    b, s_len, nq = o.shape
    t, d = h3.shape[1], h3.shape[2]
    return pl.pallas_call(
        _att_out_kernel,
        grid=(b, d // tn),
        in_specs=[pl.BlockSpec((None, s_len, nq), lambda i, j: (i, 0, 0)),
                  pl.BlockSpec((nq, tn), lambda i, j: (0, j)),
                  pl.BlockSpec((None, t, tn), lambda i, j: (i, 0, j))],
        out_specs=pl.BlockSpec((None, s_len, tn), lambda i, j: (i, 0, j)),
        out_shape=jax.ShapeDtypeStruct((b, s_len, d), F32),
        compiler_params=_params("parallel", "parallel"),
        name="att_out_res",
    )(o, w, h3)


def _rope_tables(n_tokens):
    rows_n = n_tokens // GRID_W
    rows = jnp.repeat(jnp.arange(rows_n), GRID_W).astype(F32)
    cols = jnp.tile(jnp.arange(GRID_W), rows_n).astype(F32)
    axis_dims = ATT_HEAD_DIM // 2
    freqs = ROPE_THETA ** (-jnp.arange(0, axis_dims, 2, dtype=F32) / axis_dims)
    ang = jnp.concatenate([rows[:, None] * freqs, cols[:, None] * freqs], axis=-1)
    ang = jnp.concatenate([jnp.zeros((N_META, ang.shape[1]), F32), ang], axis=0)
    cos_e = jnp.tile(jnp.repeat(jnp.cos(ang), 2, axis=1), (1, LANES // ATT_HEAD_DIM))
    sign = jnp.tile(jnp.array([-1.0, 1.0], F32), LANES // 2)
    sin_e = jnp.tile(jnp.repeat(jnp.sin(ang), 2, axis=1), (1, LANES // ATT_HEAD_DIM)) * sign
    return cos_e, sin_e


def kernel(x, meta_tokens, norm_mix, norm_ffn, norm_final, sc_w_in, sc_conv_w, gla_w_gate_fwd, gla_b_gate_fwd, gla_w_gate_bwd, gla_b_gate_bwd, gla_norm_w, sc_w_out, att_w_in, att_q_norm, att_k_norm, att_w_out, moe_w_group, moe_b_group, moe_w_expert, moe_b_expert, moe_w_gate, moe_w_up, moe_w_down):
    bsz, s_len, d = x.shape
    t = s_len + N_META
    n0, n1 = bsz * t, bsz * s_len
    tm0 = t // 3
    tm1 = 512

    w_gate, w_up, w_down = _cast_experts(moe_w_gate), _cast_experts(moe_w_up), _cast_experts(moe_w_down)
    meta = jnp.broadcast_to(meta_tokens.astype(x.dtype)[None], (bsz, N_META, d))
    h = jnp.concatenate([meta, x], axis=1).reshape(n0, d)

    nk = GLA_HEADS * GLA_DK
    mix_in = sc_w_in.shape[2]
    mix_pad = -mix_in % MXU_WIDTH
    w_in = jnp.pad(sc_w_in[0], ((0, 0), (0, mix_pad))).astype(BF16)
    u = _norm_matmul(h, norm_mix[0], w_in, tm0, mix_in + mix_pad)
    u3 = u.reshape(bsz, t, mix_in + mix_pad)
    y_a = _gated_conv(u3, sc_conv_w[0])
    wc = jnp.zeros((LANES, 2 * nk), F32)
    wc = wc.at[:GLA_GATE_RANK, :nk].set(gla_w_gate_fwd[0])
    wc = wc.at[GLA_GATE_RANK:2 * GLA_GATE_RANK, nk:].set(gla_w_gate_bwd[0])
    bc = jnp.concatenate([gla_b_gate_fwd[0], gla_b_gate_bwd[0]]).reshape(1, 2 * nk)
    y_b = _gla(u3, wc.astype(BF16), bc, gla_norm_w[0].reshape(1, GLA_DV))
    w_out = sc_w_out[0].astype(BF16)
    h = _proj2_res(y_a.reshape(n0, SC_WIDTH), y_b.reshape(n0, -1), w_out[:SC_WIDTH], w_out[SC_WIDTH:], h, tm0)
    h = _moe_layer(h, norm_ffn[0], moe_w_group[0], moe_b_group[0], moe_w_expert[0], moe_b_expert[0],
                   w_gate, w_up, w_down, 0, norm_final, 3 * LANES, False)

    u = _norm_matmul(h, norm_mix[1], att_w_in[0].astype(BF16), tm0, att_w_in.shape[2])
    u3 = u.reshape(bsz, t, -1)
    cos_e, sin_e = _rope_tables(s_len)
    qn = jnp.tile(att_q_norm[0], LANES // ATT_HEAD_DIM).reshape(1, LANES)
    kn = jnp.tile(att_k_norm[0], LANES // ATT_HEAD_DIM).reshape(1, LANES)
    qt, kp, vt = _rope(u3, cos_e, sin_e, qn, kn)
    o = _attention(qt, kp, vt)
    h = _att_out(o, att_w_out[0].astype(BF16), h.reshape(bsz, t, d), 4 * LANES).reshape(n1, d)
    h = _moe_layer(h, norm_ffn[1], moe_w_group[1], moe_b_group[1], moe_w_expert[1], moe_b_expert[1],
                   w_gate, w_up, w_down, 1, norm_final, tm1, True)
    return h.reshape(bsz, s_len, d)
```

```python
import functools

import jax
import jax.numpy as jnp
from jax import lax
from jax.experimental import pallas as pl
from jax.experimental.pallas import tpu as pltpu

F32 = jnp.float32
BF16 = jnp.bfloat16

NORM_EPS = 1e-6
N_META = 16
GRID_W = 64
ROPE_THETA = 10000.0
SC_WIDTH = 512
GLA_HEADS = 4
GLA_DK = 64
GLA_DV = 128
GLA_GATE_RANK = 16
GLA_GATE_TAU = 16.0
GLA_CHUNK = 64
ATT_HEAD_DIM = 64
ATT_HEADS = 16
ATT_KV_HEADS = 4
ATT_GROUP = ATT_HEADS // ATT_KV_HEADS
MOE_GROUPS = 4
MOE_EXPERTS_PER_GROUP = 8
N_EXPERTS = 32
MOE_D_FF = 256

LANES = 128
MXU_WIDTH = 256
MIB = 2 ** 20
VMEM_LIMIT = 56 * MIB

_NT = (((1,), (1,)), ((), ()))
_TN = (((0,), (0,)), ((), ()))


def _params(*semantics):
    return pltpu.CompilerParams(dimension_semantics=semantics, vmem_limit_bytes=VMEM_LIMIT)


def _dot(a, b):
    return jnp.dot(a, b, preferred_element_type=F32)


def _silu(x):
    return x * (1.0 / (1.0 + jnp.exp(-x)))


def _norm_matmul_kernel(h_ref, g_ref, w_ref, o_ref, xn_ref):
    @pl.when(pl.program_id(1) == 0)
    def _():
        x = h_ref[...]
        ms = jnp.mean(x * x, axis=-1, keepdims=True)
        xn_ref[...] = (x * lax.rsqrt(ms + NORM_EPS) * g_ref[...]).astype(BF16)

    o_ref[...] = _dot(xn_ref[...], w_ref[...]).astype(o_ref.dtype)


def _norm_matmul(h, g, w, tm, tn):
    n, d = h.shape
    nout = w.shape[1]
    return pl.pallas_call(
        _norm_matmul_kernel,
        grid=(n // tm, nout // tn),
        in_specs=[pl.BlockSpec((tm, d), lambda i, j: (i, 0)),
                  pl.BlockSpec((1, d), lambda i, j: (0, 0)),
                  pl.BlockSpec((d, tn), lambda i, j: (0, j))],
        out_specs=pl.BlockSpec((tm, tn), lambda i, j: (i, j)),
        out_shape=jax.ShapeDtypeStruct((n, nout), BF16),
        scratch_shapes=[pltpu.VMEM((tm, d), BF16)],
        compiler_params=_params("parallel", "arbitrary"),
        name="norm_matmul",
    )(h, g.reshape(1, d), w)


def _conv_kernel(ab_ref, ac_ref, ax_ref, w_ref, o_ref, z_ref):
    t, c = ab_ref.shape
    z = ac_ref[...].astype(F32) * ax_ref[...].astype(F32)
    z_ref[0:8, :] = jnp.zeros((8, c), F32)
    z_ref[8:8 + t, :] = z
    z_ref[8 + t:16 + t, :] = jnp.zeros((8, c), F32)
    w = w_ref[...]
    conv = w[0:1, :] * z_ref[7:7 + t, :] + w[1:2, :] * z + w[2:3, :] * z_ref[9:9 + t, :]
    o_ref[...] = (ab_ref[...].astype(F32) * conv).astype(o_ref.dtype)


def _gated_conv(u3, conv_w):
    b, t, _ = u3.shape
    nblk = SC_WIDTH // LANES

    def col(off):
        return pl.BlockSpec((None, t, LANES), lambda i, j: (i, 0, off + j))

    return pl.pallas_call(
        _conv_kernel,
        grid=(b, nblk),
        in_specs=[col(0), col(nblk), col(2 * nblk),
                  pl.BlockSpec((3, LANES), lambda i, j: (0, j))],
        out_specs=pl.BlockSpec((None, t, LANES), lambda i, j: (i, 0, j)),
        out_shape=jax.ShapeDtypeStruct((b, t, SC_WIDTH), BF16),
        scratch_shapes=[pltpu.VMEM((t + 16, LANES), F32)],
        compiler_params=_params("parallel", "parallel"),
        name="gated_conv",
    )(u3, u3, u3, conv_w)


GLA_ROW_TILE = 688
GLA_EPI_TILE = 688


def _gla_kernel(q_ref, k_ref, v_ref, r_ref, g_ref, wc_ref, bc_ref, nw_ref, y_ref,
                la_ref, of_ref, ob_ref, sf_ref, sb_ref):
    t, nk = q_ref.shape
    nv = v_ref.shape[1]
    c = GLA_CHUNK
    n_real = (t - N_META) // c

    for i in range(t // GLA_ROW_TILE):
        rows = pl.ds(i * GLA_ROW_TILE, GLA_ROW_TILE)
        pre = _dot(g_ref[rows, :], wc_ref[...]) + bc_ref[...]
        la_ref[rows, :] = (jnp.minimum(pre, 0.0) - jnp.log(1.0 + jnp.exp(-jnp.abs(pre)))) * (1.0 / GLA_GATE_TAU)

    def iota(shape, dim):
        return lax.broadcasted_iota(jnp.int32, shape, dim)

    tri_f = jnp.where(iota((c, c), 0) >= iota((c, c), 1), 1.0, 0.0).astype(BF16)
    tri_b = jnp.where(iota((c, c), 0) <= iota((c, c), 1), 1.0, 0.0).astype(BF16)
    keep_f = (iota((c, nk), 1) & (c - 1)) <= iota((c, nk), 0)
    keep_b = (iota((c, nk), 1) & (c - 1)) > iota((c, nk), 0)
    bd_k = (iota((nk, nk), 0) >> 6) == (iota((nk, nk), 1) >> 6)
    bd_v = (iota((nk, nv), 0) >> 6) == (iota((nk, nv), 1) >> 7)
    bd_s = (iota((nv, nk), 0) >> 7) == (iota((nv, nk), 1) >> 6)
    valid0 = iota((c, 1), 0) < N_META

    def cum_decay(rows, valid, tri, la_col):
        la = la_ref[rows, la_col:la_col + nk]
        if valid is not None:
            la = jnp.where(valid, la, 0.0)
        hi = la.astype(BF16)
        rem = la - hi.astype(F32)
        mid = rem.astype(BF16)
        lo = (rem - mid.astype(F32)).astype(BF16)
        return _dot(tri, hi) + _dot(tri, mid) + _dot(tri, lo)

    def chunk(rows, valid, cum, keep, tot_row, s_ref, o_ref, out_rows, n_out):
        q = q_ref[rows, :].astype(F32) * (GLA_DK ** -0.5)
        k = k_ref[rows, :].astype(F32)
        v = v_ref[rows, :].astype(F32)
        if valid is not None:
            q = jnp.where(valid, q, 0.0)
            k = jnp.where(valid, k, 0.0)
            v = jnp.where(valid, v, 0.0)
        tot = cum[tot_row:tot_row + 1, :]
        q_dec = (q * jnp.exp(cum)).astype(BF16)
        k_inv = k * jnp.exp(-cum)
        k_end = (k * jnp.exp(tot - cum)).astype(BF16)
        kbd = jnp.where(bd_k, jnp.concatenate([k_inv] * GLA_HEADS, axis=0), 0.0).astype(BF16)
        p = lax.dot_general(q_dec, kbd, _NT, preferred_element_type=F32)
        s_old = s_ref[...]
        o_inter = lax.dot_general(q_dec, s_old.astype(BF16), _NT, preferred_element_type=F32)
        kvt = lax.dot_general(v.astype(BF16), k_end, _TN, preferred_element_type=F32)
        yield
        p = jnp.where(keep, p, 0.0).astype(BF16)
        vbd = jnp.where(bd_v, jnp.concatenate([v] * GLA_HEADS, axis=0), 0.0).astype(BF16)
        o = _dot(p, vbd) + o_inter
        s_ref[...] = jnp.exp(tot) * s_old + jnp.where(bd_s, kvt, 0.0)
        o_ref[out_rows, :] = o[:n_out]

    def lockstep(*gens):
        live = list(gens)
        while live:
            nxt = []
            for gen in live:
                try:
                    next(gen)
                    nxt.append(gen)
                except StopIteration:
                    pass
            live = nxt

    sf_ref[...] = jnp.zeros(sf_ref.shape, F32)
    sb_ref[...] = jnp.zeros(sb_ref.shape, F32)
    meta_rows = pl.ds(0, c)
    lockstep(chunk(meta_rows, valid0, cum_decay(meta_rows, valid0, tri_f, 0), keep_f, c - 1,
                   sf_ref, of_ref, pl.ds(0, N_META), N_META))

    def rows_fwd(i):
        return pl.ds(pl.multiple_of(N_META + i * c, 16), c)

    def rows_bwd(i):
        return pl.ds(pl.multiple_of(N_META + (n_real - 1 - i) * c, 16), c)

    def body(i, cums):
        cum_f, cum_b = cums
        nxt = jnp.minimum(i + 1, n_real - 1)
        cums = (cum_decay(rows_fwd(nxt), None, tri_f, 0), cum_decay(rows_bwd(nxt), None, tri_b, nk))
        lockstep(chunk(rows_fwd(i), None, cum_f, keep_f, c - 1, sf_ref, of_ref, rows_fwd(i), c),
                 chunk(rows_bwd(i), None, cum_b, keep_b, 0, sb_ref, ob_ref, rows_bwd(i), c))
        return cums

    first = (cum_decay(rows_fwd(0), None, tri_f, 0), cum_decay(rows_bwd(0), None, tri_b, nk))
    lax.fori_loop(0, n_real, body, first)
    lockstep(chunk(meta_rows, valid0, cum_decay(meta_rows, valid0, tri_b, nk), keep_b, 0,
                   sb_ref, ob_ref, pl.ds(0, N_META), N_META))

    def epilogue(i, carry):
        rows = pl.ds(pl.multiple_of(i * GLA_EPI_TILE, 16), GLA_EPI_TILE)
        o = of_ref[rows, :] + ob_ref[rows, :]
        r = r_ref[rows, :].astype(F32)
        for h in range(GLA_HEADS):
            cols = slice(h * GLA_DV, (h + 1) * GLA_DV)
            oh = o[:, cols]
            ms = jnp.mean(oh * oh, axis=-1, keepdims=True)
            yh = oh * lax.rsqrt(ms + NORM_EPS) * nw_ref[...] * _silu(r[:, cols])
            y_ref[rows, cols] = yh.astype(y_ref.dtype)
        return carry

    lax.fori_loop(0, t // GLA_EPI_TILE, epilogue, 0)


def _gla(u3, wc, bc, norm_w):
    b, t, _ = u3.shape
    nk = GLA_HEADS * GLA_DK
    nv = GLA_HEADS * GLA_DV
    q_blk = 3 * SC_WIDTH // nk
    v_blk = (3 * SC_WIDTH + 2 * nk) // nv
    g_blk = (3 * SC_WIDTH + 2 * nk + 2 * nv) // LANES
    full = lambda shape: pl.BlockSpec(shape, lambda i: (0, 0))
    return pl.pallas_call(
        _gla_kernel,
        grid=(b,),
        in_specs=[pl.BlockSpec((None, t, nk), lambda i: (i, 0, q_blk)),
                  pl.BlockSpec((None, t, nk), lambda i: (i, 0, q_blk + 1)),
                  pl.BlockSpec((None, t, nv), lambda i: (i, 0, v_blk)),
                  pl.BlockSpec((None, t, nv), lambda i: (i, 0, v_blk + 1)),
                  pl.BlockSpec((None, t, LANES), lambda i: (i, 0, g_blk)),
                  full((LANES, 2 * nk)), full((1, 2 * nk)), full((1, GLA_DV))],
        out_specs=pl.BlockSpec((None, t, nv), lambda i: (i, 0, 0)),
        out_shape=jax.ShapeDtypeStruct((b, t, nv), BF16),
        scratch_shapes=[pltpu.VMEM((t, 2 * nk), F32),
                        pltpu.VMEM((t, nv), F32), pltpu.VMEM((t, nv), F32),
                        pltpu.VMEM((nv, nk), F32), pltpu.VMEM((nv, nk), F32)],
        compiler_params=_params("parallel"),
        name="gla",
    )(u3, u3, u3, u3, u3, wc, bc, norm_w)


GROUP_LANE0 = N_EXPERTS


def _route(x, g_ref, whi_ref, wlo_ref, b_ref, xn_ref, gates_ref, info_ref, inforow_ref, count_ref):
    ms = jnp.mean(x * x, axis=-1, keepdims=True)
    xn = x * lax.rsqrt(ms + NORM_EPS) * g_ref[...]
    xhi = xn.astype(BF16)
    xn_ref[...] = xhi
    xlo = (xn - xhi.astype(F32)).astype(BF16)
    logits = _dot(xhi, whi_ref[...]) + _dot(xhi, wlo_ref[...]) + _dot(xlo, whi_ref[...]) + b_ref[...]

    lane = lax.broadcasted_iota(jnp.int32, logits.shape, 1)
    neg = -jnp.inf
    big = jnp.int32(LANES)
    is_group = (lane >> 2) == (GROUP_LANE0 >> 2)
    gl = jnp.where(is_group, logits, neg)
    gmax = jnp.max(gl, axis=-1, keepdims=True)
    g_p = 1.0 / jnp.sum(jnp.exp(gl - gmax), axis=-1, keepdims=True)
    g_idx = jnp.min(jnp.where(gl == gmax, lane, big), axis=-1, keepdims=True) - GROUP_LANE0
    el = jnp.where((lane >> 3) == g_idx, logits, neg)
    m1 = jnp.max(el, axis=-1, keepdims=True)
    i1 = jnp.min(jnp.where(el == m1, lane, big), axis=-1, keepdims=True)
    el2 = jnp.where(lane == i1, neg, el)
    m2 = jnp.max(el2, axis=-1, keepdims=True)
    i2 = jnp.min(jnp.where(el2 == m2, lane, big), axis=-1, keepdims=True)
    e2 = jnp.exp(m2 - m1)
    w1 = g_p / (1.0 + e2)
    w2 = g_p * e2 / (1.0 + e2)
    gates_ref[...] = jnp.where(lane == i1, w1, 0.0) + jnp.where(lane == i2, w2, 0.0)

    tm = x.shape[0]
    onehot = jnp.where(lane == g_idx, 1.0, 0.0)
    r_i = lax.broadcasted_iota(jnp.int32, (tm, tm), 0)
    c_i = lax.broadcasted_iota(jnp.int32, (tm, tm), 1)
    before = jnp.where(r_i > c_i, 1.0, 0.0).astype(BF16)
    prefix = _dot(before, onehot.astype(BF16))
    rank = jnp.sum(jnp.where(lane == g_idx, prefix, 0.0), axis=-1, keepdims=True).astype(jnp.int32)
    info_ref[...] = jnp.where(lane == 0, g_idx, jnp.where(lane == 1, rank, 0))
    eye = r_i == c_i
    g_row = jnp.sum(jnp.where(eye, g_idx, 0), axis=0, keepdims=True)
    rank_row = jnp.sum(jnp.where(eye, rank, 0), axis=0, keepdims=True)
    sub = lax.broadcasted_iota(jnp.int32, inforow_ref.shape, 0)
    inforow_ref[...] = jnp.where(sub == 0, g_row, jnp.where(sub == 1, rank_row, 0))
    counts = jnp.sum(onehot, axis=0, keepdims=True).astype(jnp.int32)
    count_ref[...] = jnp.broadcast_to(counts, count_ref.shape)


def _router_operands(norm_g, w_group, b_group, w_expert, b_expert):
    d = w_group.shape[0]
    pad = LANES - N_EXPERTS - MOE_GROUPS
    wr = jnp.concatenate([w_expert, w_group, jnp.zeros((d, pad), F32)], axis=1)
    whi = wr.astype(BF16)
    wlo = (wr - whi.astype(F32)).astype(BF16)
    bias = jnp.concatenate([b_expert, b_group, jnp.zeros((pad,), F32)]).reshape(1, LANES)
    return norm_g.reshape(1, d), whi, wlo, bias


def _router_specs(n, d, tm):
    nb = n // tm
    const = lambda shape: pl.BlockSpec(shape, lambda i: (0,) * len(shape))
    in_specs = [const((1, d)), const((d, LANES)), const((d, LANES)), const((1, LANES))]
    out_specs = [pl.BlockSpec((tm, d), lambda i: (i, 0)),
                 pl.BlockSpec((tm, d), lambda i: (i, 0)),
                 pl.BlockSpec((tm, LANES), lambda i: (i, 0)),
                 pl.BlockSpec((tm, LANES), lambda i: (i, 0)),
                 pl.BlockSpec((None, 8, tm), lambda i: (i, 0, 0)),
                 pl.BlockSpec((None, 8, LANES), lambda i: (i, 0, 0))]
    out_shape = [jax.ShapeDtypeStruct((n, d), F32),
                 jax.ShapeDtypeStruct((n, d), BF16),
                 jax.ShapeDtypeStruct((n, LANES), F32),
                 jax.ShapeDtypeStruct((n, LANES), jnp.int32),
                 jax.ShapeDtypeStruct((nb, 8, tm), jnp.int32),
                 jax.ShapeDtypeStruct((nb, 8, LANES), jnp.int32)]
    return in_specs, out_specs, out_shape


def _proj2_router_kernel(a_ref, b_ref, wa_ref, wb_ref, res_ref, g_ref, whi_ref, wlo_ref, bias_ref,
                         h_ref, *route_out):
    x = res_ref[...] + _dot(a_ref[...], wa_ref[...]) + _dot(b_ref[...], wb_ref[...])
    h_ref[...] = x
    _route(x, g_ref, whi_ref, wlo_ref, bias_ref, *route_out)


def _proj2_router(a, b, wa, wb, res, router_ops, tm):
    n, d = res.shape
    ka, kb = a.shape[1], b.shape[1]
    r_in, out_specs, out_shape = _router_specs(n, d, tm)
    return pl.pallas_call(
        _proj2_router_kernel,
        grid=(n // tm,),
        in_specs=[pl.BlockSpec((tm, ka), lambda i: (i, 0)),
                  pl.BlockSpec((tm, kb), lambda i: (i, 0)),
                  pl.BlockSpec((ka, d), lambda i: (0, 0)),
                  pl.BlockSpec((kb, d), lambda i: (0, 0)),
                  pl.BlockSpec((tm, d), lambda i: (i, 0))] + r_in,
        out_specs=out_specs,
        out_shape=out_shape,
        compiler_params=_params("parallel"),
        name="proj2_router",
    )(a, b, wa, wb, res, *router_ops)


def _att_out_router_kernel(o_ref, w_ref, hres_ref, g_ref, whi_ref, wlo_ref, bias_ref, h_ref, *route_out):
    tm = o_ref.shape[0]
    tiles = (hres_ref.shape[0] - N_META) // tm
    r0 = pl.multiple_of(N_META + lax.rem(pl.program_id(0), tiles) * tm, 16)
    x = hres_ref[pl.ds(r0, tm), :] + _dot(o_ref[...], w_ref[...])
    h_ref[...] = x
    _route(x, g_ref, whi_ref, wlo_ref, bias_ref, *route_out)


def _att_out_router(o, w, h3, router_ops, tm):
    b, t, d = h3.shape
    n, nq = o.shape
    tiles = (t - N_META) // tm
    r_in, out_specs, out_shape = _router_specs(n, d, tm)
    return pl.pallas_call(
        _att_out_router_kernel,
        grid=(n // tm,),
        in_specs=[pl.BlockSpec((tm, nq), lambda i: (i, 0)),
                  pl.BlockSpec((nq, d), lambda i: (0, 0)),
                  pl.BlockSpec((None, t, d), lambda i: (i // tiles, 0, 0))] + r_in,
        out_specs=out_specs,
        out_shape=out_shape,
        compiler_params=_params("parallel"),
        name="att_out_router",
    )(o, w, h3, *router_ops)


MOE_TILE = 512
SEG_ALIGN = 16
STRIP = 32


def _pow2_sizes(max_rows, min_rows):
    sizes, s = [], min_rows
    while s <= max_rows:
        sizes.append(s)
        s *= 2
    return sizes[::-1]


def _strip_copies(n_rows, sizes, make_copy):
    pos = 0
    for sz in sizes:
        bit = n_rows & sz
        yield bit, make_copy(pos, sz)
        pos = pos + bit


def _slot_key(group, rank, boff_ref, b):
    key = rank
    for g in range(MOE_GROUPS):
        key = key + jnp.where(group == g, boff_ref[MOE_GROUPS * b + g], 0)
    return key


def _start_all(copies):
    for bit, cs in copies:
        @pl.when(bit != 0)
        def _():
            for c in cs:
                c.start()


def _wait_all(copies):
    for bit, cs in copies:
        @pl.when(bit != 0)
        def _():
            for c in cs:
                c.wait()


def _dispatch_kernel(off_ref, cpad_ref, boff_ref, tail_ref, x_ref, gates_ref, inforow_ref,
                     xs_hbm, gs_hbm, xstg, gstg, zx, zg, sem):
    b = pl.program_id(0)
    last = pl.num_programs(0) - 1
    tb = x_ref.shape[0]
    ks = xstg.shape[1]
    cur = lax.rem(b, 2)
    key = _slot_key(inforow_ref[0:1, :], inforow_ref[1:2, :], boff_ref, b)
    slot = lax.broadcasted_iota(jnp.int32, (ks, tb), 0)
    perm = jnp.where(slot == key, 1.0, 0.0).astype(BF16)
    xstg[cur] = _dot(perm, x_ref[...]).astype(BF16)
    gt = gates_ref[...]
    hi = gt.astype(BF16)
    lo = (gt - hi.astype(F32)).astype(BF16)
    gstg[cur] = _dot(perm, hi) + _dot(perm, lo)

    sizes = _pow2_sizes(tb, SEG_ALIGN)

    def copies(blk, sl):
        for g in range(MOE_GROUPS):
            src0 = boff_ref[MOE_GROUPS * blk + g]
            dst0 = off_ref[MOE_GROUPS * blk + g]

            def make(pos, sz, src0=src0, dst0=dst0):
                src = pl.ds(pl.multiple_of(src0 + pos, SEG_ALIGN), sz)
                dst = pl.ds(pl.multiple_of(dst0 + pos, SEG_ALIGN), sz)
                return (pltpu.make_async_copy(xstg.at[sl, src, :], xs_hbm.at[dst, :], sem.at[sl, 0]),
                        pltpu.make_async_copy(gstg.at[sl, src, :], gs_hbm.at[dst, :], sem.at[sl, 1]))

            yield from _strip_copies(cpad_ref[MOE_GROUPS * blk + g], sizes, make)

    _start_all(copies(b, cur))

    @pl.when(b > 0)
    def _():
        _wait_all(copies(jnp.maximum(b - 1, 0), 1 - cur))

    @pl.when(b == last)
    def _():
        _wait_all(copies(b, cur))
        zx[...] = jnp.zeros(zx.shape, BF16)
        zg[...] = jnp.zeros(zg.shape, F32)
        tail_sizes = _pow2_sizes(MOE_TILE // 2, SEG_ALIGN)

        def tails():
            for g in range(MOE_GROUPS):
                dst0 = tail_ref[g]

                def make(pos, sz, dst0=dst0):
                    dst = pl.ds(pl.multiple_of(dst0 + pos, SEG_ALIGN), sz)
                    return (pltpu.make_async_copy(zx.at[pl.ds(0, sz), :], xs_hbm.at[dst, :], sem.at[0, 0]),
                            pltpu.make_async_copy(zg.at[pl.ds(0, sz), :], gs_hbm.at[dst, :], sem.at[0, 1]))

                yield from _strip_copies(tail_ref[MOE_GROUPS + g], tail_sizes, make)

        _start_all(tails())
        _wait_all(tails())

        def spare(i):
            dst = pl.ds(pl.multiple_of(i * MOE_TILE, MOE_TILE), MOE_TILE)
            return (pltpu.make_async_copy(zx, xs_hbm.at[dst, :], sem.at[0, 0]),
                    pltpu.make_async_copy(zg, gs_hbm.at[dst, :], sem.at[0, 1]))

        first_spare = tail_ref[2 * MOE_GROUPS]
        n_tiles = xs_hbm.shape[0] // MOE_TILE

        @pl.loop(first_spare, n_tiles)
        def _(i):
            cx, cg = spare(i)
            cx.start()
            cg.start()

        @pl.loop(first_spare, n_tiles)
        def _(i):
            cx, cg = spare(i)
            cx.wait()
            cg.wait()


def _dispatch(xn, gates, inforow, off, cpad, boff, tail, n_sorted, tb, ks):
    n, d = xn.shape
    nb = n // tb
    grid_spec = pltpu.PrefetchScalarGridSpec(
        num_scalar_prefetch=4,
        grid=(nb,),
        in_specs=[pl.BlockSpec((tb, d), lambda i, *_: (i, 0)),
                  pl.BlockSpec((tb, LANES), lambda i, *_: (i, 0)),
                  pl.BlockSpec((None, 8, tb), lambda i, *_: (i, 0, 0))],
        out_specs=[pl.BlockSpec(memory_space=pl.ANY), pl.BlockSpec(memory_space=pl.ANY)],
        scratch_shapes=[pltpu.VMEM((2, ks, d), BF16), pltpu.VMEM((2, ks, LANES), F32),
                        pltpu.VMEM((MOE_TILE, d), BF16), pltpu.VMEM((MOE_TILE, LANES), F32),
                        pltpu.SemaphoreType.DMA((2, 2))])
    return pl.pallas_call(
        _dispatch_kernel,
        grid_spec=grid_spec,
        out_shape=[jax.ShapeDtypeStruct((n_sorted, d), BF16),
                   jax.ShapeDtypeStruct((n_sorted, LANES), F32)],
        compiler_params=_params("arbitrary"),
        name="moe_dispatch",
    )(off, cpad, boff, tail, xn, gates, inforow)


def _group_mlp_kernel(tg_ref, nused_ref, x_ref, gs_ref, wg_ref, wu_ref, wd_ref, y_ref):
    i = pl.program_id(0)

    @pl.when(i < nused_ref[0])
    def _():
        x = x_ref[...]
        gs = gs_ref[...]
        lane = lax.broadcasted_iota(jnp.int32, gs.shape, 1)
        lane0 = tg_ref[i] * MOE_EXPERTS_PER_GROUP
        acc = jnp.zeros(y_ref.shape, F32)
        for e in range(MOE_EXPERTS_PER_GROUP):
            he = _silu(_dot(x, wg_ref[e])) * _dot(x, wu_ref[e])
            gate = jnp.sum(jnp.where(lane == lane0 + e, gs, 0.0), axis=-1, keepdims=True)
            acc = acc + _dot((he * gate).astype(BF16), wd_ref[e])
        y_ref[...] = acc.astype(y_ref.dtype)

    @pl.when(i >= nused_ref[0])
    def _():
        y_ref[...] = jnp.zeros(y_ref.shape, y_ref.dtype)


def _cast_kernel(w_ref, o_ref):
    o_ref[...] = w_ref[...].astype(o_ref.dtype)


def _cast_experts(w):
    nl, ne, a, b = w.shape
    blk = (None, MOE_EXPERTS_PER_GROUP, a, b)
    return pl.pallas_call(
        _cast_kernel,
        grid=(nl, ne // MOE_EXPERTS_PER_GROUP),
        in_specs=[pl.BlockSpec(blk, lambda l, g: (l, g, 0, 0))],
        out_specs=pl.BlockSpec(blk, lambda l, g: (l, g, 0, 0)),
        out_shape=jax.ShapeDtypeStruct(w.shape, BF16),
        compiler_params=_params("parallel", "parallel"),
        name="cast_experts",
    )(w)


def _group_mlp(xs, gs, wg, wu, wd, layer, tile_group, nused):
    n_sorted, d = xs.shape
    ne = MOE_EXPERTS_PER_GROUP

    def rows(i, tg, nu):
        return (jnp.minimum(i, nu[0] - 1), 0)

    def group(i, tg, nu):
        return (layer, tg[i], 0, 0)

    grid_spec = pltpu.PrefetchScalarGridSpec(
        num_scalar_prefetch=2,
        grid=(n_sorted // MOE_TILE,),
        in_specs=[pl.BlockSpec((MOE_TILE, d), rows),
                  pl.BlockSpec((MOE_TILE, LANES), rows),
                  pl.BlockSpec((None, ne, d, MOE_D_FF), group),
                  pl.BlockSpec((None, ne, d, MOE_D_FF), group),
                  pl.BlockSpec((None, ne, MOE_D_FF, d), group)],
        out_specs=pl.BlockSpec((MOE_TILE, d), lambda i, tg, nu: (i, 0)))
    return pl.pallas_call(
        _group_mlp_kernel,
        grid_spec=grid_spec,
        out_shape=jax.ShapeDtypeStruct((n_sorted, d), BF16),
        compiler_params=_params("arbitrary"),
        name="moe_experts",
    )(tile_group, nused, xs, gs, wg, wu, wd)


def _combine_kernel(off_ref, cpad_ref, boff_ref, info_ref, res_ref, fg_ref, y_hbm, o_ref, ybuf, sem,
                    *, final_norm):
    b = pl.program_id(0)
    last = pl.num_programs(0) - 1
    tb = res_ref.shape[0]
    ks = ybuf.shape[1]
    cur = lax.rem(b, 2)
    sizes = _pow2_sizes(tb, STRIP)

    def copies(blk, sl):
        for g in range(MOE_GROUPS):
            src0 = off_ref[MOE_GROUPS * blk + g]
            dst0 = boff_ref[MOE_GROUPS * blk + g]

            def make(pos, sz, src0=src0, dst0=dst0):
                src = pl.ds(pl.multiple_of(src0 + pos, SEG_ALIGN), sz)
                dst = pl.ds(pl.multiple_of(dst0 + pos, STRIP), sz)
                return (pltpu.make_async_copy(y_hbm.at[src, :], ybuf.at[sl, dst, :], sem.at[sl]),)

            n_rows = (cpad_ref[MOE_GROUPS * blk + g] + (STRIP - 1)) & (-STRIP)
            yield from _strip_copies(n_rows, sizes, make)

    @pl.when(b == 0)
    def _():
        ybuf[...] = jnp.zeros(ybuf.shape, ybuf.dtype)
        _start_all(copies(b, cur))

    @pl.when(b < last)
    def _():
        _start_all(copies(jnp.minimum(b + 1, last), 1 - cur))

    _wait_all(copies(b, cur))

    key = _slot_key(info_ref[:, 0:1], info_ref[:, 1:2], boff_ref, b)
    slot = lax.broadcasted_iota(jnp.int32, (tb, ks), 1)
    perm = jnp.where(slot == key, 1.0, 0.0).astype(BF16)
    y = res_ref[...] + _dot(perm, ybuf[cur])
    if final_norm:
        ms = jnp.mean(y * y, axis=-1, keepdims=True)
        y = y * lax.rsqrt(ms + NORM_EPS) * fg_ref[...]
    o_ref[...] = y


def _combine(ys, info, res, final_g, off, cpad, boff, tb, ks, final_norm):
    n, d = res.shape
    grid_spec = pltpu.PrefetchScalarGridSpec(
        num_scalar_prefetch=3,
        grid=(n // tb,),
        in_specs=[pl.BlockSpec((tb, LANES), lambda i, *_: (i, 0)),
                  pl.BlockSpec((tb, d), lambda i, *_: (i, 0)),
                  pl.BlockSpec((1, d), lambda i, *_: (0, 0)),
                  pl.BlockSpec(memory_space=pl.ANY)],
        out_specs=pl.BlockSpec((tb, d), lambda i, *_: (i, 0)),
        scratch_shapes=[pltpu.VMEM((2, ks, d), BF16), pltpu.SemaphoreType.DMA((2,))])
    return pl.pallas_call(
        functools.partial(_combine_kernel, final_norm=final_norm),
        grid_spec=grid_spec,
        out_shape=jax.ShapeDtypeStruct((n, d), F32),
        compiler_params=_params("arbitrary"),
        name="moe_combine",
    )(off, cpad, boff, info, res, final_g.reshape(1, d), ys)


def _round_up(x, m):
    return (x + m - 1) // m * m


def _moe_layer(routed, w_gate, w_up, w_down, layer, final_g, tb, final_norm):
    h, xn, gates, info, inforow, counts = routed
    n, d = h.shape
    nb = n // tb

    cnt = counts[:, 0, :MOE_GROUPS]
    cpad = _round_up(cnt, SEG_ALIGN)
    strip = _round_up(cnt, STRIP)
    used = jnp.sum(cpad, axis=0)
    region = _round_up(used, MOE_TILE)
    region_start = jnp.cumsum(region) - region
    off = region_start[None, :] + jnp.cumsum(cpad, axis=0) - cpad
    boff = jnp.cumsum(strip, axis=1) - strip
    nused = (jnp.sum(region) // MOE_TILE).reshape(1)
    tail = jnp.concatenate([region_start + used, region - used, nused]).astype(jnp.int32)
    n_tiles = -(-(n + nb * MOE_GROUPS * (SEG_ALIGN - 1) + MOE_GROUPS * (MOE_TILE - 1)) // MOE_TILE) + 1
    tile_start = jnp.arange(n_tiles, dtype=jnp.int32) * MOE_TILE
    region_end = region_start + region
    tile_group = jnp.minimum(jnp.sum(tile_start[:, None] >= region_end[None, :], axis=1), MOE_GROUPS - 1)
    ks =tb + MOE_GROUPS * STRIP

    flat = lambda a: a.reshape(-1).astype(jnp.int32)
    xs, gs = _dispatch(xn, gates, inforow, flat(off), flat(cpad), flat(boff), tail,
                       n_tiles * MOE_TILE, tb, ks)
    ys = _group_mlp(xs, gs, w_gate, w_up, w_down, layer,
                    tile_group.astype(jnp.int32), nused.astype(jnp.int32))
    return _combine(ys, info, h, final_g, flat(off), flat(cpad), flat(boff), tb, ks, final_norm)


ROPE_ROW_TILE = 688
ATT_Q_TILE = 256
ATT_TILES_PER_ITER = 2
ATT_SCORES_AHEAD = 2
ATT_SCORE_SLOTS = 4
ATT_VT_ROWS = ATT_HEAD_DIM + 16
ATT_Q_SCALE = ATT_HEAD_DIM ** -0.5 * 1.4426950408889634


def _eye(n):
    return jnp.where(lax.broadcasted_iota(jnp.int32, (n, n), 0) == lax.broadcasted_iota(jnp.int32, (n, n), 1),
                     1.0, 0.0).astype(BF16)


def _transpose_bf16(x, eye):
    return lax.dot_general(eye, x, _NT, preferred_element_type=F32).astype(BF16)


def _rope_kernel(q_ref, k_ref, v_ref, cos_ref, sin_ref, qn_ref, kn_ref, qt_ref, kp_ref, vt_ref):
    t = q_ref.shape[0]
    tq = ATT_Q_TILE
    dh = ATT_HEAD_DIM
    gi = lax.broadcasted_iota(jnp.int32, (LANES, LANES), 0) >> 6
    gj = lax.broadcasted_iota(jnp.int32, (LANES, LANES), 1) >> 6
    gmat = jnp.where(gi == gj, 1.0, 0.0).astype(BF16)
    eye_q = _eye(LANES)

    def norm_rope(x, gain, cos, sin):
        lane = lax.broadcasted_iota(jnp.int32, x.shape, 1)
        sq = x * x
        hi = sq.astype(BF16)
        lo = (sq - hi.astype(F32)).astype(BF16)
        ms = (_dot(hi, gmat) + _dot(lo, gmat)) * (1.0 / dh)
        xn = x * lax.rsqrt(ms + NORM_EPS) * gain
        partner = jnp.where((lane & 1) == 0, pltpu.roll(xn, LANES - 1, 1), pltpu.roll(xn, 1, 1))
        return xn * cos + partner * sin

    for j in range(q_ref.shape[1] // LANES):
        cols = slice(j * LANES, (j + 1) * LANES)
        y = norm_rope(q_ref[N_META:, cols].astype(F32), qn_ref[...], cos_ref[N_META:, :], sin_ref[N_META:, :])
        yt = _transpose_bf16((y * ATT_Q_SCALE).astype(BF16), eye_q)
        for i in range((t - N_META) // tq):
            qt_ref[i, cols, :] = yt[:, i * tq:(i + 1) * tq]

    low = lax.broadcasted_iota(jnp.int32, (t, LANES), 1) < dh
    for j in range(k_ref.shape[1] // LANES):
        cols = slice(j * LANES, (j + 1) * LANES)
        y = norm_rope(k_ref[:, cols].astype(F32), kn_ref[...], cos_ref[...], sin_ref[...])
        kp_ref[2 * j] = jnp.where(low, y, 0.0).astype(BF16)
        kp_ref[2 * j + 1] = jnp.where(low, pltpu.roll(y, dh, 1), 0.0).astype(BF16)

    n_real = t - N_META
    eye_v = _eye(v_ref.shape[1])
    vt_real = _transpose_bf16(v_ref[N_META:, :], eye_v)
    vt_meta = _transpose_bf16(v_ref[0:N_META, :], eye_v)
    for h in range(ATT_KV_HEADS):
        vt_ref[h, 0:dh, 0:n_real] = vt_real[h * dh:(h + 1) * dh, :]
        vt_ref[h, 0:dh, n_real:t] = vt_meta[h * dh:(h + 1) * dh, :]
        extra = lax.broadcasted_iota(jnp.int32, (ATT_VT_ROWS - dh, t), 0)
        vt_ref[h, dh:ATT_VT_ROWS, :] = jnp.where(extra == 0, 1.0, 0.0).astype(BF16)


def _rope(u3, cos_e, sin_e, qn, kn):
    b, t, _ = u3.shape
    nq = ATT_HEADS * ATT_HEAD_DIM
    nkv = ATT_KV_HEADS * ATT_HEAD_DIM
    n_qt = (t - N_META) // ATT_Q_TILE
    full = lambda shape: pl.BlockSpec(shape, lambda i: (0, 0))
    return pl.pallas_call(
        _rope_kernel,
        grid=(b,),
        in_specs=[pl.BlockSpec((None, t, nq), lambda i: (i, 0, 0)),
                  pl.BlockSpec((None, t, nkv), lambda i: (i, 0, nq // nkv)),
                  pl.BlockSpec((None, t, nkv), lambda i: (i, 0, nq // nkv + 1)),
                  full((t, LANES)), full((t, LANES)), full((1, LANES)), full((1, LANES))],
        out_specs=[pl.BlockSpec((None, n_qt, nq, ATT_Q_TILE), lambda i: (i, 0, 0, 0)),
                   pl.BlockSpec((None, ATT_KV_HEADS, t, LANES), lambda i: (i, 0, 0, 0)),
                   pl.BlockSpec((None, ATT_KV_HEADS, ATT_VT_ROWS, t), lambda i: (i, 0, 0, 0))],
        out_shape=[jax.ShapeDtypeStruct((b, n_qt, nq, ATT_Q_TILE), BF16),
                   jax.ShapeDtypeStruct((b, ATT_KV_HEADS, t, LANES), BF16),
                   jax.ShapeDtypeStruct((b, ATT_KV_HEADS, ATT_VT_ROWS, t), BF16)],
        compiler_params=_params("parallel"),
        name="qk_norm_rope",
    )(u3, u3, u3, cos_e, sin_e, qn, kn)


def _attn_kernel(qt_ref, k_ref, vt_ref, o_ref, s_ref):
    n_qt, w, tq = qt_ref.shape
    dh = ATT_HEAD_DIM
    n_real = k_ref.shape[0] - N_META
    eye = _eye(tq)
    zpad = jnp.zeros((LANES - dh, tq), BF16)

    def scores(i, g, slot):
        qp = jnp.concatenate([qt_ref[i, g * dh:(g + 1) * dh, :], zpad], axis=0)
        s_r = _dot(k_ref[N_META:, :], qp)
        s_m = _dot(k_ref[0:N_META, :], qp)
        s_ref[slot, 0:n_real, :] = s_r
        s_ref[slot, n_real:, :] = s_m
        return jnp.maximum(jnp.max(s_r, axis=0, keepdims=True), jnp.max(s_m, axis=0, keepdims=True))

    steps = [(u, g) for u in range(ATT_TILES_PER_ITER) for g in range(ATT_GROUP)]
    n_steps = len(steps)
    n_iter = n_qt // ATT_TILES_PER_ITER
    ahead = ATT_SCORES_AHEAD
    n_slots = s_ref.shape[0]
    assert n_steps % n_slots == 0 and ahead < n_slots

    def issue_scores(it, n):
        it2 = jnp.minimum(it + n // n_steps, n_iter - 1)
        u, g = steps[n % n_steps]
        return scores(it2 * ATT_TILES_PER_ITER + u, g, n % n_slots)

    def q_tiles(it, pending):
        i0 = it * ATT_TILES_PER_ITER
        outs = []
        pending = list(pending)
        for n, (u, g) in enumerate(steps):
            m, slot = pending.pop(0), n % n_slots
            pending.append(issue_scores(it, n + ahead))
            p_r = jnp.exp2(s_ref[slot, 0:n_real, :] - m).astype(BF16)
            p_m = jnp.exp2(s_ref[slot, n_real:, :] - m).astype(BF16)
            ot = _dot(vt_ref[:, 0:n_real], p_r) + _dot(vt_ref[:, n_real:], p_m)
            outs.append((ot[0:dh] * (1.0 / ot[dh:dh + 1])).astype(BF16))
            if g == ATT_GROUP - 1:
                rows = pl.ds(pl.multiple_of((i0 + u) * tq, tq), tq)
                ot_all = jnp.concatenate(outs, axis=0)
                o_ref[rows, :] = lax.dot_general(eye, ot_all, _NT, preferred_element_type=F32).astype(o_ref.dtype)
                outs = []
        return tuple(pending)

    first = tuple(issue_scores(0, n) for n in range(ahead))
    lax.fori_loop(0, n_iter, q_tiles, first)


def _attention(qt, kp, vt):
    b, n_qt, nq, tq = qt.shape
    t = kp.shape[2]
    w = ATT_GROUP * ATT_HEAD_DIM
    return pl.pallas_call(
        _attn_kernel,
        grid=(b, ATT_KV_HEADS),
        in_specs=[pl.BlockSpec((None, n_qt, w, tq), lambda i, h: (i, 0, h, 0)),
                  pl.BlockSpec((None, None, t, LANES), lambda i, h: (i, h, 0, 0)),
                  pl.BlockSpec((None, None, ATT_VT_ROWS, t), lambda i, h: (i, h, 0, 0))],
        out_specs=pl.BlockSpec((None, n_qt * tq, w), lambda i, h: (i, 0, h)),
        out_shape=jax.ShapeDtypeStruct((b, n_qt * tq, nq), BF16),
        scratch_shapes=[pltpu.VMEM((ATT_SCORE_SLOTS, t, tq), F32)],
        compiler_params=_params("parallel", "parallel"),
        name="gqa_attention",
    )(qt, kp, vt)


def _rope_tables(n_tokens):
    rows_n = n_tokens // GRID_W
    rows = jnp.repeat(jnp.arange(rows_n), GRID_W).astype(F32)
    cols = jnp.tile(jnp.arange(GRID_W), rows_n).astype(F32)
    axis_dims = ATT_HEAD_DIM // 2
    freqs = ROPE_THETA ** (-jnp.arange(0, axis_dims, 2, dtype=F32) / axis_dims)
    ang = jnp.concatenate([rows[:, None] * freqs, cols[:, None] * freqs], axis=-1)
    ang = jnp.concatenate([jnp.zeros((N_META, ang.shape[1]), F32), ang], axis=0)
    cos_e = jnp.tile(jnp.repeat(jnp.cos(ang), 2, axis=1), (1, LANES // ATT_HEAD_DIM))
    sign = jnp.tile(jnp.array([-1.0, 1.0], F32), LANES // 2)
    sin_e = jnp.tile(jnp.repeat(jnp.sin(ang), 2, axis=1), (1, LANES // ATT_HEAD_DIM)) * sign
    return cos_e, sin_e


def kernel(x, meta_tokens, norm_mix, norm_ffn, norm_final, sc_w_in, sc_conv_w, gla_w_gate_fwd, gla_b_gate_fwd, gla_w_gate_bwd, gla_b_gate_bwd, gla_norm_w, sc_w_out, att_w_in, att_q_norm, att_k_norm, att_w_out, moe_w_group, moe_b_group, moe_w_expert, moe_b_expert, moe_w_gate, moe_w_up, moe_w_down):
    bsz, s_len, d = x.shape
    t = s_len + N_META
    n0, n1 = bsz * t, bsz * s_len
    tm0 = t // 3
    tm1 = 512

    w_gate, w_up, w_down = _cast_experts(moe_w_gate), _cast_experts(moe_w_up), _cast_experts(moe_w_down)
    meta = jnp.broadcast_to(meta_tokens.astype(x.dtype)[None], (bsz, N_META, d))
    h = jnp.concatenate([meta, x], axis=1).reshape(n0, d)

    nk = GLA_HEADS * GLA_DK
    mix_in = sc_w_in.shape[2]
    mix_pad = -mix_in % MXU_WIDTH
    w_in = jnp.pad(sc_w_in[0], ((0, 0), (0, mix_pad))).astype(BF16)
    u = _norm_matmul(h, norm_mix[0], w_in, tm0, mix_in + mix_pad)
    u3 = u.reshape(bsz, t, mix_in + mix_pad)
    y_a = _gated_conv(u3, sc_conv_w[0])
    wc = jnp.zeros((LANES, 2 * nk), F32)
    wc = wc.at[:GLA_GATE_RANK, :nk].set(gla_w_gate_fwd[0])
    wc = wc.at[GLA_GATE_RANK:2 * GLA_GATE_RANK, nk:].set(gla_w_gate_bwd[0])
    bc = jnp.concatenate([gla_b_gate_fwd[0], gla_b_gate_bwd[0]]).reshape(1, 2 * nk)
    y_b = _gla(u3, wc.astype(BF16), bc, gla_norm_w[0].reshape(1, GLA_DV))
    w_out = sc_w_out[0].astype(BF16)
    tb0 = 3 * LANES
    router0 = _router_operands(norm_ffn[0], moe_w_group[0], moe_b_group[0], moe_w_expert[0], moe_b_expert[0])
    routed = _proj2_router(y_a.reshape(n0, SC_WIDTH), y_b.reshape(n0, -1), w_out[:SC_WIDTH], w_out[SC_WIDTH:],
                           h, router0, tb0)
    h = _moe_layer(routed, w_gate, w_up, w_down, 0, norm_final, tb0, False)

    u = _norm_matmul(h, norm_mix[1], att_w_in[0].astype(BF16), tm0, att_w_in.shape[2])
    u3 = u.reshape(bsz, t, -1)
    cos_e, sin_e = _rope_tables(s_len)
    qn = jnp.tile(att_q_norm[0], LANES // ATT_HEAD_DIM).reshape(1, LANES)
    kn = jnp.tile(att_k_norm[0], LANES // ATT_HEAD_DIM).reshape(1, LANES)
    qt, kp, vt = _rope(u3, cos_e, sin_e, qn, kn)
    o = _attention(qt, kp, vt)
    router1 = _router_operands(norm_ffn[1], moe_w_group[1], moe_b_group[1], moe_w_expert[1], moe_b_expert[1])
    routed = _att_out_router(o.reshape(n1, d), att_w_out[0].astype(BF16), h.reshape(bsz, t, d), router1, tm1)
    h = _moe_layer(routed, w_gate, w_up, w_down, 1, norm_final, tm1, True)
    return h.reshape(bsz, s_len, d)
```

```python
import functools

import jax
import jax.numpy as jnp
from jax import lax
from jax.experimental import pallas as pl
from jax.experimental.pallas import tpu as pltpu

F32 = jnp.float32
BF16 = jnp.bfloat16

NORM_EPS = 1e-6
N_META = 16
GRID_W = 64
ROPE_THETA = 10000.0
SC_WIDTH = 512
GLA_HEADS = 4
GLA_DK = 64
GLA_DV = 128
GLA_GATE_RANK = 16
GLA_GATE_TAU = 16.0
GLA_CHUNK = 64
ATT_HEAD_DIM = 64
ATT_HEADS = 16
ATT_KV_HEADS = 4
ATT_GROUP = ATT_HEADS // ATT_KV_HEADS
MOE_GROUPS = 4
MOE_EXPERTS_PER_GROUP = 8
N_EXPERTS = 32
MOE_D_FF = 256

LANES = 128
MXU_WIDTH = 256
MIB = 2 ** 20
VMEM_LIMIT = 56 * MIB

_NT = (((1,), (1,)), ((), ()))
_TN = (((0,), (0,)), ((), ()))


def _params(*semantics):
    return pltpu.CompilerParams(dimension_semantics=semantics, vmem_limit_bytes=VMEM_LIMIT)


def _dot(a, b):
    return jnp.dot(a, b, preferred_element_type=F32)


def _silu(x):
    return x * (1.0 / (1.0 + jnp.exp(-x)))


def _norm_matmul_kernel(h_ref, g_ref, w_ref, o_ref, xn_ref):
    @pl.when(pl.program_id(1) == 0)
    def _():
        x = h_ref[...]
        ms = jnp.mean(x * x, axis=-1, keepdims=True)
        xn_ref[...] = (x * lax.rsqrt(ms + NORM_EPS) * g_ref[...]).astype(BF16)

    o_ref[...] = _dot(xn_ref[...], w_ref[...]).astype(o_ref.dtype)


def _norm_matmul(h, g, w, tm, tn):
    n, d = h.shape
    nout = w.shape[1]
    return pl.pallas_call(
        _norm_matmul_kernel,
        grid=(n // tm, nout // tn),
        in_specs=[pl.BlockSpec((tm, d), lambda i, j: (i, 0)),
                  pl.BlockSpec((1, d), lambda i, j: (0, 0)),
                  pl.BlockSpec((d, tn), lambda i, j: (0, j))],
        out_specs=pl.BlockSpec((tm, tn), lambda i, j: (i, j)),
        out_shape=jax.ShapeDtypeStruct((n, nout), BF16),
        scratch_shapes=[pltpu.VMEM((tm, d), BF16)],
        compiler_params=_params("parallel", "arbitrary"),
        name="norm_matmul",
    )(h, g.reshape(1, d), w)


def _expert_cast_specs(weights, layer, grid_shape):
    n_steps = 1
    for g in grid_shape:
        n_steps *= g
    assert N_EXPERTS % n_steps == 0
    per_step = N_EXPERTS // n_steps

    def step(*idx):
        flat = 0
        for i, g in zip(idx, grid_shape):
            flat = flat * g + i
        return flat

    in_specs = [pl.BlockSpec((None, per_step) + w.shape[2:], lambda *idx: (layer, step(*idx), 0, 0))
                for w in weights]
    out_specs = [pl.BlockSpec((per_step,) + w.shape[2:], lambda *idx: (step(*idx), 0, 0)) for w in weights]
    out_shape = [jax.ShapeDtypeStruct(w.shape[1:], BF16) for w in weights]
    return in_specs, out_specs, out_shape


def _cast_through(in_refs, out_refs):
    for src, dst in zip(in_refs, out_refs):
        dst[...] = src[...].astype(dst.dtype)


def _conv_kernel(ab_ref, ac_ref, ax_ref, w_ref, wg_ref, wu_ref, wd_ref, o_ref, wg_o, wu_o, wd_o, z_ref):
    _cast_through((wg_ref, wu_ref, wd_ref), (wg_o, wu_o, wd_o))
    t, c = ab_ref.shape
    z = ac_ref[...].astype(F32) * ax_ref[...].astype(F32)
    z_ref[0:8, :] = jnp.zeros((8, c), F32)
    z_ref[8:8 + t, :] = z
    z_ref[8 + t:16 + t, :] = jnp.zeros((8, c), F32)
    w = w_ref[...]
    conv = w[0:1, :] * z_ref[7:7 + t, :] + w[1:2, :] * z + w[2:3, :] * z_ref[9:9 + t, :]
    o_ref[...] = (ab_ref[...].astype(F32) * conv).astype(o_ref.dtype)


def _gated_conv(u3, conv_w, expert_weights, layer):
    b, t, _ = u3.shape
    nblk = SC_WIDTH // LANES
    grid = (b, nblk)
    c_in, c_out, c_shape = _expert_cast_specs(expert_weights, layer, grid)

    def col(off):
        return pl.BlockSpec((None, t, LANES), lambda i, j: (i, 0, off + j))

    y_a, *cast = pl.pallas_call(
        _conv_kernel,
        grid=grid,
        in_specs=[col(0), col(nblk), col(2 * nblk),
                  pl.BlockSpec((3, LANES), lambda i, j: (0, j))] + c_in,
        out_specs=[pl.BlockSpec((None, t, LANES), lambda i, j: (i, 0, j))] + c_out,
        out_shape=[jax.ShapeDtypeStruct((b, t, SC_WIDTH), BF16)] + c_shape,
        scratch_shapes=[pltpu.VMEM((t + 16, LANES), F32)],
        compiler_params=_params("parallel", "parallel"),
        name="gated_conv",
    )(u3, u3, u3, conv_w, *expert_weights)
    return y_a, cast


GLA_ROW_TILE = 688
GLA_EPI_TILE = 688


def _gla_kernel(q_ref, k_ref, v_ref, r_ref, g_ref, wc_ref, bc_ref, nw_ref, y_ref,
                la_ref, of_ref, ob_ref, sf_ref, sb_ref):
    t, nk = q_ref.shape
    nv = v_ref.shape[1]
    c = GLA_CHUNK
    n_real = (t - N_META) // c

    for i in range(t // GLA_ROW_TILE):
        rows = pl.ds(i * GLA_ROW_TILE, GLA_ROW_TILE)
        pre = _dot(g_ref[rows, :], wc_ref[...]) + bc_ref[...]
        la_ref[rows, :] = (jnp.minimum(pre, 0.0) - jnp.log(1.0 + jnp.exp(-jnp.abs(pre)))) * (1.0 / GLA_GATE_TAU)

    def iota(shape, dim):
        return lax.broadcasted_iota(jnp.int32, shape, dim)

    tri_f = jnp.where(iota((c, c), 0) >= iota((c, c), 1), 1.0, 0.0).astype(BF16)
    tri_b = jnp.where(iota((c, c), 0) <= iota((c, c), 1), 1.0, 0.0).astype(BF16)
    keep_f = (iota((c, nk), 1) & (c - 1)) <= iota((c, nk), 0)
    keep_b = (iota((c, nk), 1) & (c - 1)) > iota((c, nk), 0)
    bd_k = (iota((nk, nk), 0) >> 6) == (iota((nk, nk), 1) >> 6)
    bd_v = (iota((nk, nv), 0) >> 6) == (iota((nk, nv), 1) >> 7)
    bd_s = (iota((nv, nk), 0) >> 7) == (iota((nv, nk), 1) >> 6)
    valid0 = iota((c, 1), 0) < N_META

    def cum_decay(rows, valid, tri, la_col):
        la = la_ref[rows, la_col:la_col + nk]
        if valid is not None:
            la = jnp.where(valid, la, 0.0)
        hi = la.astype(BF16)
        rem = la - hi.astype(F32)
        mid = rem.astype(BF16)
        lo = (rem - mid.astype(F32)).astype(BF16)
        return _dot(tri, hi) + _dot(tri, mid) + _dot(tri, lo)

    def chunk(rows, valid, cum, keep, tot_row, s_ref, o_ref, out_rows, n_out):
        q = q_ref[rows, :].astype(F32) * (GLA_DK ** -0.5)
        k = k_ref[rows, :].astype(F32)
        v = v_ref[rows, :].astype(F32)
        if valid is not None:
            q = jnp.where(valid, q, 0.0)
            k = jnp.where(valid, k, 0.0)
            v = jnp.where(valid, v, 0.0)
        tot = cum[tot_row:tot_row + 1, :]
        q_dec = (q * jnp.exp(cum)).astype(BF16)
        k_inv = k * jnp.exp(-cum)
        k_end = (k * jnp.exp(tot - cum)).astype(BF16)
        kbd = jnp.where(bd_k, jnp.concatenate([k_inv] * GLA_HEADS, axis=0), 0.0).astype(BF16)
        p = lax.dot_general(q_dec, kbd, _NT, preferred_element_type=F32)
        s_old = s_ref[...]
        o_inter = lax.dot_general(q_dec, s_old.astype(BF16), _NT, preferred_element_type=F32)
        kvt = lax.dot_general(v.astype(BF16), k_end, _TN, preferred_element_type=F32)
        yield
        p = jnp.where(keep, p, 0.0).astype(BF16)
        vbd = jnp.where(bd_v, jnp.concatenate([v] * GLA_HEADS, axis=0), 0.0).astype(BF16)
        o = _dot(p, vbd) + o_inter
        s_ref[...] = jnp.exp(tot) * s_old + jnp.where(bd_s, kvt, 0.0)
        o_ref[out_rows, :] = o[:n_out]

    def lockstep(*gens):
        live = list(gens)
        while live:
            nxt = []
            for gen in live:
                try:
                    next(gen)
                    nxt.append(gen)
                except StopIteration:
                    pass
            live = nxt

    sf_ref[...] = jnp.zeros(sf_ref.shape, F32)
    sb_ref[...] = jnp.zeros(sb_ref.shape, F32)
    meta_rows = pl.ds(0, c)
    lockstep(chunk(meta_rows, valid0, cum_decay(meta_rows, valid0, tri_f, 0), keep_f, c - 1,
                   sf_ref, of_ref, pl.ds(0, N_META), N_META))

    def rows_fwd(i):
        return pl.ds(pl.multiple_of(N_META + i * c, 16), c)

    def rows_bwd(i):
        return pl.ds(pl.multiple_of(N_META + (n_real - 1 - i) * c, 16), c)

    def body(i, cums):
        cum_f, cum_b = cums
        nxt = jnp.minimum(i + 1, n_real - 1)
        cums = (cum_decay(rows_fwd(nxt), None, tri_f, 0), cum_decay(rows_bwd(nxt), None, tri_b, nk))
        lockstep(chunk(rows_fwd(i), None, cum_f, keep_f, c - 1, sf_ref, of_ref, rows_fwd(i), c),
                 chunk(rows_bwd(i), None, cum_b, keep_b, 0, sb_ref, ob_ref, rows_bwd(i), c))
        return cums

    first = (cum_decay(rows_fwd(0), None, tri_f, 0), cum_decay(rows_bwd(0), None, tri_b, nk))
    lax.fori_loop(0, n_real, body, first)
    lockstep(chunk(meta_rows, valid0, cum_decay(meta_rows, valid0, tri_b, nk), keep_b, 0,
                   sb_ref, ob_ref, pl.ds(0, N_META), N_META))

    def epilogue(i, carry):
        rows = pl.ds(pl.multiple_of(i * GLA_EPI_TILE, 16), GLA_EPI_TILE)
        o = of_ref[rows, :] + ob_ref[rows, :]
        r = r_ref[rows, :].astype(F32)
        for h in range(GLA_HEADS):
            cols = slice(h * GLA_DV, (h + 1) * GLA_DV)
            oh = o[:, cols]
            ms = jnp.mean(oh * oh, axis=-1, keepdims=True)
            yh = oh * lax.rsqrt(ms + NORM_EPS) * nw_ref[...] * _silu(r[:, cols])
            y_ref[rows, cols] = yh.astype(y_ref.dtype)
        return carry

    lax.fori_loop(0, t // GLA_EPI_TILE, epilogue, 0)


def _gla(u3, wc, bc, norm_w):
    b, t, _ = u3.shape
    nk = GLA_HEADS * GLA_DK
    nv = GLA_HEADS * GLA_DV
    q_blk = 3 * SC_WIDTH // nk
    v_blk = (3 * SC_WIDTH + 2 * nk) // nv
    g_blk = (3 * SC_WIDTH + 2 * nk + 2 * nv) // LANES
    full = lambda shape: pl.BlockSpec(shape, lambda i: (0, 0))
    return pl.pallas_call(
        _gla_kernel,
        grid=(b,),
        in_specs=[pl.BlockSpec((None, t, nk), lambda i: (i, 0, q_blk)),
                  pl.BlockSpec((None, t, nk), lambda i: (i, 0, q_blk + 1)),
                  pl.BlockSpec((None, t, nv), lambda i: (i, 0, v_blk)),
                  pl.BlockSpec((None, t, nv), lambda i: (i, 0, v_blk + 1)),
                  pl.BlockSpec((None, t, LANES), lambda i: (i, 0, g_blk)),
                  full((LANES, 2 * nk)), full((1, 2 * nk)), full((1, GLA_DV))],
        out_specs=pl.BlockSpec((None, t, nv), lambda i: (i, 0, 0)),
        out_shape=jax.ShapeDtypeStruct((b, t, nv), BF16),
        scratch_shapes=[pltpu.VMEM((t, 2 * nk), F32),
                        pltpu.VMEM((t, nv), F32), pltpu.VMEM((t, nv), F32),
                        pltpu.VMEM((nv, nk), F32), pltpu.VMEM((nv, nk), F32)],
        compiler_params=_params("parallel"),
        name="gla",
    )(u3, u3, u3, u3, u3, wc, bc, norm_w)


GROUP_LANE0 = N_EXPERTS


def _route(x, g_ref, whi_ref, wlo_ref, b_ref, xn_ref, gates_ref, info_ref, inforow_ref, count_ref):
    ms = jnp.mean(x * x, axis=-1, keepdims=True)
    xn = x * lax.rsqrt(ms + NORM_EPS) * g_ref[...]
    xhi = xn.astype(BF16)
    xn_ref[...] = xhi
    xlo = (xn - xhi.astype(F32)).astype(BF16)
    logits = _dot(xhi, whi_ref[...]) + _dot(xhi, wlo_ref[...]) + _dot(xlo, whi_ref[...]) + b_ref[...]

    lane = lax.broadcasted_iota(jnp.int32, logits.shape, 1)
    neg = -jnp.inf
    big = jnp.int32(LANES)
    is_group = (lane >> 2) == (GROUP_LANE0 >> 2)
    gl = jnp.where(is_group, logits, neg)
    gmax = jnp.max(gl, axis=-1, keepdims=True)
    g_p = 1.0 / jnp.sum(jnp.exp(gl - gmax), axis=-1, keepdims=True)
    g_idx = jnp.min(jnp.where(gl == gmax, lane, big), axis=-1, keepdims=True) - GROUP_LANE0
    el = jnp.where((lane >> 3) == g_idx, logits, neg)
    m1 = jnp.max(el, axis=-1, keepdims=True)
    i1 = jnp.min(jnp.where(el == m1, lane, big), axis=-1, keepdims=True)
    el2 = jnp.where(lane == i1, neg, el)
    m2 = jnp.max(el2, axis=-1, keepdims=True)
    i2 = jnp.min(jnp.where(el2 == m2, lane, big), axis=-1, keepdims=True)
    e2 = jnp.exp(m2 - m1)
    w1 = g_p / (1.0 + e2)
    w2 = g_p * e2 / (1.0 + e2)
    gates_ref[...] = jnp.where(lane == i1, w1, 0.0) + jnp.where(lane == i2, w2, 0.0)

    tm = x.shape[0]
    onehot = jnp.where(lane == g_idx, 1.0, 0.0)
    r_i = lax.broadcasted_iota(jnp.int32, (tm, tm), 0)
    c_i = lax.broadcasted_iota(jnp.int32, (tm, tm), 1)
    before = jnp.where(r_i > c_i, 1.0, 0.0).astype(BF16)
    prefix = _dot(before, onehot.astype(BF16))
    rank = jnp.sum(jnp.where(lane == g_idx, prefix, 0.0), axis=-1, keepdims=True).astype(jnp.int32)
    info_ref[...] = jnp.where(lane == 0, g_idx, jnp.where(lane == 1, rank, 0))
    eye = r_i == c_i
    g_row = jnp.sum(jnp.where(eye, g_idx, 0), axis=0, keepdims=True)
    rank_row = jnp.sum(jnp.where(eye, rank, 0), axis=0, keepdims=True)
    sub = lax.broadcasted_iota(jnp.int32, inforow_ref.shape, 0)
    inforow_ref[...] = jnp.where(sub == 0, g_row, jnp.where(sub == 1, rank_row, 0))
    counts = jnp.sum(onehot, axis=0, keepdims=True).astype(jnp.int32)
    count_ref[...] = jnp.broadcast_to(counts, count_ref.shape)


def _router_operands(norm_g, w_group, b_group, w_expert, b_expert):
    d = w_group.shape[0]
    pad = LANES - N_EXPERTS - MOE_GROUPS
    wr = jnp.concatenate([w_expert, w_group, jnp.zeros((d, pad), F32)], axis=1)
    whi = wr.astype(BF16)
    wlo = (wr - whi.astype(F32)).astype(BF16)
    bias = jnp.concatenate([b_expert, b_group, jnp.zeros((pad,), F32)]).reshape(1, LANES)
    return norm_g.reshape(1, d), whi, wlo, bias


def _router_specs(n, d, tm):
    nb = n // tm
    const = lambda shape: pl.BlockSpec(shape, lambda i: (0,) * len(shape))
    in_specs = [const((1, d)), const((d, LANES)), const((d, LANES)), const((1, LANES))]
    out_specs = [pl.BlockSpec((tm, d), lambda i: (i, 0)),
                 pl.BlockSpec((tm, d), lambda i: (i, 0)),
                 pl.BlockSpec((tm, LANES), lambda i: (i, 0)),
                 pl.BlockSpec((tm, LANES), lambda i: (i, 0)),
                 pl.BlockSpec((None, 8, tm), lambda i: (i, 0, 0)),
                 pl.BlockSpec((None, 8, LANES), lambda i: (i, 0, 0))]
    out_shape = [jax.ShapeDtypeStruct((n, d), F32),
                 jax.ShapeDtypeStruct((n, d), BF16),
                 jax.ShapeDtypeStruct((n, LANES), F32),
                 jax.ShapeDtypeStruct((n, LANES), jnp.int32),
                 jax.ShapeDtypeStruct((nb, 8, tm), jnp.int32),
                 jax.ShapeDtypeStruct((nb, 8, LANES), jnp.int32)]
    return in_specs, out_specs, out_shape


def _proj2_router_kernel(a_ref, b_ref, wa_ref, wb_ref, res_ref, g_ref, whi_ref, wlo_ref, bias_ref,
                         h_ref, *route_out):
    x = res_ref[...] + _dot(a_ref[...], wa_ref[...]) + _dot(b_ref[...], wb_ref[...])
    h_ref[...] = x
    _route(x, g_ref, whi_ref, wlo_ref, bias_ref, *route_out)


def _proj2_router(a, b, wa, wb, res, router_ops, tm):
    n, d = res.shape
    ka, kb = a.shape[1], b.shape[1]
    r_in, out_specs, out_shape = _router_specs(n, d, tm)
    return pl.pallas_call(
        _proj2_router_kernel,
        grid=(n // tm,),
        in_specs=[pl.BlockSpec((tm, ka), lambda i: (i, 0)),
                  pl.BlockSpec((tm, kb), lambda i: (i, 0)),
                  pl.BlockSpec((ka, d), lambda i: (0, 0)),
                  pl.BlockSpec((kb, d), lambda i: (0, 0)),
                  pl.BlockSpec((tm, d), lambda i: (i, 0))] + r_in,
        out_specs=out_specs,
        out_shape=out_shape,
        compiler_params=_params("parallel"),
        name="proj2_router",
    )(a, b, wa, wb, res, *router_ops)


def _att_out_router_kernel(o_ref, w_ref, hres_ref, g_ref, whi_ref, wlo_ref, bias_ref, h_ref, *route_out):
    tm = o_ref.shape[0]
    tiles = (hres_ref.shape[0] - N_META) // tm
    r0 = pl.multiple_of(N_META + lax.rem(pl.program_id(0), tiles) * tm, 16)
    x = hres_ref[pl.ds(r0, tm), :] + _dot(o_ref[...], w_ref[...])
    h_ref[...] = x
    _route(x, g_ref, whi_ref, wlo_ref, bias_ref, *route_out)


def _att_out_router(o, w, h3, router_ops, tm):
    b, t, d = h3.shape
    n, nq = o.shape
    tiles = (t - N_META) // tm
    r_in, out_specs, out_shape = _router_specs(n, d, tm)
    return pl.pallas_call(
        _att_out_router_kernel,
        grid=(n // tm,),
        in_specs=[pl.BlockSpec((tm, nq), lambda i: (i, 0)),
                  pl.BlockSpec((nq, d), lambda i: (0, 0)),
                  pl.BlockSpec((None, t, d), lambda i: (i // tiles, 0, 0))] + r_in,
        out_specs=out_specs,
        out_shape=out_shape,
        compiler_params=_params("parallel"),
        name="att_out_router",
    )(o, w, h3, *router_ops)


MOE_TILE = 512
SEG_ALIGN = 16
STRIP = 32


def _pow2_sizes(max_rows, min_rows):
    sizes, s = [], min_rows
    while s <= max_rows:
        sizes.append(s)
        s *= 2
    return sizes[::-1]


def _strip_copies(n_rows, sizes, make_copy):
    pos = 0
    for sz in sizes:
        bit = n_rows & sz
        yield bit, make_copy(pos, sz)
        pos = pos + bit


def _slot_key(group, rank, boff_ref, b):
    key = rank
    for g in range(MOE_GROUPS):
        key = key + jnp.where(group == g, boff_ref[MOE_GROUPS * b + g], 0)
    return key


def _start_all(copies):
    for bit, cs in copies:
        @pl.when(bit != 0)
        def _():
            for c in cs:
                c.start()


def _wait_all(copies):
    for bit, cs in copies:
        @pl.when(bit != 0)
        def _():
            for c in cs:
                c.wait()


def _dispatch_kernel(off_ref, cpad_ref, boff_ref, tail_ref, x_ref, gates_ref, inforow_ref,
                     xs_hbm, gs_hbm, xstg, gstg, zx, zg, sem):
    b = pl.program_id(0)
    last = pl.num_programs(0) - 1
    tb = x_ref.shape[0]
    ks = xstg.shape[1]
    cur = lax.rem(b, 2)
    key = _slot_key(inforow_ref[0:1, :], inforow_ref[1:2, :], boff_ref, b)
    slot = lax.broadcasted_iota(jnp.int32, (ks, tb), 0)
    perm = jnp.where(slot == key, 1.0, 0.0).astype(BF16)
    xstg[cur] = _dot(perm, x_ref[...]).astype(BF16)
    gt = gates_ref[...]
    hi = gt.astype(BF16)
    lo = (gt - hi.astype(F32)).astype(BF16)
    gstg[cur] = _dot(perm, hi) + _dot(perm, lo)

    sizes = _pow2_sizes(tb, SEG_ALIGN)

    def copies(blk, sl):
        for g in range(MOE_GROUPS):
            src0 = boff_ref[MOE_GROUPS * blk + g]
            dst0 = off_ref[MOE_GROUPS * blk + g]

            def make(pos, sz, src0=src0, dst0=dst0):
                src = pl.ds(pl.multiple_of(src0 + pos, SEG_ALIGN), sz)
                dst = pl.ds(pl.multiple_of(dst0 + pos, SEG_ALIGN), sz)
                return (pltpu.make_async_copy(xstg.at[sl, src, :], xs_hbm.at[dst, :], sem.at[sl, 0]),
                        pltpu.make_async_copy(gstg.at[sl, src, :], gs_hbm.at[dst, :], sem.at[sl, 1]))

            yield from _strip_copies(cpad_ref[MOE_GROUPS * blk + g], sizes, make)

    _start_all(copies(b, cur))

    @pl.when(b > 0)
    def _():
        _wait_all(copies(jnp.maximum(b - 1, 0), 1 - cur))

    @pl.when(b == last)
    def _():
        _wait_all(copies(b, cur))
        zx[...] = jnp.zeros(zx.shape, BF16)
        zg[...] = jnp.zeros(zg.shape, F32)
        tail_sizes = _pow2_sizes(MOE_TILE // 2, SEG_ALIGN)

        def tails():
            for g in range(MOE_GROUPS):
                dst0 = tail_ref[g]

                def make(pos, sz, dst0=dst0):
                    dst = pl.ds(pl.multiple_of(dst0 + pos, SEG_ALIGN), sz)
                    return (pltpu.make_async_copy(zx.at[pl.ds(0, sz), :], xs_hbm.at[dst, :], sem.at[0, 0]),
                            pltpu.make_async_copy(zg.at[pl.ds(0, sz), :], gs_hbm.at[dst, :], sem.at[0, 1]))

                yield from _strip_copies(tail_ref[MOE_GROUPS + g], tail_sizes, make)

        _start_all(tails())
        _wait_all(tails())

        def spare(i):
            dst = pl.ds(pl.multiple_of(i * MOE_TILE, MOE_TILE), MOE_TILE)
            return (pltpu.make_async_copy(zx, xs_hbm.at[dst, :], sem.at[0, 0]),
                    pltpu.make_async_copy(zg, gs_hbm.at[dst, :], sem.at[0, 1]))

        first_spare = tail_ref[2 * MOE_GROUPS]
        n_tiles = xs_hbm.shape[0] // MOE_TILE

        @pl.loop(first_spare, n_tiles)
        def _(i):
            cx, cg = spare(i)
            cx.start()
            cg.start()

        @pl.loop(first_spare, n_tiles)
        def _(i):
            cx, cg = spare(i)
            cx.wait()
            cg.wait()


def _dispatch(xn, gates, inforow, off, cpad, boff, tail, n_sorted, tb, ks):
    n, d = xn.shape
    nb = n // tb
    grid_spec = pltpu.PrefetchScalarGridSpec(
        num_scalar_prefetch=4,
        grid=(nb,),
        in_specs=[pl.BlockSpec((tb, d), lambda i, *_: (i, 0)),
                  pl.BlockSpec((tb, LANES), lambda i, *_: (i, 0)),
                  pl.BlockSpec((None, 8, tb), lambda i, *_: (i, 0, 0))],
        out_specs=[pl.BlockSpec(memory_space=pl.ANY), pl.BlockSpec(memory_space=pl.ANY)],
        scratch_shapes=[pltpu.VMEM((2, ks, d), BF16), pltpu.VMEM((2, ks, LANES), F32),
                        pltpu.VMEM((MOE_TILE, d), BF16), pltpu.VMEM((MOE_TILE, LANES), F32),
                        pltpu.SemaphoreType.DMA((2, 2))])
    return pl.pallas_call(
        _dispatch_kernel,
        grid_spec=grid_spec,
        out_shape=[jax.ShapeDtypeStruct((n_sorted, d), BF16),
                   jax.ShapeDtypeStruct((n_sorted, LANES), F32)],
        compiler_params=_params("arbitrary"),
        name="moe_dispatch",
    )(off, cpad, boff, tail, xn, gates, inforow)


def _group_mlp_kernel(tg_ref, nused_ref, x_ref, gs_ref, wg_ref, wu_ref, wd_ref, y_ref):
    i = pl.program_id(0)

    @pl.when(i < nused_ref[0])
    def _():
        x = x_ref[...]
        gs = gs_ref[...]
        lane = lax.broadcasted_iota(jnp.int32, gs.shape, 1)
        lane0 = tg_ref[i] * MOE_EXPERTS_PER_GROUP
        acc = jnp.zeros(y_ref.shape, F32)
        for e in range(MOE_EXPERTS_PER_GROUP):
            he = _silu(_dot(x, wg_ref[e])) * _dot(x, wu_ref[e])
            gate = jnp.sum(jnp.where(lane == lane0 + e, gs, 0.0), axis=-1, keepdims=True)
            acc = acc + _dot((he * gate).astype(BF16), wd_ref[e])
        y_ref[...] = acc.astype(y_ref.dtype)

    @pl.when(i >= nused_ref[0])
    def _():
        y_ref[...] = jnp.zeros(y_ref.shape, y_ref.dtype)


def _group_mlp(xs, gs, wg, wu, wd, tile_group, nused):
    n_sorted, d = xs.shape
    ne = MOE_EXPERTS_PER_GROUP

    def rows(i, tg, nu):
        return (jnp.minimum(i, nu[0] - 1), 0)

    def group(i, tg, nu):
        return (tg[i], 0, 0)

    grid_spec = pltpu.PrefetchScalarGridSpec(
        num_scalar_prefetch=2,
        grid=(n_sorted // MOE_TILE,),
        in_specs=[pl.BlockSpec((MOE_TILE, d), rows),
                  pl.BlockSpec((MOE_TILE, LANES), rows),
                  pl.BlockSpec((ne, d, MOE_D_FF), group),
                  pl.BlockSpec((ne, d, MOE_D_FF), group),
                  pl.BlockSpec((ne, MOE_D_FF, d), group)],
        out_specs=pl.BlockSpec((MOE_TILE, d), lambda i, tg, nu: (i, 0)))
    return pl.pallas_call(
        _group_mlp_kernel,
        grid_spec=grid_spec,
        out_shape=jax.ShapeDtypeStruct((n_sorted, d), BF16),
        compiler_params=_params("arbitrary"),
        name="moe_experts",
    )(tile_group, nused, xs, gs, wg, wu, wd)


def _combine_kernel(off_ref, cpad_ref, boff_ref, info_ref, res_ref, fg_ref, y_hbm, o_ref, ybuf, sem,
                    *, final_norm):
    b = pl.program_id(0)
    last = pl.num_programs(0) - 1
    tb = res_ref.shape[0]
    ks = ybuf.shape[1]
    cur = lax.rem(b, 2)
    sizes = _pow2_sizes(tb, STRIP)

    def copies(blk, sl):
        for g in range(MOE_GROUPS):
            src0 = off_ref[MOE_GROUPS * blk + g]
            dst0 = boff_ref[MOE_GROUPS * blk + g]

            def make(pos, sz, src0=src0, dst0=dst0):
                src = pl.ds(pl.multiple_of(src0 + pos, SEG_ALIGN), sz)
                dst = pl.ds(pl.multiple_of(dst0 + pos, STRIP), sz)
                return (pltpu.make_async_copy(y_hbm.at[src, :], ybuf.at[sl, dst, :], sem.at[sl]),)

            n_rows = (cpad_ref[MOE_GROUPS * blk + g] + (STRIP - 1)) & (-STRIP)
            yield from _strip_copies(n_rows, sizes, make)

    @pl.when(b == 0)
    def _():
        ybuf[...] = jnp.zeros(ybuf.shape, ybuf.dtype)
        _start_all(copies(b, cur))

    @pl.when(b < last)
    def _():
        _start_all(copies(jnp.minimum(b + 1, last), 1 - cur))

    _wait_all(copies(b, cur))

    key = _slot_key(info_ref[:, 0:1], info_ref[:, 1:2], boff_ref, b)
    slot = lax.broadcasted_iota(jnp.int32, (tb, ks), 1)
    perm = jnp.where(slot == key, 1.0, 0.0).astype(BF16)
    y = res_ref[...] + _dot(perm, ybuf[cur])
    if final_norm:
        ms = jnp.mean(y * y, axis=-1, keepdims=True)
        y = y * lax.rsqrt(ms + NORM_EPS) * fg_ref[...]
    o_ref[...] = y


def _combine(ys, info, res, final_g, off, cpad, boff, tb, ks, final_norm):
    n, d = res.shape
    grid_spec = pltpu.PrefetchScalarGridSpec(
        num_scalar_prefetch=3,
        grid=(n // tb,),
        in_specs=[pl.BlockSpec((tb, LANES), lambda i, *_: (i, 0)),
                  pl.BlockSpec((tb, d), lambda i, *_: (i, 0)),
                  pl.BlockSpec((1, d), lambda i, *_: (0, 0)),
                  pl.BlockSpec(memory_space=pl.ANY)],
        out_specs=pl.BlockSpec((tb, d), lambda i, *_: (i, 0)),
        scratch_shapes=[pltpu.VMEM((2, ks, d), BF16), pltpu.SemaphoreType.DMA((2,))])
    return pl.pallas_call(
        functools.partial(_combine_kernel, final_norm=final_norm),
        grid_spec=grid_spec,
        out_shape=jax.ShapeDtypeStruct((n, d), F32),
        compiler_params=_params("arbitrary"),
        name="moe_combine",
    )(off, cpad, boff, info, res, final_g.reshape(1, d), ys)


def _round_up(x, m):
    return (x + m - 1) // m * m


def _moe_layer(routed, expert_weights, final_g, tb, final_norm):
    h, xn, gates, info, inforow, counts = routed
    w_gate, w_up, w_down = expert_weights
    n, d = h.shape
    nb = n // tb

    cnt = counts[:, 0, :MOE_GROUPS]
    cpad = _round_up(cnt, SEG_ALIGN)
    strip = _round_up(cnt, STRIP)
    used = jnp.sum(cpad, axis=0)
    region = _round_up(used, MOE_TILE)
    region_start = jnp.cumsum(region) - region
    off = region_start[None, :] + jnp.cumsum(cpad, axis=0) - cpad
    boff = jnp.cumsum(strip, axis=1) - strip
    nused = (jnp.sum(region) // MOE_TILE).reshape(1)
    tail = jnp.concatenate([region_start + used, region - used, nused]).astype(jnp.int32)
    n_tiles = -(-(n + nb * MOE_GROUPS * (SEG_ALIGN - 1) + MOE_GROUPS * (MOE_TILE - 1)) // MOE_TILE) + 1
    tile_start = jnp.arange(n_tiles, dtype=jnp.int32) * MOE_TILE
    region_end = region_start + region
    tile_group = jnp.minimum(jnp.sum(tile_start[:, None] >= region_end[None, :], axis=1), MOE_GROUPS - 1)
    ks =tb + MOE_GROUPS * STRIP

    flat = lambda a: a.reshape(-1).astype(jnp.int32)
    xs, gs = _dispatch(xn, gates, inforow, flat(off), flat(cpad), flat(boff), tail,
                       n_tiles * MOE_TILE, tb, ks)
    ys = _group_mlp(xs, gs, w_gate, w_up, w_down,
                    tile_group.astype(jnp.int32), nused.astype(jnp.int32))
    return _combine(ys, info, h, final_g, flat(off), flat(cpad), flat(boff), tb, ks, final_norm)


ROPE_ROW_TILE = 688
ATT_Q_TILE = 256
ATT_TILES_PER_ITER = 2
ATT_SCORES_AHEAD = 2
ATT_SCORE_SLOTS = 4
ATT_VT_ROWS = ATT_HEAD_DIM + 16
ATT_Q_SCALE = ATT_HEAD_DIM ** -0.5 * 1.4426950408889634


def _eye(n):
    return jnp.where(lax.broadcasted_iota(jnp.int32, (n, n), 0) == lax.broadcasted_iota(jnp.int32, (n, n), 1),
                     1.0, 0.0).astype(BF16)


def _transpose_bf16(x, eye):
    return lax.dot_general(eye, x, _NT, preferred_element_type=F32).astype(BF16)


def _rope_kernel(q_ref, k_ref, v_ref, cos_ref, sin_ref, qn_ref, kn_ref, qt_ref, kp_ref, vt_ref):
    t = q_ref.shape[0]
    tq = ATT_Q_TILE
    dh = ATT_HEAD_DIM
    gi = lax.broadcasted_iota(jnp.int32, (LANES, LANES), 0) >> 6
    gj = lax.broadcasted_iota(jnp.int32, (LANES, LANES), 1) >> 6
    gmat = jnp.where(gi == gj, 1.0, 0.0).astype(BF16)
    eye_q = _eye(LANES)

    def norm_rope(x, gain, cos, sin):
        lane = lax.broadcasted_iota(jnp.int32, x.shape, 1)
        sq = x * x
        hi = sq.astype(BF16)
        lo = (sq - hi.astype(F32)).astype(BF16)
        ms = (_dot(hi, gmat) + _dot(lo, gmat)) * (1.0 / dh)
        xn = x * lax.rsqrt(ms + NORM_EPS) * gain
        partner = jnp.where((lane & 1) == 0, pltpu.roll(xn, LANES - 1, 1), pltpu.roll(xn, 1, 1))
        return xn * cos + partner * sin

    for j in range(q_ref.shape[1] // LANES):
        cols = slice(j * LANES, (j + 1) * LANES)
        y = norm_rope(q_ref[N_META:, cols].astype(F32), qn_ref[...], cos_ref[N_META:, :], sin_ref[N_META:, :])
        yt = _transpose_bf16((y * ATT_Q_SCALE).astype(BF16), eye_q)
        for i in range((t - N_META) // tq):
            qt_ref[i, cols, :] = yt[:, i * tq:(i + 1) * tq]

    low = lax.broadcasted_iota(jnp.int32, (t, LANES), 1) < dh
    for j in range(k_ref.shape[1] // LANES):
        cols = slice(j * LANES, (j + 1) * LANES)
        y = norm_rope(k_ref[:, cols].astype(F32), kn_ref[...], cos_ref[...], sin_ref[...])
        kp_ref[2 * j] = jnp.where(low, y, 0.0).astype(BF16)
        kp_ref[2 * j + 1] = jnp.where(low, pltpu.roll(y, dh, 1), 0.0).astype(BF16)

    n_real = t - N_META
    eye_v = _eye(v_ref.shape[1])
    vt_real = _transpose_bf16(v_ref[N_META:, :], eye_v)
    vt_meta = _transpose_bf16(v_ref[0:N_META, :], eye_v)
    for h in range(ATT_KV_HEADS):
        vt_ref[h, 0:dh, 0:n_real] = vt_real[h * dh:(h + 1) * dh, :]
        vt_ref[h, 0:dh, n_real:t] = vt_meta[h * dh:(h + 1) * dh, :]
        extra = lax.broadcasted_iota(jnp.int32, (ATT_VT_ROWS - dh, t), 0)
        vt_ref[h, dh:ATT_VT_ROWS, :] = jnp.where(extra == 0, 1.0, 0.0).astype(BF16)


def _rope(u3, cos_e, sin_e, qn, kn):
    b, t, _ = u3.shape
    nq = ATT_HEADS * ATT_HEAD_DIM
    nkv = ATT_KV_HEADS * ATT_HEAD_DIM
    n_qt = (t - N_META) // ATT_Q_TILE
    full = lambda shape: pl.BlockSpec(shape, lambda i: (0, 0))
    return pl.pallas_call(
        _rope_kernel,
        grid=(b,),
        in_specs=[pl.BlockSpec((None, t, nq), lambda i: (i, 0, 0)),
                  pl.BlockSpec((None, t, nkv), lambda i: (i, 0, nq // nkv)),
                  pl.BlockSpec((None, t, nkv), lambda i: (i, 0, nq // nkv + 1)),
                  full((t, LANES)), full((t, LANES)), full((1, LANES)), full((1, LANES))],
        out_specs=[pl.BlockSpec((None, n_qt, nq, ATT_Q_TILE), lambda i: (i, 0, 0, 0)),
                   pl.BlockSpec((None, ATT_KV_HEADS, t, LANES), lambda i: (i, 0, 0, 0)),
                   pl.BlockSpec((None, ATT_KV_HEADS, ATT_VT_ROWS, t), lambda i: (i, 0, 0, 0))],
        out_shape=[jax.ShapeDtypeStruct((b, n_qt, nq, ATT_Q_TILE), BF16),
                   jax.ShapeDtypeStruct((b, ATT_KV_HEADS, t, LANES), BF16),
                   jax.ShapeDtypeStruct((b, ATT_KV_HEADS, ATT_VT_ROWS, t), BF16)],
        compiler_params=_params("parallel"),
        name="qk_norm_rope",
    )(u3, u3, u3, cos_e, sin_e, qn, kn)


def _attn_kernel(qt_ref, k_ref, vt_ref, wg_ref, wu_ref, wd_ref, o_ref, wg_o, wu_o, wd_o, s_ref):
    _cast_through((wg_ref, wu_ref, wd_ref), (wg_o, wu_o, wd_o))
    n_qt, w, tq = qt_ref.shape
    dh = ATT_HEAD_DIM
    n_real = k_ref.shape[0] - N_META
    eye = _eye(tq)
    zpad = jnp.zeros((LANES - dh, tq), BF16)

    def scores(i, g, slot):
        qp = jnp.concatenate([qt_ref[i, g * dh:(g + 1) * dh, :], zpad], axis=0)
        s_r = _dot(k_ref[N_META:, :], qp)
        s_m = _dot(k_ref[0:N_META, :], qp)
        s_ref[slot, 0:n_real, :] = s_r
        s_ref[slot, n_real:, :] = s_m
        return jnp.maximum(jnp.max(s_r, axis=0, keepdims=True), jnp.max(s_m, axis=0, keepdims=True))

    steps = [(u, g) for u in range(ATT_TILES_PER_ITER) for g in range(ATT_GROUP)]
    n_steps = len(steps)
    n_iter = n_qt // ATT_TILES_PER_ITER
    ahead = ATT_SCORES_AHEAD
    n_slots = s_ref.shape[0]
    assert n_steps % n_slots == 0 and ahead < n_slots

    def issue_scores(it, n):
        it2 = jnp.minimum(it + n // n_steps, n_iter - 1)
        u, g = steps[n % n_steps]
        return scores(it2 * ATT_TILES_PER_ITER + u, g, n % n_slots)

    def q_tiles(it, pending):
        i0 = it * ATT_TILES_PER_ITER
        outs = []
        pending = list(pending)
        for n, (u, g) in enumerate(steps):
            m, slot = pending.pop(0), n % n_slots
            pending.append(issue_scores(it, n + ahead))
            p_r = jnp.exp2(s_ref[slot, 0:n_real, :] - m).astype(BF16)
            p_m = jnp.exp2(s_ref[slot, n_real:, :] - m).astype(BF16)
            ot = _dot(vt_ref[:, 0:n_real], p_r) + _dot(vt_ref[:, n_real:], p_m)
            outs.append((ot[0:dh] * (1.0 / ot[dh:dh + 1])).astype(BF16))
            if g == ATT_GROUP - 1:
                rows = pl.ds(pl.multiple_of((i0 + u) * tq, tq), tq)
                ot_all = jnp.concatenate(outs, axis=0)
                o_ref[rows, :] = lax.dot_general(eye, ot_all, _NT, preferred_element_type=F32).astype(o_ref.dtype)
                outs = []
        return tuple(pending)

    first = tuple(issue_scores(0, n) for n in range(ahead))
    lax.fori_loop(0, n_iter, q_tiles, first)


def _attention(qt, kp, vt, expert_weights, layer):
    b, n_qt, nq, tq = qt.shape
    t = kp.shape[2]
    w = ATT_GROUP * ATT_HEAD_DIM
    grid = (b, ATT_KV_HEADS)
    c_in, c_out, c_shape = _expert_cast_specs(expert_weights, layer, grid)
    o, *cast = pl.pallas_call(
        _attn_kernel,
        grid=grid,
        in_specs=[pl.BlockSpec((None, n_qt, w, tq), lambda i, h: (i, 0, h, 0)),
                  pl.BlockSpec((None, None, t, LANES), lambda i, h: (i, h, 0, 0)),
                  pl.BlockSpec((None, None, ATT_VT_ROWS, t), lambda i, h: (i, h, 0, 0))] + c_in,
        out_specs=[pl.BlockSpec((None, n_qt * tq, w), lambda i, h: (i, 0, h))] + c_out,
        out_shape=[jax.ShapeDtypeStruct((b, n_qt * tq, nq), BF16)] + c_shape,
        scratch_shapes=[pltpu.VMEM((ATT_SCORE_SLOTS, t, tq), F32)],
        compiler_params=_params("parallel", "parallel"),
        name="gqa_attention",
    )(qt, kp, vt, *expert_weights)
    return o, cast


def _rope_tables(n_tokens):
    rows_n = n_tokens // GRID_W
    rows = jnp.repeat(jnp.arange(rows_n), GRID_W).astype(F32)
    cols = jnp.tile(jnp.arange(GRID_W), rows_n).astype(F32)
    axis_dims = ATT_HEAD_DIM // 2
    freqs = ROPE_THETA ** (-jnp.arange(0, axis_dims, 2, dtype=F32) / axis_dims)
    ang = jnp.concatenate([rows[:, None] * freqs, cols[:, None] * freqs], axis=-1)
    ang = jnp.concatenate([jnp.zeros((N_META, ang.shape[1]), F32), ang], axis=0)
    cos_e = jnp.tile(jnp.repeat(jnp.cos(ang), 2, axis=1), (1, LANES // ATT_HEAD_DIM))
    sign = jnp.tile(jnp.array([-1.0, 1.0], F32), LANES // 2)
    sin_e = jnp.tile(jnp.repeat(jnp.sin(ang), 2, axis=1), (1, LANES // ATT_HEAD_DIM)) * sign
    return cos_e, sin_e


def kernel(x, meta_tokens, norm_mix, norm_ffn, norm_final, sc_w_in, sc_conv_w, gla_w_gate_fwd, gla_b_gate_fwd, gla_w_gate_bwd, gla_b_gate_bwd, gla_norm_w, sc_w_out, att_w_in, att_q_norm, att_k_norm, att_w_out, moe_w_group, moe_b_group, moe_w_expert, moe_b_expert, moe_w_gate, moe_w_up, moe_w_down):
    bsz, s_len, d = x.shape
    t = s_len + N_META
    n0, n1 = bsz * t, bsz * s_len
    tm0 = t // 3
    tm1 = 512

    expert_f32 = (moe_w_gate, moe_w_up, moe_w_down)
    meta = jnp.broadcast_to(meta_tokens.astype(x.dtype)[None], (bsz, N_META, d))
    h = jnp.concatenate([meta, x], axis=1).reshape(n0, d)

    nk = GLA_HEADS * GLA_DK
    mix_in = sc_w_in.shape[2]
    mix_pad = -mix_in % MXU_WIDTH
    w_in = jnp.pad(sc_w_in[0], ((0, 0), (0, mix_pad))).astype(BF16)
    u = _norm_matmul(h, norm_mix[0], w_in, tm0, mix_in + mix_pad)
    u3 = u.reshape(bsz, t, mix_in + mix_pad)
    y_a, experts0 = _gated_conv(u3, sc_conv_w[0], expert_f32, 0)
    wc = jnp.zeros((LANES, 2 * nk), F32)
    wc = wc.at[:GLA_GATE_RANK, :nk].set(gla_w_gate_fwd[0])
    wc = wc.at[GLA_GATE_RANK:2 * GLA_GATE_RANK, nk:].set(gla_w_gate_bwd[0])
    bc = jnp.concatenate([gla_b_gate_fwd[0], gla_b_gate_bwd[0]]).reshape(1, 2 * nk)
    y_b = _gla(u3, wc.astype(BF16), bc, gla_norm_w[0].reshape(1, GLA_DV))
    w_out = sc_w_out[0].astype(BF16)
    tb0 = 3 * LANES
    router0 = _router_operands(norm_ffn[0], moe_w_group[0], moe_b_group[0], moe_w_expert[0], moe_b_expert[0])
    routed = _proj2_router(y_a.reshape(n0, SC_WIDTH), y_b.reshape(n0, -1), w_out[:SC_WIDTH], w_out[SC_WIDTH:],
                           h, router0, tb0)
    h = _moe_layer(routed, experts0, norm_final, tb0, False)

    u = _norm_matmul(h, norm_mix[1], att_w_in[0].astype(BF16), tm0, att_w_in.shape[2])
    u3 = u.reshape(bsz, t, -1)
    cos_e, sin_e = _rope_tables(s_len)
    qn = jnp.tile(att_q_norm[0], LANES // ATT_HEAD_DIM).reshape(1, LANES)
    kn = jnp.tile(att_k_norm[0], LANES // ATT_HEAD_DIM).reshape(1, LANES)
    qt, kp, vt = _rope(u3, cos_e, sin_e, qn, kn)
    o, experts1 = _attention(qt, kp, vt, expert_f32, 1)
    router1 = _router_operands(norm_ffn[1], moe_w_group[1], moe_b_group[1], moe_w_expert[1], moe_b_expert[1])
    routed = _att_out_router(o.reshape(n1, d), att_w_out[0].astype(BF16), h.reshape(bsz, t, d), router1, tm1)
    h = _moe_layer(routed, experts1, norm_final, tm1, True)
    return h.reshape(bsz, s_len, d)
```

```python
import functools

import jax
import jax.numpy as jnp
from jax import lax
from jax.experimental import pallas as pl
from jax.experimental.pallas import tpu as pltpu

F32 = jnp.float32
BF16 = jnp.bfloat16

NORM_EPS = 1e-6
N_META = 16
GRID_W = 64
ROPE_THETA = 10000.0
SC_WIDTH = 512
GLA_HEADS = 4
GLA_DK = 64
GLA_DV = 128
GLA_GATE_RANK = 16
GLA_GATE_TAU = 16.0
GLA_CHUNK = 64
ATT_HEAD_DIM = 64
ATT_HEADS = 16
ATT_KV_HEADS = 4
ATT_GROUP = ATT_HEADS // ATT_KV_HEADS
MOE_GROUPS = 4
MOE_EXPERTS_PER_GROUP = 8
N_EXPERTS = 32
MOE_D_FF = 256

LANES = 128
MXU_WIDTH = 256
MIB = 2 ** 20
VMEM_LIMIT = 56 * MIB

_NT = (((1,), (1,)), ((), ()))
_TN = (((0,), (0,)), ((), ()))


def _params(*semantics):
    return pltpu.CompilerParams(dimension_semantics=semantics, vmem_limit_bytes=VMEM_LIMIT)


def _dot(a, b):
    return jnp.dot(a, b, preferred_element_type=F32)


def _silu(x):
    return x * (1.0 / (1.0 + jnp.exp(-x)))


def _norm_matmul_kernel(h_ref, g_ref, w_ref, o_ref, xn_ref):
    @pl.when(pl.program_id(1) == 0)
    def _():
        x = h_ref[...]
        ms = jnp.mean(x * x, axis=-1, keepdims=True)
        xn_ref[...] = (x * lax.rsqrt(ms + NORM_EPS) * g_ref[...]).astype(BF16)

    o_ref[...] = _dot(xn_ref[...], w_ref[...]).astype(o_ref.dtype)


def _norm_matmul(h, g, w, tm, tn):
    n, d = h.shape
    nout = w.shape[1]
    return pl.pallas_call(
        _norm_matmul_kernel,
        grid=(n // tm, nout // tn),
        in_specs=[pl.BlockSpec((tm, d), lambda i, j: (i, 0)),
                  pl.BlockSpec((1, d), lambda i, j: (0, 0)),
                  pl.BlockSpec((d, tn), lambda i, j: (0, j))],
        out_specs=pl.BlockSpec((tm, tn), lambda i, j: (i, j)),
        out_shape=jax.ShapeDtypeStruct((n, nout), BF16),
        scratch_shapes=[pltpu.VMEM((tm, d), BF16)],
        compiler_params=_params("parallel", "arbitrary"),
        name="norm_matmul",
    )(h, g.reshape(1, d), w)


def _expert_cast_specs(weights, layer, grid_shape):
    n_steps = 1
    for g in grid_shape:
        n_steps *= g
    assert N_EXPERTS % n_steps == 0
    per_step = N_EXPERTS // n_steps

    def step(*idx):
        flat = 0
        for i, g in zip(idx, grid_shape):
            flat = flat * g + i
        return flat

    in_specs = [pl.BlockSpec((None, per_step) + w.shape[2:], lambda *idx: (layer, step(*idx), 0, 0))
                for w in weights]
    out_specs = [pl.BlockSpec((per_step,) + w.shape[2:], lambda *idx: (step(*idx), 0, 0)) for w in weights]
    out_shape = [jax.ShapeDtypeStruct(w.shape[1:], BF16) for w in weights]
    return in_specs, out_specs, out_shape


def _cast_through(in_refs, out_refs):
    for src, dst in zip(in_refs, out_refs):
        dst[...] = src[...].astype(dst.dtype)


def _conv_kernel(ab_ref, ac_ref, ax_ref, w_ref, wg_ref, wu_ref, wd_ref, o_ref, wg_o, wu_o, wd_o, z_ref):
    _cast_through((wg_ref, wu_ref, wd_ref), (wg_o, wu_o, wd_o))
    t, c = ab_ref.shape
    z = ac_ref[...].astype(F32) * ax_ref[...].astype(F32)
    z_ref[0:8, :] = jnp.zeros((8, c), F32)
    z_ref[8:8 + t, :] = z
    z_ref[8 + t:16 + t, :] = jnp.zeros((8, c), F32)
    w = w_ref[...]
    conv = w[0:1, :] * z_ref[7:7 + t, :] + w[1:2, :] * z + w[2:3, :] * z_ref[9:9 + t, :]
    o_ref[...] = (ab_ref[...].astype(F32) * conv).astype(o_ref.dtype)


def _gated_conv(u3, conv_w, expert_weights, layer):
    b, t, _ = u3.shape
    nblk = SC_WIDTH // LANES
    grid = (b, nblk)
    c_in, c_out, c_shape = _expert_cast_specs(expert_weights, layer, grid)

    def col(off):
        return pl.BlockSpec((None, t, LANES), lambda i, j: (i, 0, off + j))

    y_a, *cast = pl.pallas_call(
        _conv_kernel,
        grid=grid,
        in_specs=[col(0), col(nblk), col(2 * nblk),
                  pl.BlockSpec((3, LANES), lambda i, j: (0, j))] + c_in,
        out_specs=[pl.BlockSpec((None, t, LANES), lambda i, j: (i, 0, j))] + c_out,
        out_shape=[jax.ShapeDtypeStruct((b, t, SC_WIDTH), BF16)] + c_shape,
        scratch_shapes=[pltpu.VMEM((t + 16, LANES), F32)],
        compiler_params=_params("parallel", "parallel"),
        name="gated_conv",
    )(u3, u3, u3, conv_w, *expert_weights)
    return y_a, cast


GLA_ROW_TILE = 688
GLA_EPI_TILE = 688


def _gla_kernel(q_ref, k_ref, v_ref, r_ref, g_ref, wc_ref, bc_ref, nw_ref, y_ref,
                la_ref, of_ref, ob_ref, sf_ref, sb_ref):
    t, nk = q_ref.shape
    nv = v_ref.shape[1]
    c = GLA_CHUNK
    n_real = (t - N_META) // c

    for i in range(t // GLA_ROW_TILE):
        rows = pl.ds(i * GLA_ROW_TILE, GLA_ROW_TILE)
        pre = _dot(g_ref[rows, :], wc_ref[...]) + bc_ref[...]
        la_ref[rows, :] = (jnp.minimum(pre, 0.0) - jnp.log(1.0 + jnp.exp(-jnp.abs(pre)))) * (1.0 / GLA_GATE_TAU)

    def iota(shape, dim):
        return lax.broadcasted_iota(jnp.int32, shape, dim)

    tri_f = jnp.where(iota((c, c), 0) >= iota((c, c), 1), 1.0, 0.0).astype(BF16)
    tri_b = jnp.where(iota((c, c), 0) <= iota((c, c), 1), 1.0, 0.0).astype(BF16)
    keep_f = (iota((c, nk), 1) & (c - 1)) <= iota((c, nk), 0)
    keep_b = (iota((c, nk), 1) & (c - 1)) > iota((c, nk), 0)
    bd_k = (iota((nk, nk), 0) >> 6) == (iota((nk, nk), 1) >> 6)
    bd_v = (iota((nk, nv), 0) >> 6) == (iota((nk, nv), 1) >> 7)
    bd_s = (iota((nv, nk), 0) >> 7) == (iota((nv, nk), 1) >> 6)
    valid0 = iota((c, 1), 0) < N_META

    def cum_decay(rows, valid, tri, la_col):
        la = la_ref[rows, la_col:la_col + nk]
        if valid is not None:
            la = jnp.where(valid, la, 0.0)
        hi = la.astype(BF16)
        rem = la - hi.astype(F32)
        mid = rem.astype(BF16)
        lo = (rem - mid.astype(F32)).astype(BF16)
        return _dot(tri, hi) + _dot(tri, mid) + _dot(tri, lo)

    def chunk(rows, valid, cum, keep, tot_row, s_ref, o_ref, out_rows, n_out):
        q = q_ref[rows, :].astype(F32) * (GLA_DK ** -0.5)
        k = k_ref[rows, :].astype(F32)
        v = v_ref[rows, :].astype(F32)
        if valid is not None:
            q = jnp.where(valid, q, 0.0)
            k = jnp.where(valid, k, 0.0)
            v = jnp.where(valid, v, 0.0)
        tot = cum[tot_row:tot_row + 1, :]
        q_dec = (q * jnp.exp(cum)).astype(BF16)
        k_inv = k * jnp.exp(-cum)
        k_end = (k * jnp.exp(tot - cum)).astype(BF16)
        kbd = jnp.where(bd_k, jnp.concatenate([k_inv] * GLA_HEADS, axis=0), 0.0).astype(BF16)
        p = lax.dot_general(q_dec, kbd, _NT, preferred_element_type=F32)
        s_old = s_ref[...]
        o_inter = lax.dot_general(q_dec, s_old.astype(BF16), _NT, preferred_element_type=F32)
        kvt = lax.dot_general(v.astype(BF16), k_end, _TN, preferred_element_type=F32)
        yield
        p = jnp.where(keep, p, 0.0).astype(BF16)
        vbd = jnp.where(bd_v, jnp.concatenate([v] * GLA_HEADS, axis=0), 0.0).astype(BF16)
        o = _dot(p, vbd) + o_inter
        s_ref[...] = jnp.exp(tot) * s_old + jnp.where(bd_s, kvt, 0.0)
        o_ref[out_rows, :] = o[:n_out]

    def lockstep(*gens):
        live = list(gens)
        while live:
            nxt = []
            for gen in live:
                try:
                    next(gen)
                    nxt.append(gen)
                except StopIteration:
                    pass
            live = nxt

    sf_ref[...] = jnp.zeros(sf_ref.shape, F32)
    sb_ref[...] = jnp.zeros(sb_ref.shape, F32)
    meta_rows = pl.ds(0, c)
    lockstep(chunk(meta_rows, valid0, cum_decay(meta_rows, valid0, tri_f, 0), keep_f, c - 1,
                   sf_ref, of_ref, pl.ds(0, N_META), N_META))

    def rows_fwd(i):
        return pl.ds(pl.multiple_of(N_META + i * c, 16), c)

    def rows_bwd(i):
        return pl.ds(pl.multiple_of(N_META + (n_real - 1 - i) * c, 16), c)

    def body(i, cums):
        cum_f, cum_b = cums
        nxt = jnp.minimum(i + 1, n_real - 1)
        cums = (cum_decay(rows_fwd(nxt), None, tri_f, 0), cum_decay(rows_bwd(nxt), None, tri_b, nk))
        lockstep(chunk(rows_fwd(i), None, cum_f, keep_f, c - 1, sf_ref, of_ref, rows_fwd(i), c),
                 chunk(rows_bwd(i), None, cum_b, keep_b, 0, sb_ref, ob_ref, rows_bwd(i), c))
        return cums

    first = (cum_decay(rows_fwd(0), None, tri_f, 0), cum_decay(rows_bwd(0), None, tri_b, nk))
    lax.fori_loop(0, n_real, body, first)
    lockstep(chunk(meta_rows, valid0, cum_decay(meta_rows, valid0, tri_b, nk), keep_b, 0,
                   sb_ref, ob_ref, pl.ds(0, N_META), N_META))

    def epilogue(i, carry):
        rows = pl.ds(pl.multiple_of(i * GLA_EPI_TILE, 16), GLA_EPI_TILE)
        o = of_ref[rows, :] + ob_ref[rows, :]
        r = r_ref[rows, :].astype(F32)
        for h in range(GLA_HEADS):
            cols = slice(h * GLA_DV, (h + 1) * GLA_DV)
            oh = o[:, cols]
            ms = jnp.mean(oh * oh, axis=-1, keepdims=True)
            yh = oh * lax.rsqrt(ms + NORM_EPS) * nw_ref[...] * _silu(r[:, cols])
            y_ref[rows, cols] = yh.astype(y_ref.dtype)
        return carry

    lax.fori_loop(0, t // GLA_EPI_TILE, epilogue, 0)


def _gla(u3, wc, bc, norm_w):
    b, t, _ = u3.shape
    nk = GLA_HEADS * GLA_DK
    nv = GLA_HEADS * GLA_DV
    q_blk = 3 * SC_WIDTH // nk
    v_blk = (3 * SC_WIDTH + 2 * nk) // nv
    g_blk = (3 * SC_WIDTH + 2 * nk + 2 * nv) // LANES
    full = lambda shape: pl.BlockSpec(shape, lambda i: (0, 0))
    return pl.pallas_call(
        _gla_kernel,
        grid=(b,),
        in_specs=[pl.BlockSpec((None, t, nk), lambda i: (i, 0, q_blk)),
                  pl.BlockSpec((None, t, nk), lambda i: (i, 0, q_blk + 1)),
                  pl.BlockSpec((None, t, nv), lambda i: (i, 0, v_blk)),
                  pl.BlockSpec((None, t, nv), lambda i: (i, 0, v_blk + 1)),
                  pl.BlockSpec((None, t, LANES), lambda i: (i, 0, g_blk)),
                  full((LANES, 2 * nk)), full((1, 2 * nk)), full((1, GLA_DV))],
        out_specs=pl.BlockSpec((None, t, nv), lambda i: (i, 0, 0)),
        out_shape=jax.ShapeDtypeStruct((b, t, nv), BF16),
        scratch_shapes=[pltpu.VMEM((t, 2 * nk), F32),
                        pltpu.VMEM((t, nv), F32), pltpu.VMEM((t, nv), F32),
                        pltpu.VMEM((nv, nk), F32), pltpu.VMEM((nv, nk), F32)],
        compiler_params=_params("parallel"),
        name="gla",
    )(u3, u3, u3, u3, u3, wc, bc, norm_w)


GROUP_LANE0 = N_EXPERTS


def _route(x, g_ref, whi_ref, wlo_ref, b_ref, xn_ref, gates_ref, info_ref, inforow_ref, count_ref):
    ms = jnp.mean(x * x, axis=-1, keepdims=True)
    xn = x * lax.rsqrt(ms + NORM_EPS) * g_ref[...]
    xhi = xn.astype(BF16)
    xn_ref[...] = xhi
    xlo = (xn - xhi.astype(F32)).astype(BF16)
    logits = _dot(xhi, whi_ref[...]) + _dot(xhi, wlo_ref[...]) + _dot(xlo, whi_ref[...]) + b_ref[...]

    lane = lax.broadcasted_iota(jnp.int32, logits.shape, 1)
    neg = -jnp.inf
    big = jnp.int32(LANES)
    is_group = (lane >> 2) == (GROUP_LANE0 >> 2)
    gl = jnp.where(is_group, logits, neg)
    gmax = jnp.max(gl, axis=-1, keepdims=True)
    g_p = 1.0 / jnp.sum(jnp.exp(gl - gmax), axis=-1, keepdims=True)
    g_idx = jnp.min(jnp.where(gl == gmax, lane, big), axis=-1, keepdims=True) - GROUP_LANE0
    el = jnp.where((lane >> 3) == g_idx, logits, neg)
    m1 = jnp.max(el, axis=-1, keepdims=True)
    i1 = jnp.min(jnp.where(el == m1, lane, big), axis=-1, keepdims=True)
    el2 = jnp.where(lane == i1, neg, el)
    m2 = jnp.max(el2, axis=-1, keepdims=True)
    i2 = jnp.min(jnp.where(el2 == m2, lane, big), axis=-1, keepdims=True)
    e2 = jnp.exp(m2 - m1)
    w1 = g_p / (1.0 + e2)
    w2 = g_p * e2 / (1.0 + e2)
    gates_ref[...] = jnp.where(lane == i1, w1, 0.0) + jnp.where(lane == i2, w2, 0.0)

    tm = x.shape[0]
    onehot = jnp.where(lane == g_idx, 1.0, 0.0)
    r_i = lax.broadcasted_iota(jnp.int32, (tm, tm), 0)
    c_i = lax.broadcasted_iota(jnp.int32, (tm, tm), 1)
    before = jnp.where(r_i > c_i, 1.0, 0.0).astype(BF16)
    prefix = _dot(before, onehot.astype(BF16))
    rank = jnp.sum(jnp.where(lane == g_idx, prefix, 0.0), axis=-1, keepdims=True).astype(jnp.int32)
    info_ref[...] = jnp.where(lane == 0, g_idx, jnp.where(lane == 1, rank, 0))
    eye = r_i == c_i
    g_row = jnp.sum(jnp.where(eye, g_idx, 0), axis=0, keepdims=True)
    rank_row = jnp.sum(jnp.where(eye, rank, 0), axis=0, keepdims=True)
    sub = lax.broadcasted_iota(jnp.int32, inforow_ref.shape, 0)
    inforow_ref[...] = jnp.where(sub == 0, g_row, jnp.where(sub == 1, rank_row, 0))
    counts = jnp.sum(onehot, axis=0, keepdims=True).astype(jnp.int32)
    count_ref[...] = jnp.broadcast_to(counts, count_ref.shape)


def _router_operands(norm_g, w_group, b_group, w_expert, b_expert):
    d = w_group.shape[0]
    pad = LANES - N_EXPERTS - MOE_GROUPS
    wr = jnp.concatenate([w_expert, w_group, jnp.zeros((d, pad), F32)], axis=1)
    whi = wr.astype(BF16)
    wlo = (wr - whi.astype(F32)).astype(BF16)
    bias = jnp.concatenate([b_expert, b_group, jnp.zeros((pad,), F32)]).reshape(1, LANES)
    return norm_g.reshape(1, d), whi, wlo, bias


def _router_specs(n, d, tm):
    nb = n // tm
    const = lambda shape: pl.BlockSpec(shape, lambda i: (0,) * len(shape))
    in_specs = [const((1, d)), const((d, LANES)), const((d, LANES)), const((1, LANES))]
    out_specs = [pl.BlockSpec((tm, d), lambda i: (i, 0)),
                 pl.BlockSpec((tm, d), lambda i: (i, 0)),
                 pl.BlockSpec((tm, LANES), lambda i: (i, 0)),
                 pl.BlockSpec((tm, LANES), lambda i: (i, 0)),
                 pl.BlockSpec((None, 8, tm), lambda i: (i, 0, 0)),
                 pl.BlockSpec((None, 8, LANES), lambda i: (i, 0, 0))]
    out_shape = [jax.ShapeDtypeStruct((n, d), F32),
                 jax.ShapeDtypeStruct((n, d), BF16),
                 jax.ShapeDtypeStruct((n, LANES), F32),
                 jax.ShapeDtypeStruct((n, LANES), jnp.int32),
                 jax.ShapeDtypeStruct((nb, 8, tm), jnp.int32),
                 jax.ShapeDtypeStruct((nb, 8, LANES), jnp.int32)]
    return in_specs, out_specs, out_shape


def _proj2_router_kernel(a_ref, b_ref, wa_ref, wb_ref, res_ref, g_ref, whi_ref, wlo_ref, bias_ref,
                         h_ref, *route_out):
    x = res_ref[...] + _dot(a_ref[...], wa_ref[...]) + _dot(b_ref[...], wb_ref[...])
    h_ref[...] = x
    _route(x, g_ref, whi_ref, wlo_ref, bias_ref, *route_out)


def _proj2_router(a, b, wa, wb, res, router_ops, tm):
    n, d = res.shape
    ka, kb = a.shape[1], b.shape[1]
    r_in, out_specs, out_shape = _router_specs(n, d, tm)
    return pl.pallas_call(
        _proj2_router_kernel,
        grid=(n // tm,),
        in_specs=[pl.BlockSpec((tm, ka), lambda i: (i, 0)),
                  pl.BlockSpec((tm, kb), lambda i: (i, 0)),
                  pl.BlockSpec((ka, d), lambda i: (0, 0)),
                  pl.BlockSpec((kb, d), lambda i: (0, 0)),
                  pl.BlockSpec((tm, d), lambda i: (i, 0))] + r_in,
        out_specs=out_specs,
        out_shape=out_shape,
        compiler_params=_params("parallel"),
        name="proj2_router",
    )(a, b, wa, wb, res, *router_ops)


def _att_out_router_kernel(o_ref, w_ref, hres_ref, g_ref, whi_ref, wlo_ref, bias_ref, h_ref, *route_out):
    tm = o_ref.shape[0]
    tiles = (hres_ref.shape[0] - N_META) // tm
    r0 = pl.multiple_of(N_META + lax.rem(pl.program_id(0), tiles) * tm, 16)
    x = hres_ref[pl.ds(r0, tm), :] + _dot(o_ref[...], w_ref[...])
    h_ref[...] = x
    _route(x, g_ref, whi_ref, wlo_ref, bias_ref, *route_out)


def _att_out_router(o, w, h3, router_ops, tm):
    b, t, d = h3.shape
    n, nq = o.shape
    tiles = (t - N_META) // tm
    r_in, out_specs, out_shape = _router_specs(n, d, tm)
    return pl.pallas_call(
        _att_out_router_kernel,
        grid=(n // tm,),
        in_specs=[pl.BlockSpec((tm, nq), lambda i: (i, 0)),
                  pl.BlockSpec((nq, d), lambda i: (0, 0)),
                  pl.BlockSpec((None, t, d), lambda i: (i // tiles, 0, 0))] + r_in,
        out_specs=out_specs,
        out_shape=out_shape,
        compiler_params=_params("parallel"),
        name="att_out_router",
    )(o, w, h3, *router_ops)


MOE_TILE = 512
SEG_ALIGN = 16
STRIP = 32


def _pow2_sizes(max_rows, min_rows):
    sizes, s = [], min_rows
    while s <= max_rows:
        sizes.append(s)
        s *= 2
    return sizes[::-1]


def _strip_copies(n_rows, sizes, make_copy):
    pos = 0
    for sz in sizes:
        bit = n_rows & sz
        yield bit, make_copy(pos, sz)
        pos = pos + bit


def _slot_key(group, rank, boff_ref, b):
    key = rank
    for g in range(MOE_GROUPS):
        key = key + jnp.where(group == g, boff_ref[MOE_GROUPS * b + g], 0)
    return key


def _start_all(copies):
    for bit, cs in copies:
        @pl.when(bit != 0)
        def _():
            for c in cs:
                c.start()


def _wait_all(copies):
    for bit, cs in copies:
        @pl.when(bit != 0)
        def _():
            for c in cs:
                c.wait()


def _dispatch_kernel(off_ref, cpad_ref, boff_ref, tail_ref, x_ref, gates_ref, inforow_ref,
                     xs_hbm, gs_hbm, xstg, gstg, zx, zg, sem):
    b = pl.program_id(0)
    last = pl.num_programs(0) - 1
    tb = x_ref.shape[0]
    ks = xstg.shape[1]
    cur = lax.rem(b, 2)
    key = _slot_key(inforow_ref[0:1, :], inforow_ref[1:2, :], boff_ref, b)
    slot = lax.broadcasted_iota(jnp.int32, (ks, tb), 0)
    perm = jnp.where(slot == key, 1.0, 0.0).astype(BF16)
    xstg[cur] = _dot(perm, x_ref[...]).astype(BF16)
    gt = gates_ref[...]
    hi = gt.astype(BF16)
    lo = (gt - hi.astype(F32)).astype(BF16)
    gstg[cur] = _dot(perm, hi) + _dot(perm, lo)

    sizes = _pow2_sizes(tb, SEG_ALIGN)

    def copies(blk, sl):
        for g in range(MOE_GROUPS):
            src0 = boff_ref[MOE_GROUPS * blk + g]
            dst0 = off_ref[MOE_GROUPS * blk + g]

            def make(pos, sz, src0=src0, dst0=dst0):
                src = pl.ds(pl.multiple_of(src0 + pos, SEG_ALIGN), sz)
                dst = pl.ds(pl.multiple_of(dst0 + pos, SEG_ALIGN), sz)
                return (pltpu.make_async_copy(xstg.at[sl, src, :], xs_hbm.at[dst, :], sem.at[sl, 0]),
                        pltpu.make_async_copy(gstg.at[sl, src, :], gs_hbm.at[dst, :], sem.at[sl, 1]))

            yield from _strip_copies(cpad_ref[MOE_GROUPS * blk + g], sizes, make)

    _start_all(copies(b, cur))

    @pl.when(b > 0)
    def _():
        _wait_all(copies(jnp.maximum(b - 1, 0), 1 - cur))

    @pl.when(b == last)
    def _():
        _wait_all(copies(b, cur))
        zx[...] = jnp.zeros(zx.shape, BF16)
        zg[...] = jnp.zeros(zg.shape, F32)
        tail_sizes = _pow2_sizes(MOE_TILE // 2, SEG_ALIGN)

        def tails():
            for g in range(MOE_GROUPS):
                dst0 = tail_ref[g]

                def make(pos, sz, dst0=dst0):
                    dst = pl.ds(pl.multiple_of(dst0 + pos, SEG_ALIGN), sz)
                    return (pltpu.make_async_copy(zx.at[pl.ds(0, sz), :], xs_hbm.at[dst, :], sem.at[0, 0]),
                            pltpu.make_async_copy(zg.at[pl.ds(0, sz), :], gs_hbm.at[dst, :], sem.at[0, 1]))

                yield from _strip_copies(tail_ref[MOE_GROUPS + g], tail_sizes, make)

        _start_all(tails())
        _wait_all(tails())

        def spare(i):
            dst = pl.ds(pl.multiple_of(i * MOE_TILE, MOE_TILE), MOE_TILE)
            return (pltpu.make_async_copy(zx, xs_hbm.at[dst, :], sem.at[0, 0]),
                    pltpu.make_async_copy(zg, gs_hbm.at[dst, :], sem.at[0, 1]))

        first_spare = tail_ref[2 * MOE_GROUPS]
        n_tiles = xs_hbm.shape[0] // MOE_TILE

        @pl.loop(first_spare, n_tiles)
        def _(i):
            cx, cg = spare(i)
            cx.start()
            cg.start()

        @pl.loop(first_spare, n_tiles)
        def _(i):
            cx, cg = spare(i)
            cx.wait()
            cg.wait()


def _dispatch(xn, gates, inforow, off, cpad, boff, tail, n_sorted, tb, ks):
    n, d = xn.shape
    nb = n // tb
    grid_spec = pltpu.PrefetchScalarGridSpec(
        num_scalar_prefetch=4,
        grid=(nb,),
        in_specs=[pl.BlockSpec((tb, d), lambda i, *_: (i, 0)),
                  pl.BlockSpec((tb, LANES), lambda i, *_: (i, 0)),
                  pl.BlockSpec((None, 8, tb), lambda i, *_: (i, 0, 0))],
        out_specs=[pl.BlockSpec(memory_space=pl.ANY), pl.BlockSpec(memory_space=pl.ANY)],
        scratch_shapes=[pltpu.VMEM((2, ks, d), BF16), pltpu.VMEM((2, ks, LANES), F32),
                        pltpu.VMEM((MOE_TILE, d), BF16), pltpu.VMEM((MOE_TILE, LANES), F32),
                        pltpu.SemaphoreType.DMA((2, 2))])
    return pl.pallas_call(
        _dispatch_kernel,
        grid_spec=grid_spec,
        out_shape=[jax.ShapeDtypeStruct((n_sorted, d), BF16),
                   jax.ShapeDtypeStruct((n_sorted, LANES), F32)],
        compiler_params=_params("arbitrary"),
        name="moe_dispatch",
    )(off, cpad, boff, tail, xn, gates, inforow)


def _group_mlp_kernel(tg_ref, nused_ref, x_ref, gs_ref, wg_ref, wu_ref, wd_ref, y_ref):
    i = pl.program_id(0)

    @pl.when(i < nused_ref[0])
    def _():
        x = x_ref[...]
        gs = gs_ref[...]
        lane = lax.broadcasted_iota(jnp.int32, gs.shape, 1)
        lane0 = tg_ref[i] * MOE_EXPERTS_PER_GROUP
        acc = jnp.zeros(y_ref.shape, F32)
        for e in range(MOE_EXPERTS_PER_GROUP):
            he = _silu(_dot(x, wg_ref[e])) * _dot(x, wu_ref[e])
            gate = jnp.sum(jnp.where(lane == lane0 + e, gs, 0.0), axis=-1, keepdims=True)
            acc = acc + _dot((he * gate).astype(BF16), wd_ref[e])
        y_ref[...] = acc.astype(y_ref.dtype)

    @pl.when(i >= nused_ref[0])
    def _():
        y_ref[...] = jnp.zeros(y_ref.shape, y_ref.dtype)


def _group_mlp(xs, gs, wg, wu, wd, tile_group, nused):
    n_sorted, d = xs.shape
    ne = MOE_EXPERTS_PER_GROUP

    def rows(i, tg, nu):
        return (jnp.minimum(i, nu[0] - 1), 0)

    def group(i, tg, nu):
        return (tg[i], 0, 0)

    grid_spec = pltpu.PrefetchScalarGridSpec(
        num_scalar_prefetch=2,
        grid=(n_sorted // MOE_TILE,),
        in_specs=[pl.BlockSpec((MOE_TILE, d), rows),
                  pl.BlockSpec((MOE_TILE, LANES), rows),
                  pl.BlockSpec((ne, d, MOE_D_FF), group),
                  pl.BlockSpec((ne, d, MOE_D_FF), group),
                  pl.BlockSpec((ne, MOE_D_FF, d), group)],
        out_specs=pl.BlockSpec((MOE_TILE, d), lambda i, tg, nu: (i, 0)))
    return pl.pallas_call(
        _group_mlp_kernel,
        grid_spec=grid_spec,
        out_shape=jax.ShapeDtypeStruct((n_sorted, d), BF16),
        compiler_params=_params("arbitrary"),
        name="moe_experts",
    )(tile_group, nused, xs, gs, wg, wu, wd)


def _combine_kernel(off_ref, cpad_ref, boff_ref, info_ref, res_ref, g_ref, *rest, project):
    if project:
        w_ref, y_hbm, o_ref, u_ref, ybuf, sem = rest
    else:
        y_hbm, o_ref, ybuf, sem = rest
    b = pl.program_id(0)
    last = pl.num_programs(0) - 1
    tb = res_ref.shape[0]
    ks = ybuf.shape[1]
    cur = lax.rem(b, 2)
    sizes = _pow2_sizes(tb, STRIP)

    def copies(blk, sl):
        for g in range(MOE_GROUPS):
            src0 = off_ref[MOE_GROUPS * blk + g]
            dst0 = boff_ref[MOE_GROUPS * blk + g]

            def make(pos, sz, src0=src0, dst0=dst0):
                src = pl.ds(pl.multiple_of(src0 + pos, SEG_ALIGN), sz)
                dst = pl.ds(pl.multiple_of(dst0 + pos, STRIP), sz)
                return (pltpu.make_async_copy(y_hbm.at[src, :], ybuf.at[sl, dst, :], sem.at[sl]),)

            n_rows = (cpad_ref[MOE_GROUPS * blk + g] + (STRIP - 1)) & (-STRIP)
            yield from _strip_copies(n_rows, sizes, make)

    @pl.when(b == 0)
    def _():
        ybuf[...] = jnp.zeros(ybuf.shape, ybuf.dtype)
        _start_all(copies(b, cur))

    @pl.when(b < last)
    def _():
        _start_all(copies(jnp.minimum(b + 1, last), 1 - cur))

    _wait_all(copies(b, cur))

    key = _slot_key(info_ref[:, 0:1], info_ref[:, 1:2], boff_ref, b)
    slot = lax.broadcasted_iota(jnp.int32, (tb, ks), 1)
    perm = jnp.where(slot == key, 1.0, 0.0).astype(BF16)
    y = res_ref[...] + _dot(perm, ybuf[cur])
    ms = jnp.mean(y * y, axis=-1, keepdims=True)
    yn = y * lax.rsqrt(ms + NORM_EPS) * g_ref[...]
    if project:
        o_ref[...] = y
        u_ref[...] = _dot(yn.astype(BF16), w_ref[...]).astype(u_ref.dtype)
    else:
        o_ref[...] = yn


def _combine(ys, info, res, norm_g, w_next, off, cpad, boff, tb, ks):
    n, d = res.shape
    project = w_next is not None
    row = lambda width: pl.BlockSpec((tb, width), lambda i, *_: (i, 0))
    in_specs = [row(LANES), row(d), pl.BlockSpec((1, d), lambda i, *_: (0, 0))]
    args = [info, res, norm_g.reshape(1, d)]
    out_specs, out_shape = [row(d)], [jax.ShapeDtypeStruct((n, d), F32)]
    if project:
        nout = w_next.shape[1]
        in_specs.append(pl.BlockSpec((d, nout), lambda i, *_: (0, 0)))
        args.append(w_next)
        out_specs.append(row(nout))
        out_shape.append(jax.ShapeDtypeStruct((n, nout), BF16))
    grid_spec = pltpu.PrefetchScalarGridSpec(
        num_scalar_prefetch=3,
        grid=(n // tb,),
        in_specs=in_specs + [pl.BlockSpec(memory_space=pl.ANY)],
        out_specs=out_specs,
        scratch_shapes=[pltpu.VMEM((2, ks, d), BF16), pltpu.SemaphoreType.DMA((2,))])
    out = pl.pallas_call(
        functools.partial(_combine_kernel, project=project),
        grid_spec=grid_spec,
        out_shape=out_shape,
        compiler_params=_params("arbitrary"),
        name="moe_combine",
    )(off, cpad, boff, *args, ys)
    return tuple(out) if project else out[0]


def _round_up(x, m):
    return (x + m - 1) // m * m


def _moe_layer(routed, expert_weights, norm_g, w_next, tb):
    h, xn, gates, info, inforow, counts = routed
    w_gate, w_up, w_down = expert_weights
    n, d = h.shape
    nb = n // tb

    cnt = counts[:, 0, :MOE_GROUPS]
    cpad = _round_up(cnt, SEG_ALIGN)
    strip = _round_up(cnt, STRIP)
    used = jnp.sum(cpad, axis=0)
    region = _round_up(used, MOE_TILE)
    region_start = jnp.cumsum(region) - region
    off = region_start[None, :] + jnp.cumsum(cpad, axis=0) - cpad
    boff = jnp.cumsum(strip, axis=1) - strip
    nused = (jnp.sum(region) // MOE_TILE).reshape(1)
    tail = jnp.concatenate([region_start + used, region - used, nused]).astype(jnp.int32)
    n_tiles = -(-(n + nb * MOE_GROUPS * (SEG_ALIGN - 1) + MOE_GROUPS * (MOE_TILE - 1)) // MOE_TILE) + 1
    tile_start = jnp.arange(n_tiles, dtype=jnp.int32) * MOE_TILE
    region_end = region_start + region
    tile_group = jnp.minimum(jnp.sum(tile_start[:, None] >= region_end[None, :], axis=1), MOE_GROUPS - 1)
    ks =tb + MOE_GROUPS * STRIP

    flat = lambda a: a.reshape(-1).astype(jnp.int32)
    xs, gs = _dispatch(xn, gates, inforow, flat(off), flat(cpad), flat(boff), tail,
                       n_tiles * MOE_TILE, tb, ks)
    ys = _group_mlp(xs, gs, w_gate, w_up, w_down,
                    tile_group.astype(jnp.int32), nused.astype(jnp.int32))
    return _combine(ys, info, h, norm_g, w_next, flat(off), flat(cpad), flat(boff), tb, ks)


ROPE_ROW_TILE = 688
ATT_Q_TILE = 256
ATT_TILES_PER_ITER = 2
ATT_SCORES_AHEAD = 2
ATT_SCORE_SLOTS = 4
ATT_VT_ROWS = ATT_HEAD_DIM + 16
ATT_Q_SCALE = ATT_HEAD_DIM ** -0.5 * 1.4426950408889634


def _eye(n):
    return jnp.where(lax.broadcasted_iota(jnp.int32, (n, n), 0) == lax.broadcasted_iota(jnp.int32, (n, n), 1),
                     1.0, 0.0).astype(BF16)


def _transpose_bf16(x, eye):
    return lax.dot_general(eye, x, _NT, preferred_element_type=F32).astype(BF16)


def _rope_kernel(q_ref, k_ref, v_ref, cos_ref, sin_ref, qn_ref, kn_ref, qt_ref, kp_ref, vt_ref):
    t = q_ref.shape[0]
    tq = ATT_Q_TILE
    dh = ATT_HEAD_DIM
    gi = lax.broadcasted_iota(jnp.int32, (LANES, LANES), 0) >> 6
    gj = lax.broadcasted_iota(jnp.int32, (LANES, LANES), 1) >> 6
    gmat = jnp.where(gi == gj, 1.0, 0.0).astype(BF16)
    eye_q = _eye(LANES)

    def norm_rope(x, gain, cos, sin):
        lane = lax.broadcasted_iota(jnp.int32, x.shape, 1)
        sq = x * x
        hi = sq.astype(BF16)
        lo = (sq - hi.astype(F32)).astype(BF16)
        ms = (_dot(hi, gmat) + _dot(lo, gmat)) * (1.0 / dh)
        xn = x * lax.rsqrt(ms + NORM_EPS) * gain
        partner = jnp.where((lane & 1) == 0, pltpu.roll(xn, LANES - 1, 1), pltpu.roll(xn, 1, 1))
        return xn * cos + partner * sin

    for j in range(q_ref.shape[1] // LANES):
        cols = slice(j * LANES, (j + 1) * LANES)
        y = norm_rope(q_ref[N_META:, cols].astype(F32), qn_ref[...], cos_ref[N_META:, :], sin_ref[N_META:, :])
        yt = _transpose_bf16((y * ATT_Q_SCALE).astype(BF16), eye_q)
        for i in range((t - N_META) // tq):
            qt_ref[i, cols, :] = yt[:, i * tq:(i + 1) * tq]

    low = lax.broadcasted_iota(jnp.int32, (t, LANES), 1) < dh
    for j in range(k_ref.shape[1] // LANES):
        cols = slice(j * LANES, (j + 1) * LANES)
        y = norm_rope(k_ref[:, cols].astype(F32), kn_ref[...], cos_ref[...], sin_ref[...])
        kp_ref[2 * j] = jnp.where(low, y, 0.0).astype(BF16)
        kp_ref[2 * j + 1] = jnp.where(low, pltpu.roll(y, dh, 1), 0.0).astype(BF16)

    n_real = t - N_META
    eye_v = _eye(v_ref.shape[1])
    vt_real = _transpose_bf16(v_ref[N_META:, :], eye_v)
    vt_meta = _transpose_bf16(v_ref[0:N_META, :], eye_v)
    for h in range(ATT_KV_HEADS):
        vt_ref[h, 0:dh, 0:n_real] = vt_real[h * dh:(h + 1) * dh, :]
        vt_ref[h, 0:dh, n_real:t] = vt_meta[h * dh:(h + 1) * dh, :]
        extra = lax.broadcasted_iota(jnp.int32, (ATT_VT_ROWS - dh, t), 0)
        vt_ref[h, dh:ATT_VT_ROWS, :] = jnp.where(extra == 0, 1.0, 0.0).astype(BF16)


def _rope(u3, cos_e, sin_e, qn, kn):
    b, t, _ = u3.shape
    nq = ATT_HEADS * ATT_HEAD_DIM
    nkv = ATT_KV_HEADS * ATT_HEAD_DIM
    n_qt = (t - N_META) // ATT_Q_TILE
    full = lambda shape: pl.BlockSpec(shape, lambda i: (0, 0))
    return pl.pallas_call(
        _rope_kernel,
        grid=(b,),
        in_specs=[pl.BlockSpec((None, t, nq), lambda i: (i, 0, 0)),
                  pl.BlockSpec((None, t, nkv), lambda i: (i, 0, nq // nkv)),
                  pl.BlockSpec((None, t, nkv), lambda i: (i, 0, nq // nkv + 1)),
                  full((t, LANES)), full((t, LANES)), full((1, LANES)), full((1, LANES))],
        out_specs=[pl.BlockSpec((None, n_qt, nq, ATT_Q_TILE), lambda i: (i, 0, 0, 0)),
                   pl.BlockSpec((None, ATT_KV_HEADS, t, LANES), lambda i: (i, 0, 0, 0)),
                   pl.BlockSpec((None, ATT_KV_HEADS, ATT_VT_ROWS, t), lambda i: (i, 0, 0, 0))],
        out_shape=[jax.ShapeDtypeStruct((b, n_qt, nq, ATT_Q_TILE), BF16),
                   jax.ShapeDtypeStruct((b, ATT_KV_HEADS, t, LANES), BF16),
                   jax.ShapeDtypeStruct((b, ATT_KV_HEADS, ATT_VT_ROWS, t), BF16)],
        compiler_params=_params("parallel"),
        name="qk_norm_rope",
    )(u3, u3, u3, cos_e, sin_e, qn, kn)


def _attn_kernel(qt_ref, k_ref, vt_ref, wg_ref, wu_ref, wd_ref, o_ref, wg_o, wu_o, wd_o, s_ref):
    _cast_through((wg_ref, wu_ref, wd_ref), (wg_o, wu_o, wd_o))
    n_qt, w, tq = qt_ref.shape
    dh = ATT_HEAD_DIM
    n_real = k_ref.shape[0] - N_META
    eye = _eye(tq)
    zpad = jnp.zeros((LANES - dh, tq), BF16)

    def scores(i, g, slot):
        qp = jnp.concatenate([qt_ref[i, g * dh:(g + 1) * dh, :], zpad], axis=0)
        s_r = _dot(k_ref[N_META:, :], qp)
        s_m = _dot(k_ref[0:N_META, :], qp)
        s_ref[slot, 0:n_real, :] = s_r
        s_ref[slot, n_real:, :] = s_m
        return jnp.maximum(jnp.max(s_r, axis=0, keepdims=True), jnp.max(s_m, axis=0, keepdims=True))

    steps = [(u, g) for u in range(ATT_TILES_PER_ITER) for g in range(ATT_GROUP)]
    n_steps = len(steps)
    n_iter = n_qt // ATT_TILES_PER_ITER
    ahead = ATT_SCORES_AHEAD
    n_slots = s_ref.shape[0]
    assert n_steps % n_slots == 0 and ahead < n_slots

    def issue_scores(it, n):
        it2 = jnp.minimum(it + n // n_steps, n_iter - 1)
        u, g = steps[n % n_steps]
        return scores(it2 * ATT_TILES_PER_ITER + u, g, n % n_slots)

    def q_tiles(it, pending):
        i0 = it * ATT_TILES_PER_ITER
        outs = []
        pending = list(pending)
        for n, (u, g) in enumerate(steps):
            m, slot = pending.pop(0), n % n_slots
            pending.append(issue_scores(it, n + ahead))
            p_r = jnp.exp2(s_ref[slot, 0:n_real, :] - m).astype(BF16)
            p_m = jnp.exp2(s_ref[slot, n_real:, :] - m).astype(BF16)
            ot = _dot(vt_ref[:, 0:n_real], p_r) + _dot(vt_ref[:, n_real:], p_m)
            outs.append((ot[0:dh] * (1.0 / ot[dh:dh + 1])).astype(BF16))
            if g == ATT_GROUP - 1:
                rows = pl.ds(pl.multiple_of((i0 + u) * tq, tq), tq)
                ot_all = jnp.concatenate(outs, axis=0)
                o_ref[rows, :] = lax.dot_general(eye, ot_all, _NT, preferred_element_type=F32).astype(o_ref.dtype)
                outs = []
        return tuple(pending)

    first = tuple(issue_scores(0, n) for n in range(ahead))
    lax.fori_loop(0, n_iter, q_tiles, first)


def _attention(qt, kp, vt, expert_weights, layer):
    b, n_qt, nq, tq = qt.shape
    t = kp.shape[2]
    w = ATT_GROUP * ATT_HEAD_DIM
    grid = (b, ATT_KV_HEADS)
    c_in, c_out, c_shape = _expert_cast_specs(expert_weights, layer, grid)
    o, *cast = pl.pallas_call(
        _attn_kernel,
        grid=grid,
        in_specs=[pl.BlockSpec((None, n_qt, w, tq), lambda i, h: (i, 0, h, 0)),
                  pl.BlockSpec((None, None, t, LANES), lambda i, h: (i, h, 0, 0)),
                  pl.BlockSpec((None, None, ATT_VT_ROWS, t), lambda i, h: (i, h, 0, 0))] + c_in,
        out_specs=[pl.BlockSpec((None, n_qt * tq, w), lambda i, h: (i, 0, h))] + c_out,
        out_shape=[jax.ShapeDtypeStruct((b, n_qt * tq, nq), BF16)] + c_shape,
        scratch_shapes=[pltpu.VMEM((ATT_SCORE_SLOTS, t, tq), F32)],
        compiler_params=_params("parallel", "parallel"),
        name="gqa_attention",
    )(qt, kp, vt, *expert_weights)
    return o, cast


def _rope_tables(n_tokens):
    rows_n = n_tokens // GRID_W
    rows = jnp.repeat(jnp.arange(rows_n), GRID_W).astype(F32)
    cols = jnp.tile(jnp.arange(GRID_W), rows_n).astype(F32)
    axis_dims = ATT_HEAD_DIM // 2
    freqs = ROPE_THETA ** (-jnp.arange(0, axis_dims, 2, dtype=F32) / axis_dims)
    ang = jnp.concatenate([rows[:, None] * freqs, cols[:, None] * freqs], axis=-1)
    ang = jnp.concatenate([jnp.zeros((N_META, ang.shape[1]), F32), ang], axis=0)
    cos_e = jnp.tile(jnp.repeat(jnp.cos(ang), 2, axis=1), (1, LANES // ATT_HEAD_DIM))
    sign = jnp.tile(jnp.array([-1.0, 1.0], F32), LANES // 2)
    sin_e = jnp.tile(jnp.repeat(jnp.sin(ang), 2, axis=1), (1, LANES // ATT_HEAD_DIM)) * sign
    return cos_e, sin_e


def kernel(x, meta_tokens, norm_mix, norm_ffn, norm_final, sc_w_in, sc_conv_w, gla_w_gate_fwd, gla_b_gate_fwd, gla_w_gate_bwd, gla_b_gate_bwd, gla_norm_w, sc_w_out, att_w_in, att_q_norm, att_k_norm, att_w_out, moe_w_group, moe_b_group, moe_w_expert, moe_b_expert, moe_w_gate, moe_w_up, moe_w_down):
    bsz, s_len, d = x.shape
    t = s_len + N_META
    n0, n1 = bsz * t, bsz * s_len
    tm0 = t // 3
    tm1 = 512

    expert_f32 = (moe_w_gate, moe_w_up, moe_w_down)
    meta = jnp.broadcast_to(meta_tokens.astype(x.dtype)[None], (bsz, N_META, d))
    h = jnp.concatenate([meta, x], axis=1).reshape(n0, d)

    nk = GLA_HEADS * GLA_DK
    mix_in = sc_w_in.shape[2]
    mix_pad = -mix_in % MXU_WIDTH
    w_in = jnp.pad(sc_w_in[0], ((0, 0), (0, mix_pad))).astype(BF16)
    u = _norm_matmul(h, norm_mix[0], w_in, tm0, mix_in + mix_pad)
    u3 = u.reshape(bsz, t, mix_in + mix_pad)
    y_a, experts0 = _gated_conv(u3, sc_conv_w[0], expert_f32, 0)
    wc = jnp.zeros((LANES, 2 * nk), F32)
    wc = wc.at[:GLA_GATE_RANK, :nk].set(gla_w_gate_fwd[0])
    wc = wc.at[GLA_GATE_RANK:2 * GLA_GATE_RANK, nk:].set(gla_w_gate_bwd[0])
    bc = jnp.concatenate([gla_b_gate_fwd[0], gla_b_gate_bwd[0]]).reshape(1, 2 * nk)
    y_b = _gla(u3, wc.astype(BF16), bc, gla_norm_w[0].reshape(1, GLA_DV))
    w_out = sc_w_out[0].astype(BF16)
    tb0 = 3 * LANES
    router0 = _router_operands(norm_ffn[0], moe_w_group[0], moe_b_group[0], moe_w_expert[0], moe_b_expert[0])
    routed = _proj2_router(y_a.reshape(n0, SC_WIDTH), y_b.reshape(n0, -1), w_out[:SC_WIDTH], w_out[SC_WIDTH:],
                           h, router0, tb0)
    h, u = _moe_layer(routed, experts0, norm_mix[1], att_w_in[0].astype(BF16), tb0)
    u3 = u.reshape(bsz, t, -1)
    cos_e, sin_e = _rope_tables(s_len)
    qn = jnp.tile(att_q_norm[0], LANES // ATT_HEAD_DIM).reshape(1, LANES)
    kn = jnp.tile(att_k_norm[0], LANES // ATT_HEAD_DIM).reshape(1, LANES)
    qt, kp, vt = _rope(u3, cos_e, sin_e, qn, kn)
    o, experts1 = _attention(qt, kp, vt, expert_f32, 1)
    router1 = _router_operands(norm_ffn[1], moe_w_group[1], moe_b_group[1], moe_w_expert[1], moe_b_expert[1])
    routed = _att_out_router(o.reshape(n1, d), att_w_out[0].astype(BF16), h.reshape(bsz, t, d), router1, tm1)
    out = _moe_layer(routed, experts1, norm_final, None, tm1)
    return out.reshape(bsz, s_len, d)
```

```python
import functools

import jax
import jax.numpy as jnp
from jax import lax
from jax.experimental import pallas as pl
from jax.experimental.pallas import tpu as pltpu

F32 = jnp.float32
BF16 = jnp.bfloat16

NORM_EPS = 1e-6
N_META = 16
GRID_W = 64
ROPE_THETA = 10000.0
SC_WIDTH = 512
GLA_HEADS = 4
GLA_DK = 64
GLA_DV = 128
GLA_GATE_RANK = 16
GLA_GATE_TAU = 16.0
GLA_CHUNK = 64
ATT_HEAD_DIM = 64
ATT_HEADS = 16
ATT_KV_HEADS = 4
ATT_GROUP = ATT_HEADS // ATT_KV_HEADS
MOE_GROUPS = 4
MOE_EXPERTS_PER_GROUP = 8
N_EXPERTS = 32
MOE_D_FF = 256

LANES = 128
MXU_WIDTH = 256
MIB = 2 ** 20
VMEM_LIMIT = 56 * MIB

_NT = (((1,), (1,)), ((), ()))
_TN = (((0,), (0,)), ((), ()))


def _params(*semantics):
    return pltpu.CompilerParams(dimension_semantics=semantics, vmem_limit_bytes=VMEM_LIMIT)


def _dot(a, b):
    return jnp.dot(a, b, preferred_element_type=F32)


def _silu(x):
    return x * (1.0 / (1.0 + jnp.exp(-x)))


def _norm_matmul_kernel(h_ref, g_ref, w_ref, o_ref, xn_ref):
    @pl.when(pl.program_id(1) == 0)
    def _():
        x = h_ref[...]
        ms = jnp.mean(x * x, axis=-1, keepdims=True)
        xn_ref[...] = (x * lax.rsqrt(ms + NORM_EPS) * g_ref[...]).astype(BF16)

    o_ref[...] = _dot(xn_ref[...], w_ref[...]).astype(o_ref.dtype)


def _norm_matmul(h, g, w, tm, tn):
    n, d = h.shape
    nout = w.shape[1]
    return pl.pallas_call(
        _norm_matmul_kernel,
        grid=(n // tm, nout // tn),
        in_specs=[pl.BlockSpec((tm, d), lambda i, j: (i, 0)),
                  pl.BlockSpec((1, d), lambda i, j: (0, 0)),
                  pl.BlockSpec((d, tn), lambda i, j: (0, j))],
        out_specs=pl.BlockSpec((tm, tn), lambda i, j: (i, j)),
        out_shape=jax.ShapeDtypeStruct((n, nout), BF16),
        scratch_shapes=[pltpu.VMEM((tm, d), BF16)],
        compiler_params=_params("parallel", "arbitrary"),
        name="norm_matmul",
    )(h, g.reshape(1, d), w)


def _expert_cast_specs(weights, layer, grid_shape):
    n_steps = 1
    for g in grid_shape:
        n_steps *= g
    assert N_EXPERTS % n_steps == 0
    per_step = N_EXPERTS // n_steps

    def step(*idx):
        flat = 0
        for i, g in zip(idx, grid_shape):
            flat = flat * g + i
        return flat

    in_specs = [pl.BlockSpec((None, per_step) + w.shape[2:], lambda *idx: (layer, step(*idx), 0, 0))
                for w in weights]
    out_specs = [pl.BlockSpec((per_step,) + w.shape[2:], lambda *idx: (step(*idx), 0, 0)) for w in weights]
    out_shape = [jax.ShapeDtypeStruct(w.shape[1:], BF16) for w in weights]
    return in_specs, out_specs, out_shape


def _cast_through(in_refs, out_refs):
    for src, dst in zip(in_refs, out_refs):
        dst[...] = src[...].astype(dst.dtype)


def _conv_kernel(ab_ref, ac_ref, ax_ref, w_ref, wg_ref, wu_ref, wd_ref, o_ref, wg_o, wu_o, wd_o, z_ref):
    _cast_through((wg_ref, wu_ref, wd_ref), (wg_o, wu_o, wd_o))
    t, c = ab_ref.shape
    z = ac_ref[...].astype(F32) * ax_ref[...].astype(F32)
    z_ref[0:8, :] = jnp.zeros((8, c), F32)
    z_ref[8:8 + t, :] = z
    z_ref[8 + t:16 + t, :] = jnp.zeros((8, c), F32)
    w = w_ref[...]
    conv = w[0:1, :] * z_ref[7:7 + t, :] + w[1:2, :] * z + w[2:3, :] * z_ref[9:9 + t, :]
    o_ref[...] = (ab_ref[...].astype(F32) * conv).astype(o_ref.dtype)


def _gated_conv(u3, conv_w, expert_weights, layer):
    b, t, _ = u3.shape
    nblk = SC_WIDTH // LANES
    grid = (b, nblk)
    c_in, c_out, c_shape = _expert_cast_specs(expert_weights, layer, grid)

    def col(off):
        return pl.BlockSpec((None, t, LANES), lambda i, j: (i, 0, off + j))

    y_a, *cast = pl.pallas_call(
        _conv_kernel,
        grid=grid,
        in_specs=[col(0), col(nblk), col(2 * nblk),
                  pl.BlockSpec((3, LANES), lambda i, j: (0, j))] + c_in,
        out_specs=[pl.BlockSpec((None, t, LANES), lambda i, j: (i, 0, j))] + c_out,
        out_shape=[jax.ShapeDtypeStruct((b, t, SC_WIDTH), BF16)] + c_shape,
        scratch_shapes=[pltpu.VMEM((t + 16, LANES), F32)],
        compiler_params=_params("parallel", "parallel"),
        name="gated_conv",
    )(u3, u3, u3, conv_w, *expert_weights)
    return y_a, cast


GLA_ROW_TILE = 688
GLA_EPI_TILE = 688


def _gla_kernel(q_ref, k_ref, v_ref, r_ref, g_ref, wc_ref, bc_ref, nw_ref, y_ref,
                la_ref, of_ref, ob_ref, sf_ref, sb_ref):
    t, nk = q_ref.shape
    nv = v_ref.shape[1]
    c = GLA_CHUNK
    n_real = (t - N_META) // c

    for i in range(t // GLA_ROW_TILE):
        rows = pl.ds(i * GLA_ROW_TILE, GLA_ROW_TILE)
        pre = _dot(g_ref[rows, :], wc_ref[...]) + bc_ref[...]
        la_ref[rows, :] = (jnp.minimum(pre, 0.0) - jnp.log(1.0 + jnp.exp(-jnp.abs(pre)))) * (1.0 / GLA_GATE_TAU)

    def iota(shape, dim):
        return lax.broadcasted_iota(jnp.int32, shape, dim)

    tri_f = jnp.where(iota((c, c), 0) >= iota((c, c), 1), 1.0, 0.0).astype(BF16)
    tri_b = jnp.where(iota((c, c), 0) <= iota((c, c), 1), 1.0, 0.0).astype(BF16)
    keep_f = (iota((c, nk), 1) & (c - 1)) <= iota((c, nk), 0)
    keep_b = (iota((c, nk), 1) & (c - 1)) > iota((c, nk), 0)
    bd_k = (iota((nk, nk), 0) >> 6) == (iota((nk, nk), 1) >> 6)
    bd_v = (iota((nk, nv), 0) >> 6) == (iota((nk, nv), 1) >> 7)
    bd_s = (iota((nv, nk), 0) >> 7) == (iota((nv, nk), 1) >> 6)
    valid0 = iota((c, 1), 0) < N_META

    def cum_decay(rows, valid, tri, la_col):
        la = la_ref[rows, la_col:la_col + nk]
        if valid is not None:
            la = jnp.where(valid, la, 0.0)
        hi = la.astype(BF16)
        rem = la - hi.astype(F32)
        mid = rem.astype(BF16)
        lo = (rem - mid.astype(F32)).astype(BF16)
        return _dot(tri, hi) + _dot(tri, mid) + _dot(tri, lo)

    def chunk(rows, valid, cum, keep, tot_row, s_ref, o_ref, out_rows, n_out):
        q = q_ref[rows, :].astype(F32) * (GLA_DK ** -0.5)
        k = k_ref[rows, :].astype(F32)
        v = v_ref[rows, :].astype(F32)
        if valid is not None:
            q = jnp.where(valid, q, 0.0)
            k = jnp.where(valid, k, 0.0)
            v = jnp.where(valid, v, 0.0)
        tot = cum[tot_row:tot_row + 1, :]
        q_dec = (q * jnp.exp(cum)).astype(BF16)
        k_inv = k * jnp.exp(-cum)
        k_end = (k * jnp.exp(tot - cum)).astype(BF16)
        kbd = jnp.where(bd_k, jnp.concatenate([k_inv] * GLA_HEADS, axis=0), 0.0).astype(BF16)
        p = lax.dot_general(q_dec, kbd, _NT, preferred_element_type=F32)
        s_old = s_ref[...]
        o_inter = lax.dot_general(q_dec, s_old.astype(BF16), _NT, preferred_element_type=F32)
        kvt = lax.dot_general(v.astype(BF16), k_end, _TN, preferred_element_type=F32)
        yield
        p = jnp.where(keep, p, 0.0).astype(BF16)
        vbd = jnp.where(bd_v, jnp.concatenate([v] * GLA_HEADS, axis=0), 0.0).astype(BF16)
        o = _dot(p, vbd) + o_inter
        s_ref[...] = jnp.exp(tot) * s_old + jnp.where(bd_s, kvt, 0.0)
        o_ref[out_rows, :] = o[:n_out]

    def lockstep(*gens):
        live = list(gens)
        while live:
            nxt = []
            for gen in live:
                try:
                    next(gen)
                    nxt.append(gen)
                except StopIteration:
                    pass
            live = nxt

    sf_ref[...] = jnp.zeros(sf_ref.shape, F32)
    sb_ref[...] = jnp.zeros(sb_ref.shape, F32)
    meta_rows = pl.ds(0, c)
    lockstep(chunk(meta_rows, valid0, cum_decay(meta_rows, valid0, tri_f, 0), keep_f, c - 1,
                   sf_ref, of_ref, pl.ds(0, N_META), N_META))

    def rows_fwd(i):
        return pl.ds(pl.multiple_of(N_META + i * c, 16), c)

    def rows_bwd(i):
        return pl.ds(pl.multiple_of(N_META + (n_real - 1 - i) * c, 16), c)

    def body(i, cums):
        cum_f, cum_b = cums
        nxt = jnp.minimum(i + 1, n_real - 1)
        cums = (cum_decay(rows_fwd(nxt), None, tri_f, 0), cum_decay(rows_bwd(nxt), None, tri_b, nk))
        lockstep(chunk(rows_fwd(i), None, cum_f, keep_f, c - 1, sf_ref, of_ref, rows_fwd(i), c),
                 chunk(rows_bwd(i), None, cum_b, keep_b, 0, sb_ref, ob_ref, rows_bwd(i), c))
        return cums

    first = (cum_decay(rows_fwd(0), None, tri_f, 0), cum_decay(rows_bwd(0), None, tri_b, nk))
    lax.fori_loop(0, n_real, body, first)
    lockstep(chunk(meta_rows, valid0, cum_decay(meta_rows, valid0, tri_b, nk), keep_b, 0,
                   sb_ref, ob_ref, pl.ds(0, N_META), N_META))

    def epilogue(i, carry):
        rows = pl.ds(pl.multiple_of(i * GLA_EPI_TILE, 16), GLA_EPI_TILE)
        o = of_ref[rows, :] + ob_ref[rows, :]
        r = r_ref[rows, :].astype(F32)
        for h in range(GLA_HEADS):
            cols = slice(h * GLA_DV, (h + 1) * GLA_DV)
            oh = o[:, cols]
            ms = jnp.mean(oh * oh, axis=-1, keepdims=True)
            yh = oh * lax.rsqrt(ms + NORM_EPS) * nw_ref[...] * _silu(r[:, cols])
            y_ref[rows, cols] = yh.astype(y_ref.dtype)
        return carry

    lax.fori_loop(0, t // GLA_EPI_TILE, epilogue, 0)


def _gla(u3, wc, bc, norm_w):
    b, t, _ = u3.shape
    nk = GLA_HEADS * GLA_DK
    nv = GLA_HEADS * GLA_DV
    q_blk = 3 * SC_WIDTH // nk
    v_blk = (3 * SC_WIDTH + 2 * nk) // nv
    g_blk = (3 * SC_WIDTH + 2 * nk + 2 * nv) // LANES
    full = lambda shape: pl.BlockSpec(shape, lambda i: (0, 0))
    return pl.pallas_call(
        _gla_kernel,
        grid=(b,),
        in_specs=[pl.BlockSpec((None, t, nk), lambda i: (i, 0, q_blk)),
                  pl.BlockSpec((None, t, nk), lambda i: (i, 0, q_blk + 1)),
                  pl.BlockSpec((None, t, nv), lambda i: (i, 0, v_blk)),
                  pl.BlockSpec((None, t, nv), lambda i: (i, 0, v_blk + 1)),
                  pl.BlockSpec((None, t, LANES), lambda i: (i, 0, g_blk)),
                  full((LANES, 2 * nk)), full((1, 2 * nk)), full((1, GLA_DV))],
        out_specs=pl.BlockSpec((None, t, nv), lambda i: (i, 0, 0)),
        out_shape=jax.ShapeDtypeStruct((b, t, nv), BF16),
        scratch_shapes=[pltpu.VMEM((t, 2 * nk), F32),
                        pltpu.VMEM((t, nv), F32), pltpu.VMEM((t, nv), F32),
                        pltpu.VMEM((nv, nk), F32), pltpu.VMEM((nv, nk), F32)],
        compiler_params=_params("parallel"),
        name="gla",
    )(u3, u3, u3, u3, u3, wc, bc, norm_w)


GROUP_LANE0 = N_EXPERTS


def _route(x, g_ref, whi_ref, wlo_ref, b_ref, xn_ref, gates_ref, info_ref, inforow_ref, count_ref):
    ms = jnp.mean(x * x, axis=-1, keepdims=True)
    xn = x * lax.rsqrt(ms + NORM_EPS) * g_ref[...]
    xhi = xn.astype(BF16)
    xn_ref[...] = xhi
    xlo = (xn - xhi.astype(F32)).astype(BF16)
    logits = _dot(xhi, whi_ref[...]) + _dot(xhi, wlo_ref[...]) + _dot(xlo, whi_ref[...]) + b_ref[...]

    lane = lax.broadcasted_iota(jnp.int32, logits.shape, 1)
    neg = -jnp.inf
    big = jnp.int32(LANES)
    is_group = (lane >> 2) == (GROUP_LANE0 >> 2)
    gl = jnp.where(is_group, logits, neg)
    gmax = jnp.max(gl, axis=-1, keepdims=True)
    g_p = 1.0 / jnp.sum(jnp.exp(gl - gmax), axis=-1, keepdims=True)
    g_idx = jnp.min(jnp.where(gl == gmax, lane, big), axis=-1, keepdims=True) - GROUP_LANE0
    el = jnp.where((lane >> 3) == g_idx, logits, neg)
    m1 = jnp.max(el, axis=-1, keepdims=True)
    i1 = jnp.min(jnp.where(el == m1, lane, big), axis=-1, keepdims=True)
    el2 = jnp.where(lane == i1, neg, el)
    m2 = jnp.max(el2, axis=-1, keepdims=True)
    i2 = jnp.min(jnp.where(el2 == m2, lane, big), axis=-1, keepdims=True)
    e2 = jnp.exp(m2 - m1)
    w1 = g_p / (1.0 + e2)
    w2 = g_p * e2 / (1.0 + e2)
    gates_ref[...] = jnp.where(lane == i1, w1, 0.0) + jnp.where(lane == i2, w2, 0.0)

    tm = x.shape[0]
    onehot = jnp.where(lane == g_idx, 1.0, 0.0)
    r_i = lax.broadcasted_iota(jnp.int32, (tm, tm), 0)
    c_i = lax.broadcasted_iota(jnp.int32, (tm, tm), 1)
    before = jnp.where(r_i > c_i, 1.0, 0.0).astype(BF16)
    prefix = _dot(before, onehot.astype(BF16))
    rank = jnp.sum(jnp.where(lane == g_idx, prefix, 0.0), axis=-1, keepdims=True).astype(jnp.int32)
    info_ref[...] = jnp.where(lane == 0, g_idx, jnp.where(lane == 1, rank, 0))
    eye = r_i == c_i
    g_row = jnp.sum(jnp.where(eye, g_idx, 0), axis=0, keepdims=True)
    rank_row = jnp.sum(jnp.where(eye, rank, 0), axis=0, keepdims=True)
    sub = lax.broadcasted_iota(jnp.int32, inforow_ref.shape, 0)
    inforow_ref[...] = jnp.where(sub == 0, g_row, jnp.where(sub == 1, rank_row, 0))
    counts = jnp.sum(onehot, axis=0, keepdims=True).astype(jnp.int32)
    count_ref[...] = jnp.broadcast_to(counts, count_ref.shape)


def _router_operands(norm_g, w_group, b_group, w_expert, b_expert):
    d = w_group.shape[0]
    pad = LANES - N_EXPERTS - MOE_GROUPS
    wr = jnp.concatenate([w_expert, w_group, jnp.zeros((d, pad), F32)], axis=1)
    whi = wr.astype(BF16)
    wlo = (wr - whi.astype(F32)).astype(BF16)
    bias = jnp.concatenate([b_expert, b_group, jnp.zeros((pad,), F32)]).reshape(1, LANES)
    return norm_g.reshape(1, d), whi, wlo, bias


def _router_specs(n, d, tm):
    nb = n // tm
    const = lambda shape: pl.BlockSpec(shape, lambda i: (0,) * len(shape))
    in_specs = [const((1, d)), const((d, LANES)), const((d, LANES)), const((1, LANES))]
    out_specs = [pl.BlockSpec((tm, d), lambda i: (i, 0)),
                 pl.BlockSpec((tm, d), lambda i: (i, 0)),
                 pl.BlockSpec((tm, LANES), lambda i: (i, 0)),
                 pl.BlockSpec((tm, LANES), lambda i: (i, 0)),
                 pl.BlockSpec((None, 8, tm), lambda i: (i, 0, 0)),
                 pl.BlockSpec((None, 8, LANES), lambda i: (i, 0, 0))]
    out_shape = [jax.ShapeDtypeStruct((n, d), F32),
                 jax.ShapeDtypeStruct((n, d), BF16),
                 jax.ShapeDtypeStruct((n, LANES), F32),
                 jax.ShapeDtypeStruct((n, LANES), jnp.int32),
                 jax.ShapeDtypeStruct((nb, 8, tm), jnp.int32),
                 jax.ShapeDtypeStruct((nb, 8, LANES), jnp.int32)]
    return in_specs, out_specs, out_shape


def _proj2_router_kernel(a_ref, b_ref, wa_ref, wb_ref, res_ref, g_ref, whi_ref, wlo_ref, bias_ref,
                         h_ref, *route_out):
    x = res_ref[...] + _dot(a_ref[...], wa_ref[...]) + _dot(b_ref[...], wb_ref[...])
    h_ref[...] = x
    _route(x, g_ref, whi_ref, wlo_ref, bias_ref, *route_out)


def _proj2_router(a, b, wa, wb, res, router_ops, tm):
    n, d = res.shape
    ka, kb = a.shape[1], b.shape[1]
    r_in, out_specs, out_shape = _router_specs(n, d, tm)
    return pl.pallas_call(
        _proj2_router_kernel,
        grid=(n // tm,),
        in_specs=[pl.BlockSpec((tm, ka), lambda i: (i, 0)),
                  pl.BlockSpec((tm, kb), lambda i: (i, 0)),
                  pl.BlockSpec((ka, d), lambda i: (0, 0)),
                  pl.BlockSpec((kb, d), lambda i: (0, 0)),
                  pl.BlockSpec((tm, d), lambda i: (i, 0))] + r_in,
        out_specs=out_specs,
        out_shape=out_shape,
        compiler_params=_params("parallel"),
        name="proj2_router",
    )(a, b, wa, wb, res, *router_ops)


def _att_out_router_kernel(o_ref, w_ref, hres_ref, g_ref, whi_ref, wlo_ref, bias_ref, h_ref, *route_out):
    tm = o_ref.shape[0]
    tiles = (hres_ref.shape[0] - N_META) // tm
    r0 = pl.multiple_of(N_META + lax.rem(pl.program_id(0), tiles) * tm, 16)
    x = hres_ref[pl.ds(r0, tm), :] + _dot(o_ref[...], w_ref[...])
    h_ref[...] = x
    _route(x, g_ref, whi_ref, wlo_ref, bias_ref, *route_out)


def _att_out_router(o, w, h3, router_ops, tm):
    b, t, d = h3.shape
    n, nq = o.shape
    tiles = (t - N_META) // tm
    r_in, out_specs, out_shape = _router_specs(n, d, tm)
    return pl.pallas_call(
        _att_out_router_kernel,
        grid=(n // tm,),
        in_specs=[pl.BlockSpec((tm, nq), lambda i: (i, 0)),
                  pl.BlockSpec((nq, d), lambda i: (0, 0)),
                  pl.BlockSpec((None, t, d), lambda i: (i // tiles, 0, 0))] + r_in,
        out_specs=out_specs,
        out_shape=out_shape,
        compiler_params=_params("parallel"),
        name="att_out_router",
    )(o, w, h3, *router_ops)


MOE_TILE = 512
SEG_ALIGN = 16
STRIP = 32


def _pow2_sizes(max_rows, min_rows):
    sizes, s = [], min_rows
    while s <= max_rows:
        sizes.append(s)
        s *= 2
    return sizes[::-1]


def _strip_copies(n_rows, sizes, make_copy):
    pos = 0
    for sz in sizes:
        bit = n_rows & sz
        yield bit, make_copy(pos, sz)
        pos = pos + bit


def _slot_key(group, rank, boff_ref, b):
    key = rank
    for g in range(MOE_GROUPS):
        key = key + jnp.where(group == g, boff_ref[MOE_GROUPS * b + g], 0)
    return key


def _start_all(copies):
    for bit, cs in copies:
        @pl.when(bit != 0)
        def _():
            for c in cs:
                c.start()


def _wait_all(copies):
    for bit, cs in copies:
        @pl.when(bit != 0)
        def _():
            for c in cs:
                c.wait()


def _dispatch_kernel(off_ref, cpad_ref, boff_ref, tail_ref, x_ref, gates_ref, inforow_ref,
                     xs_hbm, gs_hbm, xstg, gstg, zx, zg, sem):
    b = pl.program_id(0)
    last = pl.num_programs(0) - 1
    tb = x_ref.shape[0]
    ks = xstg.shape[1]
    cur = lax.rem(b, 2)
    key = _slot_key(inforow_ref[0:1, :], inforow_ref[1:2, :], boff_ref, b)
    slot = lax.broadcasted_iota(jnp.int32, (ks, tb), 0)
    perm = jnp.where(slot == key, 1.0, 0.0).astype(BF16)
    xstg[cur] = _dot(perm, x_ref[...]).astype(BF16)
    gt = gates_ref[...]
    hi = gt.astype(BF16)
    lo = (gt - hi.astype(F32)).astype(BF16)
    gstg[cur] = _dot(perm, hi) + _dot(perm, lo)

    sizes = _pow2_sizes(tb, SEG_ALIGN)

    def copies(blk, sl):
        for g in range(MOE_GROUPS):
            src0 = boff_ref[MOE_GROUPS * blk + g]
            dst0 = off_ref[MOE_GROUPS * blk + g]

            def make(pos, sz, src0=src0, dst0=dst0):
                src = pl.ds(pl.multiple_of(src0 + pos, SEG_ALIGN), sz)
                dst = pl.ds(pl.multiple_of(dst0 + pos, SEG_ALIGN), sz)
                return (pltpu.make_async_copy(xstg.at[sl, src, :], xs_hbm.at[dst, :], sem.at[sl, 0]),
                        pltpu.make_async_copy(gstg.at[sl, src, :], gs_hbm.at[dst, :], sem.at[sl, 1]))

            yield from _strip_copies(cpad_ref[MOE_GROUPS * blk + g], sizes, make)

    _start_all(copies(b, cur))

    @pl.when(b > 0)
    def _():
        _wait_all(copies(jnp.maximum(b - 1, 0), 1 - cur))

    @pl.when(b == last)
    def _():
        _wait_all(copies(b, cur))
        zx[...] = jnp.zeros(zx.shape, BF16)
        zg[...] = jnp.zeros(zg.shape, F32)
        tail_sizes = _pow2_sizes(MOE_TILE // 2, SEG_ALIGN)

        def tails():
            for g in range(MOE_GROUPS):
                dst0 = tail_ref[g]

                def make(pos, sz, dst0=dst0):
                    dst = pl.ds(pl.multiple_of(dst0 + pos, SEG_ALIGN), sz)
                    return (pltpu.make_async_copy(zx.at[pl.ds(0, sz), :], xs_hbm.at[dst, :], sem.at[0, 0]),
                            pltpu.make_async_copy(zg.at[pl.ds(0, sz), :], gs_hbm.at[dst, :], sem.at[0, 1]))

                yield from _strip_copies(tail_ref[MOE_GROUPS + g], tail_sizes, make)

        _start_all(tails())
        _wait_all(tails())

        def spare(i):
            dst = pl.ds(pl.multiple_of(i * MOE_TILE, MOE_TILE), MOE_TILE)
            return (pltpu.make_async_copy(zx, xs_hbm.at[dst, :], sem.at[0, 0]),
                    pltpu.make_async_copy(zg, gs_hbm.at[dst, :], sem.at[0, 1]))

        first_spare = tail_ref[2 * MOE_GROUPS]
        n_tiles = xs_hbm.shape[0] // MOE_TILE

        @pl.loop(first_spare, n_tiles)
        def _(i):
            cx, cg = spare(i)
            cx.start()
            cg.start()

        @pl.loop(first_spare, n_tiles)
        def _(i):
            cx, cg = spare(i)
            cx.wait()
            cg.wait()


def _dispatch(xn, gates, inforow, off, cpad, boff, tail, n_sorted, tb, ks):
    n, d = xn.shape
    nb = n // tb
    grid_spec = pltpu.PrefetchScalarGridSpec(
        num_scalar_prefetch=4,
        grid=(nb,),
        in_specs=[pl.BlockSpec((tb, d), lambda i, *_: (i, 0)),
                  pl.BlockSpec((tb, LANES), lambda i, *_: (i, 0)),
                  pl.BlockSpec((None, 8, tb), lambda i, *_: (i, 0, 0))],
        out_specs=[pl.BlockSpec(memory_space=pl.ANY), pl.BlockSpec(memory_space=pl.ANY)],
        scratch_shapes=[pltpu.VMEM((2, ks, d), BF16), pltpu.VMEM((2, ks, LANES), F32),
                        pltpu.VMEM((MOE_TILE, d), BF16), pltpu.VMEM((MOE_TILE, LANES), F32),
                        pltpu.SemaphoreType.DMA((2, 2))])
    return pl.pallas_call(
        _dispatch_kernel,
        grid_spec=grid_spec,
        out_shape=[jax.ShapeDtypeStruct((n_sorted, d), BF16),
                   jax.ShapeDtypeStruct((n_sorted, LANES), F32)],
        compiler_params=_params("arbitrary"),
        name="moe_dispatch",
    )(off, cpad, boff, tail, xn, gates, inforow)


def _group_mlp_kernel(tg_ref, nused_ref, x_ref, gs_ref, wg_ref, wu_ref, wd_ref, y_ref):
    i = pl.program_id(0)

    @pl.when(i < nused_ref[0])
    def _():
        x = x_ref[...]
        gs = gs_ref[...]
        lane = lax.broadcasted_iota(jnp.int32, gs.shape, 1)
        lane0 = tg_ref[i] * MOE_EXPERTS_PER_GROUP
        acc = jnp.zeros(y_ref.shape, F32)
        for e in range(MOE_EXPERTS_PER_GROUP):
            he = _silu(_dot(x, wg_ref[e])) * _dot(x, wu_ref[e])
            gate = jnp.sum(jnp.where(lane == lane0 + e, gs, 0.0), axis=-1, keepdims=True)
            acc = acc + _dot((he * gate).astype(BF16), wd_ref[e])
        y_ref[...] = acc.astype(y_ref.dtype)

    @pl.when(i >= nused_ref[0])
    def _():
        y_ref[...] = jnp.zeros(y_ref.shape, y_ref.dtype)


def _group_mlp(xs, gs, wg, wu, wd, tile_group, nused):
    n_sorted, d = xs.shape
    ne = MOE_EXPERTS_PER_GROUP

    def rows(i, tg, nu):
        return (jnp.minimum(i, nu[0] - 1), 0)

    def group(i, tg, nu):
        return (tg[i], 0, 0)

    grid_spec = pltpu.PrefetchScalarGridSpec(
        num_scalar_prefetch=2,
        grid=(n_sorted // MOE_TILE,),
        in_specs=[pl.BlockSpec((MOE_TILE, d), rows),
                  pl.BlockSpec((MOE_TILE, LANES), rows),
                  pl.BlockSpec((ne, d, MOE_D_FF), group),
                  pl.BlockSpec((ne, d, MOE_D_FF), group),
                  pl.BlockSpec((ne, MOE_D_FF, d), group)],
        out_specs=pl.BlockSpec((MOE_TILE, d), lambda i, tg, nu: (i, 0)))
    return pl.pallas_call(
        _group_mlp_kernel,
        grid_spec=grid_spec,
        out_shape=jax.ShapeDtypeStruct((n_sorted, d), BF16),
        compiler_params=_params("arbitrary"),
        name="moe_experts",
    )(tile_group, nused, xs, gs, wg, wu, wd)


def _combine_kernel(off_ref, cpad_ref, boff_ref, info_ref, res_ref, g_ref, *rest, project):
    if project:
        w_ref, y_hbm, o_ref, u_ref, ybuf, sem = rest
    else:
        y_hbm, o_ref, ybuf, sem = rest
    b = pl.program_id(0)
    last = pl.num_programs(0) - 1
    tb = res_ref.shape[0]
    ks = ybuf.shape[1]
    cur = lax.rem(b, 2)
    sizes = _pow2_sizes(tb, STRIP)

    def copies(blk, sl):
        for g in range(MOE_GROUPS):
            src0 = off_ref[MOE_GROUPS * blk + g]
            dst0 = boff_ref[MOE_GROUPS * blk + g]

            def make(pos, sz, src0=src0, dst0=dst0):
                src = pl.ds(pl.multiple_of(src0 + pos, SEG_ALIGN), sz)
                dst = pl.ds(pl.multiple_of(dst0 + pos, STRIP), sz)
                return (pltpu.make_async_copy(y_hbm.at[src, :], ybuf.at[sl, dst, :], sem.at[sl]),)

            n_rows = (cpad_ref[MOE_GROUPS * blk + g] + (STRIP - 1)) & (-STRIP)
            yield from _strip_copies(n_rows, sizes, make)

    @pl.when(b == 0)
    def _():
        ybuf[...] = jnp.zeros(ybuf.shape, ybuf.dtype)
        _start_all(copies(b, cur))

    @pl.when(b < last)
    def _():
        _start_all(copies(jnp.minimum(b + 1, last), 1 - cur))

    _wait_all(copies(b, cur))

    key = _slot_key(info_ref[:, 0:1], info_ref[:, 1:2], boff_ref, b)
    slot = lax.broadcasted_iota(jnp.int32, (tb, ks), 1)
    perm = jnp.where(slot == key, 1.0, 0.0).astype(BF16)
    y = res_ref[...] + _dot(perm, ybuf[cur])
    ms = jnp.mean(y * y, axis=-1, keepdims=True)
    yn = y * lax.rsqrt(ms + NORM_EPS) * g_ref[...]
    if project:
        o_ref[...] = y
        u_ref[...] = _dot(yn.astype(BF16), w_ref[...]).astype(u_ref.dtype)
    else:
        o_ref[...] = yn


def _combine(ys, info, res, norm_g, w_next, off, cpad, boff, tb, ks):
    n, d = res.shape
    project = w_next is not None
    row = lambda width: pl.BlockSpec((tb, width), lambda i, *_: (i, 0))
    in_specs = [row(LANES), row(d), pl.BlockSpec((1, d), lambda i, *_: (0, 0))]
    args = [info, res, norm_g.reshape(1, d)]
    out_specs, out_shape = [row(d)], [jax.ShapeDtypeStruct((n, d), F32)]
    if project:
        nout = w_next.shape[1]
        in_specs.append(pl.BlockSpec((d, nout), lambda i, *_: (0, 0)))
        args.append(w_next)
        out_specs.append(row(nout))
        out_shape.append(jax.ShapeDtypeStruct((n, nout), BF16))
    grid_spec = pltpu.PrefetchScalarGridSpec(
        num_scalar_prefetch=3,
        grid=(n // tb,),
        in_specs=in_specs + [pl.BlockSpec(memory_space=pl.ANY)],
        out_specs=out_specs,
        scratch_shapes=[pltpu.VMEM((2, ks, d), BF16), pltpu.SemaphoreType.DMA((2,))])
    out = pl.pallas_call(
        functools.partial(_combine_kernel, project=project),
        grid_spec=grid_spec,
        out_shape=out_shape,
        compiler_params=_params("arbitrary"),
        name="moe_combine",
    )(off, cpad, boff, *args, ys)
    return tuple(out) if project else out[0]


def _round_up(x, m):
    return (x + m - 1) // m * m


def _moe_layer(routed, expert_weights, norm_g, w_next, tb):
    h, xn, gates, info, inforow, counts = routed
    w_gate, w_up, w_down = expert_weights
    n, d = h.shape
    nb = n // tb

    cnt = counts[:, 0, :MOE_GROUPS]
    cpad = _round_up(cnt, SEG_ALIGN)
    strip = _round_up(cnt, STRIP)
    used = jnp.sum(cpad, axis=0)
    region = _round_up(used, MOE_TILE)
    region_start = jnp.cumsum(region) - region
    off = region_start[None, :] + jnp.cumsum(cpad, axis=0) - cpad
    boff = jnp.cumsum(strip, axis=1) - strip
    nused = (jnp.sum(region) // MOE_TILE).reshape(1)
    tail = jnp.concatenate([region_start + used, region - used, nused]).astype(jnp.int32)
    n_tiles = -(-(n + nb * MOE_GROUPS * (SEG_ALIGN - 1) + MOE_GROUPS * (MOE_TILE - 1)) // MOE_TILE) + 1
    tile_start = jnp.arange(n_tiles, dtype=jnp.int32) * MOE_TILE
    region_end = region_start + region
    tile_group = jnp.minimum(jnp.sum(tile_start[:, None] >= region_end[None, :], axis=1), MOE_GROUPS - 1)
    ks =tb + MOE_GROUPS * STRIP

    flat = lambda a: a.reshape(-1).astype(jnp.int32)
    xs, gs = _dispatch(xn, gates, inforow, flat(off), flat(cpad), flat(boff), tail,
                       n_tiles * MOE_TILE, tb, ks)
    ys = _group_mlp(xs, gs, w_gate, w_up, w_down,
                    tile_group.astype(jnp.int32), nused.astype(jnp.int32))
    return _combine(ys, info, h, norm_g, w_next, flat(off), flat(cpad), flat(boff), tb, ks)


ATT_Q_TILE = 256
ATT_TILES_PER_ITER = 2
ATT_SCORES_AHEAD = 2
ATT_SCORE_SLOTS = 4
ATT_VT_ROWS = ATT_HEAD_DIM + 16
ATT_Q_SCALE = ATT_HEAD_DIM ** -0.5 * 1.4426950408889634


def _eye(n):
    return jnp.where(lax.broadcasted_iota(jnp.int32, (n, n), 0) == lax.broadcasted_iota(jnp.int32, (n, n), 1),
                     1.0, 0.0).astype(BF16)


def _transpose_bf16(x, eye):
    return lax.dot_general(eye, x, _NT, preferred_element_type=F32).astype(BF16)


def _rope_kernel(q_ref, k_ref, v_ref, aq_ref, bq_ref, ak_ref, bk_ref, qt_ref, kp_ref, vt_ref):
    t = q_ref.shape[0]
    tq = ATT_Q_TILE
    dh = ATT_HEAD_DIM
    gi = lax.broadcasted_iota(jnp.int32, (LANES, LANES), 0) >> 6
    gj = lax.broadcasted_iota(jnp.int32, (LANES, LANES), 1) >> 6
    gmat = jnp.where(gi == gj, 1.0, 0.0).astype(BF16)
    eye_q = _eye(LANES)

    def norm_rope(x, a, b):
        lane = lax.broadcasted_iota(jnp.int32, x.shape, 1)
        sq = x * x
        hi = sq.astype(BF16)
        lo = (sq - hi.astype(F32)).astype(BF16)
        ms = (_dot(hi, gmat) + _dot(lo, gmat)) * (1.0 / dh)
        partner = jnp.where((lane & 1) == 0, pltpu.roll(x, LANES - 1, 1), pltpu.roll(x, 1, 1))
        return lax.rsqrt(ms + NORM_EPS) * (x * a + partner * b)

    for j in range(q_ref.shape[1] // LANES):
        cols = slice(j * LANES, (j + 1) * LANES)
        y = norm_rope(q_ref[N_META:, cols].astype(F32), aq_ref[N_META:, :], bq_ref[N_META:, :])
        yt = _transpose_bf16(y.astype(BF16), eye_q)
        for i in range((t - N_META) // tq):
            qt_ref[i, cols, :] = yt[:, i * tq:(i + 1) * tq]

    low = lax.broadcasted_iota(jnp.int32, (t, LANES), 1) < dh
    for j in range(k_ref.shape[1] // LANES):
        cols = slice(j * LANES, (j + 1) * LANES)
        y = norm_rope(k_ref[:, cols].astype(F32), ak_ref[...], bk_ref[...])
        kp_ref[2 * j] = jnp.where(low, y, 0.0).astype(BF16)
        kp_ref[2 * j + 1] = jnp.where(low, pltpu.roll(y, dh, 1), 0.0).astype(BF16)

    n_real = t - N_META
    eye_v = _eye(v_ref.shape[1])
    vt_real = _transpose_bf16(v_ref[N_META:, :], eye_v)
    vt_meta = _transpose_bf16(v_ref[0:N_META, :], eye_v)
    for h in range(ATT_KV_HEADS):
        vt_ref[h, 0:dh, 0:n_real] = vt_real[h * dh:(h + 1) * dh, :]
        vt_ref[h, 0:dh, n_real:t] = vt_meta[h * dh:(h + 1) * dh, :]
        extra = lax.broadcasted_iota(jnp.int32, (ATT_VT_ROWS - dh, t), 0)
        vt_ref[h, dh:ATT_VT_ROWS, :] = jnp.where(extra == 0, 1.0, 0.0).astype(BF16)


def _rope(u3, tables):
    b, t, _ = u3.shape
    nq = ATT_HEADS * ATT_HEAD_DIM
    nkv = ATT_KV_HEADS * ATT_HEAD_DIM
    n_qt = (t - N_META) // ATT_Q_TILE
    full = lambda shape: pl.BlockSpec(shape, lambda i: (0, 0))
    return pl.pallas_call(
        _rope_kernel,
        grid=(b,),
        in_specs=[pl.BlockSpec((None, t, nq), lambda i: (i, 0, 0)),
                  pl.BlockSpec((None, t, nkv), lambda i: (i, 0, nq // nkv)),
                  pl.BlockSpec((None, t, nkv), lambda i: (i, 0, nq // nkv + 1)),
                  full((t, LANES)), full((t, LANES)), full((t, LANES)), full((t, LANES))],
        out_specs=[pl.BlockSpec((None, n_qt, nq, ATT_Q_TILE), lambda i: (i, 0, 0, 0)),
                   pl.BlockSpec((None, ATT_KV_HEADS, t, LANES), lambda i: (i, 0, 0, 0)),
                   pl.BlockSpec((None, ATT_KV_HEADS, ATT_VT_ROWS, t), lambda i: (i, 0, 0, 0))],
        out_shape=[jax.ShapeDtypeStruct((b, n_qt, nq, ATT_Q_TILE), BF16),
                   jax.ShapeDtypeStruct((b, ATT_KV_HEADS, t, LANES), BF16),
                   jax.ShapeDtypeStruct((b, ATT_KV_HEADS, ATT_VT_ROWS, t), BF16)],
        compiler_params=_params("parallel"),
        name="qk_norm_rope",
    )(u3, u3, u3, *tables)


def _attn_kernel(qt_ref, k_ref, vt_ref, wg_ref, wu_ref, wd_ref, o_ref, wg_o, wu_o, wd_o, s_ref):
    _cast_through((wg_ref, wu_ref, wd_ref), (wg_o, wu_o, wd_o))
    n_qt, w, tq = qt_ref.shape
    dh = ATT_HEAD_DIM
    n_real = k_ref.shape[0] - N_META
    eye = _eye(tq)
    zpad = jnp.zeros((LANES - dh, tq), BF16)

    def scores(i, g, slot):
        qp = jnp.concatenate([qt_ref[i, g * dh:(g + 1) * dh, :], zpad], axis=0)
        s_r = _dot(k_ref[N_META:, :], qp)
        s_m = _dot(k_ref[0:N_META, :], qp)
        s_ref[slot, 0:n_real, :] = s_r
        s_ref[slot, n_real:, :] = s_m
        return jnp.maximum(jnp.max(s_r, axis=0, keepdims=True), jnp.max(s_m, axis=0, keepdims=True))

    steps = [(u, g) for u in range(ATT_TILES_PER_ITER) for g in range(ATT_GROUP)]
    n_steps = len(steps)
    n_iter = n_qt // ATT_TILES_PER_ITER
    ahead = ATT_SCORES_AHEAD
    n_slots = s_ref.shape[0]
    assert n_steps % n_slots == 0 and ahead < n_slots

    def issue_scores(it, n):
        it2 = jnp.minimum(it + n // n_steps, n_iter - 1)
        u, g = steps[n % n_steps]
        return scores(it2 * ATT_TILES_PER_ITER + u, g, n % n_slots)

    def q_tiles(it, pending):
        i0 = it * ATT_TILES_PER_ITER
        outs = []
        pending = list(pending)
        for n, (u, g) in enumerate(steps):
            m, slot = pending.pop(0), n % n_slots
            pending.append(issue_scores(it, n + ahead))
            p_r = jnp.exp2(s_ref[slot, 0:n_real, :] - m).astype(BF16)
            p_m = jnp.exp2(s_ref[slot, n_real:, :] - m).astype(BF16)
            ot = _dot(vt_ref[:, 0:n_real], p_r) + _dot(vt_ref[:, n_real:], p_m)
            outs.append((ot[0:dh] * (1.0 / ot[dh:dh + 1])).astype(BF16))
            if g == ATT_GROUP - 1:
                rows = pl.ds(pl.multiple_of((i0 + u) * tq, tq), tq)
                ot_all = jnp.concatenate(outs, axis=0)
                o_ref[rows, :] = lax.dot_general(eye, ot_all, _NT, preferred_element_type=F32).astype(o_ref.dtype)
                outs = []
        return tuple(pending)

    first = tuple(issue_scores(0, n) for n in range(ahead))
    lax.fori_loop(0, n_iter, q_tiles, first)


def _attention(qt, kp, vt, expert_weights, layer):
    b, n_qt, nq, tq = qt.shape
    t = kp.shape[2]
    w = ATT_GROUP * ATT_HEAD_DIM
    grid = (b, ATT_KV_HEADS)
    c_in, c_out, c_shape = _expert_cast_specs(expert_weights, layer, grid)
    o, *cast = pl.pallas_call(
        _attn_kernel,
        grid=grid,
        in_specs=[pl.BlockSpec((None, n_qt, w, tq), lambda i, h: (i, 0, h, 0)),
                  pl.BlockSpec((None, None, t, LANES), lambda i, h: (i, h, 0, 0)),
                  pl.BlockSpec((None, None, ATT_VT_ROWS, t), lambda i, h: (i, h, 0, 0))] + c_in,
        out_specs=[pl.BlockSpec((None, n_qt * tq, w), lambda i, h: (i, 0, h))] + c_out,
        out_shape=[jax.ShapeDtypeStruct((b, n_qt * tq, nq), BF16)] + c_shape,
        scratch_shapes=[pltpu.VMEM((ATT_SCORE_SLOTS, t, tq), F32)],
        compiler_params=_params("parallel", "parallel"),
        name="gqa_attention",
    )(qt, kp, vt, *expert_weights)
    return o, cast


def _rope_tables(n_tokens, q_gain, k_gain):
    rows_n = n_tokens // GRID_W
    rows = jnp.repeat(jnp.arange(rows_n), GRID_W).astype(F32)
    cols = jnp.tile(jnp.arange(GRID_W), rows_n).astype(F32)
    axis_dims = ATT_HEAD_DIM // 2
    freqs = ROPE_THETA ** (-jnp.arange(0, axis_dims, 2, dtype=F32) / axis_dims)
    ang = jnp.concatenate([rows[:, None] * freqs, cols[:, None] * freqs], axis=-1)
    ang = jnp.concatenate([jnp.zeros((N_META, ang.shape[1]), F32), ang], axis=0)
    cos_e = jnp.tile(jnp.repeat(jnp.cos(ang), 2, axis=1), (1, LANES // ATT_HEAD_DIM))
    sign = jnp.tile(jnp.array([-1.0, 1.0], F32), LANES // 2)
    sin_e = jnp.tile(jnp.repeat(jnp.sin(ang), 2, axis=1), (1, LANES // ATT_HEAD_DIM)) * sign

    def tables(gain, scale):
        g = jnp.tile(gain.astype(F32), LANES // ATT_HEAD_DIM)
        g_partner = g.reshape(-1, 2)[:, ::-1].reshape(-1)
        return cos_e * (g * scale), sin_e * (g_partner * scale)

    return tables(q_gain, ATT_Q_SCALE) + tables(k_gain, 1.0)


def kernel(x, meta_tokens, norm_mix, norm_ffn, norm_final, sc_w_in, sc_conv_w, gla_w_gate_fwd, gla_b_gate_fwd, gla_w_gate_bwd, gla_b_gate_bwd, gla_norm_w, sc_w_out, att_w_in, att_q_norm, att_k_norm, att_w_out, moe_w_group, moe_b_group, moe_w_expert, moe_b_expert, moe_w_gate, moe_w_up, moe_w_down):
    bsz, s_len, d = x.shape
    t = s_len + N_META
    n0, n1 = bsz * t, bsz * s_len
    tm0 = t // 3
    tb0 = 3 * LANES
    tm1 = 4 * LANES
    assert t % tm0 == 0 and tm0 % 16 == 0 and n0 % tb0 == 0 and n1 % tm1 == 0 and s_len % tm1 == 0

    expert_f32 = (moe_w_gate, moe_w_up, moe_w_down)
    meta = jnp.broadcast_to(meta_tokens.astype(x.dtype)[None], (bsz, N_META, d))
    h = jnp.concatenate([meta, x], axis=1).reshape(n0, d)

    nk = GLA_HEADS * GLA_DK
    mix_in = sc_w_in.shape[2]
    mix_pad = -mix_in % MXU_WIDTH
    w_in = jnp.pad(sc_w_in[0], ((0, 0), (0, mix_pad))).astype(BF16)
    u = _norm_matmul(h, norm_mix[0], w_in, tm0, mix_in + mix_pad)
    u3 = u.reshape(bsz, t, mix_in + mix_pad)
    y_a, experts0 = _gated_conv(u3, sc_conv_w[0], expert_f32, 0)
    wc = jnp.zeros((LANES, 2 * nk), F32)
    wc = wc.at[:GLA_GATE_RANK, :nk].set(gla_w_gate_fwd[0])
    wc = wc.at[GLA_GATE_RANK:2 * GLA_GATE_RANK, nk:].set(gla_w_gate_bwd[0])
    bc = jnp.concatenate([gla_b_gate_fwd[0], gla_b_gate_bwd[0]]).reshape(1, 2 * nk)
    y_b = _gla(u3, wc.astype(BF16), bc, gla_norm_w[0].reshape(1, GLA_DV))
    w_out = sc_w_out[0].astype(BF16)
    router0 = _router_operands(norm_ffn[0], moe_w_group[0], moe_b_group[0], moe_w_expert[0], moe_b_expert[0])
    routed = _proj2_router(y_a.reshape(n0, SC_WIDTH), y_b.reshape(n0, -1), w_out[:SC_WIDTH], w_out[SC_WIDTH:],
                           h, router0, tb0)
    h, u = _moe_layer(routed, experts0, norm_mix[1], att_w_in[0].astype(BF16), tb0)
    u3 = u.reshape(bsz, t, -1)
    qt, kp, vt = _rope(u3, _rope_tables(s_len, att_q_norm[0], att_k_norm[0]))
    o, experts1 = _attention(qt, kp, vt, expert_f32, 1)
    router1 = _router_operands(norm_ffn[1], moe_w_group[1], moe_b_group[1], moe_w_expert[1], moe_b_expert[1])
    routed = _att_out_router(o.reshape(n1, d), att_w_out[0].astype(BF16), h.reshape(bsz, t, d), router1, tm1)
    out = _moe_layer(routed, experts1, norm_final, None, tm1)
    return out.reshape(bsz, s_len, d)
```

```python
import functools

import jax
import jax.numpy as jnp
from jax import lax
from jax.experimental import pallas as pl
from jax.experimental.pallas import tpu as pltpu

F32 = jnp.float32
BF16 = jnp.bfloat16

NORM_EPS = 1e-6
N_META = 16
GRID_W = 64
ROPE_THETA = 10000.0
SC_WIDTH = 512
GLA_HEADS = 4
GLA_DK = 64
GLA_DV = 128
GLA_GATE_RANK = 16
GLA_GATE_TAU = 16.0
GLA_CHUNK = 64
ATT_HEAD_DIM = 64
ATT_HEADS = 16
ATT_KV_HEADS = 4
ATT_GROUP = ATT_HEADS // ATT_KV_HEADS
MOE_GROUPS = 4
MOE_EXPERTS_PER_GROUP = 8
N_EXPERTS = 32
MOE_D_FF = 256

LANES = 128
MXU_WIDTH = 256
MIB = 2 ** 20
VMEM_LIMIT = 56 * MIB

_NT = (((1,), (1,)), ((), ()))
_TN = (((0,), (0,)), ((), ()))


def _params(*semantics):
    return pltpu.CompilerParams(dimension_semantics=semantics, vmem_limit_bytes=VMEM_LIMIT)


def _dot(a, b):
    return jnp.dot(a, b, preferred_element_type=F32)


def _silu(x):
    return x * (1.0 / (1.0 + jnp.exp(-x)))


def _norm_matmul_kernel(h_ref, g_ref, w_ref, o_ref, xn_ref):
    @pl.when(pl.program_id(1) == 0)
    def _():
        x = h_ref[...]
        ms = jnp.mean(x * x, axis=-1, keepdims=True)
        xn_ref[...] = (x * lax.rsqrt(ms + NORM_EPS) * g_ref[...]).astype(BF16)

    o_ref[...] = _dot(xn_ref[...], w_ref[...]).astype(o_ref.dtype)


def _norm_matmul(h, g, w, tm, tn):
    n, d = h.shape
    nout = w.shape[1]
    return pl.pallas_call(
        _norm_matmul_kernel,
        grid=(n // tm, nout // tn),
        in_specs=[pl.BlockSpec((tm, d), lambda i, j: (i, 0)),
                  pl.BlockSpec((1, d), lambda i, j: (0, 0)),
                  pl.BlockSpec((d, tn), lambda i, j: (0, j))],
        out_specs=pl.BlockSpec((tm, tn), lambda i, j: (i, j)),
        out_shape=jax.ShapeDtypeStruct((n, nout), BF16),
        scratch_shapes=[pltpu.VMEM((tm, d), BF16)],
        compiler_params=_params("parallel", "arbitrary"),
        name="norm_matmul",
    )(h, g.reshape(1, d), w)


def _expert_cast_specs(weights, layer, grid_shape):
    n_steps = 1
    for g in grid_shape:
        n_steps *= g
    assert N_EXPERTS % n_steps == 0
    per_step = N_EXPERTS // n_steps

    def step(*idx):
        flat = 0
        for i, g in zip(idx, grid_shape):
            flat = flat * g + i
        return flat

    in_specs = [pl.BlockSpec((None, per_step) + w.shape[2:], lambda *idx: (layer, step(*idx), 0, 0))
                for w in weights]
    out_specs = [pl.BlockSpec((per_step,) + w.shape[2:], lambda *idx: (step(*idx), 0, 0)) for w in weights]
    out_shape = [jax.ShapeDtypeStruct(w.shape[1:], BF16) for w in weights]
    return in_specs, out_specs, out_shape


def _cast_through(in_refs, out_refs):
    for src, dst in zip(in_refs, out_refs):
        dst[...] = src[...].astype(dst.dtype)


def _conv_kernel(ab_ref, ac_ref, ax_ref, w_ref, wg_ref, wu_ref, wd_ref, o_ref, wg_o, wu_o, wd_o, z_ref):
    _cast_through((wg_ref, wu_ref, wd_ref), (wg_o, wu_o, wd_o))
    t, c = ab_ref.shape
    z = ac_ref[...].astype(F32) * ax_ref[...].astype(F32)
    z_ref[0:8, :] = jnp.zeros((8, c), F32)
    z_ref[8:8 + t, :] = z
    z_ref[8 + t:16 + t, :] = jnp.zeros((8, c), F32)
    w = w_ref[...]
    conv = w[0:1, :] * z_ref[7:7 + t, :] + w[1:2, :] * z + w[2:3, :] * z_ref[9:9 + t, :]
    o_ref[...] = (ab_ref[...].astype(F32) * conv).astype(o_ref.dtype)


def _gated_conv(u3, conv_w, expert_weights, layer):
    b, t, _ = u3.shape
    nblk = SC_WIDTH // LANES
    grid = (b, nblk)
    c_in, c_out, c_shape = _expert_cast_specs(expert_weights, layer, grid)

    def col(off):
        return pl.BlockSpec((None, t, LANES), lambda i, j: (i, 0, off + j))

    y_a, *cast = pl.pallas_call(
        _conv_kernel,
        grid=grid,
        in_specs=[col(0), col(nblk), col(2 * nblk),
                  pl.BlockSpec((3, LANES), lambda i, j: (0, j))] + c_in,
        out_specs=[pl.BlockSpec((None, t, LANES), lambda i, j: (i, 0, j))] + c_out,
        out_shape=[jax.ShapeDtypeStruct((b, t, SC_WIDTH), BF16)] + c_shape,
        scratch_shapes=[pltpu.VMEM((t + 16, LANES), F32)],
        compiler_params=_params("parallel", "parallel"),
        name="gated_conv",
    )(u3, u3, u3, conv_w, *expert_weights)
    return y_a, cast


GLA_ROW_TILE = 688
GLA_EPI_TILE = 688


def _gla_kernel(q_ref, k_ref, v_ref, r_ref, g_ref, wc_ref, bc_ref, nw_ref, y_ref,
                la_ref, of_ref, ob_ref, sf_ref, sb_ref):
    t, nk = q_ref.shape
    nv = v_ref.shape[1]
    c = GLA_CHUNK
    n_real = (t - N_META) // c

    for i in range(t // GLA_ROW_TILE):
        rows = pl.ds(i * GLA_ROW_TILE, GLA_ROW_TILE)
        pre = _dot(g_ref[rows, :], wc_ref[...]) + bc_ref[...]
        la_ref[rows, :] = (jnp.minimum(pre, 0.0) - jnp.log(1.0 + jnp.exp(-jnp.abs(pre)))) * (1.0 / GLA_GATE_TAU)

    def iota(shape, dim):
        return lax.broadcasted_iota(jnp.int32, shape, dim)

    tri_f = jnp.where(iota((c, c), 0) >= iota((c, c), 1), 1.0, 0.0).astype(BF16)
    tri_b = jnp.where(iota((c, c), 0) <= iota((c, c), 1), 1.0, 0.0).astype(BF16)
    keep_f = (iota((c, nk), 1) & (c - 1)) <= iota((c, nk), 0)
    keep_b = (iota((c, nk), 1) & (c - 1)) > iota((c, nk), 0)
    bd_k = (iota((nk, nk), 0) >> 6) == (iota((nk, nk), 1) >> 6)
    bd_v = (iota((nk, nv), 0) >> 6) == (iota((nk, nv), 1) >> 7)
    bd_s = (iota((nv, nk), 0) >> 7) == (iota((nv, nk), 1) >> 6)
    valid0 = iota((c, 1), 0) < N_META

    def cum_decay(rows, valid, tri, la_col):
        la = la_ref[rows, la_col:la_col + nk]
        if valid is not None:
            la = jnp.where(valid, la, 0.0)
        hi = la.astype(BF16)
        rem = la - hi.astype(F32)
        mid = rem.astype(BF16)
        lo = (rem - mid.astype(F32)).astype(BF16)
        return _dot(tri, hi) + _dot(tri, mid) + _dot(tri, lo)

    def chunk(rows, valid, cum, keep, tot_row, s_ref, o_ref, out_rows, n_out):
        q = q_ref[rows, :].astype(F32) * (GLA_DK ** -0.5)
        k = k_ref[rows, :].astype(F32)
        v = v_ref[rows, :].astype(F32)
        if valid is not None:
            q = jnp.where(valid, q, 0.0)
            k = jnp.where(valid, k, 0.0)
            v = jnp.where(valid, v, 0.0)
        tot = cum[tot_row:tot_row + 1, :]
        q_dec = (q * jnp.exp(cum)).astype(BF16)
        k_inv = k * jnp.exp(-cum)
        k_end = (k * jnp.exp(tot - cum)).astype(BF16)
        kbd = jnp.where(bd_k, jnp.concatenate([k_inv] * GLA_HEADS, axis=0), 0.0).astype(BF16)
        p = lax.dot_general(q_dec, kbd, _NT, preferred_element_type=F32)
        s_old = s_ref[...]
        o_inter = lax.dot_general(q_dec, s_old.astype(BF16), _NT, preferred_element_type=F32)
        kvt = lax.dot_general(v.astype(BF16), k_end, _TN, preferred_element_type=F32)
        yield
        p = jnp.where(keep, p, 0.0).astype(BF16)
        vbd = jnp.where(bd_v, jnp.concatenate([v] * GLA_HEADS, axis=0), 0.0).astype(BF16)
        o = _dot(p, vbd) + o_inter
        s_ref[...] = jnp.exp(tot) * s_old + jnp.where(bd_s, kvt, 0.0)
        o_ref[out_rows, :] = o[:n_out]

    def lockstep(*gens):
        live = list(gens)
        while live:
            nxt = []
            for gen in live:
                try:
                    next(gen)
                    nxt.append(gen)
                except StopIteration:
                    pass
            live = nxt

    sf_ref[...] = jnp.zeros(sf_ref.shape, F32)
    sb_ref[...] = jnp.zeros(sb_ref.shape, F32)
    meta_rows = pl.ds(0, c)
    lockstep(chunk(meta_rows, valid0, cum_decay(meta_rows, valid0, tri_f, 0), keep_f, c - 1,
                   sf_ref, of_ref, pl.ds(0, N_META), N_META))

    def rows_fwd(i):
        return pl.ds(pl.multiple_of(N_META + i * c, 16), c)

    def rows_bwd(i):
        return pl.ds(pl.multiple_of(N_META + (n_real - 1 - i) * c, 16), c)

    def body(i, cums):
        cum_f, cum_b = cums
        nxt = jnp.minimum(i + 1, n_real - 1)
        cums = (cum_decay(rows_fwd(nxt), None, tri_f, 0), cum_decay(rows_bwd(nxt), None, tri_b, nk))
        lockstep(chunk(rows_fwd(i), None, cum_f, keep_f, c - 1, sf_ref, of_ref, rows_fwd(i), c),
                 chunk(rows_bwd(i), None, cum_b, keep_b, 0, sb_ref, ob_ref, rows_bwd(i), c))
        return cums

    first = (cum_decay(rows_fwd(0), None, tri_f, 0), cum_decay(rows_bwd(0), None, tri_b, nk))
    lax.fori_loop(0, n_real, body, first, unroll=4)
    lockstep(chunk(meta_rows, valid0, cum_decay(meta_rows, valid0, tri_b, nk), keep_b, 0,
                   sb_ref, ob_ref, pl.ds(0, N_META), N_META))

    def epilogue(i, carry):
        rows = pl.ds(pl.multiple_of(i * GLA_EPI_TILE, 16), GLA_EPI_TILE)
        o = of_ref[rows, :] + ob_ref[rows, :]
        r = r_ref[rows, :].astype(F32)
        for h in range(GLA_HEADS):
            cols = slice(h * GLA_DV, (h + 1) * GLA_DV)
            oh = o[:, cols]
            ms = jnp.mean(oh * oh, axis=-1, keepdims=True)
            yh = oh * lax.rsqrt(ms + NORM_EPS) * nw_ref[...] * _silu(r[:, cols])
            y_ref[rows, cols] = yh.astype(y_ref.dtype)
        return carry

    lax.fori_loop(0, t // GLA_EPI_TILE, epilogue, 0)


def _gla(u3, wc, bc, norm_w):
    b, t, _ = u3.shape
    nk = GLA_HEADS * GLA_DK
    nv = GLA_HEADS * GLA_DV
    q_blk = 3 * SC_WIDTH // nk
    v_blk = (3 * SC_WIDTH + 2 * nk) // nv
    g_blk = (3 * SC_WIDTH + 2 * nk + 2 * nv) // LANES
    full = lambda shape: pl.BlockSpec(shape, lambda i: (0, 0))
    return pl.pallas_call(
        _gla_kernel,
        grid=(b,),
        in_specs=[pl.BlockSpec((None, t, nk), lambda i: (i, 0, q_blk)),
                  pl.BlockSpec((None, t, nk), lambda i: (i, 0, q_blk + 1)),
                  pl.BlockSpec((None, t, nv), lambda i: (i, 0, v_blk)),
                  pl.BlockSpec((None, t, nv), lambda i: (i, 0, v_blk + 1)),
                  pl.BlockSpec((None, t, LANES), lambda i: (i, 0, g_blk)),
                  full((LANES, 2 * nk)), full((1, 2 * nk)), full((1, GLA_DV))],
        out_specs=pl.BlockSpec((None, t, nv), lambda i: (i, 0, 0)),
        out_shape=jax.ShapeDtypeStruct((b, t, nv), BF16),
        scratch_shapes=[pltpu.VMEM((t, 2 * nk), F32),
                        pltpu.VMEM((t, nv), F32), pltpu.VMEM((t, nv), F32),
                        pltpu.VMEM((nv, nk), F32), pltpu.VMEM((nv, nk), F32)],
        compiler_params=_params("parallel"),
        name="gla",
    )(u3, u3, u3, u3, u3, wc, bc, norm_w)


GROUP_LANE0 = N_EXPERTS


def _route(x, g_ref, whi_ref, wlo_ref, b_ref, xn_ref, gates_ref, info_ref, inforow_ref, count_ref):
    ms = jnp.mean(x * x, axis=-1, keepdims=True)
    xn = x * lax.rsqrt(ms + NORM_EPS) * g_ref[...]
    xhi = xn.astype(BF16)
    xn_ref[...] = xhi
    xlo = (xn - xhi.astype(F32)).astype(BF16)
    logits = _dot(xhi, whi_ref[...]) + _dot(xhi, wlo_ref[...]) + _dot(xlo, whi_ref[...]) + b_ref[...]

    lane = lax.broadcasted_iota(jnp.int32, logits.shape, 1)
    neg = -jnp.inf
    big = jnp.int32(LANES)
    is_group = (lane >> 2) == (GROUP_LANE0 >> 2)
    gl = jnp.where(is_group, logits, neg)
    gmax = jnp.max(gl, axis=-1, keepdims=True)
    g_p = 1.0 / jnp.sum(jnp.exp(gl - gmax), axis=-1, keepdims=True)
    g_idx = jnp.min(jnp.where(gl == gmax, lane, big), axis=-1, keepdims=True) - GROUP_LANE0
    el = jnp.where((lane >> 3) == g_idx, logits, neg)
    m1 = jnp.max(el, axis=-1, keepdims=True)
    i1 = jnp.min(jnp.where(el == m1, lane, big), axis=-1, keepdims=True)
    el2 = jnp.where(lane == i1, neg, el)
    m2 = jnp.max(el2, axis=-1, keepdims=True)
    i2 = jnp.min(jnp.where(el2 == m2, lane, big), axis=-1, keepdims=True)
    e2 = jnp.exp(m2 - m1)
    w1 = g_p / (1.0 + e2)
    w2 = g_p * e2 / (1.0 + e2)
    gates_ref[...] = jnp.where(lane == i1, w1, 0.0) + jnp.where(lane == i2, w2, 0.0)

    tm = x.shape[0]
    onehot = jnp.where(lane == g_idx, 1.0, 0.0)
    r_i = lax.broadcasted_iota(jnp.int32, (tm, tm), 0)
    c_i = lax.broadcasted_iota(jnp.int32, (tm, tm), 1)
    before = jnp.where(r_i > c_i, 1.0, 0.0).astype(BF16)
    prefix = _dot(before, onehot.astype(BF16))
    rank = jnp.sum(jnp.where(lane == g_idx, prefix, 0.0), axis=-1, keepdims=True).astype(jnp.int32)
    info_ref[...] = jnp.where(lane == 0, g_idx, jnp.where(lane == 1, rank, 0))
    eye = r_i == c_i
    g_row = jnp.sum(jnp.where(eye, g_idx, 0), axis=0, keepdims=True)
    rank_row = jnp.sum(jnp.where(eye, rank, 0), axis=0, keepdims=True)
    sub = lax.broadcasted_iota(jnp.int32, inforow_ref.shape, 0)
    inforow_ref[...] = jnp.where(sub == 0, g_row, jnp.where(sub == 1, rank_row, 0))
    counts = jnp.sum(onehot, axis=0, keepdims=True).astype(jnp.int32)
    count_ref[...] = jnp.broadcast_to(counts, count_ref.shape)


def _router_operands(norm_g, w_group, b_group, w_expert, b_expert):
    d = w_group.shape[0]
    pad = LANES - N_EXPERTS - MOE_GROUPS
    wr = jnp.concatenate([w_expert, w_group, jnp.zeros((d, pad), F32)], axis=1)
    whi = wr.astype(BF16)
    wlo = (wr - whi.astype(F32)).astype(BF16)
    bias = jnp.concatenate([b_expert, b_group, jnp.zeros((pad,), F32)]).reshape(1, LANES)
    return norm_g.reshape(1, d), whi, wlo, bias


def _router_specs(n, d, tm):
    nb = n // tm
    const = lambda shape: pl.BlockSpec(shape, lambda i: (0,) * len(shape))
    in_specs = [const((1, d)), const((d, LANES)), const((d, LANES)), const((1, LANES))]
    out_specs = [pl.BlockSpec((tm, d), lambda i: (i, 0)),
                 pl.BlockSpec((tm, d), lambda i: (i, 0)),
                 pl.BlockSpec((tm, LANES), lambda i: (i, 0)),
                 pl.BlockSpec((tm, LANES), lambda i: (i, 0)),
                 pl.BlockSpec((None, 8, tm), lambda i: (i, 0, 0)),
                 pl.BlockSpec((None, 8, LANES), lambda i: (i, 0, 0))]
    out_shape = [jax.ShapeDtypeStruct((n, d), F32),
                 jax.ShapeDtypeStruct((n, d), BF16),
                 jax.ShapeDtypeStruct((n, LANES), F32),
                 jax.ShapeDtypeStruct((n, LANES), jnp.int32),
                 jax.ShapeDtypeStruct((nb, 8, tm), jnp.int32),
                 jax.ShapeDtypeStruct((nb, 8, LANES), jnp.int32)]
    return in_specs, out_specs, out_shape


def _proj2_router_kernel(a_ref, b_ref, wa_ref, wb_ref, res_ref, g_ref, whi_ref, wlo_ref, bias_ref,
                         h_ref, *route_out):
    x = res_ref[...] + _dot(a_ref[...], wa_ref[...]) + _dot(b_ref[...], wb_ref[...])
    h_ref[...] = x
    _route(x, g_ref, whi_ref, wlo_ref, bias_ref, *route_out)


def _proj2_router(a, b, wa, wb, res, router_ops, tm):
    n, d = res.shape
    ka, kb = a.shape[1], b.shape[1]
    r_in, out_specs, out_shape = _router_specs(n, d, tm)
    return pl.pallas_call(
        _proj2_router_kernel,
        grid=(n // tm,),
        in_specs=[pl.BlockSpec((tm, ka), lambda i: (i, 0)),
                  pl.BlockSpec((tm, kb), lambda i: (i, 0)),
                  pl.BlockSpec((ka, d), lambda i: (0, 0)),
                  pl.BlockSpec((kb, d), lambda i: (0, 0)),
                  pl.BlockSpec((tm, d), lambda i: (i, 0))] + r_in,
        out_specs=out_specs,
        out_shape=out_shape,
        compiler_params=_params("parallel"),
        name="proj2_router",
    )(a, b, wa, wb, res, *router_ops)


def _att_out_router_kernel(o_ref, w_ref, hres_ref, g_ref, whi_ref, wlo_ref, bias_ref, h_ref, *route_out):
    tm = o_ref.shape[0]
    tiles = (hres_ref.shape[0] - N_META) // tm
    r0 = pl.multiple_of(N_META + lax.rem(pl.program_id(0), tiles) * tm, 16)
    x = hres_ref[pl.ds(r0, tm), :] + _dot(o_ref[...], w_ref[...])
    h_ref[...] = x
    _route(x, g_ref, whi_ref, wlo_ref, bias_ref, *route_out)


def _att_out_router(o, w, h3, router_ops, tm):
    b, t, d = h3.shape
    n, nq = o.shape
    tiles = (t - N_META) // tm
    r_in, out_specs, out_shape = _router_specs(n, d, tm)
    return pl.pallas_call(
        _att_out_router_kernel,
        grid=(n // tm,),
        in_specs=[pl.BlockSpec((tm, nq), lambda i: (i, 0)),
                  pl.BlockSpec((nq, d), lambda i: (0, 0)),
                  pl.BlockSpec((None, t, d), lambda i: (i // tiles, 0, 0))] + r_in,
        out_specs=out_specs,
        out_shape=out_shape,
        compiler_params=_params("parallel"),
        name="att_out_router",
    )(o, w, h3, *router_ops)


MOE_TILE = 512
SEG_ALIGN = 16
STRIP = 32


def _pow2_sizes(max_rows, min_rows):
    sizes, s = [], min_rows
    while s <= max_rows:
        sizes.append(s)
        s *= 2
    return sizes[::-1]


def _strip_copies(n_rows, sizes, make_copy):
    pos = 0
    for sz in sizes:
        bit = n_rows & sz
        yield bit, make_copy(pos, sz)
        pos = pos + bit


def _slot_key(group, rank, boff_ref, b):
    key = rank
    for g in range(MOE_GROUPS):
        key = key + jnp.where(group == g, boff_ref[MOE_GROUPS * b + g], 0)
    return key


def _start_all(copies):
    for bit, cs in copies:
        @pl.when(bit != 0)
        def _():
            for c in cs:
                c.start()


def _wait_all(copies):
    for bit, cs in copies:
        @pl.when(bit != 0)
        def _():
            for c in cs:
                c.wait()


def _dispatch_kernel(off_ref, cpad_ref, boff_ref, tail_ref, x_ref, gates_ref, inforow_ref,
                     xs_hbm, gs_hbm, xstg, gstg, zx, zg, sem):
    b = pl.program_id(0)
    last = pl.num_programs(0) - 1
    tb = x_ref.shape[0]
    ks = xstg.shape[1]
    cur = lax.rem(b, 2)
    key = _slot_key(inforow_ref[0:1, :], inforow_ref[1:2, :], boff_ref, b)
    slot = lax.broadcasted_iota(jnp.int32, (ks, tb), 0)
    perm = jnp.where(slot == key, 1.0, 0.0).astype(BF16)
    xstg[cur] = _dot(perm, x_ref[...]).astype(BF16)
    gt = gates_ref[...]
    hi = gt.astype(BF16)
    lo = (gt - hi.astype(F32)).astype(BF16)
    gstg[cur] = _dot(perm, hi) + _dot(perm, lo)

    sizes = _pow2_sizes(tb, SEG_ALIGN)

    def copies(blk, sl):
        for g in range(MOE_GROUPS):
            src0 = boff_ref[MOE_GROUPS * blk + g]
            dst0 = off_ref[MOE_GROUPS * blk + g]

            def make(pos, sz, src0=src0, dst0=dst0):
                src = pl.ds(pl.multiple_of(src0 + pos, SEG_ALIGN), sz)
                dst = pl.ds(pl.multiple_of(dst0 + pos, SEG_ALIGN), sz)
                return (pltpu.make_async_copy(xstg.at[sl, src, :], xs_hbm.at[dst, :], sem.at[sl, 0]),
                        pltpu.make_async_copy(gstg.at[sl, src, :], gs_hbm.at[dst, :], sem.at[sl, 1]))

            yield from _strip_copies(cpad_ref[MOE_GROUPS * blk + g], sizes, make)

    _start_all(copies(b, cur))

    @pl.when(b > 0)
    def _():
        _wait_all(copies(jnp.maximum(b - 1, 0), 1 - cur))

    @pl.when(b == last)
    def _():
        _wait_all(copies(b, cur))
        zx[...] = jnp.zeros(zx.shape, BF16)
        zg[...] = jnp.zeros(zg.shape, F32)
        tail_sizes = _pow2_sizes(MOE_TILE // 2, SEG_ALIGN)

        def tails():
            for g in range(MOE_GROUPS):
                dst0 = tail_ref[g]

                def make(pos, sz, dst0=dst0):
                    dst = pl.ds(pl.multiple_of(dst0 + pos, SEG_ALIGN), sz)
                    return (pltpu.make_async_copy(zx.at[pl.ds(0, sz), :], xs_hbm.at[dst, :], sem.at[0, 0]),
                            pltpu.make_async_copy(zg.at[pl.ds(0, sz), :], gs_hbm.at[dst, :], sem.at[0, 1]))

                yield from _strip_copies(tail_ref[MOE_GROUPS + g], tail_sizes, make)

        _start_all(tails())
        _wait_all(tails())

        def spare(i):
            dst = pl.ds(pl.multiple_of(i * MOE_TILE, MOE_TILE), MOE_TILE)
            return (pltpu.make_async_copy(zx, xs_hbm.at[dst, :], sem.at[0, 0]),
                    pltpu.make_async_copy(zg, gs_hbm.at[dst, :], sem.at[0, 1]))

        first_spare = tail_ref[2 * MOE_GROUPS]
        n_tiles = xs_hbm.shape[0] // MOE_TILE

        @pl.loop(first_spare, n_tiles)
        def _(i):
            cx, cg = spare(i)
            cx.start()
            cg.start()

        @pl.loop(first_spare, n_tiles)
        def _(i):
            cx, cg = spare(i)
            cx.wait()
            cg.wait()


def _dispatch(xn, gates, inforow, off, cpad, boff, tail, n_sorted, tb, ks):
    n, d = xn.shape
    nb = n // tb
    grid_spec = pltpu.PrefetchScalarGridSpec(
        num_scalar_prefetch=4,
        grid=(nb,),
        in_specs=[pl.BlockSpec((tb, d), lambda i, *_: (i, 0)),
                  pl.BlockSpec((tb, LANES), lambda i, *_: (i, 0)),
                  pl.BlockSpec((None, 8, tb), lambda i, *_: (i, 0, 0))],
        out_specs=[pl.BlockSpec(memory_space=pl.ANY), pl.BlockSpec(memory_space=pl.ANY)],
        scratch_shapes=[pltpu.VMEM((2, ks, d), BF16), pltpu.VMEM((2, ks, LANES), F32),
                        pltpu.VMEM((MOE_TILE, d), BF16), pltpu.VMEM((MOE_TILE, LANES), F32),
                        pltpu.SemaphoreType.DMA((2, 2))])
    return pl.pallas_call(
        _dispatch_kernel,
        grid_spec=grid_spec,
        out_shape=[jax.ShapeDtypeStruct((n_sorted, d), BF16),
                   jax.ShapeDtypeStruct((n_sorted, LANES), F32)],
        compiler_params=_params("arbitrary"),
        name="moe_dispatch",
    )(off, cpad, boff, tail, xn, gates, inforow)


def _group_mlp_kernel(tg_ref, nused_ref, x_ref, gs_ref, wg_ref, wu_ref, wd_ref, y_ref):
    i = pl.program_id(0)

    @pl.when(i < nused_ref[0])
    def _():
        x = x_ref[...]
        gs = gs_ref[...]
        lane = lax.broadcasted_iota(jnp.int32, gs.shape, 1)
        lane0 = tg_ref[i] * MOE_EXPERTS_PER_GROUP
        acc = jnp.zeros(y_ref.shape, F32)
        nxt = (_dot(x, wg_ref[0]), _dot(x, wu_ref[0]))
        for e in range(MOE_EXPERTS_PER_GROUP):
            hg, hu = nxt
            if e + 1 < MOE_EXPERTS_PER_GROUP:
                nxt = (_dot(x, wg_ref[e + 1]), _dot(x, wu_ref[e + 1]))
            gate = jnp.sum(jnp.where(lane == lane0 + e, gs, 0.0), axis=-1, keepdims=True)
            acc = acc + _dot((_silu(hg) * hu * gate).astype(BF16), wd_ref[e])
        y_ref[...] = acc.astype(y_ref.dtype)

    @pl.when(i >= nused_ref[0])
    def _():
        y_ref[...] = jnp.zeros(y_ref.shape, y_ref.dtype)


def _group_mlp(xs, gs, wg, wu, wd, tile_group, nused):
    n_sorted, d = xs.shape
    ne = MOE_EXPERTS_PER_GROUP

    def rows(i, tg, nu):
        return (jnp.minimum(i, nu[0] - 1), 0)

    def group(i, tg, nu):
        return (tg[i], 0, 0)

    grid_spec = pltpu.PrefetchScalarGridSpec(
        num_scalar_prefetch=2,
        grid=(n_sorted // MOE_TILE,),
        in_specs=[pl.BlockSpec((MOE_TILE, d), rows),
                  pl.BlockSpec((MOE_TILE, LANES), rows),
                  pl.BlockSpec((ne, d, MOE_D_FF), group),
                  pl.BlockSpec((ne, d, MOE_D_FF), group),
                  pl.BlockSpec((ne, MOE_D_FF, d), group)],
        out_specs=pl.BlockSpec((MOE_TILE, d), lambda i, tg, nu: (i, 0)))
    return pl.pallas_call(
        _group_mlp_kernel,
        grid_spec=grid_spec,
        out_shape=jax.ShapeDtypeStruct((n_sorted, d), BF16),
        compiler_params=_params("arbitrary"),
        name="moe_experts",
    )(tile_group, nused, xs, gs, wg, wu, wd)


def _combine_kernel(off_ref, cpad_ref, boff_ref, info_ref, res_ref, g_ref, *rest, project):
    if project:
        w_ref, y_hbm, o_ref, u_ref, ybuf, sem = rest
    else:
        y_hbm, o_ref, ybuf, sem = rest
    b = pl.program_id(0)
    last = pl.num_programs(0) - 1
    tb = res_ref.shape[0]
    ks = ybuf.shape[1]
    cur = lax.rem(b, 2)
    sizes = _pow2_sizes(tb, STRIP)

    def copies(blk, sl):
        for g in range(MOE_GROUPS):
            src0 = off_ref[MOE_GROUPS * blk + g]
            dst0 = boff_ref[MOE_GROUPS * blk + g]

            def make(pos, sz, src0=src0, dst0=dst0):
                src = pl.ds(pl.multiple_of(src0 + pos, SEG_ALIGN), sz)
                dst = pl.ds(pl.multiple_of(dst0 + pos, STRIP), sz)
                return (pltpu.make_async_copy(y_hbm.at[src, :], ybuf.at[sl, dst, :], sem.at[sl]),)

            n_rows = (cpad_ref[MOE_GROUPS * blk + g] + (STRIP - 1)) & (-STRIP)
            yield from _strip_copies(n_rows, sizes, make)

    @pl.when(b == 0)
    def _():
        ybuf[...] = jnp.zeros(ybuf.shape, ybuf.dtype)
        _start_all(copies(b, cur))

    @pl.when(b < last)
    def _():
        _start_all(copies(jnp.minimum(b + 1, last), 1 - cur))

    _wait_all(copies(b, cur))

    key = _slot_key(info_ref[:, 0:1], info_ref[:, 1:2], boff_ref, b)
    slot = lax.broadcasted_iota(jnp.int32, (tb, ks), 1)
    perm = jnp.where(slot == key, 1.0, 0.0).astype(BF16)
    y = res_ref[...] + _dot(perm, ybuf[cur])
    ms = jnp.mean(y * y, axis=-1, keepdims=True)
    yn = y * lax.rsqrt(ms + NORM_EPS) * g_ref[...]
    if project:
        o_ref[...] = y
        u_ref[...] = _dot(yn.astype(BF16), w_ref[...]).astype(u_ref.dtype)
    else:
        o_ref[...] = yn


def _combine(ys, info, res, norm_g, w_next, off, cpad, boff, tb, ks):
    n, d = res.shape
    project = w_next is not None
    row = lambda width: pl.BlockSpec((tb, width), lambda i, *_: (i, 0))
    in_specs = [row(LANES), row(d), pl.BlockSpec((1, d), lambda i, *_: (0, 0))]
    args = [info, res, norm_g.reshape(1, d)]
    out_specs, out_shape = [row(d)], [jax.ShapeDtypeStruct((n, d), F32)]
    if project:
        nout = w_next.shape[1]
        in_specs.append(pl.BlockSpec((d, nout), lambda i, *_: (0, 0)))
        args.append(w_next)
        out_specs.append(row(nout))
        out_shape.append(jax.ShapeDtypeStruct((n, nout), BF16))
    grid_spec = pltpu.PrefetchScalarGridSpec(
        num_scalar_prefetch=3,
        grid=(n // tb,),
        in_specs=in_specs + [pl.BlockSpec(memory_space=pl.ANY)],
        out_specs=out_specs,
        scratch_shapes=[pltpu.VMEM((2, ks, d), BF16), pltpu.SemaphoreType.DMA((2,))])
    out = pl.pallas_call(
        functools.partial(_combine_kernel, project=project),
        grid_spec=grid_spec,
        out_shape=out_shape,
        compiler_params=_params("arbitrary"),
        name="moe_combine",
    )(off, cpad, boff, *args, ys)
    return tuple(out) if project else out[0]


def _round_up(x, m):
    return (x + m - 1) // m * m


def _moe_layer(routed, expert_weights, norm_g, w_next, tb):
    h, xn, gates, info, inforow, counts = routed
    w_gate, w_up, w_down = expert_weights
    n, d = h.shape
    nb = n // tb

    cnt = counts[:, 0, :MOE_GROUPS]
    cpad = _round_up(cnt, SEG_ALIGN)
    strip = _round_up(cnt, STRIP)
    used = jnp.sum(cpad, axis=0)
    region = _round_up(used, MOE_TILE)
    region_start = jnp.cumsum(region) - region
    off = region_start[None, :] + jnp.cumsum(cpad, axis=0) - cpad
    boff = jnp.cumsum(strip, axis=1) - strip
    nused = (jnp.sum(region) // MOE_TILE).reshape(1)
    tail = jnp.concatenate([region_start + used, region - used, nused]).astype(jnp.int32)
    n_tiles = -(-(n + nb * MOE_GROUPS * (SEG_ALIGN - 1) + MOE_GROUPS * (MOE_TILE - 1)) // MOE_TILE) + 1
    tile_start = jnp.arange(n_tiles, dtype=jnp.int32) * MOE_TILE
    region_end = region_start + region
    tile_group = jnp.minimum(jnp.sum(tile_start[:, None] >= region_end[None, :], axis=1), MOE_GROUPS - 1)
    ks =tb + MOE_GROUPS * STRIP

    flat = lambda a: a.reshape(-1).astype(jnp.int32)
    xs, gs = _dispatch(xn, gates, inforow, flat(off), flat(cpad), flat(boff), tail,
                       n_tiles * MOE_TILE, tb, ks)
    ys = _group_mlp(xs, gs, w_gate, w_up, w_down,
                    tile_group.astype(jnp.int32), nused.astype(jnp.int32))
    return _combine(ys, info, h, norm_g, w_next, flat(off), flat(cpad), flat(boff), tb, ks)


ATT_Q_TILE = 256
ATT_TILES_PER_ITER = 2
ATT_SCORES_AHEAD = 2
ATT_SCORE_SLOTS = 4
ATT_VT_ROWS = ATT_HEAD_DIM + 16
ATT_Q_SCALE = ATT_HEAD_DIM ** -0.5 * 1.4426950408889634


def _eye(n):
    return jnp.where(lax.broadcasted_iota(jnp.int32, (n, n), 0) == lax.broadcasted_iota(jnp.int32, (n, n), 1),
                     1.0, 0.0).astype(BF16)


def _transpose_bf16(x, eye):
    return lax.dot_general(eye, x, _NT, preferred_element_type=F32).astype(BF16)


def _rope_kernel(q_ref, k_ref, v_ref, aq_ref, bq_ref, ak_ref, bk_ref, qt_ref, kp_ref, vt_ref):
    t = q_ref.shape[0]
    tq = ATT_Q_TILE
    dh = ATT_HEAD_DIM
    gi = lax.broadcasted_iota(jnp.int32, (LANES, LANES), 0) >> 6
    gj = lax.broadcasted_iota(jnp.int32, (LANES, LANES), 1) >> 6
    gmat = jnp.where(gi == gj, 1.0, 0.0).astype(BF16)
    eye_q = _eye(LANES)

    def norm_rope(x, a, b):
        lane = lax.broadcasted_iota(jnp.int32, x.shape, 1)
        sq = x * x
        hi = sq.astype(BF16)
        lo = (sq - hi.astype(F32)).astype(BF16)
        ms = (_dot(hi, gmat) + _dot(lo, gmat)) * (1.0 / dh)
        partner = jnp.where((lane & 1) == 0, pltpu.roll(x, LANES - 1, 1), pltpu.roll(x, 1, 1))
        return lax.rsqrt(ms + NORM_EPS) * (x * a + partner * b)

    for j in range(q_ref.shape[1] // LANES):
        cols = slice(j * LANES, (j + 1) * LANES)
        y = norm_rope(q_ref[N_META:, cols].astype(F32), aq_ref[N_META:, :], bq_ref[N_META:, :])
        yt = _transpose_bf16(y.astype(BF16), eye_q)
        for i in range((t - N_META) // tq):
            qt_ref[i, cols, :] = yt[:, i * tq:(i + 1) * tq]

    low = lax.broadcasted_iota(jnp.int32, (t, LANES), 1) < dh
    for j in range(k_ref.shape[1] // LANES):
        cols = slice(j * LANES, (j + 1) * LANES)
        y = norm_rope(k_ref[:, cols].astype(F32), ak_ref[...], bk_ref[...])
        kp_ref[2 * j] = jnp.where(low, y, 0.0).astype(BF16)
        kp_ref[2 * j + 1] = jnp.where(low, pltpu.roll(y, dh, 1), 0.0).astype(BF16)

    n_real = t - N_META
    eye_v = _eye(v_ref.shape[1])
    vt_real = _transpose_bf16(v_ref[N_META:, :], eye_v)
    vt_meta = _transpose_bf16(v_ref[0:N_META, :], eye_v)
    for h in range(ATT_KV_HEADS):
        vt_ref[h, 0:dh, 0:n_real] = vt_real[h * dh:(h + 1) * dh, :]
        vt_ref[h, 0:dh, n_real:t] = vt_meta[h * dh:(h + 1) * dh, :]
        extra = lax.broadcasted_iota(jnp.int32, (ATT_VT_ROWS - dh, t), 0)
        vt_ref[h, dh:ATT_VT_ROWS, :] = jnp.where(extra == 0, 1.0, 0.0).astype(BF16)


def _rope(u3, tables):
    b, t, _ = u3.shape
    nq = ATT_HEADS * ATT_HEAD_DIM
    nkv = ATT_KV_HEADS * ATT_HEAD_DIM
    n_qt = (t - N_META) // ATT_Q_TILE
    full = lambda shape: pl.BlockSpec(shape, lambda i: (0, 0))
    return pl.pallas_call(
        _rope_kernel,
        grid=(b,),
        in_specs=[pl.BlockSpec((None, t, nq), lambda i: (i, 0, 0)),
                  pl.BlockSpec((None, t, nkv), lambda i: (i, 0, nq // nkv)),
                  pl.BlockSpec((None, t, nkv), lambda i: (i, 0, nq // nkv + 1)),
                  full((t, LANES)), full((t, LANES)), full((t, LANES)), full((t, LANES))],
        out_specs=[pl.BlockSpec((None, n_qt, nq, ATT_Q_TILE), lambda i: (i, 0, 0, 0)),
                   pl.BlockSpec((None, ATT_KV_HEADS, t, LANES), lambda i: (i, 0, 0, 0)),
                   pl.BlockSpec((None, ATT_KV_HEADS, ATT_VT_ROWS, t), lambda i: (i, 0, 0, 0))],
        out_shape=[jax.ShapeDtypeStruct((b, n_qt, nq, ATT_Q_TILE), BF16),
                   jax.ShapeDtypeStruct((b, ATT_KV_HEADS, t, LANES), BF16),
                   jax.ShapeDtypeStruct((b, ATT_KV_HEADS, ATT_VT_ROWS, t), BF16)],
        compiler_params=_params("parallel"),
        name="qk_norm_rope",
    )(u3, u3, u3, *tables)


def _attn_kernel(qt_ref, k_ref, vt_ref, wg_ref, wu_ref, wd_ref, o_ref, wg_o, wu_o, wd_o, s_ref):
    _cast_through((wg_ref, wu_ref, wd_ref), (wg_o, wu_o, wd_o))
    n_qt, w, tq = qt_ref.shape
    dh = ATT_HEAD_DIM
    n_real = k_ref.shape[0] - N_META
    eye = _eye(tq)
    zpad = jnp.zeros((LANES - dh, tq), BF16)

    def scores(i, g, slot):
        qp = jnp.concatenate([qt_ref[i, g * dh:(g + 1) * dh, :], zpad], axis=0)
        s_r = _dot(k_ref[N_META:, :], qp)
        s_m = _dot(k_ref[0:N_META, :], qp)
        s_ref[slot, 0:n_real, :] = s_r
        s_ref[slot, n_real:, :] = s_m
        return jnp.maximum(jnp.max(s_r, axis=0, keepdims=True), jnp.max(s_m, axis=0, keepdims=True))

    steps = [(u, g) for u in range(ATT_TILES_PER_ITER) for g in range(ATT_GROUP)]
    n_steps = len(steps)
    n_iter = n_qt // ATT_TILES_PER_ITER
    ahead = ATT_SCORES_AHEAD
    n_slots = s_ref.shape[0]
    assert n_steps % n_slots == 0 and ahead < n_slots

    def issue_scores(it, n):
        it2 = jnp.minimum(it + n // n_steps, n_iter - 1)
        u, g = steps[n % n_steps]
        return scores(it2 * ATT_TILES_PER_ITER + u, g, n % n_slots)

    def q_tiles(it, pending):
        i0 = it * ATT_TILES_PER_ITER
        outs = []
        pending = list(pending)
        for n, (u, g) in enumerate(steps):
            m, slot = pending.pop(0), n % n_slots
            pending.append(issue_scores(it, n + ahead))
            p_r = jnp.exp2(s_ref[slot, 0:n_real, :] - m).astype(BF16)
            p_m = jnp.exp2(s_ref[slot, n_real:, :] - m).astype(BF16)
            ot = _dot(vt_ref[:, 0:n_real], p_r) + _dot(vt_ref[:, n_real:], p_m)
            outs.append((ot[0:dh] * (1.0 / ot[dh:dh + 1])).astype(BF16))
            if g == ATT_GROUP - 1:
                rows = pl.ds(pl.multiple_of((i0 + u) * tq, tq), tq)
                ot_all = jnp.concatenate(outs, axis=0)
                o_ref[rows, :] = lax.dot_general(eye, ot_all, _NT, preferred_element_type=F32).astype(o_ref.dtype)
                outs = []
        return tuple(pending)

    first = tuple(issue_scores(0, n) for n in range(ahead))
    lax.fori_loop(0, n_iter, q_tiles, first)


def _attention(qt, kp, vt, expert_weights, layer):
    b, n_qt, nq, tq = qt.shape
    t = kp.shape[2]
    w = ATT_GROUP * ATT_HEAD_DIM
    grid = (b, ATT_KV_HEADS)
    c_in, c_out, c_shape = _expert_cast_specs(expert_weights, layer, grid)
    o, *cast = pl.pallas_call(
        _attn_kernel,
        grid=grid,
        in_specs=[pl.BlockSpec((None, n_qt, w, tq), lambda i, h: (i, 0, h, 0)),
                  pl.BlockSpec((None, None, t, LANES), lambda i, h: (i, h, 0, 0)),
                  pl.BlockSpec((None, None, ATT_VT_ROWS, t), lambda i, h: (i, h, 0, 0))] + c_in,
        out_specs=[pl.BlockSpec((None, n_qt * tq, w), lambda i, h: (i, 0, h))] + c_out,
        out_shape=[jax.ShapeDtypeStruct((b, n_qt * tq, nq), BF16)] + c_shape,
        scratch_shapes=[pltpu.VMEM((ATT_SCORE_SLOTS, t, tq), F32)],
        compiler_params=_params("parallel", "parallel"),
        name="gqa_attention",
    )(qt, kp, vt, *expert_weights)
    return o, cast


def _rope_tables(n_tokens, q_gain, k_gain):
    rows_n = n_tokens // GRID_W
    rows = jnp.repeat(jnp.arange(rows_n), GRID_W).astype(F32)
    cols = jnp.tile(jnp.arange(GRID_W), rows_n).astype(F32)
    axis_dims = ATT_HEAD_DIM // 2
    freqs = ROPE_THETA ** (-jnp.arange(0, axis_dims, 2, dtype=F32) / axis_dims)
    ang = jnp.concatenate([rows[:, None] * freqs, cols[:, None] * freqs], axis=-1)
    ang = jnp.concatenate([jnp.zeros((N_META, ang.shape[1]), F32), ang], axis=0)
    cos_e = jnp.tile(jnp.repeat(jnp.cos(ang), 2, axis=1), (1, LANES // ATT_HEAD_DIM))
    sign = jnp.tile(jnp.array([-1.0, 1.0], F32), LANES // 2)
    sin_e = jnp.tile(jnp.repeat(jnp.sin(ang), 2, axis=1), (1, LANES // ATT_HEAD_DIM)) * sign

    def tables(gain, scale):
        g = jnp.tile(gain.astype(F32), LANES // ATT_HEAD_DIM)
        g_partner = g.reshape(-1, 2)[:, ::-1].reshape(-1)
        return cos_e * (g * scale), sin_e * (g_partner * scale)

    return tables(q_gain, ATT_Q_SCALE) + tables(k_gain, 1.0)


def kernel(x, meta_tokens, norm_mix, norm_ffn, norm_final, sc_w_in, sc_conv_w, gla_w_gate_fwd, gla_b_gate_fwd, gla_w_gate_bwd, gla_b_gate_bwd, gla_norm_w, sc_w_out, att_w_in, att_q_norm, att_k_norm, att_w_out, moe_w_group, moe_b_group, moe_w_expert, moe_b_expert, moe_w_gate, moe_w_up, moe_w_down):
    bsz, s_len, d = x.shape
    t = s_len + N_META
    n0, n1 = bsz * t, bsz * s_len
    tm0 = t // 3
    tb0 = 3 * LANES
    tm1 = 4 * LANES
    assert t % tm0 == 0 and tm0 % 16 == 0 and n0 % tb0 == 0 and n1 % tm1 == 0 and s_len % tm1 == 0

    expert_f32 = (moe_w_gate, moe_w_up, moe_w_down)
    meta = jnp.broadcast_to(meta_tokens.astype(x.dtype)[None], (bsz, N_META, d))
    h = jnp.concatenate([meta, x], axis=1).reshape(n0, d)

    nk = GLA_HEADS * GLA_DK
    mix_in = sc_w_in.shape[2]
    mix_pad = -mix_in % MXU_WIDTH
    w_in = jnp.pad(sc_w_in[0], ((0, 0), (0, mix_pad))).astype(BF16)
    u = _norm_matmul(h, norm_mix[0], w_in, tm0, mix_in + mix_pad)
    u3 = u.reshape(bsz, t, mix_in + mix_pad)
    y_a, experts0 = _gated_conv(u3, sc_conv_w[0], expert_f32, 0)
    wc = jnp.zeros((LANES, 2 * nk), F32)
    wc = wc.at[:GLA_GATE_RANK, :nk].set(gla_w_gate_fwd[0])
    wc = wc.at[GLA_GATE_RANK:2 * GLA_GATE_RANK, nk:].set(gla_w_gate_bwd[0])
    bc = jnp.concatenate([gla_b_gate_fwd[0], gla_b_gate_bwd[0]]).reshape(1, 2 * nk)
    y_b = _gla(u3, wc.astype(BF16), bc, gla_norm_w[0].reshape(1, GLA_DV))
    w_out = sc_w_out[0].astype(BF16)
    router0 = _router_operands(norm_ffn[0], moe_w_group[0], moe_b_group[0], moe_w_expert[0], moe_b_expert[0])
    routed = _proj2_router(y_a.reshape(n0, SC_WIDTH), y_b.reshape(n0, -1), w_out[:SC_WIDTH], w_out[SC_WIDTH:],
                           h, router0, tb0)
    h, u = _moe_layer(routed, experts0, norm_mix[1], att_w_in[0].astype(BF16), tb0)
    u3 = u.reshape(bsz, t, -1)
    qt, kp, vt = _rope(u3, _rope_tables(s_len, att_q_norm[0], att_k_norm[0]))
    o, experts1 = _attention(qt, kp, vt, expert_f32, 1)
    router1 = _router_operands(norm_ffn[1], moe_w_group[1], moe_b_group[1], moe_w_expert[1], moe_b_expert[1])
    routed = _att_out_router(o.reshape(n1, d), att_w_out[0].astype(BF16), h.reshape(bsz, t, d), router1, tm1)
    out = _moe_layer(routed, experts1, norm_final, None, tm1)
    return out.reshape(bsz, s_len, d)
```

```python
import functools

import jax
import jax.numpy as jnp
from jax import lax
from jax.experimental import pallas as pl
from jax.experimental.pallas import tpu as pltpu

F32 = jnp.float32
BF16 = jnp.bfloat16

NORM_EPS = 1e-6
N_META = 16
GRID_W = 64
ROPE_THETA = 10000.0
SC_WIDTH = 512
GLA_HEADS = 4
GLA_DK = 64
GLA_DV = 128
GLA_GATE_RANK = 16
GLA_GATE_TAU = 16.0
GLA_CHUNK = 64
ATT_HEAD_DIM = 64
ATT_HEADS = 16
ATT_KV_HEADS = 4
ATT_GROUP = ATT_HEADS // ATT_KV_HEADS
MOE_GROUPS = 4
MOE_EXPERTS_PER_GROUP = 8
N_EXPERTS = 32
MOE_D_FF = 256

LANES = 128
MXU_WIDTH = 256
MIB = 2 ** 20
VMEM_LIMIT = 56 * MIB

_NT = (((1,), (1,)), ((), ()))
_TN = (((0,), (0,)), ((), ()))


def _params(*semantics):
    return pltpu.CompilerParams(dimension_semantics=semantics, vmem_limit_bytes=VMEM_LIMIT)


def _dot(a, b):
    return jnp.dot(a, b, preferred_element_type=F32)


def _silu(x):
    return x * (1.0 / (1.0 + jnp.exp(-x)))


def _norm_matmul_kernel(h_ref, g_ref, w_ref, o_ref, xn_ref):
    @pl.when(pl.program_id(1) == 0)
    def _():
        x = h_ref[...]
        ms = jnp.mean(x * x, axis=-1, keepdims=True)
        xn_ref[...] = (x * lax.rsqrt(ms + NORM_EPS) * g_ref[...]).astype(BF16)

    o_ref[...] = _dot(xn_ref[...], w_ref[...]).astype(o_ref.dtype)


def _norm_matmul(h, g, w, tm, tn):
    n, d = h.shape
    nout = w.shape[1]
    return pl.pallas_call(
        _norm_matmul_kernel,
        grid=(n // tm, nout // tn),
        in_specs=[pl.BlockSpec((tm, d), lambda i, j: (i, 0)),
                  pl.BlockSpec((1, d), lambda i, j: (0, 0)),
                  pl.BlockSpec((d, tn), lambda i, j: (0, j))],
        out_specs=pl.BlockSpec((tm, tn), lambda i, j: (i, j)),
        out_shape=jax.ShapeDtypeStruct((n, nout), BF16),
        scratch_shapes=[pltpu.VMEM((tm, d), BF16)],
        compiler_params=_params("parallel", "arbitrary"),
        name="norm_matmul",
    )(h, g.reshape(1, d), w)


def _expert_cast_specs(weights, layer, grid_shape):
    n_steps = 1
    for g in grid_shape:
        n_steps *= g
    assert N_EXPERTS % n_steps == 0
    per_step = N_EXPERTS // n_steps

    def step(*idx):
        flat = 0
        for i, g in zip(idx, grid_shape):
            flat = flat * g + i
        return flat

    in_specs = [pl.BlockSpec((None, per_step) + w.shape[2:], lambda *idx: (layer, step(*idx), 0, 0))
                for w in weights]
    out_specs = [pl.BlockSpec((per_step,) + w.shape[2:], lambda *idx: (step(*idx), 0, 0)) for w in weights]
    out_shape = [jax.ShapeDtypeStruct(w.shape[1:], BF16) for w in weights]
    return in_specs, out_specs, out_shape


def _cast_through(in_refs, out_refs):
    for src, dst in zip(in_refs, out_refs):
        dst[...] = src[...].astype(dst.dtype)


def _conv_kernel(ab_ref, ac_ref, ax_ref, w_ref, *rest):
    n_cast = (len(rest) - 2) // 2
    o_ref, z_ref = rest[n_cast], rest[-1]
    _cast_through(rest[:n_cast], rest[n_cast + 1:-1])
    t, c = ab_ref.shape
    z = ac_ref[...].astype(F32) * ax_ref[...].astype(F32)
    z_ref[0:8, :] = jnp.zeros((8, c), F32)
    z_ref[8:8 + t, :] = z
    z_ref[8 + t:16 + t, :] = jnp.zeros((8, c), F32)
    w = w_ref[...]
    conv = w[0:1, :] * z_ref[7:7 + t, :] + w[1:2, :] * z + w[2:3, :] * z_ref[9:9 + t, :]
    o_ref[...] = (ab_ref[...].astype(F32) * conv).astype(o_ref.dtype)


def _gated_conv(u3, conv_w, expert_weights, layer):
    b, t, _ = u3.shape
    nblk = SC_WIDTH // LANES
    grid = (b, nblk)
    c_in, c_out, c_shape = _expert_cast_specs(expert_weights, layer, grid)

    def col(off):
        return pl.BlockSpec((None, t, LANES), lambda i, j: (i, 0, off + j))

    y_a, *cast = pl.pallas_call(
        _conv_kernel,
        grid=grid,
        in_specs=[col(0), col(nblk), col(2 * nblk),
                  pl.BlockSpec((3, LANES), lambda i, j: (0, j))] + c_in,
        out_specs=[pl.BlockSpec((None, t, LANES), lambda i, j: (i, 0, j))] + c_out,
        out_shape=[jax.ShapeDtypeStruct((b, t, SC_WIDTH), BF16)] + c_shape,
        scratch_shapes=[pltpu.VMEM((t + 16, LANES), F32)],
        compiler_params=_params("parallel", "parallel"),
        name="gated_conv",
    )(u3, u3, u3, conv_w, *expert_weights)
    return y_a, cast


GLA_ROW_TILE = 688
GLA_EPI_TILE = 688


def _gla_kernel(q_ref, k_ref, v_ref, r_ref, g_ref, wc_ref, bc_ref, nw_ref, *rest):
    n_cast = (len(rest) - 6) // 2
    y_ref = rest[n_cast]
    la_ref, of_ref, ob_ref, sf_ref, sb_ref = rest[-5:]
    _cast_through(rest[:n_cast], rest[n_cast + 1:-5])
    t, nk = q_ref.shape
    nv = v_ref.shape[1]
    c = GLA_CHUNK
    n_real = (t - N_META) // c

    for i in range(t // GLA_ROW_TILE):
        rows = pl.ds(i * GLA_ROW_TILE, GLA_ROW_TILE)
        pre = _dot(g_ref[rows, :], wc_ref[...]) + bc_ref[...]
        la_ref[rows, :] = (jnp.minimum(pre, 0.0) - jnp.log(1.0 + jnp.exp(-jnp.abs(pre)))) * (1.0 / GLA_GATE_TAU)

    def iota(shape, dim):
        return lax.broadcasted_iota(jnp.int32, shape, dim)

    tri_f = jnp.where(iota((c, c), 0) >= iota((c, c), 1), 1.0, 0.0).astype(BF16)
    tri_b = jnp.where(iota((c, c), 0) <= iota((c, c), 1), 1.0, 0.0).astype(BF16)
    keep_f = (iota((c, nk), 1) & (c - 1)) <= iota((c, nk), 0)
    keep_b = (iota((c, nk), 1) & (c - 1)) > iota((c, nk), 0)
    bd_k = (iota((nk, nk), 0) >> 6) == (iota((nk, nk), 1) >> 6)
    bd_v = (iota((nk, nv), 0) >> 6) == (iota((nk, nv), 1) >> 7)
    bd_s = (iota((nv, nk), 0) >> 7) == (iota((nv, nk), 1) >> 6)
    valid0 = iota((c, 1), 0) < N_META

    def cum_decay(rows, valid, tri, la_col):
        la = la_ref[rows, la_col:la_col + nk]
        if valid is not None:
            la = jnp.where(valid, la, 0.0)
        hi = la.astype(BF16)
        rem = la - hi.astype(F32)
        mid = rem.astype(BF16)
        lo = (rem - mid.astype(F32)).astype(BF16)
        return _dot(tri, hi) + _dot(tri, mid) + _dot(tri, lo)

    def chunk(rows, valid, cum, keep, tot_row, s_ref, o_ref, out_rows, n_out):
        q = q_ref[rows, :].astype(F32) * (GLA_DK ** -0.5)
        k = k_ref[rows, :].astype(F32)
        v = v_ref[rows, :].astype(F32)
        if valid is not None:
            q = jnp.where(valid, q, 0.0)
            k = jnp.where(valid, k, 0.0)
            v = jnp.where(valid, v, 0.0)
        tot = cum[tot_row:tot_row + 1, :]
        q_dec = (q * jnp.exp(cum)).astype(BF16)
        k_inv = k * jnp.exp(-cum)
        k_end = (k * jnp.exp(tot - cum)).astype(BF16)
        kbd = jnp.where(bd_k, jnp.concatenate([k_inv] * GLA_HEADS, axis=0), 0.0).astype(BF16)
        p = lax.dot_general(q_dec, kbd, _NT, preferred_element_type=F32)
        s_old = s_ref[...]
        o_inter = lax.dot_general(q_dec, s_old.astype(BF16), _NT, preferred_element_type=F32)
        kvt = lax.dot_general(v.astype(BF16), k_end, _TN, preferred_element_type=F32)
        yield
        p = jnp.where(keep, p, 0.0).astype(BF16)
        vbd = jnp.where(bd_v, jnp.concatenate([v] * GLA_HEADS, axis=0), 0.0).astype(BF16)
        o = _dot(p, vbd) + o_inter
        s_ref[...] = jnp.exp(tot) * s_old + jnp.where(bd_s, kvt, 0.0)
        o_ref[out_rows, :] = o[:n_out]

    def lockstep(*gens):
        live = list(gens)
        while live:
            nxt = []
            for gen in live:
                try:
                    next(gen)
                    nxt.append(gen)
                except StopIteration:
                    pass
            live = nxt

    sf_ref[...] = jnp.zeros(sf_ref.shape, F32)
    sb_ref[...] = jnp.zeros(sb_ref.shape, F32)
    meta_rows = pl.ds(0, c)
    lockstep(chunk(meta_rows, valid0, cum_decay(meta_rows, valid0, tri_f, 0), keep_f, c - 1,
                   sf_ref, of_ref, pl.ds(0, N_META), N_META))

    def rows_fwd(i):
        return pl.ds(pl.multiple_of(N_META + i * c, 16), c)

    def rows_bwd(i):
        return pl.ds(pl.multiple_of(N_META + (n_real - 1 - i) * c, 16), c)

    def body(i, cums):
        cum_f, cum_b = cums
        nxt = jnp.minimum(i + 1, n_real - 1)
        cums = (cum_decay(rows_fwd(nxt), None, tri_f, 0), cum_decay(rows_bwd(nxt), None, tri_b, nk))
        lockstep(chunk(rows_fwd(i), None, cum_f, keep_f, c - 1, sf_ref, of_ref, rows_fwd(i), c),
                 chunk(rows_bwd(i), None, cum_b, keep_b, 0, sb_ref, ob_ref, rows_bwd(i), c))
        return cums

    first = (cum_decay(rows_fwd(0), None, tri_f, 0), cum_decay(rows_bwd(0), None, tri_b, nk))
    lax.fori_loop(0, n_real, body, first, unroll=4)
    lockstep(chunk(meta_rows, valid0, cum_decay(meta_rows, valid0, tri_b, nk), keep_b, 0,
                   sb_ref, ob_ref, pl.ds(0, N_META), N_META))

    def epilogue(i, carry):
        rows = pl.ds(pl.multiple_of(i * GLA_EPI_TILE, 16), GLA_EPI_TILE)
        o = of_ref[rows, :] + ob_ref[rows, :]
        r = r_ref[rows, :].astype(F32)
        for h in range(GLA_HEADS):
            cols = slice(h * GLA_DV, (h + 1) * GLA_DV)
            oh = o[:, cols]
            ms = jnp.mean(oh * oh, axis=-1, keepdims=True)
            yh = oh * lax.rsqrt(ms + NORM_EPS) * nw_ref[...] * _silu(r[:, cols])
            y_ref[rows, cols] = yh.astype(y_ref.dtype)
        return carry

    lax.fori_loop(0, t // GLA_EPI_TILE, epilogue, 0)


def _gla(u3, wc, bc, norm_w, expert_weights, layer):
    b, t, _ = u3.shape
    nk = GLA_HEADS * GLA_DK
    nv = GLA_HEADS * GLA_DV
    c_in, c_out, c_shape = _expert_cast_specs(expert_weights, layer, (b,))
    q_blk = 3 * SC_WIDTH // nk
    v_blk = (3 * SC_WIDTH + 2 * nk) // nv
    g_blk = (3 * SC_WIDTH + 2 * nk + 2 * nv) // LANES
    full = lambda shape: pl.BlockSpec(shape, lambda i: (0, 0))
    y_b, *cast = pl.pallas_call(
        _gla_kernel,
        grid=(b,),
        in_specs=[pl.BlockSpec((None, t, nk), lambda i: (i, 0, q_blk)),
                  pl.BlockSpec((None, t, nk), lambda i: (i, 0, q_blk + 1)),
                  pl.BlockSpec((None, t, nv), lambda i: (i, 0, v_blk)),
                  pl.BlockSpec((None, t, nv), lambda i: (i, 0, v_blk + 1)),
                  pl.BlockSpec((None, t, LANES), lambda i: (i, 0, g_blk)),
                  full((LANES, 2 * nk)), full((1, 2 * nk)), full((1, GLA_DV))] + c_in,
        out_specs=[pl.BlockSpec((None, t, nv), lambda i: (i, 0, 0))] + c_out,
        out_shape=[jax.ShapeDtypeStruct((b, t, nv), BF16)] + c_shape,
        scratch_shapes=[pltpu.VMEM((t, 2 * nk), F32),
                        pltpu.VMEM((t, nv), F32), pltpu.VMEM((t, nv), F32),
                        pltpu.VMEM((nv, nk), F32), pltpu.VMEM((nv, nk), F32)],
        compiler_params=_params("parallel"),
        name="gla",
    )(u3, u3, u3, u3, u3, wc, bc, norm_w, *expert_weights)
    return y_b, cast


GROUP_LANE0 = N_EXPERTS


def _route(x, g_ref, whi_ref, wlo_ref, b_ref, xn_ref, gates_ref, info_ref, inforow_ref, count_ref):
    ms = jnp.mean(x * x, axis=-1, keepdims=True)
    xn = x * lax.rsqrt(ms + NORM_EPS) * g_ref[...]
    xhi = xn.astype(BF16)
    xn_ref[...] = xhi
    xlo = (xn - xhi.astype(F32)).astype(BF16)
    logits = _dot(xhi, whi_ref[...]) + _dot(xhi, wlo_ref[...]) + _dot(xlo, whi_ref[...]) + b_ref[...]

    lane = lax.broadcasted_iota(jnp.int32, logits.shape, 1)
    neg = -jnp.inf
    big = jnp.int32(LANES)
    is_group = (lane >> 2) == (GROUP_LANE0 >> 2)
    gl = jnp.where(is_group, logits, neg)
    gmax = jnp.max(gl, axis=-1, keepdims=True)
    g_p = 1.0 / jnp.sum(jnp.exp(gl - gmax), axis=-1, keepdims=True)
    g_idx = jnp.min(jnp.where(gl == gmax, lane, big), axis=-1, keepdims=True) - GROUP_LANE0
    el = jnp.where((lane >> 3) == g_idx, logits, neg)
    m1 = jnp.max(el, axis=-1, keepdims=True)
    i1 = jnp.min(jnp.where(el == m1, lane, big), axis=-1, keepdims=True)
    el2 = jnp.where(lane == i1, neg, el)
    m2 = jnp.max(el2, axis=-1, keepdims=True)
    i2 = jnp.min(jnp.where(el2 == m2, lane, big), axis=-1, keepdims=True)
    e2 = jnp.exp(m2 - m1)
    w1 = g_p / (1.0 + e2)
    w2 = g_p * e2 / (1.0 + e2)
    gates_ref[...] = jnp.where(lane == i1, w1, 0.0) + jnp.where(lane == i2, w2, 0.0)

    tm = x.shape[0]
    onehot = jnp.where(lane == g_idx, 1.0, 0.0)
    r_i = lax.broadcasted_iota(jnp.int32, (tm, tm), 0)
    c_i = lax.broadcasted_iota(jnp.int32, (tm, tm), 1)
    before = jnp.where(r_i > c_i, 1.0, 0.0).astype(BF16)
    prefix = _dot(before, onehot.astype(BF16))
    rank = jnp.sum(jnp.where(lane == g_idx, prefix, 0.0), axis=-1, keepdims=True).astype(jnp.int32)
    info_ref[...] = jnp.where(lane == 0, g_idx, jnp.where(lane == 1, rank, 0))
    eye = r_i == c_i
    g_row = jnp.sum(jnp.where(eye, g_idx, 0), axis=0, keepdims=True)
    rank_row = jnp.sum(jnp.where(eye, rank, 0), axis=0, keepdims=True)
    sub = lax.broadcasted_iota(jnp.int32, inforow_ref.shape, 0)
    inforow_ref[...] = jnp.where(sub == 0, g_row, jnp.where(sub == 1, rank_row, 0))
    counts = jnp.sum(onehot, axis=0, keepdims=True).astype(jnp.int32)
    count_ref[...] = jnp.broadcast_to(counts, count_ref.shape)


def _router_operands(norm_g, w_group, b_group, w_expert, b_expert):
    d = w_group.shape[0]
    pad = LANES - N_EXPERTS - MOE_GROUPS
    wr = jnp.concatenate([w_expert, w_group, jnp.zeros((d, pad), F32)], axis=1)
    whi = wr.astype(BF16)
    wlo = (wr - whi.astype(F32)).astype(BF16)
    bias = jnp.concatenate([b_expert, b_group, jnp.zeros((pad,), F32)]).reshape(1, LANES)
    return norm_g.reshape(1, d), whi, wlo, bias


def _router_specs(n, d, tm):
    nb = n // tm
    const = lambda shape: pl.BlockSpec(shape, lambda i: (0,) * len(shape))
    in_specs = [const((1, d)), const((d, LANES)), const((d, LANES)), const((1, LANES))]
    out_specs = [pl.BlockSpec((tm, d), lambda i: (i, 0)),
                 pl.BlockSpec((tm, d), lambda i: (i, 0)),
                 pl.BlockSpec((tm, LANES), lambda i: (i, 0)),
                 pl.BlockSpec((tm, LANES), lambda i: (i, 0)),
                 pl.BlockSpec((None, 8, tm), lambda i: (i, 0, 0)),
                 pl.BlockSpec((None, 8, LANES), lambda i: (i, 0, 0))]
    out_shape = [jax.ShapeDtypeStruct((n, d), F32),
                 jax.ShapeDtypeStruct((n, d), BF16),
                 jax.ShapeDtypeStruct((n, LANES), F32),
                 jax.ShapeDtypeStruct((n, LANES), jnp.int32),
                 jax.ShapeDtypeStruct((nb, 8, tm), jnp.int32),
                 jax.ShapeDtypeStruct((nb, 8, LANES), jnp.int32)]
    return in_specs, out_specs, out_shape


def _proj2_router_kernel(a_ref, b_ref, wa_ref, wb_ref, res_ref, g_ref, whi_ref, wlo_ref, bias_ref,
                         h_ref, *route_out):
    x = res_ref[...] + _dot(a_ref[...], wa_ref[...]) + _dot(b_ref[...], wb_ref[...])
    h_ref[...] = x
    _route(x, g_ref, whi_ref, wlo_ref, bias_ref, *route_out)


def _proj2_router(a, b, wa, wb, res, router_ops, tm):
    n, d = res.shape
    ka, kb = a.shape[1], b.shape[1]
    r_in, out_specs, out_shape = _router_specs(n, d, tm)
    return pl.pallas_call(
        _proj2_router_kernel,
        grid=(n // tm,),
        in_specs=[pl.BlockSpec((tm, ka), lambda i: (i, 0)),
                  pl.BlockSpec((tm, kb), lambda i: (i, 0)),
                  pl.BlockSpec((ka, d), lambda i: (0, 0)),
                  pl.BlockSpec((kb, d), lambda i: (0, 0)),
                  pl.BlockSpec((tm, d), lambda i: (i, 0))] + r_in,
        out_specs=out_specs,
        out_shape=out_shape,
        compiler_params=_params("parallel"),
        name="proj2_router",
    )(a, b, wa, wb, res, *router_ops)


def _att_out_router_kernel(o_ref, w_ref, hres_ref, g_ref, whi_ref, wlo_ref, bias_ref, h_ref, *route_out):
    tm = o_ref.shape[0]
    tiles = (hres_ref.shape[0] - N_META) // tm
    r0 = pl.multiple_of(N_META + lax.rem(pl.program_id(0), tiles) * tm, 16)
    x = hres_ref[pl.ds(r0, tm), :] + _dot(o_ref[...], w_ref[...])
    h_ref[...] = x
    _route(x, g_ref, whi_ref, wlo_ref, bias_ref, *route_out)


def _att_out_router(o, w, h3, router_ops, tm):
    b, t, d = h3.shape
    n, nq = o.shape
    tiles = (t - N_META) // tm
    r_in, out_specs, out_shape = _router_specs(n, d, tm)
    return pl.pallas_call(
        _att_out_router_kernel,
        grid=(n // tm,),
        in_specs=[pl.BlockSpec((tm, nq), lambda i: (i, 0)),
                  pl.BlockSpec((nq, d), lambda i: (0, 0)),
                  pl.BlockSpec((None, t, d), lambda i: (i // tiles, 0, 0))] + r_in,
        out_specs=out_specs,
        out_shape=out_shape,
        compiler_params=_params("parallel"),
        name="att_out_router",
    )(o, w, h3, *router_ops)


MOE_TILE = 512
SEG_ALIGN = 16
STRIP = 32


def _pow2_sizes(max_rows, min_rows):
    sizes, s = [], min_rows
    while s <= max_rows:
        sizes.append(s)
        s *= 2
    return sizes[::-1]


def _strip_copies(n_rows, sizes, make_copy):
    pos = 0
    for sz in sizes:
        bit = n_rows & sz
        yield bit, make_copy(pos, sz)
        pos = pos + bit


def _slot_key(group, rank, boff_ref, b):
    key = rank
    for g in range(MOE_GROUPS):
        key = key + jnp.where(group == g, boff_ref[MOE_GROUPS * b + g], 0)
    return key


def _start_all(copies):
    for bit, cs in copies:
        @pl.when(bit != 0)
        def _():
            for c in cs:
                c.start()


def _wait_all(copies):
    for bit, cs in copies:
        @pl.when(bit != 0)
        def _():
            for c in cs:
                c.wait()


def _dispatch_kernel(off_ref, cpad_ref, boff_ref, tail_ref, x_ref, gates_ref, inforow_ref,
                     xs_hbm, gs_hbm, xstg, gstg, zx, zg, sem):
    b = pl.program_id(0)
    last = pl.num_programs(0) - 1
    tb = x_ref.shape[0]
    ks = xstg.shape[1]
    cur = lax.rem(b, 2)
    key = _slot_key(inforow_ref[0:1, :], inforow_ref[1:2, :], boff_ref, b)
    slot = lax.broadcasted_iota(jnp.int32, (ks, tb), 0)
    perm = jnp.where(slot == key, 1.0, 0.0).astype(BF16)
    xstg[cur] = _dot(perm, x_ref[...]).astype(BF16)
    gt = gates_ref[...]
    hi = gt.astype(BF16)
    lo = (gt - hi.astype(F32)).astype(BF16)
    gstg[cur] = _dot(perm, hi) + _dot(perm, lo)

    sizes = _pow2_sizes(tb, SEG_ALIGN)

    def copies(blk, sl):
        for g in range(MOE_GROUPS):
            src0 = boff_ref[MOE_GROUPS * blk + g]
            dst0 = off_ref[MOE_GROUPS * blk + g]

            def make(pos, sz, src0=src0, dst0=dst0):
                src = pl.ds(pl.multiple_of(src0 + pos, SEG_ALIGN), sz)
                dst = pl.ds(pl.multiple_of(dst0 + pos, SEG_ALIGN), sz)
                return (pltpu.make_async_copy(xstg.at[sl, src, :], xs_hbm.at[dst, :], sem.at[sl, 0]),
                        pltpu.make_async_copy(gstg.at[sl, src, :], gs_hbm.at[dst, :], sem.at[sl, 1]))

            yield from _strip_copies(cpad_ref[MOE_GROUPS * blk + g], sizes, make)

    _start_all(copies(b, cur))

    @pl.when(b > 0)
    def _():
        _wait_all(copies(jnp.maximum(b - 1, 0), 1 - cur))

    @pl.when(b == last)
    def _():
        _wait_all(copies(b, cur))
        zx[...] = jnp.zeros(zx.shape, BF16)
        zg[...] = jnp.zeros(zg.shape, F32)
        tail_sizes = _pow2_sizes(MOE_TILE // 2, SEG_ALIGN)

        def tails():
            for g in range(MOE_GROUPS):
                dst0 = tail_ref[g]

                def make(pos, sz, dst0=dst0):
                    dst = pl.ds(pl.multiple_of(dst0 + pos, SEG_ALIGN), sz)
                    return (pltpu.make_async_copy(zx.at[pl.ds(0, sz), :], xs_hbm.at[dst, :], sem.at[0, 0]),
                            pltpu.make_async_copy(zg.at[pl.ds(0, sz), :], gs_hbm.at[dst, :], sem.at[0, 1]))

                yield from _strip_copies(tail_ref[MOE_GROUPS + g], tail_sizes, make)

        _start_all(tails())
        _wait_all(tails())

        def spare(i):
            dst = pl.ds(pl.multiple_of(i * MOE_TILE, MOE_TILE), MOE_TILE)
            return (pltpu.make_async_copy(zx, xs_hbm.at[dst, :], sem.at[0, 0]),
                    pltpu.make_async_copy(zg, gs_hbm.at[dst, :], sem.at[0, 1]))

        first_spare = tail_ref[2 * MOE_GROUPS]
        n_tiles = xs_hbm.shape[0] // MOE_TILE

        @pl.loop(first_spare, n_tiles)
        def _(i):
            cx, cg = spare(i)
            cx.start()
            cg.start()

        @pl.loop(first_spare, n_tiles)
        def _(i):
            cx, cg = spare(i)
            cx.wait()
            cg.wait()


def _dispatch(xn, gates, inforow, off, cpad, boff, tail, n_sorted, tb, ks):
    n, d = xn.shape
    nb = n // tb
    grid_spec = pltpu.PrefetchScalarGridSpec(
        num_scalar_prefetch=4,
        grid=(nb,),
        in_specs=[pl.BlockSpec((tb, d), lambda i, *_: (i, 0)),
                  pl.BlockSpec((tb, LANES), lambda i, *_: (i, 0)),
                  pl.BlockSpec((None, 8, tb), lambda i, *_: (i, 0, 0))],
        out_specs=[pl.BlockSpec(memory_space=pl.ANY), pl.BlockSpec(memory_space=pl.ANY)],
        scratch_shapes=[pltpu.VMEM((2, ks, d), BF16), pltpu.VMEM((2, ks, LANES), F32),
                        pltpu.VMEM((MOE_TILE, d), BF16), pltpu.VMEM((MOE_TILE, LANES), F32),
                        pltpu.SemaphoreType.DMA((2, 2))])
    return pl.pallas_call(
        _dispatch_kernel,
        grid_spec=grid_spec,
        out_shape=[jax.ShapeDtypeStruct((n_sorted, d), BF16),
                   jax.ShapeDtypeStruct((n_sorted, LANES), F32)],
        compiler_params=_params("arbitrary"),
        name="moe_dispatch",
    )(off, cpad, boff, tail, xn, gates, inforow)


def _group_mlp_kernel(tg_ref, nused_ref, x_ref, gs_ref, wg_ref, wu_ref, wd_ref, y_ref):
    i = pl.program_id(0)

    @pl.when(i < nused_ref[0])
    def _():
        x = x_ref[...]
        gs = gs_ref[...]
        lane = lax.broadcasted_iota(jnp.int32, gs.shape, 1)
        lane0 = tg_ref[i] * MOE_EXPERTS_PER_GROUP
        acc = jnp.zeros(y_ref.shape, F32)
        nxt = (_dot(x, wg_ref[0]), _dot(x, wu_ref[0]))
        for e in range(MOE_EXPERTS_PER_GROUP):
            hg, hu = nxt
            if e + 1 < MOE_EXPERTS_PER_GROUP:
                nxt = (_dot(x, wg_ref[e + 1]), _dot(x, wu_ref[e + 1]))
            gate = jnp.sum(jnp.where(lane == lane0 + e, gs, 0.0), axis=-1, keepdims=True)
            acc = acc + _dot((_silu(hg) * hu * gate).astype(BF16), wd_ref[e])
        y_ref[...] = acc.astype(y_ref.dtype)

    @pl.when(i >= nused_ref[0])
    def _():
        y_ref[...] = jnp.zeros(y_ref.shape, y_ref.dtype)


def _group_mlp(xs, gs, wg, wu, wd, tile_group, nused):
    n_sorted, d = xs.shape
    ne = MOE_EXPERTS_PER_GROUP

    def rows(i, tg, nu):
        return (jnp.minimum(i, nu[0] - 1), 0)

    def group(i, tg, nu):
        return (tg[i], 0, 0)

    grid_spec = pltpu.PrefetchScalarGridSpec(
        num_scalar_prefetch=2,
        grid=(n_sorted // MOE_TILE,),
        in_specs=[pl.BlockSpec((MOE_TILE, d), rows),
                  pl.BlockSpec((MOE_TILE, LANES), rows),
                  pl.BlockSpec((ne, d, MOE_D_FF), group),
                  pl.BlockSpec((ne, d, MOE_D_FF), group),
                  pl.BlockSpec((ne, MOE_D_FF, d), group)],
        out_specs=pl.BlockSpec((MOE_TILE, d), lambda i, tg, nu: (i, 0)))
    return pl.pallas_call(
        _group_mlp_kernel,
        grid_spec=grid_spec,
        out_shape=jax.ShapeDtypeStruct((n_sorted, d), BF16),
        compiler_params=_params("arbitrary"),
        name="moe_experts",
    )(tile_group, nused, xs, gs, wg, wu, wd)


def _combine_kernel(off_ref, cpad_ref, boff_ref, info_ref, res_ref, g_ref, *rest, project):
    if project:
        w_ref, y_hbm, o_ref, u_ref, ybuf, sem = rest
    else:
        y_hbm, o_ref, ybuf, sem = rest
    b = pl.program_id(0)
    last = pl.num_programs(0) - 1
    tb = res_ref.shape[0]
    ks = ybuf.shape[1]
    cur = lax.rem(b, 2)
    sizes = _pow2_sizes(tb, STRIP)

    def copies(blk, sl):
        for g in range(MOE_GROUPS):
            src0 = off_ref[MOE_GROUPS * blk + g]
            dst0 = boff_ref[MOE_GROUPS * blk + g]

            def make(pos, sz, src0=src0, dst0=dst0):
                src = pl.ds(pl.multiple_of(src0 + pos, SEG_ALIGN), sz)
                dst = pl.ds(pl.multiple_of(dst0 + pos, STRIP), sz)
                return (pltpu.make_async_copy(y_hbm.at[src, :], ybuf.at[sl, dst, :], sem.at[sl]),)

            n_rows = (cpad_ref[MOE_GROUPS * blk + g] + (STRIP - 1)) & (-STRIP)
            yield from _strip_copies(n_rows, sizes, make)

    @pl.when(b == 0)
    def _():
        ybuf[...] = jnp.zeros(ybuf.shape, ybuf.dtype)
        _start_all(copies(b, cur))

    @pl.when(b < last)
    def _():
        _start_all(copies(jnp.minimum(b + 1, last), 1 - cur))

    _wait_all(copies(b, cur))

    key = _slot_key(info_ref[:, 0:1], info_ref[:, 1:2], boff_ref, b)
    slot = lax.broadcasted_iota(jnp.int32, (tb, ks), 1)
    perm = jnp.where(slot == key, 1.0, 0.0).astype(BF16)
    y = res_ref[...] + _dot(perm, ybuf[cur])
    ms = jnp.mean(y * y, axis=-1, keepdims=True)
    yn = y * lax.rsqrt(ms + NORM_EPS) * g_ref[...]
    if project:
        o_ref[...] = y
        u_ref[...] = _dot(yn.astype(BF16), w_ref[...]).astype(u_ref.dtype)
    else:
        o_ref[...] = yn


def _combine(ys, info, res, norm_g, w_next, off, cpad, boff, tb, ks):
    n, d = res.shape
    project = w_next is not None
    row = lambda width: pl.BlockSpec((tb, width), lambda i, *_: (i, 0))
    in_specs = [row(LANES), row(d), pl.BlockSpec((1, d), lambda i, *_: (0, 0))]
    args = [info, res, norm_g.reshape(1, d)]
    out_specs, out_shape = [row(d)], [jax.ShapeDtypeStruct((n, d), F32)]
    if project:
        nout = w_next.shape[1]
        in_specs.append(pl.BlockSpec((d, nout), lambda i, *_: (0, 0)))
        args.append(w_next)
        out_specs.append(row(nout))
        out_shape.append(jax.ShapeDtypeStruct((n, nout), BF16))
    grid_spec = pltpu.PrefetchScalarGridSpec(
        num_scalar_prefetch=3,
        grid=(n // tb,),
        in_specs=in_specs + [pl.BlockSpec(memory_space=pl.ANY)],
        out_specs=out_specs,
        scratch_shapes=[pltpu.VMEM((2, ks, d), BF16), pltpu.SemaphoreType.DMA((2,))])
    out = pl.pallas_call(
        functools.partial(_combine_kernel, project=project),
        grid_spec=grid_spec,
        out_shape=out_shape,
        compiler_params=_params("arbitrary"),
        name="moe_combine",
    )(off, cpad, boff, *args, ys)
    return tuple(out) if project else out[0]


def _round_up(x, m):
    return (x + m - 1) // m * m


def _moe_layer(routed, expert_weights, norm_g, w_next, tb):
    h, xn, gates, info, inforow, counts = routed
    w_gate, w_up, w_down = expert_weights
    n, d = h.shape
    nb = n // tb

    cnt = counts[:, 0, :MOE_GROUPS]
    cpad = _round_up(cnt, SEG_ALIGN)
    strip = _round_up(cnt, STRIP)
    used = jnp.sum(cpad, axis=0)
    region = _round_up(used, MOE_TILE)
    region_start = jnp.cumsum(region) - region
    off = region_start[None, :] + jnp.cumsum(cpad, axis=0) - cpad
    boff = jnp.cumsum(strip, axis=1) - strip
    nused = (jnp.sum(region) // MOE_TILE).reshape(1)
    tail = jnp.concatenate([region_start + used, region - used, nused]).astype(jnp.int32)
    n_tiles = -(-(n + nb * MOE_GROUPS * (SEG_ALIGN - 1) + MOE_GROUPS * (MOE_TILE - 1)) // MOE_TILE) + 1
    tile_start = jnp.arange(n_tiles, dtype=jnp.int32) * MOE_TILE
    region_end = region_start + region
    tile_group = jnp.minimum(jnp.sum(tile_start[:, None] >= region_end[None, :], axis=1), MOE_GROUPS - 1)
    ks =tb + MOE_GROUPS * STRIP

    flat = lambda a: a.reshape(-1).astype(jnp.int32)
    xs, gs = _dispatch(xn, gates, inforow, flat(off), flat(cpad), flat(boff), tail,
                       n_tiles * MOE_TILE, tb, ks)
    ys = _group_mlp(xs, gs, w_gate, w_up, w_down,
                    tile_group.astype(jnp.int32), nused.astype(jnp.int32))
    return _combine(ys, info, h, norm_g, w_next, flat(off), flat(cpad), flat(boff), tb, ks)


ATT_Q_TILE = 256
ATT_TILES_PER_ITER = 2
ATT_SCORES_AHEAD = 2
ATT_SCORE_SLOTS = 4
ATT_VT_ROWS = ATT_HEAD_DIM + 16
ATT_Q_SCALE = ATT_HEAD_DIM ** -0.5 * 1.4426950408889634


def _eye(n):
    return jnp.where(lax.broadcasted_iota(jnp.int32, (n, n), 0) == lax.broadcasted_iota(jnp.int32, (n, n), 1),
                     1.0, 0.0).astype(BF16)


def _transpose_bf16(x, eye):
    return lax.dot_general(eye, x, _NT, preferred_element_type=F32).astype(BF16)


def _rope_kernel(q_ref, k_ref, v_ref, aq_ref, bq_ref, ak_ref, bk_ref, qt_ref, kp_ref, vt_ref):
    t = q_ref.shape[0]
    tq = ATT_Q_TILE
    dh = ATT_HEAD_DIM
    gi = lax.broadcasted_iota(jnp.int32, (LANES, LANES), 0) >> 6
    gj = lax.broadcasted_iota(jnp.int32, (LANES, LANES), 1) >> 6
    gmat = jnp.where(gi == gj, 1.0, 0.0).astype(BF16)
    eye_q = _eye(LANES)

    def norm_rope(x, a, b):
        lane = lax.broadcasted_iota(jnp.int32, x.shape, 1)
        sq = x * x
        hi = sq.astype(BF16)
        lo = (sq - hi.astype(F32)).astype(BF16)
        ms = (_dot(hi, gmat) + _dot(lo, gmat)) * (1.0 / dh)
        partner = jnp.where((lane & 1) == 0, pltpu.roll(x, LANES - 1, 1), pltpu.roll(x, 1, 1))
        return lax.rsqrt(ms + NORM_EPS) * (x * a + partner * b)

    for j in range(q_ref.shape[1] // LANES):
        cols = slice(j * LANES, (j + 1) * LANES)
        y = norm_rope(q_ref[N_META:, cols].astype(F32), aq_ref[N_META:, :], bq_ref[N_META:, :])
        yt = _transpose_bf16(y.astype(BF16), eye_q)
        for i in range((t - N_META) // tq):
            qt_ref[i, cols, :] = yt[:, i * tq:(i + 1) * tq]

    low = lax.broadcasted_iota(jnp.int32, (t, LANES), 1) < dh
    for j in range(k_ref.shape[1] // LANES):
        cols = slice(j * LANES, (j + 1) * LANES)
        y = norm_rope(k_ref[:, cols].astype(F32), ak_ref[...], bk_ref[...])
        kp_ref[2 * j] = jnp.where(low, y, 0.0).astype(BF16)
        kp_ref[2 * j + 1] = jnp.where(low, pltpu.roll(y, dh, 1), 0.0).astype(BF16)

    n_real = t - N_META
    eye_v = _eye(v_ref.shape[1])
    vt_real = _transpose_bf16(v_ref[N_META:, :], eye_v)
    vt_meta = _transpose_bf16(v_ref[0:N_META, :], eye_v)
    for h in range(ATT_KV_HEADS):
        vt_ref[h, 0:dh, 0:n_real] = vt_real[h * dh:(h + 1) * dh, :]
        vt_ref[h, 0:dh, n_real:t] = vt_meta[h * dh:(h + 1) * dh, :]
        extra = lax.broadcasted_iota(jnp.int32, (ATT_VT_ROWS - dh, t), 0)
        vt_ref[h, dh:ATT_VT_ROWS, :] = jnp.where(extra == 0, 1.0, 0.0).astype(BF16)


def _rope(u3, tables):
    b, t, _ = u3.shape
    nq = ATT_HEADS * ATT_HEAD_DIM
    nkv = ATT_KV_HEADS * ATT_HEAD_DIM
    n_qt = (t - N_META) // ATT_Q_TILE
    full = lambda shape: pl.BlockSpec(shape, lambda i: (0, 0))
    return pl.pallas_call(
        _rope_kernel,
        grid=(b,),
        in_specs=[pl.BlockSpec((None, t, nq), lambda i: (i, 0, 0)),
                  pl.BlockSpec((None, t, nkv), lambda i: (i, 0, nq // nkv)),
                  pl.BlockSpec((None, t, nkv), lambda i: (i, 0, nq // nkv + 1)),
                  full((t, LANES)), full((t, LANES)), full((t, LANES)), full((t, LANES))],
        out_specs=[pl.BlockSpec((None, n_qt, nq, ATT_Q_TILE), lambda i: (i, 0, 0, 0)),
                   pl.BlockSpec((None, ATT_KV_HEADS, t, LANES), lambda i: (i, 0, 0, 0)),
                   pl.BlockSpec((None, ATT_KV_HEADS, ATT_VT_ROWS, t), lambda i: (i, 0, 0, 0))],
        out_shape=[jax.ShapeDtypeStruct((b, n_qt, nq, ATT_Q_TILE), BF16),
                   jax.ShapeDtypeStruct((b, ATT_KV_HEADS, t, LANES), BF16),
                   jax.ShapeDtypeStruct((b, ATT_KV_HEADS, ATT_VT_ROWS, t), BF16)],
        compiler_params=_params("parallel"),
        name="qk_norm_rope",
    )(u3, u3, u3, *tables)


def _attn_kernel(qt_ref, k_ref, vt_ref, wg_ref, wu_ref, wd_ref, o_ref, wg_o, wu_o, wd_o, s_ref):
    _cast_through((wg_ref, wu_ref, wd_ref), (wg_o, wu_o, wd_o))
    n_qt, w, tq = qt_ref.shape
    dh = ATT_HEAD_DIM
    n_real = k_ref.shape[0] - N_META
    eye = _eye(tq)
    zpad = jnp.zeros((LANES - dh, tq), BF16)

    def scores(i, g, slot):
        qp = jnp.concatenate([qt_ref[i, g * dh:(g + 1) * dh, :], zpad], axis=0)
        s_r = _dot(k_ref[N_META:, :], qp)
        s_m = _dot(k_ref[0:N_META, :], qp)
        s_ref[slot, 0:n_real, :] = s_r
        s_ref[slot, n_real:, :] = s_m
        return jnp.maximum(jnp.max(s_r, axis=0, keepdims=True), jnp.max(s_m, axis=0, keepdims=True))

    steps = [(u, g) for u in range(ATT_TILES_PER_ITER) for g in range(ATT_GROUP)]
    n_steps = len(steps)
    n_iter = n_qt // ATT_TILES_PER_ITER
    ahead = ATT_SCORES_AHEAD
    n_slots = s_ref.shape[0]
    assert n_steps % n_slots == 0 and ahead < n_slots

    def issue_scores(it, n):
        it2 = jnp.minimum(it + n // n_steps, n_iter - 1)
        u, g = steps[n % n_steps]
        return scores(it2 * ATT_TILES_PER_ITER + u, g, n % n_slots)

    def q_tiles(it, pending):
        i0 = it * ATT_TILES_PER_ITER
        outs = []
        pending = list(pending)
        for n, (u, g) in enumerate(steps):
            m, slot = pending.pop(0), n % n_slots
            pending.append(issue_scores(it, n + ahead))
            p_r = jnp.exp2(s_ref[slot, 0:n_real, :] - m).astype(BF16)
            p_m = jnp.exp2(s_ref[slot, n_real:, :] - m).astype(BF16)
            ot = _dot(vt_ref[:, 0:n_real], p_r) + _dot(vt_ref[:, n_real:], p_m)
            outs.append((ot[0:dh] * (1.0 / ot[dh:dh + 1])).astype(BF16))
            if g == ATT_GROUP - 1:
                rows = pl.ds(pl.multiple_of((i0 + u) * tq, tq), tq)
                ot_all = jnp.concatenate(outs, axis=0)
                o_ref[rows, :] = lax.dot_general(eye, ot_all, _NT, preferred_element_type=F32).astype(o_ref.dtype)
                outs = []
        return tuple(pending)

    first = tuple(issue_scores(0, n) for n in range(ahead))
    lax.fori_loop(0, n_iter, q_tiles, first)


def _attention(qt, kp, vt, expert_weights, layer):
    b, n_qt, nq, tq = qt.shape
    t = kp.shape[2]
    w = ATT_GROUP * ATT_HEAD_DIM
    grid = (b, ATT_KV_HEADS)
    c_in, c_out, c_shape = _expert_cast_specs(expert_weights, layer, grid)
    o, *cast = pl.pallas_call(
        _attn_kernel,
        grid=grid,
        in_specs=[pl.BlockSpec((None, n_qt, w, tq), lambda i, h: (i, 0, h, 0)),
                  pl.BlockSpec((None, None, t, LANES), lambda i, h: (i, h, 0, 0)),
                  pl.BlockSpec((None, None, ATT_VT_ROWS, t), lambda i, h: (i, h, 0, 0))] + c_in,
        out_specs=[pl.BlockSpec((None, n_qt * tq, w), lambda i, h: (i, 0, h))] + c_out,
        out_shape=[jax.ShapeDtypeStruct((b, n_qt * tq, nq), BF16)] + c_shape,
        scratch_shapes=[pltpu.VMEM((ATT_SCORE_SLOTS, t, tq), F32)],
        compiler_params=_params("parallel", "parallel"),
        name="gqa_attention",
    )(qt, kp, vt, *expert_weights)
    return o, cast


def _rope_tables(n_tokens, q_gain, k_gain):
    rows_n = n_tokens // GRID_W
    rows = jnp.repeat(jnp.arange(rows_n), GRID_W).astype(F32)
    cols = jnp.tile(jnp.arange(GRID_W), rows_n).astype(F32)
    axis_dims = ATT_HEAD_DIM // 2
    freqs = ROPE_THETA ** (-jnp.arange(0, axis_dims, 2, dtype=F32) / axis_dims)
    ang = jnp.concatenate([rows[:, None] * freqs, cols[:, None] * freqs], axis=-1)
    ang = jnp.concatenate([jnp.zeros((N_META, ang.shape[1]), F32), ang], axis=0)
    cos_e = jnp.tile(jnp.repeat(jnp.cos(ang), 2, axis=1), (1, LANES // ATT_HEAD_DIM))
    sign = jnp.tile(jnp.array([-1.0, 1.0], F32), LANES // 2)
    sin_e = jnp.tile(jnp.repeat(jnp.sin(ang), 2, axis=1), (1, LANES // ATT_HEAD_DIM)) * sign

    def tables(gain, scale):
        g = jnp.tile(gain.astype(F32), LANES // ATT_HEAD_DIM)
        g_partner = g.reshape(-1, 2)[:, ::-1].reshape(-1)
        return cos_e * (g * scale), sin_e * (g_partner * scale)

    return tables(q_gain, ATT_Q_SCALE) + tables(k_gain, 1.0)


def kernel(x, meta_tokens, norm_mix, norm_ffn, norm_final, sc_w_in, sc_conv_w, gla_w_gate_fwd, gla_b_gate_fwd, gla_w_gate_bwd, gla_b_gate_bwd, gla_norm_w, sc_w_out, att_w_in, att_q_norm, att_k_norm, att_w_out, moe_w_group, moe_b_group, moe_w_expert, moe_b_expert, moe_w_gate, moe_w_up, moe_w_down):
    bsz, s_len, d = x.shape
    t = s_len + N_META
    n0, n1 = bsz * t, bsz * s_len
    tm0 = t // 3
    tb0 = 3 * LANES
    tm1 = 4 * LANES
    assert t % tm0 == 0 and tm0 % 16 == 0 and n0 % tb0 == 0 and n1 % tm1 == 0 and s_len % tm1 == 0

    expert_f32 = (moe_w_gate, moe_w_up, moe_w_down)
    meta = jnp.broadcast_to(meta_tokens.astype(x.dtype)[None], (bsz, N_META, d))
    h = jnp.concatenate([meta, x], axis=1).reshape(n0, d)

    nk = GLA_HEADS * GLA_DK
    mix_in = sc_w_in.shape[2]
    mix_pad = -mix_in % MXU_WIDTH
    w_in = jnp.pad(sc_w_in[0], ((0, 0), (0, mix_pad))).astype(BF16)
    u = _norm_matmul(h, norm_mix[0], w_in, tm0, mix_in + mix_pad)
    u3 = u.reshape(bsz, t, mix_in + mix_pad)
    y_a, gate_up0 = _gated_conv(u3, sc_conv_w[0], expert_f32[:2], 0)
    wc = jnp.zeros((LANES, 2 * nk), F32)
    wc = wc.at[:GLA_GATE_RANK, :nk].set(gla_w_gate_fwd[0])
    wc = wc.at[GLA_GATE_RANK:2 * GLA_GATE_RANK, nk:].set(gla_w_gate_bwd[0])
    bc = jnp.concatenate([gla_b_gate_fwd[0], gla_b_gate_bwd[0]]).reshape(1, 2 * nk)
    y_b, down0 = _gla(u3, wc.astype(BF16), bc, gla_norm_w[0].reshape(1, GLA_DV), expert_f32[2:], 0)
    experts0 = gate_up0 + down0
    w_out = sc_w_out[0].astype(BF16)
    router0 = _router_operands(norm_ffn[0], moe_w_group[0], moe_b_group[0], moe_w_expert[0], moe_b_expert[0])
    routed = _proj2_router(y_a.reshape(n0, SC_WIDTH), y_b.reshape(n0, -1), w_out[:SC_WIDTH], w_out[SC_WIDTH:],
                           h, router0, tb0)
    h, u = _moe_layer(routed, experts0, norm_mix[1], att_w_in[0].astype(BF16), tb0)
    u3 = u.reshape(bsz, t, -1)
    qt, kp, vt = _rope(u3, _rope_tables(s_len, att_q_norm[0], att_k_norm[0]))
    o, experts1 = _attention(qt, kp, vt, expert_f32, 1)
    router1 = _router_operands(norm_ffn[1], moe_w_group[1], moe_b_group[1], moe_w_expert[1], moe_b_expert[1])
    routed = _att_out_router(o.reshape(n1, d), att_w_out[0].astype(BF16), h.reshape(bsz, t, d), router1, tm1)
    out = _moe_layer(routed, experts1, norm_final, None, tm1)
    return out.reshape(bsz, s_len, d)
```

```python
import functools

import jax
import jax.numpy as jnp
from jax import lax
from jax.experimental import pallas as pl
from jax.experimental.pallas import tpu as pltpu

F32 = jnp.float32
BF16 = jnp.bfloat16

NORM_EPS = 1e-6
N_META = 16
GRID_W = 64
ROPE_THETA = 10000.0
SC_WIDTH = 512
GLA_HEADS = 4
GLA_DK = 64
GLA_DV = 128
GLA_GATE_RANK = 16
GLA_GATE_TAU = 16.0
GLA_CHUNK = 64
ATT_HEAD_DIM = 64
ATT_HEADS = 16
ATT_KV_HEADS = 4
ATT_GROUP = ATT_HEADS // ATT_KV_HEADS
MOE_GROUPS = 4
MOE_EXPERTS_PER_GROUP = 8
N_EXPERTS = 32
MOE_D_FF = 256

LANES = 128
MXU_WIDTH = 256
MIB = 2 ** 20
VMEM_LIMIT = 56 * MIB

_NT = (((1,), (1,)), ((), ()))
_TN = (((0,), (0,)), ((), ()))


def _params(*semantics):
    return pltpu.CompilerParams(dimension_semantics=semantics, vmem_limit_bytes=VMEM_LIMIT)


def _dot(a, b):
    return jnp.dot(a, b, preferred_element_type=F32)


def _silu(x):
    return x * (1.0 / (1.0 + jnp.exp(-x)))


def _norm_matmul_kernel(h_ref, g_ref, w_ref, o_ref, xn_ref):
    @pl.when(pl.program_id(1) == 0)
    def _():
        x = h_ref[...]
        ms = jnp.mean(x * x, axis=-1, keepdims=True)
        xn_ref[...] = (x * lax.rsqrt(ms + NORM_EPS) * g_ref[...]).astype(BF16)

    o_ref[...] = _dot(xn_ref[...], w_ref[...]).astype(o_ref.dtype)


def _norm_matmul(h, g, w, tm, tn):
    n, d = h.shape
    nout = w.shape[1]
    return pl.pallas_call(
        _norm_matmul_kernel,
        grid=(n // tm, nout // tn),
        in_specs=[pl.BlockSpec((tm, d), lambda i, j: (i, 0)),
                  pl.BlockSpec((1, d), lambda i, j: (0, 0)),
                  pl.BlockSpec((d, tn), lambda i, j: (0, j))],
        out_specs=pl.BlockSpec((tm, tn), lambda i, j: (i, j)),
        out_shape=jax.ShapeDtypeStruct((n, nout), BF16),
        scratch_shapes=[pltpu.VMEM((tm, d), BF16)],
        compiler_params=_params("parallel", "arbitrary"),
        name="norm_matmul",
    )(h, g.reshape(1, d), w)


def _expert_cast_specs(weights, layer, grid_shape):
    n_steps = 1
    for g in grid_shape:
        n_steps *= g
    assert N_EXPERTS % n_steps == 0
    per_step = N_EXPERTS // n_steps

    def step(*idx):
        flat = 0
        for i, g in zip(idx, grid_shape):
            flat = flat * g + i
        return flat

    in_specs = [pl.BlockSpec((None, per_step) + w.shape[2:], lambda *idx: (layer, step(*idx), 0, 0))
                for w in weights]
    out_specs = [pl.BlockSpec((per_step,) + w.shape[2:], lambda *idx: (step(*idx), 0, 0)) for w in weights]
    out_shape = [jax.ShapeDtypeStruct(w.shape[1:], BF16) for w in weights]
    return in_specs, out_specs, out_shape


def _cast_through(in_refs, out_refs):
    for src, dst in zip(in_refs, out_refs):
        dst[...] = src[...].astype(dst.dtype)


def _conv_kernel(ab_ref, ac_ref, ax_ref, w_ref, *rest):
    n_cast = (len(rest) - 2) // 2
    o_ref, z_ref = rest[n_cast], rest[-1]
    _cast_through(rest[:n_cast], rest[n_cast + 1:-1])
    t, c = ab_ref.shape
    z = ac_ref[...].astype(F32) * ax_ref[...].astype(F32)
    z_ref[0:8, :] = jnp.zeros((8, c), F32)
    z_ref[8:8 + t, :] = z
    z_ref[8 + t:16 + t, :] = jnp.zeros((8, c), F32)
    w = w_ref[...]
    conv = w[0:1, :] * z_ref[7:7 + t, :] + w[1:2, :] * z + w[2:3, :] * z_ref[9:9 + t, :]
    o_ref[...] = (ab_ref[...].astype(F32) * conv).astype(o_ref.dtype)


def _gated_conv(u3, conv_w, expert_weights, layer):
    b, t, _ = u3.shape
    nblk = SC_WIDTH // LANES
    grid = (b, nblk)
    c_in, c_out, c_shape = _expert_cast_specs(expert_weights, layer, grid)

    def col(off):
        return pl.BlockSpec((None, t, LANES), lambda i, j: (i, 0, off + j))

    y_a, *cast = pl.pallas_call(
        _conv_kernel,
        grid=grid,
        in_specs=[col(0), col(nblk), col(2 * nblk),
                  pl.BlockSpec((3, LANES), lambda i, j: (0, j))] + c_in,
        out_specs=[pl.BlockSpec((None, t, LANES), lambda i, j: (i, 0, j))] + c_out,
        out_shape=[jax.ShapeDtypeStruct((b, t, SC_WIDTH), BF16)] + c_shape,
        scratch_shapes=[pltpu.VMEM((t + 16, LANES), F32)],
        compiler_params=_params("parallel", "parallel"),
        name="gated_conv",
    )(u3, u3, u3, conv_w, *expert_weights)
    return y_a, cast


GLA_ROW_TILE = 688
GLA_EPI_TILE = 688


def _gla_kernel(q_ref, k_ref, v_ref, r_ref, g_ref, wc_ref, bc_ref, nw_ref, *rest):
    n_cast = (len(rest) - 6) // 2
    y_ref = rest[n_cast]
    la_ref, of_ref, ob_ref, sf_ref, sb_ref = rest[-5:]
    _cast_through(rest[:n_cast], rest[n_cast + 1:-5])
    t, nk = q_ref.shape
    nv = v_ref.shape[1]
    c = GLA_CHUNK
    n_real = (t - N_META) // c

    for i in range(t // GLA_ROW_TILE):
        rows = pl.ds(i * GLA_ROW_TILE, GLA_ROW_TILE)
        pre = _dot(g_ref[rows, :], wc_ref[...]) + bc_ref[...]
        la_ref[rows, :] = (jnp.minimum(pre, 0.0) - jnp.log(1.0 + jnp.exp(-jnp.abs(pre)))) * (1.0 / GLA_GATE_TAU)

    def iota(shape, dim):
        return lax.broadcasted_iota(jnp.int32, shape, dim)

    tri_f = jnp.where(iota((c, c), 0) >= iota((c, c), 1), 1.0, 0.0).astype(BF16)
    tri_b = jnp.where(iota((c, c), 0) <= iota((c, c), 1), 1.0, 0.0).astype(BF16)
    keep_f = (iota((c, nk), 1) & (c - 1)) <= iota((c, nk), 0)
    keep_b = (iota((c, nk), 1) & (c - 1)) > iota((c, nk), 0)
    bd_k = (iota((nk, nk), 0) >> 6) == (iota((nk, nk), 1) >> 6)
    bd_v = (iota((nk, nv), 0) >> 6) == (iota((nk, nv), 1) >> 7)
    bd_s = (iota((nv, nk), 0) >> 7) == (iota((nv, nk), 1) >> 6)
    valid0 = iota((c, 1), 0) < N_META

    def cum_decay(rows, valid, tri, la_col):
        la = la_ref[rows, la_col:la_col + nk]
        if valid is not None:
            la = jnp.where(valid, la, 0.0)
        hi = la.astype(BF16)
        rem = la - hi.astype(F32)
        mid = rem.astype(BF16)
        lo = (rem - mid.astype(F32)).astype(BF16)
        return _dot(tri, hi) + _dot(tri, mid) + _dot(tri, lo)

    def chunk(rows, valid, cum, keep, tot_row, s_ref, o_ref, out_rows, n_out):
        q = q_ref[rows, :].astype(F32) * (GLA_DK ** -0.5)
        k = k_ref[rows, :].astype(F32)
        v = v_ref[rows, :].astype(F32)
        if valid is not None:
            q = jnp.where(valid, q, 0.0)
            k = jnp.where(valid, k, 0.0)
            v = jnp.where(valid, v, 0.0)
        tot = cum[tot_row:tot_row + 1, :]
        q_dec = (q * jnp.exp(cum)).astype(BF16)
        k_inv = k * jnp.exp(-cum)
        k_end = (k * jnp.exp(tot - cum)).astype(BF16)
        kbd = jnp.where(bd_k, jnp.concatenate([k_inv] * GLA_HEADS, axis=0), 0.0).astype(BF16)
        p = lax.dot_general(q_dec, kbd, _NT, preferred_element_type=F32)
        s_old = s_ref[...]
        o_inter = lax.dot_general(q_dec, s_old.astype(BF16), _NT, preferred_element_type=F32)
        kvt = lax.dot_general(v.astype(BF16), k_end, _TN, preferred_element_type=F32)
        yield
        p = jnp.where(keep, p, 0.0).astype(BF16)
        vbd = jnp.where(bd_v, jnp.concatenate([v] * GLA_HEADS, axis=0), 0.0).astype(BF16)
        o = _dot(p, vbd) + o_inter
        s_ref[...] = jnp.exp(tot) * s_old + jnp.where(bd_s, kvt, 0.0)
        o_ref[out_rows, :] = o[:n_out]

    def lockstep(*gens):
        live = list(gens)
        while live:
            nxt = []
            for gen in live:
                try:
                    next(gen)
                    nxt.append(gen)
                except StopIteration:
                    pass
            live = nxt

    sf_ref[...] = jnp.zeros(sf_ref.shape, F32)
    sb_ref[...] = jnp.zeros(sb_ref.shape, F32)
    meta_rows = pl.ds(0, c)
    lockstep(chunk(meta_rows, valid0, cum_decay(meta_rows, valid0, tri_f, 0), keep_f, c - 1,
                   sf_ref, of_ref, pl.ds(0, N_META), N_META))

    def rows_fwd(i):
        return pl.ds(pl.multiple_of(N_META + i * c, 16), c)

    def rows_bwd(i):
        return pl.ds(pl.multiple_of(N_META + (n_real - 1 - i) * c, 16), c)

    def body(i, cums):
        cum_f, cum_b = cums
        nxt = jnp.minimum(i + 1, n_real - 1)
        cums = (cum_decay(rows_fwd(nxt), None, tri_f, 0), cum_decay(rows_bwd(nxt), None, tri_b, nk))
        lockstep(chunk(rows_fwd(i), None, cum_f, keep_f, c - 1, sf_ref, of_ref, rows_fwd(i), c),
                 chunk(rows_bwd(i), None, cum_b, keep_b, 0, sb_ref, ob_ref, rows_bwd(i), c))
        return cums

    first = (cum_decay(rows_fwd(0), None, tri_f, 0), cum_decay(rows_bwd(0), None, tri_b, nk))
    lax.fori_loop(0, n_real, body, first, unroll=4)
    lockstep(chunk(meta_rows, valid0, cum_decay(meta_rows, valid0, tri_b, nk), keep_b, 0,
                   sb_ref, ob_ref, pl.ds(0, N_META), N_META))

    def epilogue(i, carry):
        rows = pl.ds(pl.multiple_of(i * GLA_EPI_TILE, 16), GLA_EPI_TILE)
        o = of_ref[rows, :] + ob_ref[rows, :]
        r = r_ref[rows, :].astype(F32)
        for h in range(GLA_HEADS):
            cols = slice(h * GLA_DV, (h + 1) * GLA_DV)
            oh = o[:, cols]
            ms = jnp.mean(oh * oh, axis=-1, keepdims=True)
            yh = oh * lax.rsqrt(ms + NORM_EPS) * nw_ref[...] * _silu(r[:, cols])
            y_ref[rows, cols] = yh.astype(y_ref.dtype)
        return carry

    lax.fori_loop(0, t // GLA_EPI_TILE, epilogue, 0)


def _gla(u3, wc, bc, norm_w, expert_weights, layer):
    b, t, _ = u3.shape
    nk = GLA_HEADS * GLA_DK
    nv = GLA_HEADS * GLA_DV
    c_in, c_out, c_shape = _expert_cast_specs(expert_weights, layer, (b,))
    q_blk = 3 * SC_WIDTH // nk
    v_blk = (3 * SC_WIDTH + 2 * nk) // nv
    g_blk = (3 * SC_WIDTH + 2 * nk + 2 * nv) // LANES
    full = lambda shape: pl.BlockSpec(shape, lambda i: (0, 0))
    y_b, *cast = pl.pallas_call(
        _gla_kernel,
        grid=(b,),
        in_specs=[pl.BlockSpec((None, t, nk), lambda i: (i, 0, q_blk)),
                  pl.BlockSpec((None, t, nk), lambda i: (i, 0, q_blk + 1)),
                  pl.BlockSpec((None, t, nv), lambda i: (i, 0, v_blk)),
                  pl.BlockSpec((None, t, nv), lambda i: (i, 0, v_blk + 1)),
                  pl.BlockSpec((None, t, LANES), lambda i: (i, 0, g_blk)),
                  full((LANES, 2 * nk)), full((1, 2 * nk)), full((1, GLA_DV))] + c_in,
        out_specs=[pl.BlockSpec((None, t, nv), lambda i: (i, 0, 0))] + c_out,
        out_shape=[jax.ShapeDtypeStruct((b, t, nv), BF16)] + c_shape,
        scratch_shapes=[pltpu.VMEM((t, 2 * nk), F32),
                        pltpu.VMEM((t, nv), F32), pltpu.VMEM((t, nv), F32),
                        pltpu.VMEM((nv, nk), F32), pltpu.VMEM((nv, nk), F32)],
        compiler_params=_params("parallel"),
        name="gla",
    )(u3, u3, u3, u3, u3, wc, bc, norm_w, *expert_weights)
    return y_b, cast


GROUP_LANE0 = N_EXPERTS


def _route(x, g_ref, whl_ref, b_ref, xn_ref, gates_ref, info_ref, inforow_ref, count_ref):
    ms = jnp.mean(x * x, axis=-1, keepdims=True)
    xn = x * lax.rsqrt(ms + NORM_EPS) * g_ref[...]
    xhi = xn.astype(BF16)
    xn_ref[...] = xhi
    xlo = (xn - xhi.astype(F32)).astype(BF16)
    both = _dot(xhi, whl_ref[...])
    logits = both[:, :LANES] + both[:, LANES:] + _dot(xlo, whl_ref[:, :LANES]) + b_ref[...]

    lane = lax.broadcasted_iota(jnp.int32, logits.shape, 1)
    neg = -jnp.inf
    big = jnp.int32(LANES)
    is_group = (lane >> 2) == (GROUP_LANE0 >> 2)
    gl = jnp.where(is_group, logits, neg)
    gmax = jnp.max(gl, axis=-1, keepdims=True)
    g_p = 1.0 / jnp.sum(jnp.exp(gl - gmax), axis=-1, keepdims=True)
    g_idx = jnp.min(jnp.where(gl == gmax, lane, big), axis=-1, keepdims=True) - GROUP_LANE0
    el = jnp.where((lane >> 3) == g_idx, logits, neg)
    m1 = jnp.max(el, axis=-1, keepdims=True)
    i1 = jnp.min(jnp.where(el == m1, lane, big), axis=-1, keepdims=True)
    el2 = jnp.where(lane == i1, neg, el)
    m2 = jnp.max(el2, axis=-1, keepdims=True)
    i2 = jnp.min(jnp.where(el2 == m2, lane, big), axis=-1, keepdims=True)
    e2 = jnp.exp(m2 - m1)
    w1 = g_p / (1.0 + e2)
    w2 = g_p * e2 / (1.0 + e2)
    gates_ref[...] = jnp.where(lane == i1, w1, 0.0) + jnp.where(lane == i2, w2, 0.0)

    tm = x.shape[0]
    onehot = jnp.where(lane == g_idx, 1.0, 0.0)
    r_i = lax.broadcasted_iota(jnp.int32, (tm, tm), 0)
    c_i = lax.broadcasted_iota(jnp.int32, (tm, tm), 1)
    before = jnp.where(r_i > c_i, 1.0, 0.0).astype(BF16)
    prefix = _dot(before, onehot.astype(BF16))
    rank = jnp.sum(jnp.where(lane == g_idx, prefix, 0.0), axis=-1, keepdims=True).astype(jnp.int32)
    info_ref[...] = jnp.where(lane == 0, g_idx, jnp.where(lane == 1, rank, 0))
    eye = r_i == c_i
    g_row = jnp.sum(jnp.where(eye, g_idx, 0), axis=0, keepdims=True)
    rank_row = jnp.sum(jnp.where(eye, rank, 0), axis=0, keepdims=True)
    sub = lax.broadcasted_iota(jnp.int32, inforow_ref.shape, 0)
    inforow_ref[...] = jnp.where(sub == 0, g_row, jnp.where(sub == 1, rank_row, 0))
    counts = jnp.sum(onehot, axis=0, keepdims=True).astype(jnp.int32)
    count_ref[...] = jnp.broadcast_to(counts, count_ref.shape)


def _router_operands(norm_g, w_group, b_group, w_expert, b_expert):
    d = w_group.shape[0]
    pad = LANES - N_EXPERTS - MOE_GROUPS
    wr = jnp.concatenate([w_expert, w_group, jnp.zeros((d, pad), F32)], axis=1)
    whi = wr.astype(BF16)
    wlo = (wr - whi.astype(F32)).astype(BF16)
    bias = jnp.concatenate([b_expert, b_group, jnp.zeros((pad,), F32)]).reshape(1, LANES)
    return norm_g.reshape(1, d), jnp.concatenate([whi, wlo], axis=1), bias


def _router_specs(n, d, tm):
    nb = n // tm
    const = lambda shape: pl.BlockSpec(shape, lambda i: (0,) * len(shape))
    in_specs = [const((1, d)), const((d, 2 * LANES)), const((1, LANES))]
    out_specs = [pl.BlockSpec((tm, d), lambda i: (i, 0)),
                 pl.BlockSpec((tm, d), lambda i: (i, 0)),
                 pl.BlockSpec((tm, LANES), lambda i: (i, 0)),
                 pl.BlockSpec((tm, LANES), lambda i: (i, 0)),
                 pl.BlockSpec((None, 8, tm), lambda i: (i, 0, 0)),
                 pl.BlockSpec((None, 8, LANES), lambda i: (i, 0, 0))]
    out_shape = [jax.ShapeDtypeStruct((n, d), F32),
                 jax.ShapeDtypeStruct((n, d), BF16),
                 jax.ShapeDtypeStruct((n, LANES), F32),
                 jax.ShapeDtypeStruct((n, LANES), jnp.int32),
                 jax.ShapeDtypeStruct((nb, 8, tm), jnp.int32),
                 jax.ShapeDtypeStruct((nb, 8, LANES), jnp.int32)]
    return in_specs, out_specs, out_shape


def _proj2_router_kernel(a_ref, b_ref, wa_ref, wb_ref, res_ref, g_ref, whl_ref, bias_ref,
                         h_ref, *route_out):
    x = res_ref[...] + _dot(a_ref[...], wa_ref[...]) + _dot(b_ref[...], wb_ref[...])
    h_ref[...] = x
    _route(x, g_ref, whl_ref, bias_ref, *route_out)


def _proj2_router(a, b, wa, wb, res, router_ops, tm):
    n, d = res.shape
    ka, kb = a.shape[1], b.shape[1]
    r_in, out_specs, out_shape = _router_specs(n, d, tm)
    return pl.pallas_call(
        _proj2_router_kernel,
        grid=(n // tm,),
        in_specs=[pl.BlockSpec((tm, ka), lambda i: (i, 0)),
                  pl.BlockSpec((tm, kb), lambda i: (i, 0)),
                  pl.BlockSpec((ka, d), lambda i: (0, 0)),
                  pl.BlockSpec((kb, d), lambda i: (0, 0)),
                  pl.BlockSpec((tm, d), lambda i: (i, 0))] + r_in,
        out_specs=out_specs,
        out_shape=out_shape,
        compiler_params=_params("parallel"),
        name="proj2_router",
    )(a, b, wa, wb, res, *router_ops)


def _att_out_router_kernel(o_ref, w_ref, hres_ref, g_ref, whl_ref, bias_ref, h_ref, *route_out):
    tm = o_ref.shape[0]
    tiles = (hres_ref.shape[0] - N_META) // tm
    r0 = pl.multiple_of(N_META + lax.rem(pl.program_id(0), tiles) * tm, 16)
    x = hres_ref[pl.ds(r0, tm), :] + _dot(o_ref[...], w_ref[...])
    h_ref[...] = x
    _route(x, g_ref, whl_ref, bias_ref, *route_out)


def _att_out_router(o, w, h3, router_ops, tm):
    b, t, d = h3.shape
    n, nq = o.shape
    tiles = (t - N_META) // tm
    r_in, out_specs, out_shape = _router_specs(n, d, tm)
    return pl.pallas_call(
        _att_out_router_kernel,
        grid=(n // tm,),
        in_specs=[pl.BlockSpec((tm, nq), lambda i: (i, 0)),
                  pl.BlockSpec((nq, d), lambda i: (0, 0)),
                  pl.BlockSpec((None, t, d), lambda i: (i // tiles, 0, 0))] + r_in,
        out_specs=out_specs,
        out_shape=out_shape,
        compiler_params=_params("parallel"),
        name="att_out_router",
    )(o, w, h3, *router_ops)


MOE_TILE = 512
SEG_ALIGN = 16
STRIP = 32


def _pow2_sizes(max_rows, min_rows):
    sizes, s = [], min_rows
    while s <= max_rows:
        sizes.append(s)
        s *= 2
    return sizes[::-1]


def _strip_copies(n_rows, sizes, make_copy):
    pos = 0
    for sz in sizes:
        bit = n_rows & sz
        yield bit, make_copy(pos, sz)
        pos = pos + bit


def _slot_key(group, rank, boff_ref, b):
    key = rank
    for g in range(MOE_GROUPS):
        key = key + jnp.where(group == g, boff_ref[MOE_GROUPS * b + g], 0)
    return key


def _start_all(copies):
    for bit, cs in copies:
        @pl.when(bit != 0)
        def _():
            for c in cs:
                c.start()


def _wait_all(copies):
    for bit, cs in copies:
        @pl.when(bit != 0)
        def _():
            for c in cs:
                c.wait()


def _dispatch_kernel(off_ref, cpad_ref, boff_ref, tail_ref, x_ref, gates_ref, inforow_ref,
                     xs_hbm, gs_hbm, xstg, gstg, zx, zg, sem):
    b = pl.program_id(0)
    last = pl.num_programs(0) - 1
    tb = x_ref.shape[0]
    ks = xstg.shape[1]
    cur = lax.rem(b, 2)
    key = _slot_key(inforow_ref[0:1, :], inforow_ref[1:2, :], boff_ref, b)
    slot = lax.broadcasted_iota(jnp.int32, (ks, tb), 0)
    perm = jnp.where(slot == key, 1.0, 0.0).astype(BF16)
    xstg[cur] = _dot(perm, x_ref[...]).astype(BF16)
    gt = gates_ref[...]
    hi = gt.astype(BF16)
    lo = (gt - hi.astype(F32)).astype(BF16)
    both = _dot(perm, jnp.concatenate([hi, lo], axis=1))
    gstg[cur] = both[:, :LANES] + both[:, LANES:]

    sizes = _pow2_sizes(tb, SEG_ALIGN)

    def copies(blk, sl):
        for g in range(MOE_GROUPS):
            src0 = boff_ref[MOE_GROUPS * blk + g]
            dst0 = off_ref[MOE_GROUPS * blk + g]

            def make(pos, sz, src0=src0, dst0=dst0):
                src = pl.ds(pl.multiple_of(src0 + pos, SEG_ALIGN), sz)
                dst = pl.ds(pl.multiple_of(dst0 + pos, SEG_ALIGN), sz)
                return (pltpu.make_async_copy(xstg.at[sl, src, :], xs_hbm.at[dst, :], sem.at[sl, 0]),
                        pltpu.make_async_copy(gstg.at[sl, src, :], gs_hbm.at[dst, :], sem.at[sl, 1]))

            yield from _strip_copies(cpad_ref[MOE_GROUPS * blk + g], sizes, make)

    _start_all(copies(b, cur))

    @pl.when(b > 0)
    def _():
        _wait_all(copies(jnp.maximum(b - 1, 0), 1 - cur))

    @pl.when(b == last)
    def _():
        _wait_all(copies(b, cur))
        zx[...] = jnp.zeros(zx.shape, BF16)
        zg[...] = jnp.zeros(zg.shape, F32)
        tail_sizes = _pow2_sizes(MOE_TILE // 2, SEG_ALIGN)

        def tails():
            for g in range(MOE_GROUPS):
                dst0 = tail_ref[g]

                def make(pos, sz, dst0=dst0):
                    dst = pl.ds(pl.multiple_of(dst0 + pos, SEG_ALIGN), sz)
                    return (pltpu.make_async_copy(zx.at[pl.ds(0, sz), :], xs_hbm.at[dst, :], sem.at[0, 0]),
                            pltpu.make_async_copy(zg.at[pl.ds(0, sz), :], gs_hbm.at[dst, :], sem.at[0, 1]))

                yield from _strip_copies(tail_ref[MOE_GROUPS + g], tail_sizes, make)

        _start_all(tails())
        _wait_all(tails())

        def spare(i):
            dst = pl.ds(pl.multiple_of(i * MOE_TILE, MOE_TILE), MOE_TILE)
            return (pltpu.make_async_copy(zx, xs_hbm.at[dst, :], sem.at[0, 0]),
                    pltpu.make_async_copy(zg, gs_hbm.at[dst, :], sem.at[0, 1]))

        first_spare = tail_ref[2 * MOE_GROUPS]
        n_tiles = xs_hbm.shape[0] // MOE_TILE

        @pl.loop(first_spare, n_tiles)
        def _(i):
            cx, cg = spare(i)
            cx.start()
            cg.start()

        @pl.loop(first_spare, n_tiles)
        def _(i):
            cx, cg = spare(i)
            cx.wait()
            cg.wait()


def _dispatch(xn, gates, inforow, off, cpad, boff, tail, n_sorted, tb, ks):
    n, d = xn.shape
    nb = n // tb
    grid_spec = pltpu.PrefetchScalarGridSpec(
        num_scalar_prefetch=4,
        grid=(nb,),
        in_specs=[pl.BlockSpec((tb, d), lambda i, *_: (i, 0)),
                  pl.BlockSpec((tb, LANES), lambda i, *_: (i, 0)),
                  pl.BlockSpec((None, 8, tb), lambda i, *_: (i, 0, 0))],
        out_specs=[pl.BlockSpec(memory_space=pl.ANY), pl.BlockSpec(memory_space=pl.ANY)],
        scratch_shapes=[pltpu.VMEM((2, ks, d), BF16), pltpu.VMEM((2, ks, LANES), F32),
                        pltpu.VMEM((MOE_TILE, d), BF16), pltpu.VMEM((MOE_TILE, LANES), F32),
                        pltpu.SemaphoreType.DMA((2, 2))])
    return pl.pallas_call(
        _dispatch_kernel,
        grid_spec=grid_spec,
        out_shape=[jax.ShapeDtypeStruct((n_sorted, d), BF16),
                   jax.ShapeDtypeStruct((n_sorted, LANES), F32)],
        compiler_params=_params("arbitrary"),
        name="moe_dispatch",
    )(off, cpad, boff, tail, xn, gates, inforow)


def _group_mlp_kernel(tg_ref, nused_ref, x_ref, gs_ref, wg_ref, wu_ref, wd_ref, y_ref):
    i = pl.program_id(0)

    @pl.when(i < nused_ref[0])
    def _():
        x = x_ref[...]
        gs = gs_ref[...]
        lane = lax.broadcasted_iota(jnp.int32, gs.shape, 1)
        lane0 = tg_ref[i] * MOE_EXPERTS_PER_GROUP
        acc = jnp.zeros(y_ref.shape, F32)
        nxt = (_dot(x, wg_ref[0]), _dot(x, wu_ref[0]))
        for e in range(MOE_EXPERTS_PER_GROUP):
            hg, hu = nxt
            if e + 1 < MOE_EXPERTS_PER_GROUP:
                nxt = (_dot(x, wg_ref[e + 1]), _dot(x, wu_ref[e + 1]))
            gate = jnp.sum(jnp.where(lane == lane0 + e, gs, 0.0), axis=-1, keepdims=True)
            acc = acc + _dot((_silu(hg) * hu * gate).astype(BF16), wd_ref[e])
        y_ref[...] = acc.astype(y_ref.dtype)

    @pl.when(i >= nused_ref[0])
    def _():
        y_ref[...] = jnp.zeros(y_ref.shape, y_ref.dtype)


def _group_mlp(xs, gs, wg, wu, wd, tile_group, nused):
    n_sorted, d = xs.shape
    ne = MOE_EXPERTS_PER_GROUP

    def rows(i, tg, nu):
        return (jnp.minimum(i, nu[0] - 1), 0)

    def group(i, tg, nu):
        return (tg[i], 0, 0)

    grid_spec = pltpu.PrefetchScalarGridSpec(
        num_scalar_prefetch=2,
        grid=(n_sorted // MOE_TILE,),
        in_specs=[pl.BlockSpec((MOE_TILE, d), rows),
                  pl.BlockSpec((MOE_TILE, LANES), rows),
                  pl.BlockSpec((ne, d, MOE_D_FF), group),
                  pl.BlockSpec((ne, d, MOE_D_FF), group),
                  pl.BlockSpec((ne, MOE_D_FF, d), group)],
        out_specs=pl.BlockSpec((MOE_TILE, d), lambda i, tg, nu: (i, 0)))
    return pl.pallas_call(
        _group_mlp_kernel,
        grid_spec=grid_spec,
        out_shape=jax.ShapeDtypeStruct((n_sorted, d), BF16),
        compiler_params=_params("arbitrary"),
        name="moe_experts",
    )(tile_group, nused, xs, gs, wg, wu, wd)


def _combine_kernel(off_ref, cpad_ref, boff_ref, info_ref, res_ref, g_ref, *rest, project):
    if project:
        w_ref, y_hbm, o_ref, u_ref, ybuf, sem = rest
    else:
        y_hbm, o_ref, ybuf, sem = rest
    b = pl.program_id(0)
    last = pl.num_programs(0) - 1
    tb = res_ref.shape[0]
    ks = ybuf.shape[1]
    cur = lax.rem(b, 2)
    sizes = _pow2_sizes(tb, STRIP)

    def copies(blk, sl):
        for g in range(MOE_GROUPS):
            src0 = off_ref[MOE_GROUPS * blk + g]
            dst0 = boff_ref[MOE_GROUPS * blk + g]

            def make(pos, sz, src0=src0, dst0=dst0):
                src = pl.ds(pl.multiple_of(src0 + pos, SEG_ALIGN), sz)
                dst = pl.ds(pl.multiple_of(dst0 + pos, STRIP), sz)
                return (pltpu.make_async_copy(y_hbm.at[src, :], ybuf.at[sl, dst, :], sem.at[sl]),)

            n_rows = (cpad_ref[MOE_GROUPS * blk + g] + (STRIP - 1)) & (-STRIP)
            yield from _strip_copies(n_rows, sizes, make)

    @pl.when(b == 0)
    def _():
        ybuf[...] = jnp.zeros(ybuf.shape, ybuf.dtype)
        _start_all(copies(b, cur))

    @pl.when(b < last)
    def _():
        _start_all(copies(jnp.minimum(b + 1, last), 1 - cur))

    _wait_all(copies(b, cur))

    key = _slot_key(info_ref[:, 0:1], info_ref[:, 1:2], boff_ref, b)
    slot = lax.broadcasted_iota(jnp.int32, (tb, ks), 1)
    perm = jnp.where(slot == key, 1.0, 0.0).astype(BF16)
    y = res_ref[...] + _dot(perm, ybuf[cur])
    ms = jnp.mean(y * y, axis=-1, keepdims=True)
    yn = y * lax.rsqrt(ms + NORM_EPS) * g_ref[...]
    if project:
        o_ref[...] = y
        u_ref[...] = _dot(yn.astype(BF16), w_ref[...]).astype(u_ref.dtype)
    else:
        o_ref[...] = yn


def _combine(ys, info, res, norm_g, w_next, off, cpad, boff, tb, ks):
    n, d = res.shape
    project = w_next is not None
    row = lambda width: pl.BlockSpec((tb, width), lambda i, *_: (i, 0))
    in_specs = [row(LANES), row(d), pl.BlockSpec((1, d), lambda i, *_: (0, 0))]
    args = [info, res, norm_g.reshape(1, d)]
    out_specs, out_shape = [row(d)], [jax.ShapeDtypeStruct((n, d), F32)]
    if project:
        nout = w_next.shape[1]
        in_specs.append(pl.BlockSpec((d, nout), lambda i, *_: (0, 0)))
        args.append(w_next)
        out_specs.append(row(nout))
        out_shape.append(jax.ShapeDtypeStruct((n, nout), BF16))
    grid_spec = pltpu.PrefetchScalarGridSpec(
        num_scalar_prefetch=3,
        grid=(n // tb,),
        in_specs=in_specs + [pl.BlockSpec(memory_space=pl.ANY)],
        out_specs=out_specs,
        scratch_shapes=[pltpu.VMEM((2, ks, d), BF16), pltpu.SemaphoreType.DMA((2,))])
    out = pl.pallas_call(
        functools.partial(_combine_kernel, project=project),
        grid_spec=grid_spec,
        out_shape=out_shape,
        compiler_params=_params("arbitrary"),
        name="moe_combine",
    )(off, cpad, boff, *args, ys)
    return tuple(out) if project else out[0]


def _round_up(x, m):
    return (x + m - 1) // m * m


def _moe_layer(routed, expert_weights, norm_g, w_next, tb):
    h, xn, gates, info, inforow, counts = routed
    w_gate, w_up, w_down = expert_weights
    n, d = h.shape
    nb = n // tb

    cnt = counts[:, 0, :MOE_GROUPS]
    cpad = _round_up(cnt, SEG_ALIGN)
    strip = _round_up(cnt, STRIP)
    used = jnp.sum(cpad, axis=0)
    region = _round_up(used, MOE_TILE)
    region_start = jnp.cumsum(region) - region
    off = region_start[None, :] + jnp.cumsum(cpad, axis=0) - cpad
    boff = jnp.cumsum(strip, axis=1) - strip
    nused = (jnp.sum(region) // MOE_TILE).reshape(1)
    tail = jnp.concatenate([region_start + used, region - used, nused]).astype(jnp.int32)
    n_tiles = -(-(n + nb * MOE_GROUPS * (SEG_ALIGN - 1) + MOE_GROUPS * (MOE_TILE - 1)) // MOE_TILE) + 1
    tile_start = jnp.arange(n_tiles, dtype=jnp.int32) * MOE_TILE
    region_end = region_start + region
    tile_group = jnp.minimum(jnp.sum(tile_start[:, None] >= region_end[None, :], axis=1), MOE_GROUPS - 1)
    ks =tb + MOE_GROUPS * STRIP

    flat = lambda a: a.reshape(-1).astype(jnp.int32)
    xs, gs = _dispatch(xn, gates, inforow, flat(off), flat(cpad), flat(boff), tail,
                       n_tiles * MOE_TILE, tb, ks)
    ys = _group_mlp(xs, gs, w_gate, w_up, w_down,
                    tile_group.astype(jnp.int32), nused.astype(jnp.int32))
    return _combine(ys, info, h, norm_g, w_next, flat(off), flat(cpad), flat(boff), tb, ks)


ATT_Q_TILE = 256
ATT_TILES_PER_ITER = 2
ATT_SCORES_AHEAD = 2
ATT_SCORE_SLOTS = 4
ATT_VT_ROWS = ATT_HEAD_DIM + 16
ATT_Q_SCALE = ATT_HEAD_DIM ** -0.5 * 1.4426950408889634


def _eye(n):
    return jnp.where(lax.broadcasted_iota(jnp.int32, (n, n), 0) == lax.broadcasted_iota(jnp.int32, (n, n), 1),
                     1.0, 0.0).astype(BF16)


def _transpose_bf16(x, eye):
    return lax.dot_general(eye, x, _NT, preferred_element_type=F32).astype(BF16)


def _rope_kernel(q_ref, k_ref, v_ref, aq_ref, bq_ref, ak_ref, bk_ref, qt_ref, kp_ref, vt_ref):
    t = q_ref.shape[0]
    tq = ATT_Q_TILE
    dh = ATT_HEAD_DIM
    gi = lax.broadcasted_iota(jnp.int32, (LANES, LANES), 0) >> 6
    gj = lax.broadcasted_iota(jnp.int32, (LANES, LANES), 1) >> 6
    gmat = jnp.where(gi == gj, 1.0, 0.0).astype(BF16)
    eye_q = _eye(LANES)

    def norm_rope(x, a, b):
        lane = lax.broadcasted_iota(jnp.int32, x.shape, 1)
        sq = x * x
        hi = sq.astype(BF16)
        lo = (sq - hi.astype(F32)).astype(BF16)
        ms = (_dot(hi, gmat) + _dot(lo, gmat)) * (1.0 / dh)
        partner = jnp.where((lane & 1) == 0, pltpu.roll(x, LANES - 1, 1), pltpu.roll(x, 1, 1))
        return lax.rsqrt(ms + NORM_EPS) * (x * a + partner * b)

    for j in range(q_ref.shape[1] // LANES):
        cols = slice(j * LANES, (j + 1) * LANES)
        y = norm_rope(q_ref[N_META:, cols].astype(F32), aq_ref[N_META:, :], bq_ref[N_META:, :])
        yt = _transpose_bf16(y.astype(BF16), eye_q)
        for i in range((t - N_META) // tq):
            qt_ref[i, cols, :] = yt[:, i * tq:(i + 1) * tq]

    low = lax.broadcasted_iota(jnp.int32, (t, LANES), 1) < dh
    for j in range(k_ref.shape[1] // LANES):
        cols = slice(j * LANES, (j + 1) * LANES)
        y = norm_rope(k_ref[:, cols].astype(F32), ak_ref[...], bk_ref[...])
        kp_ref[2 * j] = jnp.where(low, y, 0.0).astype(BF16)
        kp_ref[2 * j + 1] = jnp.where(low, pltpu.roll(y, dh, 1), 0.0).astype(BF16)

    n_real = t - N_META
    eye_v = _eye(v_ref.shape[1])
    vt_real = _transpose_bf16(v_ref[N_META:, :], eye_v)
    vt_meta = _transpose_bf16(v_ref[0:N_META, :], eye_v)
    for h in range(ATT_KV_HEADS):
        vt_ref[h, 0:dh, 0:n_real] = vt_real[h * dh:(h + 1) * dh, :]
        vt_ref[h, 0:dh, n_real:t] = vt_meta[h * dh:(h + 1) * dh, :]
        extra = lax.broadcasted_iota(jnp.int32, (ATT_VT_ROWS - dh, t), 0)
        vt_ref[h, dh:ATT_VT_ROWS, :] = jnp.where(extra == 0, 1.0, 0.0).astype(BF16)


def _rope(u3, tables):
    b, t, _ = u3.shape
    nq = ATT_HEADS * ATT_HEAD_DIM
    nkv = ATT_KV_HEADS * ATT_HEAD_DIM
    n_qt = (t - N_META) // ATT_Q_TILE
    full = lambda shape: pl.BlockSpec(shape, lambda i: (0, 0))
    return pl.pallas_call(
        _rope_kernel,
        grid=(b,),
        in_specs=[pl.BlockSpec((None, t, nq), lambda i: (i, 0, 0)),
                  pl.BlockSpec((None, t, nkv), lambda i: (i, 0, nq // nkv)),
                  pl.BlockSpec((None, t, nkv), lambda i: (i, 0, nq // nkv + 1)),
                  full((t, LANES)), full((t, LANES)), full((t, LANES)), full((t, LANES))],
        out_specs=[pl.BlockSpec((None, n_qt, nq, ATT_Q_TILE), lambda i: (i, 0, 0, 0)),
                   pl.BlockSpec((None, ATT_KV_HEADS, t, LANES), lambda i: (i, 0, 0, 0)),
                   pl.BlockSpec((None, ATT_KV_HEADS, ATT_VT_ROWS, t), lambda i: (i, 0, 0, 0))],
        out_shape=[jax.ShapeDtypeStruct((b, n_qt, nq, ATT_Q_TILE), BF16),
                   jax.ShapeDtypeStruct((b, ATT_KV_HEADS, t, LANES), BF16),
                   jax.ShapeDtypeStruct((b, ATT_KV_HEADS, ATT_VT_ROWS, t), BF16)],
        compiler_params=_params("parallel"),
        name="qk_norm_rope",
    )(u3, u3, u3, *tables)


def _attn_kernel(qt_ref, k_ref, vt_ref, wg_ref, wu_ref, wd_ref, o_ref, wg_o, wu_o, wd_o, s_ref):
    _cast_through((wg_ref, wu_ref, wd_ref), (wg_o, wu_o, wd_o))
    n_qt, w, tq = qt_ref.shape
    dh = ATT_HEAD_DIM
    n_real = k_ref.shape[0] - N_META
    eye = _eye(tq)
    zpad = jnp.zeros((LANES - dh, tq), BF16)

    def scores(i, g, slot):
        qp = jnp.concatenate([qt_ref[i, g * dh:(g + 1) * dh, :], zpad], axis=0)
        s_r = _dot(k_ref[N_META:, :], qp)
        s_m = _dot(k_ref[0:N_META, :], qp)
        s_ref[slot, 0:n_real, :] = s_r
        s_ref[slot, n_real:, :] = s_m
        return jnp.maximum(jnp.max(s_r, axis=0, keepdims=True), jnp.max(s_m, axis=0, keepdims=True))

    steps = [(u, g) for u in range(ATT_TILES_PER_ITER) for g in range(ATT_GROUP)]
    n_steps = len(steps)
    n_iter = n_qt // ATT_TILES_PER_ITER
    ahead = ATT_SCORES_AHEAD
    n_slots = s_ref.shape[0]
    assert n_steps % n_slots == 0 and ahead < n_slots

    def issue_scores(it, n):
        it2 = jnp.minimum(it + n // n_steps, n_iter - 1)
        u, g = steps[n % n_steps]
        return scores(it2 * ATT_TILES_PER_ITER + u, g, n % n_slots)

    def q_tiles(it, pending):
        i0 = it * ATT_TILES_PER_ITER
        outs = []
        pending = list(pending)
        for n, (u, g) in enumerate(steps):
            m, slot = pending.pop(0), n % n_slots
            pending.append(issue_scores(it, n + ahead))
            p_r = jnp.exp2(s_ref[slot, 0:n_real, :] - m).astype(BF16)
            p_m = jnp.exp2(s_ref[slot, n_real:, :] - m).astype(BF16)
            ot = _dot(vt_ref[:, 0:n_real], p_r) + _dot(vt_ref[:, n_real:], p_m)
            outs.append((ot[0:dh] * (1.0 / ot[dh:dh + 1])).astype(BF16))
            if g == ATT_GROUP - 1:
                rows = pl.ds(pl.multiple_of((i0 + u) * tq, tq), tq)
                ot_all = jnp.concatenate(outs, axis=0)
                o_ref[rows, :] = lax.dot_general(eye, ot_all, _NT, preferred_element_type=F32).astype(o_ref.dtype)
                outs = []
        return tuple(pending)

    first = tuple(issue_scores(0, n) for n in range(ahead))
    lax.fori_loop(0, n_iter, q_tiles, first)


def _attention(qt, kp, vt, expert_weights, layer):
    b, n_qt, nq, tq = qt.shape
    t = kp.shape[2]
    w = ATT_GROUP * ATT_HEAD_DIM
    grid = (b, ATT_KV_HEADS)
    c_in, c_out, c_shape = _expert_cast_specs(expert_weights, layer, grid)
    o, *cast = pl.pallas_call(
        _attn_kernel,
        grid=grid,
        in_specs=[pl.BlockSpec((None, n_qt, w, tq), lambda i, h: (i, 0, h, 0)),
                  pl.BlockSpec((None, None, t, LANES), lambda i, h: (i, h, 0, 0)),
                  pl.BlockSpec((None, None, ATT_VT_ROWS, t), lambda i, h: (i, h, 0, 0))] + c_in,
        out_specs=[pl.BlockSpec((None, n_qt * tq, w), lambda i, h: (i, 0, h))] + c_out,
        out_shape=[jax.ShapeDtypeStruct((b, n_qt * tq, nq), BF16)] + c_shape,
        scratch_shapes=[pltpu.VMEM((ATT_SCORE_SLOTS, t, tq), F32)],
        compiler_params=_params("parallel", "parallel"),
        name="gqa_attention",
    )(qt, kp, vt, *expert_weights)
    return o, cast


def _rope_tables(n_tokens, q_gain, k_gain):
    rows_n = n_tokens // GRID_W
    rows = jnp.repeat(jnp.arange(rows_n), GRID_W).astype(F32)
    cols = jnp.tile(jnp.arange(GRID_W), rows_n).astype(F32)
    axis_dims = ATT_HEAD_DIM // 2
    freqs = ROPE_THETA ** (-jnp.arange(0, axis_dims, 2, dtype=F32) / axis_dims)
    ang = jnp.concatenate([rows[:, None] * freqs, cols[:, None] * freqs], axis=-1)
    ang = jnp.concatenate([jnp.zeros((N_META, ang.shape[1]), F32), ang], axis=0)
    cos_e = jnp.tile(jnp.repeat(jnp.cos(ang), 2, axis=1), (1, LANES // ATT_HEAD_DIM))
    sign = jnp.tile(jnp.array([-1.0, 1.0], F32), LANES // 2)
    sin_e = jnp.tile(jnp.repeat(jnp.sin(ang), 2, axis=1), (1, LANES // ATT_HEAD_DIM)) * sign

    def tables(gain, scale):
        g = jnp.tile(gain.astype(F32), LANES // ATT_HEAD_DIM)
        g_partner = g.reshape(-1, 2)[:, ::-1].reshape(-1)
        return cos_e * (g * scale), sin_e * (g_partner * scale)

    return tables(q_gain, ATT_Q_SCALE) + tables(k_gain, 1.0)


def kernel(x, meta_tokens, norm_mix, norm_ffn, norm_final, sc_w_in, sc_conv_w, gla_w_gate_fwd, gla_b_gate_fwd, gla_w_gate_bwd, gla_b_gate_bwd, gla_norm_w, sc_w_out, att_w_in, att_q_norm, att_k_norm, att_w_out, moe_w_group, moe_b_group, moe_w_expert, moe_b_expert, moe_w_gate, moe_w_up, moe_w_down):
    bsz, s_len, d = x.shape
    t = s_len + N_META
    n0, n1 = bsz * t, bsz * s_len
    tm0 = t // 3
    tb0 = 3 * LANES
    tm1 = 4 * LANES
    assert t % tm0 == 0 and tm0 % 16 == 0 and n0 % tb0 == 0 and n1 % tm1 == 0 and s_len % tm1 == 0

    expert_f32 = (moe_w_gate, moe_w_up, moe_w_down)
    meta = jnp.broadcast_to(meta_tokens.astype(x.dtype)[None], (bsz, N_META, d))
    h = jnp.concatenate([meta, x], axis=1).reshape(n0, d)

    nk = GLA_HEADS * GLA_DK
    mix_in = sc_w_in.shape[2]
    mix_pad = -mix_in % MXU_WIDTH
    w_in = jnp.pad(sc_w_in[0], ((0, 0), (0, mix_pad))).astype(BF16)
    u = _norm_matmul(h, norm_mix[0], w_in, tm0, mix_in + mix_pad)
    u3 = u.reshape(bsz, t, mix_in + mix_pad)
    y_a, gate_up0 = _gated_conv(u3, sc_conv_w[0], expert_f32[:2], 0)
    wc = jnp.zeros((LANES, 2 * nk), F32)
    wc = wc.at[:GLA_GATE_RANK, :nk].set(gla_w_gate_fwd[0])
    wc = wc.at[GLA_GATE_RANK:2 * GLA_GATE_RANK, nk:].set(gla_w_gate_bwd[0])
    bc = jnp.concatenate([gla_b_gate_fwd[0], gla_b_gate_bwd[0]]).reshape(1, 2 * nk)
    y_b, down0 = _gla(u3, wc.astype(BF16), bc, gla_norm_w[0].reshape(1, GLA_DV), expert_f32[2:], 0)
    experts0 = gate_up0 + down0
    w_out = sc_w_out[0].astype(BF16)
    router0 = _router_operands(norm_ffn[0], moe_w_group[0], moe_b_group[0], moe_w_expert[0], moe_b_expert[0])
    routed = _proj2_router(y_a.reshape(n0, SC_WIDTH), y_b.reshape(n0, -1), w_out[:SC_WIDTH], w_out[SC_WIDTH:],
                           h, router0, tb0)
    h, u = _moe_layer(routed, experts0, norm_mix[1], att_w_in[0].astype(BF16), tb0)
    u3 = u.reshape(bsz, t, -1)
    qt, kp, vt = _rope(u3, _rope_tables(s_len, att_q_norm[0], att_k_norm[0]))
    o, experts1 = _attention(qt, kp, vt, expert_f32, 1)
    router1 = _router_operands(norm_ffn[1], moe_w_group[1], moe_b_group[1], moe_w_expert[1], moe_b_expert[1])
    routed = _att_out_router(o.reshape(n1, d), att_w_out[0].astype(BF16), h.reshape(bsz, t, d), router1, tm1)
    out = _moe_layer(routed, experts1, norm_final, None, tm1)
    return out.reshape(bsz, s_len, d)
```

```python
import functools

import jax
import jax.numpy as jnp
from jax import lax
from jax.experimental import pallas as pl
from jax.experimental.pallas import tpu as pltpu

F32 = jnp.float32
BF16 = jnp.bfloat16

NORM_EPS = 1e-6
N_META = 16
GRID_W = 64
ROPE_THETA = 10000.0
SC_WIDTH = 512
GLA_HEADS = 4
GLA_DK = 64
GLA_DV = 128
GLA_GATE_RANK = 16
GLA_GATE_TAU = 16.0
GLA_CHUNK = 64
ATT_HEAD_DIM = 64
ATT_HEADS = 16
ATT_KV_HEADS = 4
ATT_GROUP = ATT_HEADS // ATT_KV_HEADS
MOE_GROUPS = 4
MOE_EXPERTS_PER_GROUP = 8
N_EXPERTS = 32
MOE_D_FF = 256

LANES = 128
MXU_WIDTH = 256
MIB = 2 ** 20
VMEM_LIMIT = 56 * MIB

_NT = (((1,), (1,)), ((), ()))
_TN = (((0,), (0,)), ((), ()))


def _params(*semantics):
    return pltpu.CompilerParams(dimension_semantics=semantics, vmem_limit_bytes=VMEM_LIMIT)


def _dot(a, b):
    return jnp.dot(a, b, preferred_element_type=F32)


def _silu(x):
    return x * (1.0 / (1.0 + jnp.exp(-x)))


def _embed_norm_matmul_kernel(x_ref, meta_ref, g_ref, w_ref, x_any, meta_any, o_ref, h_any, xn_ref, sem):
    b, j = pl.program_id(0), pl.program_id(1)
    tm = xn_ref.shape[0]
    s_len = x_ref.shape[1]

    def copies():
        return (pltpu.make_async_copy(x_any.at[b], h_any.at[b, pl.ds(N_META, s_len)], sem.at[0]),
                pltpu.make_async_copy(meta_any, h_any.at[b, pl.ds(0, N_META)], sem.at[1]))

    @pl.when(j == 0)
    def _():
        for c in copies():
            c.start()

    def norm(x):
        ms = jnp.mean(x * x, axis=-1, keepdims=True)
        return (x * lax.rsqrt(ms + NORM_EPS) * g_ref[...]).astype(BF16)

    @pl.when(j == 0)
    def _():
        xn_ref[:N_META] = norm(meta_ref[...])
        xn_ref[N_META:] = norm(x_ref[0, :tm - N_META])

    @pl.when(j > 0)
    def _():
        start = pl.multiple_of(j * tm - N_META, N_META)
        xn_ref[...] = norm(x_ref[0, pl.ds(start, tm)])

    o_ref[...] = _dot(xn_ref[...], w_ref[...]).astype(o_ref.dtype)

    @pl.when(j == pl.num_programs(1) - 1)
    def _():
        for c in copies():
            c.wait()


def _embed_norm_matmul(x, meta, g, w, tm):
    bsz, s_len, d = x.shape
    t = s_len + N_META
    nout = w.shape[1]
    nj = t // tm
    any_spec = pl.BlockSpec(memory_space=pl.ANY)
    return pl.pallas_call(
        _embed_norm_matmul_kernel,
        grid=(bsz, nj),
        in_specs=[pl.BlockSpec((1, s_len, d), lambda b, j: (b, 0, 0)),
                  pl.BlockSpec((N_META, d), lambda b, j: (0, 0)),
                  pl.BlockSpec((1, d), lambda b, j: (0, 0)),
                  pl.BlockSpec((d, nout), lambda b, j: (0, 0)),
                  any_spec, any_spec],
        out_specs=[pl.BlockSpec((tm, nout), lambda b, j: (b * nj + j, 0)), any_spec],
        out_shape=[jax.ShapeDtypeStruct((bsz * t, nout), BF16), jax.ShapeDtypeStruct((bsz, t, d), x.dtype)],
        scratch_shapes=[pltpu.VMEM((tm, d), BF16), pltpu.SemaphoreType.DMA((2,))],
        compiler_params=_params("arbitrary", "arbitrary"),
        name="norm_matmul",
    )(x, meta, g.reshape(1, d), w, x, meta)


def _expert_cast_specs(weights, layer, grid_shape):
    n_steps = 1
    for g in grid_shape:
        n_steps *= g
    assert N_EXPERTS % n_steps == 0
    per_step = N_EXPERTS // n_steps

    def step(*idx):
        flat = 0
        for i, g in zip(idx, grid_shape):
            flat = flat * g + i
        return flat

    in_specs = [pl.BlockSpec((None, per_step) + w.shape[2:], lambda *idx: (layer, step(*idx), 0, 0))
                for w in weights]
    out_specs = [pl.BlockSpec((per_step,) + w.shape[2:], lambda *idx: (step(*idx), 0, 0)) for w in weights]
    out_shape = [jax.ShapeDtypeStruct(w.shape[1:], BF16) for w in weights]
    return in_specs, out_specs, out_shape


def _cast_through(in_refs, out_refs):
    for src, dst in zip(in_refs, out_refs):
        dst[...] = src[...].astype(dst.dtype)


def _conv_kernel(ab_ref, ac_ref, ax_ref, w_ref, *rest):
    n_cast = (len(rest) - 2) // 2
    o_ref, z_ref = rest[n_cast], rest[-1]
    _cast_through(rest[:n_cast], rest[n_cast + 1:-1])
    t, c = ab_ref.shape
    z = ac_ref[...].astype(F32) * ax_ref[...].astype(F32)
    z_ref[0:8, :] = jnp.zeros((8, c), F32)
    z_ref[8:8 + t, :] = z
    z_ref[8 + t:16 + t, :] = jnp.zeros((8, c), F32)
    w = w_ref[...]
    conv = w[0:1, :] * z_ref[7:7 + t, :] + w[1:2, :] * z + w[2:3, :] * z_ref[9:9 + t, :]
    o_ref[...] = (ab_ref[...].astype(F32) * conv).astype(o_ref.dtype)


def _gated_conv(u3, conv_w, expert_weights, layer):
    b, t, _ = u3.shape
    nblk = SC_WIDTH // LANES
    grid = (b, nblk)
    c_in, c_out, c_shape = _expert_cast_specs(expert_weights, layer, grid)

    def col(off):
        return pl.BlockSpec((None, t, LANES), lambda i, j: (i, 0, off + j))

    y_a, *cast = pl.pallas_call(
        _conv_kernel,
        grid=grid,
        in_specs=[col(0), col(nblk), col(2 * nblk),
                  pl.BlockSpec((3, LANES), lambda i, j: (0, j))] + c_in,
        out_specs=[pl.BlockSpec((None, t, LANES), lambda i, j: (i, 0, j))] + c_out,
        out_shape=[jax.ShapeDtypeStruct((b, t, SC_WIDTH), BF16)] + c_shape,
        scratch_shapes=[pltpu.VMEM((t + 16, LANES), F32)],
        compiler_params=_params("parallel", "parallel"),
        name="gated_conv",
    )(u3, u3, u3, conv_w, *expert_weights)
    return y_a, cast


GLA_ROW_TILE = 688
GLA_EPI_TILE = 688


def _gla_kernel(q_ref, k_ref, v_ref, r_ref, g_ref, wc_ref, bc_ref, nw_ref, *rest):
    n_cast = (len(rest) - 6) // 2
    y_ref = rest[n_cast]
    la_ref, of_ref, ob_ref, sf_ref, sb_ref = rest[-5:]
    _cast_through(rest[:n_cast], rest[n_cast + 1:-5])
    t, nk = q_ref.shape
    nv = v_ref.shape[1]
    c = GLA_CHUNK
    n_real = (t - N_META) // c

    for i in range(t // GLA_ROW_TILE):
        rows = pl.ds(i * GLA_ROW_TILE, GLA_ROW_TILE)
        pre = _dot(g_ref[rows, :], wc_ref[...]) + bc_ref[...]
        la_ref[rows, :] = (jnp.minimum(pre, 0.0) - jnp.log(1.0 + jnp.exp(-jnp.abs(pre)))) * (1.0 / GLA_GATE_TAU)

    def iota(shape, dim):
        return lax.broadcasted_iota(jnp.int32, shape, dim)

    tri_f = jnp.where(iota((c, c), 0) >= iota((c, c), 1), 1.0, 0.0).astype(BF16)
    tri_b = jnp.where(iota((c, c), 0) <= iota((c, c), 1), 1.0, 0.0).astype(BF16)
    keep_f = (iota((c, nk), 1) & (c - 1)) <= iota((c, nk), 0)
    keep_b = (iota((c, nk), 1) & (c - 1)) > iota((c, nk), 0)
    bd_k = (iota((nk, nk), 0) >> 6) == (iota((nk, nk), 1) >> 6)
    bd_v = (iota((nk, nv), 0) >> 6) == (iota((nk, nv), 1) >> 7)
    bd_s = (iota((nv, nk), 0) >> 7) == (iota((nv, nk), 1) >> 6)
    valid0 = iota((c, 1), 0) < N_META

    def cum_decay(rows, valid, tri, la_col):
        la = la_ref[rows, la_col:la_col + nk]
        if valid is not None:
            la = jnp.where(valid, la, 0.0)
        hi = la.astype(BF16)
        rem = la - hi.astype(F32)
        mid = rem.astype(BF16)
        lo = (rem - mid.astype(F32)).astype(BF16)
        return _dot(tri, hi) + _dot(tri, mid) + _dot(tri, lo)

    def chunk(rows, valid, cum, keep, tot_row, s_ref, o_ref, out_rows, n_out):
        q = q_ref[rows, :].astype(F32) * (GLA_DK ** -0.5)
        k = k_ref[rows, :].astype(F32)
        v = v_ref[rows, :].astype(F32)
        if valid is not None:
            q = jnp.where(valid, q, 0.0)
            k = jnp.where(valid, k, 0.0)
            v = jnp.where(valid, v, 0.0)
        tot = cum[tot_row:tot_row + 1, :]
        q_dec = (q * jnp.exp(cum)).astype(BF16)
        k_inv = k * jnp.exp(-cum)
        k_end = (k * jnp.exp(tot - cum)).astype(BF16)
        kbd = jnp.where(bd_k, jnp.concatenate([k_inv] * GLA_HEADS, axis=0), 0.0).astype(BF16)
        p = lax.dot_general(q_dec, kbd, _NT, preferred_element_type=F32)
        s_old = s_ref[...]
        o_inter = lax.dot_general(q_dec, s_old.astype(BF16), _NT, preferred_element_type=F32)
        kvt = lax.dot_general(v.astype(BF16), k_end, _TN, preferred_element_type=F32)
        yield
        p = jnp.where(keep, p, 0.0).astype(BF16)
        vbd = jnp.where(bd_v, jnp.concatenate([v] * GLA_HEADS, axis=0), 0.0).astype(BF16)
        o = _dot(p, vbd) + o_inter
        s_ref[...] = jnp.exp(tot) * s_old + jnp.where(bd_s, kvt, 0.0)
        o_ref[out_rows, :] = o[:n_out]

    def lockstep(*gens):
        live = list(gens)
        while live:
            nxt = []
            for gen in live:
                try:
                    next(gen)
                    nxt.append(gen)
                except StopIteration:
                    pass
            live = nxt

    sf_ref[...] = jnp.zeros(sf_ref.shape, F32)
    sb_ref[...] = jnp.zeros(sb_ref.shape, F32)
    meta_rows = pl.ds(0, c)
    lockstep(chunk(meta_rows, valid0, cum_decay(meta_rows, valid0, tri_f, 0), keep_f, c - 1,
                   sf_ref, of_ref, pl.ds(0, N_META), N_META))

    def rows_fwd(i):
        return pl.ds(pl.multiple_of(N_META + i * c, 16), c)

    def rows_bwd(i):
        return pl.ds(pl.multiple_of(N_META + (n_real - 1 - i) * c, 16), c)

    def body(i, cums):
        cum_f, cum_b = cums
        nxt = jnp.minimum(i + 1, n_real - 1)
        cums = (cum_decay(rows_fwd(nxt), None, tri_f, 0), cum_decay(rows_bwd(nxt), None, tri_b, nk))
        lockstep(chunk(rows_fwd(i), None, cum_f, keep_f, c - 1, sf_ref, of_ref, rows_fwd(i), c),
                 chunk(rows_bwd(i), None, cum_b, keep_b, 0, sb_ref, ob_ref, rows_bwd(i), c))
        return cums

    first = (cum_decay(rows_fwd(0), None, tri_f, 0), cum_decay(rows_bwd(0), None, tri_b, nk))
    lax.fori_loop(0, n_real, body, first, unroll=4)
    lockstep(chunk(meta_rows, valid0, cum_decay(meta_rows, valid0, tri_b, nk), keep_b, 0,
                   sb_ref, ob_ref, pl.ds(0, N_META), N_META))

    def epilogue(i, carry):
        rows = pl.ds(pl.multiple_of(i * GLA_EPI_TILE, 16), GLA_EPI_TILE)
        o = of_ref[rows, :] + ob_ref[rows, :]
        r = r_ref[rows, :].astype(F32)
        for h in range(GLA_HEADS):
            cols = slice(h * GLA_DV, (h + 1) * GLA_DV)
            oh = o[:, cols]
            ms = jnp.mean(oh * oh, axis=-1, keepdims=True)
            yh = oh * lax.rsqrt(ms + NORM_EPS) * nw_ref[...] * _silu(r[:, cols])
            y_ref[rows, cols] = yh.astype(y_ref.dtype)
        return carry

    lax.fori_loop(0, t // GLA_EPI_TILE, epilogue, 0)


def _gla(u3, wc, bc, norm_w, expert_weights, layer):
    b, t, _ = u3.shape
    nk = GLA_HEADS * GLA_DK
    nv = GLA_HEADS * GLA_DV
    c_in, c_out, c_shape = _expert_cast_specs(expert_weights, layer, (b,))
    q_blk = 3 * SC_WIDTH // nk
    v_blk = (3 * SC_WIDTH + 2 * nk) // nv
    g_blk = (3 * SC_WIDTH + 2 * nk + 2 * nv) // LANES
    full = lambda shape: pl.BlockSpec(shape, lambda i: (0, 0))
    y_b, *cast = pl.pallas_call(
        _gla_kernel,
        grid=(b,),
        in_specs=[pl.BlockSpec((None, t, nk), lambda i: (i, 0, q_blk)),
                  pl.BlockSpec((None, t, nk), lambda i: (i, 0, q_blk + 1)),
                  pl.BlockSpec((None, t, nv), lambda i: (i, 0, v_blk)),
                  pl.BlockSpec((None, t, nv), lambda i: (i, 0, v_blk + 1)),
                  pl.BlockSpec((None, t, LANES), lambda i: (i, 0, g_blk)),
                  full((LANES, 2 * nk)), full((1, 2 * nk)), full((1, GLA_DV))] + c_in,
        out_specs=[pl.BlockSpec((None, t, nv), lambda i: (i, 0, 0))] + c_out,
        out_shape=[jax.ShapeDtypeStruct((b, t, nv), BF16)] + c_shape,
        scratch_shapes=[pltpu.VMEM((t, 2 * nk), F32),
                        pltpu.VMEM((t, nv), F32), pltpu.VMEM((t, nv), F32),
                        pltpu.VMEM((nv, nk), F32), pltpu.VMEM((nv, nk), F32)],
        compiler_params=_params("parallel"),
        name="gla",
    )(u3, u3, u3, u3, u3, wc, bc, norm_w, *expert_weights)
    return y_b, cast


GROUP_LANE0 = N_EXPERTS


def _route(x, g_ref, whl_ref, b_ref, xn_ref, gates_ref, info_ref, inforow_ref, count_ref):
    ms = jnp.mean(x * x, axis=-1, keepdims=True)
    xn = x * lax.rsqrt(ms + NORM_EPS) * g_ref[...]
    xhi = xn.astype(BF16)
    xn_ref[...] = xhi
    xlo = (xn - xhi.astype(F32)).astype(BF16)
    both = _dot(xhi, whl_ref[...])
    logits = both[:, :LANES] + both[:, LANES:] + _dot(xlo, whl_ref[:, :LANES]) + b_ref[...]

    lane = lax.broadcasted_iota(jnp.int32, logits.shape, 1)
    neg = -jnp.inf
    big = jnp.int32(LANES)
    is_group = (lane >> 2) == (GROUP_LANE0 >> 2)
    gl = jnp.where(is_group, logits, neg)
    gmax = jnp.max(gl, axis=-1, keepdims=True)
    g_p = 1.0 / jnp.sum(jnp.exp(gl - gmax), axis=-1, keepdims=True)
    g_idx = jnp.min(jnp.where(gl == gmax, lane, big), axis=-1, keepdims=True) - GROUP_LANE0
    el = jnp.where((lane >> 3) == g_idx, logits, neg)
    m1 = jnp.max(el, axis=-1, keepdims=True)
    i1 = jnp.min(jnp.where(el == m1, lane, big), axis=-1, keepdims=True)
    el2 = jnp.where(lane == i1, neg, el)
    m2 = jnp.max(el2, axis=-1, keepdims=True)
    i2 = jnp.min(jnp.where(el2 == m2, lane, big), axis=-1, keepdims=True)
    e2 = jnp.exp(m2 - m1)
    w1 = g_p / (1.0 + e2)
    w2 = g_p * e2 / (1.0 + e2)
    gates_ref[...] = jnp.where(lane == i1, w1, 0.0) + jnp.where(lane == i2, w2, 0.0)

    tm = x.shape[0]
    onehot = jnp.where(lane == g_idx, 1.0, 0.0)
    r_i = lax.broadcasted_iota(jnp.int32, (tm, tm), 0)
    c_i = lax.broadcasted_iota(jnp.int32, (tm, tm), 1)
    before = jnp.where(r_i > c_i, 1.0, 0.0).astype(BF16)
    prefix = _dot(before, onehot.astype(BF16))
    rank = jnp.sum(jnp.where(lane == g_idx, prefix, 0.0), axis=-1, keepdims=True).astype(jnp.int32)
    info_ref[...] = jnp.where(lane == 0, g_idx, jnp.where(lane == 1, rank, 0))
    eye = r_i == c_i
    g_row = jnp.sum(jnp.where(eye, g_idx, 0), axis=0, keepdims=True)
    rank_row = jnp.sum(jnp.where(eye, rank, 0), axis=0, keepdims=True)
    sub = lax.broadcasted_iota(jnp.int32, inforow_ref.shape, 0)
    inforow_ref[...] = jnp.where(sub == 0, g_row, jnp.where(sub == 1, rank_row, 0))
    counts = jnp.sum(onehot, axis=0, keepdims=True).astype(jnp.int32)
    count_ref[...] = jnp.broadcast_to(counts, count_ref.shape)


def _router_operands(norm_g, w_group, b_group, w_expert, b_expert):
    d = w_group.shape[0]
    pad = LANES - N_EXPERTS - MOE_GROUPS
    wr = jnp.concatenate([w_expert, w_group, jnp.zeros((d, pad), F32)], axis=1)
    whi = wr.astype(BF16)
    wlo = (wr - whi.astype(F32)).astype(BF16)
    bias = jnp.concatenate([b_expert, b_group, jnp.zeros((pad,), F32)]).reshape(1, LANES)
    return norm_g.reshape(1, d), jnp.concatenate([whi, wlo], axis=1), bias


def _router_specs(n, d, tm):
    nb = n // tm
    const = lambda shape: pl.BlockSpec(shape, lambda i: (0,) * len(shape))
    in_specs = [const((1, d)), const((d, 2 * LANES)), const((1, LANES))]
    out_specs = [pl.BlockSpec((tm, d), lambda i: (i, 0)),
                 pl.BlockSpec((tm, d), lambda i: (i, 0)),
                 pl.BlockSpec((tm, LANES), lambda i: (i, 0)),
                 pl.BlockSpec((tm, LANES), lambda i: (i, 0)),
                 pl.BlockSpec((None, 8, tm), lambda i: (i, 0, 0)),
                 pl.BlockSpec((None, 8, LANES), lambda i: (i, 0, 0))]
    out_shape = [jax.ShapeDtypeStruct((n, d), F32),
                 jax.ShapeDtypeStruct((n, d), BF16),
                 jax.ShapeDtypeStruct((n, LANES), F32),
                 jax.ShapeDtypeStruct((n, LANES), jnp.int32),
                 jax.ShapeDtypeStruct((nb, 8, tm), jnp.int32),
                 jax.ShapeDtypeStruct((nb, 8, LANES), jnp.int32)]
    return in_specs, out_specs, out_shape


def _proj2_router_kernel(a_ref, b_ref, wa_ref, wb_ref, res_ref, g_ref, whl_ref, bias_ref,
                         h_ref, *route_out):
    x = res_ref[...] + _dot(a_ref[...], wa_ref[...]) + _dot(b_ref[...], wb_ref[...])
    h_ref[...] = x
    _route(x, g_ref, whl_ref, bias_ref, *route_out)


def _proj2_router(a, b, wa, wb, res, router_ops, tm):
    n, d = res.shape
    ka, kb = a.shape[1], b.shape[1]
    r_in, out_specs, out_shape = _router_specs(n, d, tm)
    return pl.pallas_call(
        _proj2_router_kernel,
        grid=(n // tm,),
        in_specs=[pl.BlockSpec((tm, ka), lambda i: (i, 0)),
                  pl.BlockSpec((tm, kb), lambda i: (i, 0)),
                  pl.BlockSpec((ka, d), lambda i: (0, 0)),
                  pl.BlockSpec((kb, d), lambda i: (0, 0)),
                  pl.BlockSpec((tm, d), lambda i: (i, 0))] + r_in,
        out_specs=out_specs,
        out_shape=out_shape,
        compiler_params=_params("parallel"),
        name="proj2_router",
    )(a, b, wa, wb, res, *router_ops)


def _att_out_router_kernel(o_ref, w_ref, hres_ref, g_ref, whl_ref, bias_ref, h_ref, *route_out):
    tm = o_ref.shape[0]
    tiles = (hres_ref.shape[0] - N_META) // tm
    r0 = pl.multiple_of(N_META + lax.rem(pl.program_id(0), tiles) * tm, 16)
    x = hres_ref[pl.ds(r0, tm), :] + _dot(o_ref[...], w_ref[...])
    h_ref[...] = x
    _route(x, g_ref, whl_ref, bias_ref, *route_out)


def _att_out_router(o, w, h3, router_ops, tm):
    b, t, d = h3.shape
    n, nq = o.shape
    tiles = (t - N_META) // tm
    r_in, out_specs, out_shape = _router_specs(n, d, tm)
    return pl.pallas_call(
        _att_out_router_kernel,
        grid=(n // tm,),
        in_specs=[pl.BlockSpec((tm, nq), lambda i: (i, 0)),
                  pl.BlockSpec((nq, d), lambda i: (0, 0)),
                  pl.BlockSpec((None, t, d), lambda i: (i // tiles, 0, 0))] + r_in,
        out_specs=out_specs,
        out_shape=out_shape,
        compiler_params=_params("parallel"),
        name="att_out_router",
    )(o, w, h3, *router_ops)


MOE_TILE = 512
SEG_ALIGN = 16
STRIP = 32


def _pow2_sizes(max_rows, min_rows):
    sizes, s = [], min_rows
    while s <= max_rows:
        sizes.append(s)
        s *= 2
    return sizes[::-1]


def _strip_copies(n_rows, sizes, make_copy):
    pos = 0
    for sz in sizes:
        bit = n_rows & sz
        yield bit, make_copy(pos, sz)
        pos = pos + bit


def _slot_key(group, rank, boff_ref, b):
    key = rank
    for g in range(MOE_GROUPS):
        key = key + jnp.where(group == g, boff_ref[MOE_GROUPS * b + g], 0)
    return key


def _start_all(copies):
    for bit, cs in copies:
        @pl.when(bit != 0)
        def _():
            for c in cs:
                c.start()


def _wait_all(copies):
    for bit, cs in copies:
        @pl.when(bit != 0)
        def _():
            for c in cs:
                c.wait()


def _dispatch_kernel(off_ref, cpad_ref, boff_ref, tail_ref, x_ref, gates_ref, inforow_ref,
                     xs_hbm, gs_hbm, xstg, gstg, zx, zg, sem):
    b = pl.program_id(0)
    last = pl.num_programs(0) - 1
    tb = x_ref.shape[0]
    ks = xstg.shape[1]
    cur = lax.rem(b, 2)
    key = _slot_key(inforow_ref[0:1, :], inforow_ref[1:2, :], boff_ref, b)
    slot = lax.broadcasted_iota(jnp.int32, (ks, tb), 0)
    perm = jnp.where(slot == key, 1.0, 0.0).astype(BF16)
    xstg[cur] = _dot(perm, x_ref[...]).astype(BF16)
    gt = gates_ref[...]
    hi = gt.astype(BF16)
    lo = (gt - hi.astype(F32)).astype(BF16)
    both = _dot(perm, jnp.concatenate([hi, lo], axis=1))
    gstg[cur] = both[:, :LANES] + both[:, LANES:]

    sizes = _pow2_sizes(tb, SEG_ALIGN)

    def copies(blk, sl):
        for g in range(MOE_GROUPS):
            src0 = boff_ref[MOE_GROUPS * blk + g]
            dst0 = off_ref[MOE_GROUPS * blk + g]

            def make(pos, sz, src0=src0, dst0=dst0):
                src = pl.ds(pl.multiple_of(src0 + pos, SEG_ALIGN), sz)
                dst = pl.ds(pl.multiple_of(dst0 + pos, SEG_ALIGN), sz)
                return (pltpu.make_async_copy(xstg.at[sl, src, :], xs_hbm.at[dst, :], sem.at[sl, 0]),
                        pltpu.make_async_copy(gstg.at[sl, src, :], gs_hbm.at[dst, :], sem.at[sl, 1]))

            yield from _strip_copies(cpad_ref[MOE_GROUPS * blk + g], sizes, make)

    _start_all(copies(b, cur))

    @pl.when(b > 0)
    def _():
        _wait_all(copies(jnp.maximum(b - 1, 0), 1 - cur))

    @pl.when(b == last)
    def _():
        _wait_all(copies(b, cur))
        zx[...] = jnp.zeros(zx.shape, BF16)
        zg[...] = jnp.zeros(zg.shape, F32)
        tail_sizes = _pow2_sizes(MOE_TILE // 2, SEG_ALIGN)

        def tails():
            for g in range(MOE_GROUPS):
                dst0 = tail_ref[g]

                def make(pos, sz, dst0=dst0):
                    dst = pl.ds(pl.multiple_of(dst0 + pos, SEG_ALIGN), sz)
                    return (pltpu.make_async_copy(zx.at[pl.ds(0, sz), :], xs_hbm.at[dst, :], sem.at[0, 0]),
                            pltpu.make_async_copy(zg.at[pl.ds(0, sz), :], gs_hbm.at[dst, :], sem.at[0, 1]))

                yield from _strip_copies(tail_ref[MOE_GROUPS + g], tail_sizes, make)

        _start_all(tails())
        _wait_all(tails())

        def spare(i):
            dst = pl.ds(pl.multiple_of(i * MOE_TILE, MOE_TILE), MOE_TILE)
            return (pltpu.make_async_copy(zx, xs_hbm.at[dst, :], sem.at[0, 0]),
                    pltpu.make_async_copy(zg, gs_hbm.at[dst, :], sem.at[0, 1]))

        first_spare = tail_ref[2 * MOE_GROUPS]
        n_tiles = xs_hbm.shape[0] // MOE_TILE

        @pl.loop(first_spare, n_tiles)
        def _(i):
            cx, cg = spare(i)
            cx.start()
            cg.start()

        @pl.loop(first_spare, n_tiles)
        def _(i):
            cx, cg = spare(i)
            cx.wait()
            cg.wait()


def _dispatch(xn, gates, inforow, off, cpad, boff, tail, n_sorted, tb, ks):
    n, d = xn.shape
    nb = n // tb
    grid_spec = pltpu.PrefetchScalarGridSpec(
        num_scalar_prefetch=4,
        grid=(nb,),
        in_specs=[pl.BlockSpec((tb, d), lambda i, *_: (i, 0)),
                  pl.BlockSpec((tb, LANES), lambda i, *_: (i, 0)),
                  pl.BlockSpec((None, 8, tb), lambda i, *_: (i, 0, 0))],
        out_specs=[pl.BlockSpec(memory_space=pl.ANY), pl.BlockSpec(memory_space=pl.ANY)],
        scratch_shapes=[pltpu.VMEM((2, ks, d), BF16), pltpu.VMEM((2, ks, LANES), F32),
                        pltpu.VMEM((MOE_TILE, d), BF16), pltpu.VMEM((MOE_TILE, LANES), F32),
                        pltpu.SemaphoreType.DMA((2, 2))])
    return pl.pallas_call(
        _dispatch_kernel,
        grid_spec=grid_spec,
        out_shape=[jax.ShapeDtypeStruct((n_sorted, d), BF16),
                   jax.ShapeDtypeStruct((n_sorted, LANES), F32)],
        compiler_params=_params("arbitrary"),
        name="moe_dispatch",
    )(off, cpad, boff, tail, xn, gates, inforow)


def _group_mlp_kernel(tg_ref, nused_ref, x_ref, gs_ref, wg_ref, wu_ref, wd_ref, y_ref):
    i = pl.program_id(0)

    @pl.when(i < nused_ref[0])
    def _():
        x = x_ref[...]
        gs = gs_ref[...]
        lane = lax.broadcasted_iota(jnp.int32, gs.shape, 1)
        lane0 = tg_ref[i] * MOE_EXPERTS_PER_GROUP
        acc = jnp.zeros(y_ref.shape, F32)
        nxt = (_dot(x, wg_ref[0]), _dot(x, wu_ref[0]))
        for e in range(MOE_EXPERTS_PER_GROUP):
            hg, hu = nxt
            if e + 1 < MOE_EXPERTS_PER_GROUP:
                nxt = (_dot(x, wg_ref[e + 1]), _dot(x, wu_ref[e + 1]))
            gate = jnp.sum(jnp.where(lane == lane0 + e, gs, 0.0), axis=-1, keepdims=True)
            acc = acc + _dot((_silu(hg) * hu * gate).astype(BF16), wd_ref[e])
        y_ref[...] = acc.astype(y_ref.dtype)

    @pl.when(i >= nused_ref[0])
    def _():
        y_ref[...] = jnp.zeros(y_ref.shape, y_ref.dtype)


def _group_mlp(xs, gs, wg, wu, wd, tile_group, nused):
    n_sorted, d = xs.shape
    ne = MOE_EXPERTS_PER_GROUP

    def rows(i, tg, nu):
        return (jnp.minimum(i, nu[0] - 1), 0)

    def group(i, tg, nu):
        return (tg[i], 0, 0)

    grid_spec = pltpu.PrefetchScalarGridSpec(
        num_scalar_prefetch=2,
        grid=(n_sorted // MOE_TILE,),
        in_specs=[pl.BlockSpec((MOE_TILE, d), rows),
                  pl.BlockSpec((MOE_TILE, LANES), rows),
                  pl.BlockSpec((ne, d, MOE_D_FF), group),
                  pl.BlockSpec((ne, d, MOE_D_FF), group),
                  pl.BlockSpec((ne, MOE_D_FF, d), group)],
        out_specs=pl.BlockSpec((MOE_TILE, d), lambda i, tg, nu: (i, 0)))
    return pl.pallas_call(
        _group_mlp_kernel,
        grid_spec=grid_spec,
        out_shape=jax.ShapeDtypeStruct((n_sorted, d), BF16),
        compiler_params=_params("arbitrary"),
        name="moe_experts",
    )(tile_group, nused, xs, gs, wg, wu, wd)


def _combine_kernel(off_ref, cpad_ref, boff_ref, info_ref, res_ref, g_ref, *rest, project):
    if project:
        w_ref, y_hbm, o_ref, u_ref, ybuf, sem = rest
    else:
        y_hbm, o_ref, ybuf, sem = rest
    b = pl.program_id(0)
    last = pl.num_programs(0) - 1
    tb = res_ref.shape[0]
    ks = ybuf.shape[1]
    cur = lax.rem(b, 2)
    sizes = _pow2_sizes(tb, STRIP)

    def copies(blk, sl):
        for g in range(MOE_GROUPS):
            src0 = off_ref[MOE_GROUPS * blk + g]
            dst0 = boff_ref[MOE_GROUPS * blk + g]

            def make(pos, sz, src0=src0, dst0=dst0):
                src = pl.ds(pl.multiple_of(src0 + pos, SEG_ALIGN), sz)
                dst = pl.ds(pl.multiple_of(dst0 + pos, STRIP), sz)
                return (pltpu.make_async_copy(y_hbm.at[src, :], ybuf.at[sl, dst, :], sem.at[sl]),)

            n_rows = (cpad_ref[MOE_GROUPS * blk + g] + (STRIP - 1)) & (-STRIP)
            yield from _strip_copies(n_rows, sizes, make)

    @pl.when(b == 0)
    def _():
        ybuf[...] = jnp.zeros(ybuf.shape, ybuf.dtype)
        _start_all(copies(b, cur))

    @pl.when(b < last)
    def _():
        _start_all(copies(jnp.minimum(b + 1, last), 1 - cur))

    _wait_all(copies(b, cur))

    key = _slot_key(info_ref[:, 0:1], info_ref[:, 1:2], boff_ref, b)
    slot = lax.broadcasted_iota(jnp.int32, (tb, ks), 1)
    perm = jnp.where(slot == key, 1.0, 0.0).astype(BF16)
    y = res_ref[...] + _dot(perm, ybuf[cur])
    ms = jnp.mean(y * y, axis=-1, keepdims=True)
    yn = y * lax.rsqrt(ms + NORM_EPS) * g_ref[...]
    if project:
        o_ref[...] = y
        u_ref[...] = _dot(yn.astype(BF16), w_ref[...]).astype(u_ref.dtype)
    else:
        o_ref[...] = yn


def _combine(ys, info, res, norm_g, w_next, off, cpad, boff, tb, ks):
    n, d = res.shape
    project = w_next is not None
    row = lambda width: pl.BlockSpec((tb, width), lambda i, *_: (i, 0))
    in_specs = [row(LANES), row(d), pl.BlockSpec((1, d), lambda i, *_: (0, 0))]
    args = [info, res, norm_g.reshape(1, d)]
    out_specs, out_shape = [row(d)], [jax.ShapeDtypeStruct((n, d), F32)]
    if project:
        nout = w_next.shape[1]
        in_specs.append(pl.BlockSpec((d, nout), lambda i, *_: (0, 0)))
        args.append(w_next)
        out_specs.append(row(nout))
        out_shape.append(jax.ShapeDtypeStruct((n, nout), BF16))
    grid_spec = pltpu.PrefetchScalarGridSpec(
        num_scalar_prefetch=3,
        grid=(n // tb,),
        in_specs=in_specs + [pl.BlockSpec(memory_space=pl.ANY)],
        out_specs=out_specs,
        scratch_shapes=[pltpu.VMEM((2, ks, d), BF16), pltpu.SemaphoreType.DMA((2,))])
    out = pl.pallas_call(
        functools.partial(_combine_kernel, project=project),
        grid_spec=grid_spec,
        out_shape=out_shape,
        compiler_params=_params("arbitrary"),
        name="moe_combine",
    )(off, cpad, boff, *args, ys)
    return tuple(out) if project else out[0]


def _round_up(x, m):
    return (x + m - 1) // m * m


def _moe_layer(routed, expert_weights, norm_g, w_next, tb):
    h, xn, gates, info, inforow, counts = routed
    w_gate, w_up, w_down = expert_weights
    n, d = h.shape
    nb = n // tb

    cnt = counts[:, 0, :MOE_GROUPS]
    cpad = _round_up(cnt, SEG_ALIGN)
    strip = _round_up(cnt, STRIP)
    used = jnp.sum(cpad, axis=0)
    region = _round_up(used, MOE_TILE)
    region_start = jnp.cumsum(region) - region
    off = region_start[None, :] + jnp.cumsum(cpad, axis=0) - cpad
    boff = jnp.cumsum(strip, axis=1) - strip
    nused = (jnp.sum(region) // MOE_TILE).reshape(1)
    tail = jnp.concatenate([region_start + used, region - used, nused]).astype(jnp.int32)
    n_tiles = -(-(n + nb * MOE_GROUPS * (SEG_ALIGN - 1) + MOE_GROUPS * (MOE_TILE - 1)) // MOE_TILE) + 1
    tile_start = jnp.arange(n_tiles, dtype=jnp.int32) * MOE_TILE
    region_end = region_start + region
    tile_group = jnp.minimum(jnp.sum(tile_start[:, None] >= region_end[None, :], axis=1), MOE_GROUPS - 1)
    ks =tb + MOE_GROUPS * STRIP

    flat = lambda a: a.reshape(-1).astype(jnp.int32)
    xs, gs = _dispatch(xn, gates, inforow, flat(off), flat(cpad), flat(boff), tail,
                       n_tiles * MOE_TILE, tb, ks)
    ys = _group_mlp(xs, gs, w_gate, w_up, w_down,
                    tile_group.astype(jnp.int32), nused.astype(jnp.int32))
    return _combine(ys, info, h, norm_g, w_next, flat(off), flat(cpad), flat(boff), tb, ks)


ATT_Q_TILE = 256
ATT_TILES_PER_ITER = 2
ATT_SCORES_AHEAD = 2
ATT_SCORE_SLOTS = 4
ATT_VT_ROWS = ATT_HEAD_DIM + 16
ATT_Q_SCALE = ATT_HEAD_DIM ** -0.5 * 1.4426950408889634


def _eye(n):
    return jnp.where(lax.broadcasted_iota(jnp.int32, (n, n), 0) == lax.broadcasted_iota(jnp.int32, (n, n), 1),
                     1.0, 0.0).astype(BF16)


def _transpose_bf16(x, eye):
    return lax.dot_general(eye, x, _NT, preferred_element_type=F32).astype(BF16)


def _rope_kernel(q_ref, k_ref, v_ref, aq_ref, bq_ref, ak_ref, bk_ref, qt_ref, kp_ref, vt_ref):
    t = q_ref.shape[0]
    tq = ATT_Q_TILE
    dh = ATT_HEAD_DIM
    gi = lax.broadcasted_iota(jnp.int32, (LANES, LANES), 0) >> 6
    gj = lax.broadcasted_iota(jnp.int32, (LANES, LANES), 1) >> 6
    gmat = jnp.where(gi == gj, 1.0, 0.0).astype(BF16)
    eye_q = _eye(LANES)

    def norm_rope(x, a, b):
        lane = lax.broadcasted_iota(jnp.int32, x.shape, 1)
        sq = x * x
        hi = sq.astype(BF16)
        lo = (sq - hi.astype(F32)).astype(BF16)
        ms = (_dot(hi, gmat) + _dot(lo, gmat)) * (1.0 / dh)
        partner = jnp.where((lane & 1) == 0, pltpu.roll(x, LANES - 1, 1), pltpu.roll(x, 1, 1))
        return lax.rsqrt(ms + NORM_EPS) * (x * a + partner * b)

    for j in range(q_ref.shape[1] // LANES):
        cols = slice(j * LANES, (j + 1) * LANES)
        y = norm_rope(q_ref[N_META:, cols].astype(F32), aq_ref[N_META:, :], bq_ref[N_META:, :])
        yt = _transpose_bf16(y.astype(BF16), eye_q)
        for i in range((t - N_META) // tq):
            qt_ref[i, cols, :] = yt[:, i * tq:(i + 1) * tq]

    low = lax.broadcasted_iota(jnp.int32, (t, LANES), 1) < dh
    for j in range(k_ref.shape[1] // LANES):
        cols = slice(j * LANES, (j + 1) * LANES)
        y = norm_rope(k_ref[:, cols].astype(F32), ak_ref[...], bk_ref[...])
        kp_ref[2 * j] = jnp.where(low, y, 0.0).astype(BF16)
        kp_ref[2 * j + 1] = jnp.where(low, pltpu.roll(y, dh, 1), 0.0).astype(BF16)

    n_real = t - N_META
    eye_v = _eye(v_ref.shape[1])
    vt_real = _transpose_bf16(v_ref[N_META:, :], eye_v)
    vt_meta = _transpose_bf16(v_ref[0:N_META, :], eye_v)
    for h in range(ATT_KV_HEADS):
        vt_ref[h, 0:dh, 0:n_real] = vt_real[h * dh:(h + 1) * dh, :]
        vt_ref[h, 0:dh, n_real:t] = vt_meta[h * dh:(h + 1) * dh, :]
        extra = lax.broadcasted_iota(jnp.int32, (ATT_VT_ROWS - dh, t), 0)
        vt_ref[h, dh:ATT_VT_ROWS, :] = jnp.where(extra == 0, 1.0, 0.0).astype(BF16)


def _rope(u3, tables):
    b, t, _ = u3.shape
    nq = ATT_HEADS * ATT_HEAD_DIM
    nkv = ATT_KV_HEADS * ATT_HEAD_DIM
    n_qt = (t - N_META) // ATT_Q_TILE
    full = lambda shape: pl.BlockSpec(shape, lambda i: (0, 0))
    return pl.pallas_call(
        _rope_kernel,
        grid=(b,),
        in_specs=[pl.BlockSpec((None, t, nq), lambda i: (i, 0, 0)),
                  pl.BlockSpec((None, t, nkv), lambda i: (i, 0, nq // nkv)),
                  pl.BlockSpec((None, t, nkv), lambda i: (i, 0, nq // nkv + 1)),
                  full((t, LANES)), full((t, LANES)), full((t, LANES)), full((t, LANES))],
        out_specs=[pl.BlockSpec((None, n_qt, nq, ATT_Q_TILE), lambda i: (i, 0, 0, 0)),
                   pl.BlockSpec((None, ATT_KV_HEADS, t, LANES), lambda i: (i, 0, 0, 0)),
                   pl.BlockSpec((None, ATT_KV_HEADS, ATT_VT_ROWS, t), lambda i: (i, 0, 0, 0))],
        out_shape=[jax.ShapeDtypeStruct((b, n_qt, nq, ATT_Q_TILE), BF16),
                   jax.ShapeDtypeStruct((b, ATT_KV_HEADS, t, LANES), BF16),
                   jax.ShapeDtypeStruct((b, ATT_KV_HEADS, ATT_VT_ROWS, t), BF16)],
        compiler_params=_params("parallel"),
        name="qk_norm_rope",
    )(u3, u3, u3, *tables)


def _attn_kernel(qt_ref, k_ref, vt_ref, wg_ref, wu_ref, wd_ref, o_ref, wg_o, wu_o, wd_o, s_ref):
    _cast_through((wg_ref, wu_ref, wd_ref), (wg_o, wu_o, wd_o))
    n_qt, w, tq = qt_ref.shape
    dh = ATT_HEAD_DIM
    n_real = k_ref.shape[0] - N_META
    eye = _eye(tq)
    zpad = jnp.zeros((LANES - dh, tq), BF16)

    def scores(i, g, slot):
        qp = jnp.concatenate([qt_ref[i, g * dh:(g + 1) * dh, :], zpad], axis=0)
        s_r = _dot(k_ref[N_META:, :], qp)
        s_m = _dot(k_ref[0:N_META, :], qp)
        s_ref[slot, 0:n_real, :] = s_r
        s_ref[slot, n_real:, :] = s_m
        return jnp.maximum(jnp.max(s_r, axis=0, keepdims=True), jnp.max(s_m, axis=0, keepdims=True))

    steps = [(u, g) for u in range(ATT_TILES_PER_ITER) for g in range(ATT_GROUP)]
    n_steps = len(steps)
    n_iter = n_qt // ATT_TILES_PER_ITER
    ahead = ATT_SCORES_AHEAD
    n_slots = s_ref.shape[0]
    assert n_steps % n_slots == 0 and ahead < n_slots

    def issue_scores(it, n):
        it2 = jnp.minimum(it + n // n_steps, n_iter - 1)
        u, g = steps[n % n_steps]
        return scores(it2 * ATT_TILES_PER_ITER + u, g, n % n_slots)

    def q_tiles(it, pending):
        i0 = it * ATT_TILES_PER_ITER
        outs = []
        pending = list(pending)
        for n, (u, g) in enumerate(steps):
            m, slot = pending.pop(0), n % n_slots
            pending.append(issue_scores(it, n + ahead))
            p_r = jnp.exp2(s_ref[slot, 0:n_real, :] - m).astype(BF16)
            p_m = jnp.exp2(s_ref[slot, n_real:, :] - m).astype(BF16)
            ot = _dot(vt_ref[:, 0:n_real], p_r) + _dot(vt_ref[:, n_real:], p_m)
            outs.append((ot[0:dh] * (1.0 / ot[dh:dh + 1])).astype(BF16))
            if g == ATT_GROUP - 1:
                rows = pl.ds(pl.multiple_of((i0 + u) * tq, tq), tq)
                ot_all = jnp.concatenate(outs, axis=0)
                o_ref[rows, :] = lax.dot_general(eye, ot_all, _NT, preferred_element_type=F32).astype(o_ref.dtype)
                outs = []
        return tuple(pending)

    first = tuple(issue_scores(0, n) for n in range(ahead))
    lax.fori_loop(0, n_iter, q_tiles, first)


def _attention(qt, kp, vt, expert_weights, layer):
    b, n_qt, nq, tq = qt.shape
    t = kp.shape[2]
    w = ATT_GROUP * ATT_HEAD_DIM
    grid = (b, ATT_KV_HEADS)
    c_in, c_out, c_shape = _expert_cast_specs(expert_weights, layer, grid)
    o, *cast = pl.pallas_call(
        _attn_kernel,
        grid=grid,
        in_specs=[pl.BlockSpec((None, n_qt, w, tq), lambda i, h: (i, 0, h, 0)),
                  pl.BlockSpec((None, None, t, LANES), lambda i, h: (i, h, 0, 0)),
                  pl.BlockSpec((None, None, ATT_VT_ROWS, t), lambda i, h: (i, h, 0, 0))] + c_in,
        out_specs=[pl.BlockSpec((None, n_qt * tq, w), lambda i, h: (i, 0, h))] + c_out,
        out_shape=[jax.ShapeDtypeStruct((b, n_qt * tq, nq), BF16)] + c_shape,
        scratch_shapes=[pltpu.VMEM((ATT_SCORE_SLOTS, t, tq), F32)],
        compiler_params=_params("parallel", "parallel"),
        name="gqa_attention",
    )(qt, kp, vt, *expert_weights)
    return o, cast


def _rope_tables(n_tokens, q_gain, k_gain):
    rows_n = n_tokens // GRID_W
    rows = jnp.repeat(jnp.arange(rows_n), GRID_W).astype(F32)
    cols = jnp.tile(jnp.arange(GRID_W), rows_n).astype(F32)
    axis_dims = ATT_HEAD_DIM // 2
    freqs = ROPE_THETA ** (-jnp.arange(0, axis_dims, 2, dtype=F32) / axis_dims)
    ang = jnp.concatenate([rows[:, None] * freqs, cols[:, None] * freqs], axis=-1)
    ang = jnp.concatenate([jnp.zeros((N_META, ang.shape[1]), F32), ang], axis=0)
    cos_e = jnp.tile(jnp.repeat(jnp.cos(ang), 2, axis=1), (1, LANES // ATT_HEAD_DIM))
    sign = jnp.tile(jnp.array([-1.0, 1.0], F32), LANES // 2)
    sin_e = jnp.tile(jnp.repeat(jnp.sin(ang), 2, axis=1), (1, LANES // ATT_HEAD_DIM)) * sign

    def tables(gain, scale):
        g = jnp.tile(gain.astype(F32), LANES // ATT_HEAD_DIM)
        g_partner = g.reshape(-1, 2)[:, ::-1].reshape(-1)
        return cos_e * (g * scale), sin_e * (g_partner * scale)

    return tables(q_gain, ATT_Q_SCALE) + tables(k_gain, 1.0)


def kernel(x, meta_tokens, norm_mix, norm_ffn, norm_final, sc_w_in, sc_conv_w, gla_w_gate_fwd, gla_b_gate_fwd, gla_w_gate_bwd, gla_b_gate_bwd, gla_norm_w, sc_w_out, att_w_in, att_q_norm, att_k_norm, att_w_out, moe_w_group, moe_b_group, moe_w_expert, moe_b_expert, moe_w_gate, moe_w_up, moe_w_down):
    bsz, s_len, d = x.shape
    t = s_len + N_META
    n0, n1 = bsz * t, bsz * s_len
    tm0 = t // 3
    tb0 = 3 * LANES
    tm1 = 4 * LANES
    assert t % tm0 == 0 and tm0 % 16 == 0 and n0 % tb0 == 0 and n1 % tm1 == 0 and s_len % tm1 == 0

    expert_f32 = (moe_w_gate, moe_w_up, moe_w_down)

    nk = GLA_HEADS * GLA_DK
    mix_in = sc_w_in.shape[2]
    mix_pad = -mix_in % MXU_WIDTH
    w_in = jnp.pad(sc_w_in[0], ((0, 0), (0, mix_pad))).astype(BF16)
    u, h = _embed_norm_matmul(x, meta_tokens.astype(x.dtype), norm_mix[0], w_in, tm0)
    h = h.reshape(n0, d)
    u3 = u.reshape(bsz, t, mix_in + mix_pad)
    y_a, gate_up0 = _gated_conv(u3, sc_conv_w[0], expert_f32[:2], 0)
    wc = jnp.zeros((LANES, 2 * nk), F32)
    wc = wc.at[:GLA_GATE_RANK, :nk].set(gla_w_gate_fwd[0])
    wc = wc.at[GLA_GATE_RANK:2 * GLA_GATE_RANK, nk:].set(gla_w_gate_bwd[0])
    bc = jnp.concatenate([gla_b_gate_fwd[0], gla_b_gate_bwd[0]]).reshape(1, 2 * nk)
    y_b, down0 = _gla(u3, wc.astype(BF16), bc, gla_norm_w[0].reshape(1, GLA_DV), expert_f32[2:], 0)
    experts0 = gate_up0 + down0
    w_out = sc_w_out[0].astype(BF16)
    router0 = _router_operands(norm_ffn[0], moe_w_group[0], moe_b_group[0], moe_w_expert[0], moe_b_expert[0])
    routed = _proj2_router(y_a.reshape(n0, SC_WIDTH), y_b.reshape(n0, -1), w_out[:SC_WIDTH], w_out[SC_WIDTH:],
                           h, router0, tb0)
    h, u = _moe_layer(routed, experts0, norm_mix[1], att_w_in[0].astype(BF16), tb0)
    u3 = u.reshape(bsz, t, -1)
    qt, kp, vt = _rope(u3, _rope_tables(s_len, att_q_norm[0], att_k_norm[0]))
    o, experts1 = _attention(qt, kp, vt, expert_f32, 1)
    router1 = _router_operands(norm_ffn[1], moe_w_group[1], moe_b_group[1], moe_w_expert[1], moe_b_expert[1])
    routed = _att_out_router(o.reshape(n1, d), att_w_out[0].astype(BF16), h.reshape(bsz, t, d), router1, tm1)
    out = _moe_layer(routed, experts1, norm_final, None, tm1)
    return out.reshape(bsz, s_len, d)
```

```python
import functools

import jax
import jax.numpy as jnp
from jax import lax
from jax.experimental import pallas as pl
from jax.experimental.pallas import tpu as pltpu

F32 = jnp.float32
BF16 = jnp.bfloat16

NORM_EPS = 1e-6
N_META = 16
GRID_W = 64
ROPE_THETA = 10000.0
SC_WIDTH = 512
GLA_HEADS = 4
GLA_DK = 64
GLA_DV = 128
GLA_GATE_RANK = 16
GLA_GATE_TAU = 16.0
GLA_CHUNK = 64
ATT_HEAD_DIM = 64
ATT_HEADS = 16
ATT_KV_HEADS = 4
ATT_GROUP = ATT_HEADS // ATT_KV_HEADS
MOE_GROUPS = 4
MOE_EXPERTS_PER_GROUP = 8
N_EXPERTS = 32
MOE_D_FF = 256

LANES = 128
MXU_WIDTH = 256
MIB = 2 ** 20
VMEM_LIMIT = 56 * MIB

_NT = (((1,), (1,)), ((), ()))
_TN = (((0,), (0,)), ((), ()))


def _params(*semantics):
    return pltpu.CompilerParams(dimension_semantics=semantics, vmem_limit_bytes=VMEM_LIMIT)


def _dot(a, b):
    return jnp.dot(a, b, preferred_element_type=F32)


def _silu(x):
    return x * (1.0 / (1.0 + jnp.exp(-x)))


def _embed_norm_matmul_kernel(x_ref, meta_ref, g_ref, w_ref, o_ref, h_any, xn_ref, sem):
    b, j = pl.program_id(0), pl.program_id(1)
    tm = xn_ref.shape[0]
    s_len = x_ref.shape[1]

    def copies():
        return (pltpu.make_async_copy(x_ref.at[0], h_any.at[b, pl.ds(N_META, s_len)], sem.at[0]),
                pltpu.make_async_copy(meta_ref, h_any.at[b, pl.ds(0, N_META)], sem.at[1]))

    @pl.when(j == 0)
    def _():
        for c in copies():
            c.start()

    def norm(x):
        ms = jnp.mean(x * x, axis=-1, keepdims=True)
        return (x * lax.rsqrt(ms + NORM_EPS) * g_ref[...]).astype(BF16)

    @pl.when(j == 0)
    def _():
        xn_ref[:N_META] = norm(meta_ref[...])
        xn_ref[N_META:] = norm(x_ref[0, :tm - N_META])

    @pl.when(j > 0)
    def _():
        start = pl.multiple_of(j * tm - N_META, N_META)
        xn_ref[...] = norm(x_ref[0, pl.ds(start, tm)])

    o_ref[...] = _dot(xn_ref[...], w_ref[...]).astype(o_ref.dtype)

    @pl.when(j == pl.num_programs(1) - 1)
    def _():
        for c in copies():
            c.wait()


def _embed_norm_matmul(x, meta, g, w, tm):
    bsz, s_len, d = x.shape
    t = s_len + N_META
    nout = w.shape[1]
    nj = t // tm
    any_spec = pl.BlockSpec(memory_space=pl.ANY)
    return pl.pallas_call(
        _embed_norm_matmul_kernel,
        grid=(bsz, nj),
        in_specs=[pl.BlockSpec((1, s_len, d), lambda b, j: (b, 0, 0)),
                  pl.BlockSpec((N_META, d), lambda b, j: (0, 0)),
                  pl.BlockSpec((1, d), lambda b, j: (0, 0)),
                  pl.BlockSpec((d, nout), lambda b, j: (0, 0))],
        out_specs=[pl.BlockSpec((tm, nout), lambda b, j: (b * nj + j, 0)), any_spec],
        out_shape=[jax.ShapeDtypeStruct((bsz * t, nout), BF16), jax.ShapeDtypeStruct((bsz, t, d), x.dtype)],
        scratch_shapes=[pltpu.VMEM((tm, d), BF16), pltpu.SemaphoreType.DMA((2,))],
        compiler_params=_params("arbitrary", "arbitrary"),
        name="norm_matmul",
    )(x, meta, g.reshape(1, d), w)


def _expert_cast_specs(weights, layer, grid_shape):
    n_steps = 1
    for g in grid_shape:
        n_steps *= g
    assert N_EXPERTS % n_steps == 0
    per_step = N_EXPERTS // n_steps

    def step(*idx):
        flat = 0
        for i, g in zip(idx, grid_shape):
            flat = flat * g + i
        return flat

    in_specs = [pl.BlockSpec((None, per_step) + w.shape[2:], lambda *idx: (layer, step(*idx), 0, 0))
                for w in weights]
    out_specs = [pl.BlockSpec((per_step,) + w.shape[2:], lambda *idx: (step(*idx), 0, 0)) for w in weights]
    out_shape = [jax.ShapeDtypeStruct(w.shape[1:], BF16) for w in weights]
    return in_specs, out_specs, out_shape


def _cast_through(in_refs, out_refs):
    for src, dst in zip(in_refs, out_refs):
        dst[...] = src[...].astype(dst.dtype)


def _conv_kernel(ab_ref, ac_ref, ax_ref, w_ref, *rest):
    n_cast = (len(rest) - 2) // 2
    o_ref, z_ref = rest[n_cast], rest[-1]
    _cast_through(rest[:n_cast], rest[n_cast + 1:-1])
    t, c = ab_ref.shape
    z = ac_ref[...].astype(F32) * ax_ref[...].astype(F32)
    z_ref[0:8, :] = jnp.zeros((8, c), F32)
    z_ref[8:8 + t, :] = z
    z_ref[8 + t:16 + t, :] = jnp.zeros((8, c), F32)
    w = w_ref[...]
    conv = w[0:1, :] * z_ref[7:7 + t, :] + w[1:2, :] * z + w[2:3, :] * z_ref[9:9 + t, :]
    o_ref[...] = (ab_ref[...].astype(F32) * conv).astype(o_ref.dtype)


def _gated_conv(u3, conv_w, expert_weights, layer):
    b, t, _ = u3.shape
    nblk = SC_WIDTH // LANES
    grid = (b, nblk)
    c_in, c_out, c_shape = _expert_cast_specs(expert_weights, layer, grid)

    def col(off):
        return pl.BlockSpec((None, t, LANES), lambda i, j: (i, 0, off + j))

    y_a, *cast = pl.pallas_call(
        _conv_kernel,
        grid=grid,
        in_specs=[col(0), col(nblk), col(2 * nblk),
                  pl.BlockSpec((3, LANES), lambda i, j: (0, j))] + c_in,
        out_specs=[pl.BlockSpec((None, t, LANES), lambda i, j: (i, 0, j))] + c_out,
        out_shape=[jax.ShapeDtypeStruct((b, t, SC_WIDTH), BF16)] + c_shape,
        scratch_shapes=[pltpu.VMEM((t + 16, LANES), F32)],
        compiler_params=_params("parallel", "parallel"),
        name="gated_conv",
    )(u3, u3, u3, conv_w, *expert_weights)
    return y_a, cast


GLA_ROW_TILE = 688
GLA_EPI_TILE = 688


def _gla_kernel(q_ref, k_ref, v_ref, r_ref, g_ref, wc_ref, bc_ref, nw_ref, *rest):
    n_cast = (len(rest) - 6) // 2
    y_ref = rest[n_cast]
    la_ref, of_ref, ob_ref, sf_ref, sb_ref = rest[-5:]
    _cast_through(rest[:n_cast], rest[n_cast + 1:-5])
    t, nk = q_ref.shape
    nv = v_ref.shape[1]
    c = GLA_CHUNK
    n_real = (t - N_META) // c

    for i in range(t // GLA_ROW_TILE):
        rows = pl.ds(i * GLA_ROW_TILE, GLA_ROW_TILE)
        pre = _dot(g_ref[rows, :], wc_ref[...]) + bc_ref[...]
        la_ref[rows, :] = (jnp.minimum(pre, 0.0) - jnp.log(1.0 + jnp.exp(-jnp.abs(pre)))) * (1.0 / GLA_GATE_TAU)

    def iota(shape, dim):
        return lax.broadcasted_iota(jnp.int32, shape, dim)

    tri_f = jnp.where(iota((c, c), 0) >= iota((c, c), 1), 1.0, 0.0).astype(BF16)
    tri_b = jnp.where(iota((c, c), 0) <= iota((c, c), 1), 1.0, 0.0).astype(BF16)
    keep_f = (iota((c, nk), 1) & (c - 1)) <= iota((c, nk), 0)
    keep_b = (iota((c, nk), 1) & (c - 1)) > iota((c, nk), 0)
    bd_k = (iota((nk, nk), 0) >> 6) == (iota((nk, nk), 1) >> 6)
    bd_v = (iota((nk, nv), 0) >> 6) == (iota((nk, nv), 1) >> 7)
    bd_s = (iota((nv, nk), 0) >> 7) == (iota((nv, nk), 1) >> 6)
    valid0 = iota((c, 1), 0) < N_META

    def cum_decay(rows, valid, tri, la_col):
        la = la_ref[rows, la_col:la_col + nk]
        if valid is not None:
            la = jnp.where(valid, la, 0.0)
        hi = la.astype(BF16)
        rem = la - hi.astype(F32)
        mid = rem.astype(BF16)
        lo = (rem - mid.astype(F32)).astype(BF16)
        return _dot(tri, hi) + _dot(tri, mid) + _dot(tri, lo)

    def chunk(rows, valid, cum, keep, tot_row, s_ref, o_ref, out_rows, n_out):
        q = q_ref[rows, :].astype(F32) * (GLA_DK ** -0.5)
        k = k_ref[rows, :].astype(F32)
        v = v_ref[rows, :].astype(F32)
        if valid is not None:
            q = jnp.where(valid, q, 0.0)
            k = jnp.where(valid, k, 0.0)
            v = jnp.where(valid, v, 0.0)
        tot = cum[tot_row:tot_row + 1, :]
        q_dec = (q * jnp.exp(cum)).astype(BF16)
        k_inv = k * jnp.exp(-cum)
        k_end = (k * jnp.exp(tot - cum)).astype(BF16)
        kbd = jnp.where(bd_k, jnp.concatenate([k_inv] * GLA_HEADS, axis=0), 0.0).astype(BF16)
        p = lax.dot_general(q_dec, kbd, _NT, preferred_element_type=F32)
        s_old = s_ref[...]
        o_inter = lax.dot_general(q_dec, s_old.astype(BF16), _NT, preferred_element_type=F32)
        kvt = lax.dot_general(v.astype(BF16), k_end, _TN, preferred_element_type=F32)
        yield
        p = jnp.where(keep, p, 0.0).astype(BF16)
        vbd = jnp.where(bd_v, jnp.concatenate([v] * GLA_HEADS, axis=0), 0.0).astype(BF16)
        o = _dot(p, vbd) + o_inter
        s_ref[...] = jnp.exp(tot) * s_old + jnp.where(bd_s, kvt, 0.0)
        o_ref[out_rows, :] = o[:n_out]

    def lockstep(*gens):
        live = list(gens)
        while live:
            nxt = []
            for gen in live:
                try:
                    next(gen)
                    nxt.append(gen)
                except StopIteration:
                    pass
            live = nxt

    sf_ref[...] = jnp.zeros(sf_ref.shape, F32)
    sb_ref[...] = jnp.zeros(sb_ref.shape, F32)
    meta_rows = pl.ds(0, c)
    lockstep(chunk(meta_rows, valid0, cum_decay(meta_rows, valid0, tri_f, 0), keep_f, c - 1,
                   sf_ref, of_ref, pl.ds(0, N_META), N_META))

    def rows_fwd(i):
        return pl.ds(pl.multiple_of(N_META + i * c, 16), c)

    def rows_bwd(i):
        return pl.ds(pl.multiple_of(N_META + (n_real - 1 - i) * c, 16), c)

    def body(i, cums):
        cum_f, cum_b = cums
        nxt = jnp.minimum(i + 1, n_real - 1)
        cums = (cum_decay(rows_fwd(nxt), None, tri_f, 0), cum_decay(rows_bwd(nxt), None, tri_b, nk))
        lockstep(chunk(rows_fwd(i), None, cum_f, keep_f, c - 1, sf_ref, of_ref, rows_fwd(i), c),
                 chunk(rows_bwd(i), None, cum_b, keep_b, 0, sb_ref, ob_ref, rows_bwd(i), c))
        return cums

    first = (cum_decay(rows_fwd(0), None, tri_f, 0), cum_decay(rows_bwd(0), None, tri_b, nk))
    lax.fori_loop(0, n_real, body, first, unroll=4)
    lockstep(chunk(meta_rows, valid0, cum_decay(meta_rows, valid0, tri_b, nk), keep_b, 0,
                   sb_ref, ob_ref, pl.ds(0, N_META), N_META))

    def epilogue(i, carry):
        rows = pl.ds(pl.multiple_of(i * GLA_EPI_TILE, 16), GLA_EPI_TILE)
        o = of_ref[rows, :] + ob_ref[rows, :]
        r = r_ref[rows, :].astype(F32)
        for h in range(GLA_HEADS):
            cols = slice(h * GLA_DV, (h + 1) * GLA_DV)
            oh = o[:, cols]
            ms = jnp.mean(oh * oh, axis=-1, keepdims=True)
            yh = oh * lax.rsqrt(ms + NORM_EPS) * nw_ref[...] * _silu(r[:, cols])
            y_ref[rows, cols] = yh.astype(y_ref.dtype)
        return carry

    lax.fori_loop(0, t // GLA_EPI_TILE, epilogue, 0)


def _gla(u3, wc, bc, norm_w, expert_weights, layer):
    b, t, _ = u3.shape
    nk = GLA_HEADS * GLA_DK
    nv = GLA_HEADS * GLA_DV
    c_in, c_out, c_shape = _expert_cast_specs(expert_weights, layer, (b,))
    q_blk = 3 * SC_WIDTH // nk
    v_blk = (3 * SC_WIDTH + 2 * nk) // nv
    g_blk = (3 * SC_WIDTH + 2 * nk + 2 * nv) // LANES
    full = lambda shape: pl.BlockSpec(shape, lambda i: (0, 0))
    y_b, *cast = pl.pallas_call(
        _gla_kernel,
        grid=(b,),
        in_specs=[pl.BlockSpec((None, t, nk), lambda i: (i, 0, q_blk)),
                  pl.BlockSpec((None, t, nk), lambda i: (i, 0, q_blk + 1)),
                  pl.BlockSpec((None, t, nv), lambda i: (i, 0, v_blk)),
                  pl.BlockSpec((None, t, nv), lambda i: (i, 0, v_blk + 1)),
                  pl.BlockSpec((None, t, LANES), lambda i: (i, 0, g_blk)),
                  full((LANES, 2 * nk)), full((1, 2 * nk)), full((1, GLA_DV))] + c_in,
        out_specs=[pl.BlockSpec((None, t, nv), lambda i: (i, 0, 0))] + c_out,
        out_shape=[jax.ShapeDtypeStruct((b, t, nv), BF16)] + c_shape,
        scratch_shapes=[pltpu.VMEM((t, 2 * nk), F32),
                        pltpu.VMEM((t, nv), F32), pltpu.VMEM((t, nv), F32),
                        pltpu.VMEM((nv, nk), F32), pltpu.VMEM((nv, nk), F32)],
        compiler_params=_params("parallel"),
        name="gla",
    )(u3, u3, u3, u3, u3, wc, bc, norm_w, *expert_weights)
    return y_b, cast


GROUP_LANE0 = N_EXPERTS


def _route(x, g_ref, whl_ref, b_ref, xn_ref, gates_ref, info_ref, inforow_ref, count_ref):
    ms = jnp.mean(x * x, axis=-1, keepdims=True)
    xn = x * lax.rsqrt(ms + NORM_EPS) * g_ref[...]
    xhi = xn.astype(BF16)
    xn_ref[...] = xhi
    xlo = (xn - xhi.astype(F32)).astype(BF16)
    both = _dot(xhi, whl_ref[...])
    logits = both[:, :LANES] + both[:, LANES:] + _dot(xlo, whl_ref[:, :LANES]) + b_ref[...]

    lane = lax.broadcasted_iota(jnp.int32, logits.shape, 1)
    neg = -jnp.inf
    big = jnp.int32(LANES)
    is_group = (lane >> 2) == (GROUP_LANE0 >> 2)
    gl = jnp.where(is_group, logits, neg)
    gmax = jnp.max(gl, axis=-1, keepdims=True)
    g_p = 1.0 / jnp.sum(jnp.exp(gl - gmax), axis=-1, keepdims=True)
    g_idx = jnp.min(jnp.where(gl == gmax, lane, big), axis=-1, keepdims=True) - GROUP_LANE0
    el = jnp.where((lane >> 3) == g_idx, logits, neg)
    m1 = jnp.max(el, axis=-1, keepdims=True)
    i1 = jnp.min(jnp.where(el == m1, lane, big), axis=-1, keepdims=True)
    el2 = jnp.where(lane == i1, neg, el)
    m2 = jnp.max(el2, axis=-1, keepdims=True)
    i2 = jnp.min(jnp.where(el2 == m2, lane, big), axis=-1, keepdims=True)
    e2 = jnp.exp(m2 - m1)
    w1 = g_p / (1.0 + e2)
    w2 = g_p * e2 / (1.0 + e2)
    gates_ref[...] = jnp.where(lane == i1, w1, 0.0) + jnp.where(lane == i2, w2, 0.0)

    tm = x.shape[0]
    onehot = jnp.where(lane == g_idx, 1.0, 0.0)
    r_i = lax.broadcasted_iota(jnp.int32, (tm, tm), 0)
    c_i = lax.broadcasted_iota(jnp.int32, (tm, tm), 1)
    before = jnp.where(r_i > c_i, 1.0, 0.0).astype(BF16)
    prefix = _dot(before, onehot.astype(BF16))
    rank = jnp.sum(jnp.where(lane == g_idx, prefix, 0.0), axis=-1, keepdims=True).astype(jnp.int32)
    info_ref[...] = jnp.where(lane == 0, g_idx, jnp.where(lane == 1, rank, 0))
    eye = r_i == c_i
    g_row = jnp.sum(jnp.where(eye, g_idx, 0), axis=0, keepdims=True)
    rank_row = jnp.sum(jnp.where(eye, rank, 0), axis=0, keepdims=True)
    sub = lax.broadcasted_iota(jnp.int32, inforow_ref.shape, 0)
    inforow_ref[...] = jnp.where(sub == 0, g_row, jnp.where(sub == 1, rank_row, 0))
    counts = jnp.sum(onehot, axis=0, keepdims=True).astype(jnp.int32)
    count_ref[...] = jnp.broadcast_to(counts, count_ref.shape)


def _router_operands(norm_g, w_group, b_group, w_expert, b_expert):
    d = w_group.shape[0]
    pad = LANES - N_EXPERTS - MOE_GROUPS
    wr = jnp.concatenate([w_expert, w_group, jnp.zeros((d, pad), F32)], axis=1)
    whi = wr.astype(BF16)
    wlo = (wr - whi.astype(F32)).astype(BF16)
    bias = jnp.concatenate([b_expert, b_group, jnp.zeros((pad,), F32)]).reshape(1, LANES)
    return norm_g.reshape(1, d), jnp.concatenate([whi, wlo], axis=1), bias


def _router_specs(n, d, tm):
    nb = n // tm
    const = lambda shape: pl.BlockSpec(shape, lambda i: (0,) * len(shape))
    in_specs = [const((1, d)), const((d, 2 * LANES)), const((1, LANES))]
    out_specs = [pl.BlockSpec((tm, d), lambda i: (i, 0)),
                 pl.BlockSpec((tm, d), lambda i: (i, 0)),
                 pl.BlockSpec((tm, LANES), lambda i: (i, 0)),
                 pl.BlockSpec((tm, LANES), lambda i: (i, 0)),
                 pl.BlockSpec((None, 8, tm), lambda i: (i, 0, 0)),
                 pl.BlockSpec((None, 8, LANES), lambda i: (i, 0, 0))]
    out_shape = [jax.ShapeDtypeStruct((n, d), F32),
                 jax.ShapeDtypeStruct((n, d), BF16),
                 jax.ShapeDtypeStruct((n, LANES), F32),
                 jax.ShapeDtypeStruct((n, LANES), jnp.int32),
                 jax.ShapeDtypeStruct((nb, 8, tm), jnp.int32),
                 jax.ShapeDtypeStruct((nb, 8, LANES), jnp.int32)]
    return in_specs, out_specs, out_shape


def _proj2_router_kernel(a_ref, b_ref, wa_ref, wb_ref, res_ref, g_ref, whl_ref, bias_ref,
                         h_ref, *route_out):
    x = res_ref[...] + _dot(a_ref[...], wa_ref[...]) + _dot(b_ref[...], wb_ref[...])
    h_ref[...] = x
    _route(x, g_ref, whl_ref, bias_ref, *route_out)


def _proj2_router(a, b, wa, wb, res, router_ops, tm):
    n, d = res.shape
    ka, kb = a.shape[1], b.shape[1]
    r_in, out_specs, out_shape = _router_specs(n, d, tm)
    return pl.pallas_call(
        _proj2_router_kernel,
        grid=(n // tm,),
        in_specs=[pl.BlockSpec((tm, ka), lambda i: (i, 0)),
                  pl.BlockSpec((tm, kb), lambda i: (i, 0)),
                  pl.BlockSpec((ka, d), lambda i: (0, 0)),
                  pl.BlockSpec((kb, d), lambda i: (0, 0)),
                  pl.BlockSpec((tm, d), lambda i: (i, 0))] + r_in,
        out_specs=out_specs,
        out_shape=out_shape,
        compiler_params=_params("parallel"),
        name="proj2_router",
    )(a, b, wa, wb, res, *router_ops)


def _att_out_router_kernel(o_ref, w_ref, hres_ref, g_ref, whl_ref, bias_ref, h_ref, *route_out):
    tm = o_ref.shape[0]
    tiles = (hres_ref.shape[0] - N_META) // tm
    r0 = pl.multiple_of(N_META + lax.rem(pl.program_id(0), tiles) * tm, 16)
    x = hres_ref[pl.ds(r0, tm), :] + _dot(o_ref[...], w_ref[...])
    h_ref[...] = x
    _route(x, g_ref, whl_ref, bias_ref, *route_out)


def _att_out_router(o, w, h3, router_ops, tm):
    b, t, d = h3.shape
    n, nq = o.shape
    tiles = (t - N_META) // tm
    r_in, out_specs, out_shape = _router_specs(n, d, tm)
    return pl.pallas_call(
        _att_out_router_kernel,
        grid=(n // tm,),
        in_specs=[pl.BlockSpec((tm, nq), lambda i: (i, 0)),
                  pl.BlockSpec((nq, d), lambda i: (0, 0)),
                  pl.BlockSpec((None, t, d), lambda i: (i // tiles, 0, 0))] + r_in,
        out_specs=out_specs,
        out_shape=out_shape,
        compiler_params=_params("parallel"),
        name="att_out_router",
    )(o, w, h3, *router_ops)


MOE_TILE = 512
SEG_ALIGN = 16
STRIP = 32


def _pow2_sizes(max_rows, min_rows):
    sizes, s = [], min_rows
    while s <= max_rows:
        sizes.append(s)
        s *= 2
    return sizes[::-1]


def _strip_copies(n_rows, sizes, make_copy):
    pos = 0
    for sz in sizes:
        bit = n_rows & sz
        yield bit, make_copy(pos, sz)
        pos = pos + bit


def _slot_key(group, rank, boff_ref, b):
    key = rank
    for g in range(MOE_GROUPS):
        key = key + jnp.where(group == g, boff_ref[MOE_GROUPS * b + g], 0)
    return key


def _start_all(copies):
    for bit, cs in copies:
        @pl.when(bit != 0)
        def _():
            for c in cs:
                c.start()


def _wait_all(copies):
    for bit, cs in copies:
        @pl.when(bit != 0)
        def _():
            for c in cs:
                c.wait()


def _dispatch_kernel(off_ref, cpad_ref, boff_ref, tail_ref, x_ref, gates_ref, inforow_ref,
                     xs_hbm, gs_hbm, xstg, gstg, zx, zg, sem):
    b = pl.program_id(0)
    last = pl.num_programs(0) - 1
    tb = x_ref.shape[0]
    ks = xstg.shape[1]
    cur = lax.rem(b, 2)
    key = _slot_key(inforow_ref[0:1, :], inforow_ref[1:2, :], boff_ref, b)
    slot = lax.broadcasted_iota(jnp.int32, (ks, tb), 0)
    perm = jnp.where(slot == key, 1.0, 0.0).astype(BF16)
    xstg[cur] = _dot(perm, x_ref[...]).astype(BF16)
    gt = gates_ref[...]
    hi = gt.astype(BF16)
    lo = (gt - hi.astype(F32)).astype(BF16)
    both = _dot(perm, jnp.concatenate([hi, lo], axis=1))
    gstg[cur] = both[:, :LANES] + both[:, LANES:]

    sizes = _pow2_sizes(tb, SEG_ALIGN)

    def copies(blk, sl):
        for g in range(MOE_GROUPS):
            src0 = boff_ref[MOE_GROUPS * blk + g]
            dst0 = off_ref[MOE_GROUPS * blk + g]

            def make(pos, sz, src0=src0, dst0=dst0):
                src = pl.ds(pl.multiple_of(src0 + pos, SEG_ALIGN), sz)
                dst = pl.ds(pl.multiple_of(dst0 + pos, SEG_ALIGN), sz)
                return (pltpu.make_async_copy(xstg.at[sl, src, :], xs_hbm.at[dst, :], sem.at[sl, 0]),
                        pltpu.make_async_copy(gstg.at[sl, src, :], gs_hbm.at[dst, :], sem.at[sl, 1]))

            yield from _strip_copies(cpad_ref[MOE_GROUPS * blk + g], sizes, make)

    _start_all(copies(b, cur))

    @pl.when(b > 0)
    def _():
        _wait_all(copies(jnp.maximum(b - 1, 0), 1 - cur))

    @pl.when(b == last)
    def _():
        _wait_all(copies(b, cur))
        zx[...] = jnp.zeros(zx.shape, BF16)
        zg[...] = jnp.zeros(zg.shape, F32)
        tail_sizes = _pow2_sizes(MOE_TILE // 2, SEG_ALIGN)

        def tails():
            for g in range(MOE_GROUPS):
                dst0 = tail_ref[g]

                def make(pos, sz, dst0=dst0):
                    dst = pl.ds(pl.multiple_of(dst0 + pos, SEG_ALIGN), sz)
                    return (pltpu.make_async_copy(zx.at[pl.ds(0, sz), :], xs_hbm.at[dst, :], sem.at[0, 0]),
                            pltpu.make_async_copy(zg.at[pl.ds(0, sz), :], gs_hbm.at[dst, :], sem.at[0, 1]))

                yield from _strip_copies(tail_ref[MOE_GROUPS + g], tail_sizes, make)

        _start_all(tails())
        _wait_all(tails())

        def spare(i):
            dst = pl.ds(pl.multiple_of(i * MOE_TILE, MOE_TILE), MOE_TILE)
            return (pltpu.make_async_copy(zx, xs_hbm.at[dst, :], sem.at[0, 0]),
                    pltpu.make_async_copy(zg, gs_hbm.at[dst, :], sem.at[0, 1]))

        first_spare = tail_ref[2 * MOE_GROUPS]
        n_tiles = xs_hbm.shape[0] // MOE_TILE

        @pl.loop(first_spare, n_tiles)
        def _(i):
            cx, cg = spare(i)
            cx.start()
            cg.start()

        @pl.loop(first_spare, n_tiles)
        def _(i):
            cx, cg = spare(i)
            cx.wait()
            cg.wait()


def _dispatch(xn, gates, inforow, off, cpad, boff, tail, n_sorted, tb, ks):
    n, d = xn.shape
    nb = n // tb
    grid_spec = pltpu.PrefetchScalarGridSpec(
        num_scalar_prefetch=4,
        grid=(nb,),
        in_specs=[pl.BlockSpec((tb, d), lambda i, *_: (i, 0)),
                  pl.BlockSpec((tb, LANES), lambda i, *_: (i, 0)),
                  pl.BlockSpec((None, 8, tb), lambda i, *_: (i, 0, 0))],
        out_specs=[pl.BlockSpec(memory_space=pl.ANY), pl.BlockSpec(memory_space=pl.ANY)],
        scratch_shapes=[pltpu.VMEM((2, ks, d), BF16), pltpu.VMEM((2, ks, LANES), F32),
                        pltpu.VMEM((MOE_TILE, d), BF16), pltpu.VMEM((MOE_TILE, LANES), F32),
                        pltpu.SemaphoreType.DMA((2, 2))])
    return pl.pallas_call(
        _dispatch_kernel,
        grid_spec=grid_spec,
        out_shape=[jax.ShapeDtypeStruct((n_sorted, d), BF16),
                   jax.ShapeDtypeStruct((n_sorted, LANES), F32)],
        compiler_params=_params("arbitrary"),
        name="moe_dispatch",
    )(off, cpad, boff, tail, xn, gates, inforow)


def _group_mlp_kernel(tg_ref, nused_ref, x_ref, gs_ref, wg_ref, wu_ref, wd_ref, y_ref):
    i = pl.program_id(0)

    @pl.when(i < nused_ref[0])
    def _():
        x = x_ref[...]
        gs = gs_ref[...]
        lane = lax.broadcasted_iota(jnp.int32, gs.shape, 1)
        lane0 = tg_ref[i] * MOE_EXPERTS_PER_GROUP
        acc = jnp.zeros(y_ref.shape, F32)
        nxt = (_dot(x, wg_ref[0]), _dot(x, wu_ref[0]))
        for e in range(MOE_EXPERTS_PER_GROUP):
            hg, hu = nxt
            if e + 1 < MOE_EXPERTS_PER_GROUP:
                nxt = (_dot(x, wg_ref[e + 1]), _dot(x, wu_ref[e + 1]))
            gate = jnp.sum(jnp.where(lane == lane0 + e, gs, 0.0), axis=-1, keepdims=True)
            acc = acc + _dot((_silu(hg) * hu * gate).astype(BF16), wd_ref[e])
        y_ref[...] = acc.astype(y_ref.dtype)

    @pl.when(i >= nused_ref[0])
    def _():
        y_ref[...] = jnp.zeros(y_ref.shape, y_ref.dtype)


def _group_mlp(xs, gs, wg, wu, wd, tile_group, nused):
    n_sorted, d = xs.shape
    ne = MOE_EXPERTS_PER_GROUP

    def rows(i, tg, nu):
        return (jnp.minimum(i, nu[0] - 1), 0)

    def group(i, tg, nu):
        return (tg[i], 0, 0)

    grid_spec = pltpu.PrefetchScalarGridSpec(
        num_scalar_prefetch=2,
        grid=(n_sorted // MOE_TILE,),
        in_specs=[pl.BlockSpec((MOE_TILE, d), rows),
                  pl.BlockSpec((MOE_TILE, LANES), rows),
                  pl.BlockSpec((ne, d, MOE_D_FF), group),
                  pl.BlockSpec((ne, d, MOE_D_FF), group),
                  pl.BlockSpec((ne, MOE_D_FF, d), group)],
        out_specs=pl.BlockSpec((MOE_TILE, d), lambda i, tg, nu: (i, 0)))
    return pl.pallas_call(
        _group_mlp_kernel,
        grid_spec=grid_spec,
        out_shape=jax.ShapeDtypeStruct((n_sorted, d), BF16),
        compiler_params=_params("arbitrary"),
        name="moe_experts",
    )(tile_group, nused, xs, gs, wg, wu, wd)


def _combine_kernel(off_ref, cpad_ref, boff_ref, info_ref, res_ref, g_ref, *rest, project):
    if project:
        w_ref, y_hbm, o_ref, u_ref, ybuf, sem = rest
    else:
        y_hbm, o_ref, ybuf, sem = rest
    b = pl.program_id(0)
    last = pl.num_programs(0) - 1
    tb = res_ref.shape[0]
    ks = ybuf.shape[1]
    cur = lax.rem(b, 2)
    sizes = _pow2_sizes(tb, STRIP)

    def copies(blk, sl):
        for g in range(MOE_GROUPS):
            src0 = off_ref[MOE_GROUPS * blk + g]
            dst0 = boff_ref[MOE_GROUPS * blk + g]

            def make(pos, sz, src0=src0, dst0=dst0):
                src = pl.ds(pl.multiple_of(src0 + pos, SEG_ALIGN), sz)
                dst = pl.ds(pl.multiple_of(dst0 + pos, STRIP), sz)
                return (pltpu.make_async_copy(y_hbm.at[src, :], ybuf.at[sl, dst, :], sem.at[sl]),)

            n_rows = (cpad_ref[MOE_GROUPS * blk + g] + (STRIP - 1)) & (-STRIP)
            yield from _strip_copies(n_rows, sizes, make)

    @pl.when(b == 0)
    def _():
        ybuf[...] = jnp.zeros(ybuf.shape, ybuf.dtype)
        _start_all(copies(b, cur))

    @pl.when(b < last)
    def _():
        _start_all(copies(jnp.minimum(b + 1, last), 1 - cur))

    _wait_all(copies(b, cur))

    key = _slot_key(info_ref[:, 0:1], info_ref[:, 1:2], boff_ref, b)
    slot = lax.broadcasted_iota(jnp.int32, (tb, ks), 1)
    perm = jnp.where(slot == key, 1.0, 0.0).astype(BF16)
    y = res_ref[...] + _dot(perm, ybuf[cur])
    ms = jnp.mean(y * y, axis=-1, keepdims=True)
    yn = y * lax.rsqrt(ms + NORM_EPS) * g_ref[...]
    if project:
        o_ref[...] = y
        u_ref[...] = _dot(yn.astype(BF16), w_ref[...]).astype(u_ref.dtype)
    else:
        o_ref[...] = yn


def _combine(ys, info, res, norm_g, w_next, off, cpad, boff, tb, ks):
    n, d = res.shape
    project = w_next is not None
    row = lambda width: pl.BlockSpec((tb, width), lambda i, *_: (i, 0))
    in_specs = [row(LANES), row(d), pl.BlockSpec((1, d), lambda i, *_: (0, 0))]
    args = [info, res, norm_g.reshape(1, d)]
    out_specs, out_shape = [row(d)], [jax.ShapeDtypeStruct((n, d), F32)]
    if project:
        nout = w_next.shape[1]
        in_specs.append(pl.BlockSpec((d, nout), lambda i, *_: (0, 0)))
        args.append(w_next)
        out_specs.append(row(nout))
        out_shape.append(jax.ShapeDtypeStruct((n, nout), BF16))
    grid_spec = pltpu.PrefetchScalarGridSpec(
        num_scalar_prefetch=3,
        grid=(n // tb,),
        in_specs=in_specs + [pl.BlockSpec(memory_space=pl.ANY)],
        out_specs=out_specs,
        scratch_shapes=[pltpu.VMEM((2, ks, d), BF16), pltpu.SemaphoreType.DMA((2,))])
    out = pl.pallas_call(
        functools.partial(_combine_kernel, project=project),
        grid_spec=grid_spec,
        out_shape=out_shape,
        compiler_params=_params("arbitrary"),
        name="moe_combine",
    )(off, cpad, boff, *args, ys)
    return tuple(out) if project else out[0]


def _round_up(x, m):
    return (x + m - 1) // m * m


def _moe_layer(routed, expert_weights, norm_g, w_next, tb):
    h, xn, gates, info, inforow, counts = routed
    w_gate, w_up, w_down = expert_weights
    n, d = h.shape
    nb = n // tb

    cnt = counts[:, 0, :MOE_GROUPS]
    cpad = _round_up(cnt, SEG_ALIGN)
    strip = _round_up(cnt, STRIP)
    used = jnp.sum(cpad, axis=0)
    region = _round_up(used, MOE_TILE)
    region_start = jnp.cumsum(region) - region
    off = region_start[None, :] + jnp.cumsum(cpad, axis=0) - cpad
    boff = jnp.cumsum(strip, axis=1) - strip
    nused = (jnp.sum(region) // MOE_TILE).reshape(1)
    tail = jnp.concatenate([region_start + used, region - used, nused]).astype(jnp.int32)
    n_tiles = -(-(n + nb * MOE_GROUPS * (SEG_ALIGN - 1) + MOE_GROUPS * (MOE_TILE - 1)) // MOE_TILE) + 1
    tile_start = jnp.arange(n_tiles, dtype=jnp.int32) * MOE_TILE
    region_end = region_start + region
    tile_group = jnp.minimum(jnp.sum(tile_start[:, None] >= region_end[None, :], axis=1), MOE_GROUPS - 1)
    ks =tb + MOE_GROUPS * STRIP

    flat = lambda a: a.reshape(-1).astype(jnp.int32)
    xs, gs = _dispatch(xn, gates, inforow, flat(off), flat(cpad), flat(boff), tail,
                       n_tiles * MOE_TILE, tb, ks)
    ys = _group_mlp(xs, gs, w_gate, w_up, w_down,
                    tile_group.astype(jnp.int32), nused.astype(jnp.int32))
    return _combine(ys, info, h, norm_g, w_next, flat(off), flat(cpad), flat(boff), tb, ks)


ATT_Q_TILE = 256
ATT_TILES_PER_ITER = 2
ATT_SCORES_AHEAD = 2
ATT_SCORE_SLOTS = 4
ATT_VT_ROWS = ATT_HEAD_DIM + 16
ATT_Q_SCALE = ATT_HEAD_DIM ** -0.5 * 1.4426950408889634


def _eye(n):
    return jnp.where(lax.broadcasted_iota(jnp.int32, (n, n), 0) == lax.broadcasted_iota(jnp.int32, (n, n), 1),
                     1.0, 0.0).astype(BF16)


def _transpose_bf16(x, eye):
    return lax.dot_general(eye, x, _NT, preferred_element_type=F32).astype(BF16)


def _rope_kernel(q_ref, k_ref, v_ref, aq_ref, bq_ref, ak_ref, bk_ref, qt_ref, kp_ref, vt_ref):
    t = q_ref.shape[0]
    tq = ATT_Q_TILE
    dh = ATT_HEAD_DIM
    gi = lax.broadcasted_iota(jnp.int32, (LANES, LANES), 0) >> 6
    gj = lax.broadcasted_iota(jnp.int32, (LANES, LANES), 1) >> 6
    gmat = jnp.where(gi == gj, 1.0, 0.0).astype(BF16)
    eye_q = _eye(LANES)

    def norm_rope(x, a, b):
        lane = lax.broadcasted_iota(jnp.int32, x.shape, 1)
        sq = x * x
        hi = sq.astype(BF16)
        lo = (sq - hi.astype(F32)).astype(BF16)
        ms = (_dot(hi, gmat) + _dot(lo, gmat)) * (1.0 / dh)
        partner = jnp.where((lane & 1) == 0, pltpu.roll(x, LANES - 1, 1), pltpu.roll(x, 1, 1))
        return lax.rsqrt(ms + NORM_EPS) * (x * a + partner * b)

    for j in range(q_ref.shape[1] // LANES):
        cols = slice(j * LANES, (j + 1) * LANES)
        y = norm_rope(q_ref[N_META:, cols].astype(F32), aq_ref[N_META:, :], bq_ref[N_META:, :])
        yt = _transpose_bf16(y.astype(BF16), eye_q)
        for i in range((t - N_META) // tq):
            qt_ref[i, cols, :] = yt[:, i * tq:(i + 1) * tq]

    low = lax.broadcasted_iota(jnp.int32, (t, LANES), 1) < dh
    for j in range(k_ref.shape[1] // LANES):
        cols = slice(j * LANES, (j + 1) * LANES)
        y = norm_rope(k_ref[:, cols].astype(F32), ak_ref[...], bk_ref[...])
        kp_ref[2 * j] = jnp.where(low, y, 0.0).astype(BF16)
        kp_ref[2 * j + 1] = jnp.where(low, pltpu.roll(y, dh, 1), 0.0).astype(BF16)

    n_real = t - N_META
    eye_v = _eye(v_ref.shape[1])
    vt_real = _transpose_bf16(v_ref[N_META:, :], eye_v)
    vt_meta = _transpose_bf16(v_ref[0:N_META, :], eye_v)
    for h in range(ATT_KV_HEADS):
        vt_ref[h, 0:dh, 0:n_real] = vt_real[h * dh:(h + 1) * dh, :]
        vt_ref[h, 0:dh, n_real:t] = vt_meta[h * dh:(h + 1) * dh, :]
        extra = lax.broadcasted_iota(jnp.int32, (ATT_VT_ROWS - dh, t), 0)
        vt_ref[h, dh:ATT_VT_ROWS, :] = jnp.where(extra == 0, 1.0, 0.0).astype(BF16)


def _rope(u3, tables):
    b, t, _ = u3.shape
    nq = ATT_HEADS * ATT_HEAD_DIM
    nkv = ATT_KV_HEADS * ATT_HEAD_DIM
    n_qt = (t - N_META) // ATT_Q_TILE
    full = lambda shape: pl.BlockSpec(shape, lambda i: (0, 0))
    return pl.pallas_call(
        _rope_kernel,
        grid=(b,),
        in_specs=[pl.BlockSpec((None, t, nq), lambda i: (i, 0, 0)),
                  pl.BlockSpec((None, t, nkv), lambda i: (i, 0, nq // nkv)),
                  pl.BlockSpec((None, t, nkv), lambda i: (i, 0, nq // nkv + 1)),
                  full((t, LANES)), full((t, LANES)), full((t, LANES)), full((t, LANES))],
        out_specs=[pl.BlockSpec((None, n_qt, nq, ATT_Q_TILE), lambda i: (i, 0, 0, 0)),
                   pl.BlockSpec((None, ATT_KV_HEADS, t, LANES), lambda i: (i, 0, 0, 0)),
                   pl.BlockSpec((None, ATT_KV_HEADS, ATT_VT_ROWS, t), lambda i: (i, 0, 0, 0))],
        out_shape=[jax.ShapeDtypeStruct((b, n_qt, nq, ATT_Q_TILE), BF16),
                   jax.ShapeDtypeStruct((b, ATT_KV_HEADS, t, LANES), BF16),
                   jax.ShapeDtypeStruct((b, ATT_KV_HEADS, ATT_VT_ROWS, t), BF16)],
        compiler_params=_params("parallel"),
        name="qk_norm_rope",
    )(u3, u3, u3, *tables)


def _attn_kernel(qt_ref, k_ref, vt_ref, wg_ref, wu_ref, wd_ref, o_ref, wg_o, wu_o, wd_o, s_ref):
    _cast_through((wg_ref, wu_ref, wd_ref), (wg_o, wu_o, wd_o))
    n_qt, w, tq = qt_ref.shape
    dh = ATT_HEAD_DIM
    n_real = k_ref.shape[0] - N_META
    eye = _eye(tq)
    zpad = jnp.zeros((LANES - dh, tq), BF16)

    def scores(i, g, slot):
        qp = jnp.concatenate([qt_ref[i, g * dh:(g + 1) * dh, :], zpad], axis=0)
        s_r = _dot(k_ref[N_META:, :], qp)
        s_m = _dot(k_ref[0:N_META, :], qp)
        s_ref[slot, 0:n_real, :] = s_r
        s_ref[slot, n_real:, :] = s_m
        return jnp.maximum(jnp.max(s_r, axis=0, keepdims=True), jnp.max(s_m, axis=0, keepdims=True))

    steps = [(u, g) for u in range(ATT_TILES_PER_ITER) for g in range(ATT_GROUP)]
    n_steps = len(steps)
    n_iter = n_qt // ATT_TILES_PER_ITER
    ahead = ATT_SCORES_AHEAD
    n_slots = s_ref.shape[0]
    assert n_steps % n_slots == 0 and ahead < n_slots

    def issue_scores(it, n):
        it2 = jnp.minimum(it + n // n_steps, n_iter - 1)
        u, g = steps[n % n_steps]
        return scores(it2 * ATT_TILES_PER_ITER + u, g, n % n_slots)

    def q_tiles(it, pending):
        i0 = it * ATT_TILES_PER_ITER
        outs = []
        pending = list(pending)
        for n, (u, g) in enumerate(steps):
            m, slot = pending.pop(0), n % n_slots
            pending.append(issue_scores(it, n + ahead))
            p_r = jnp.exp2(s_ref[slot, 0:n_real, :] - m).astype(BF16)
            p_m = jnp.exp2(s_ref[slot, n_real:, :] - m).astype(BF16)
            ot = _dot(vt_ref[:, 0:n_real], p_r) + _dot(vt_ref[:, n_real:], p_m)
            outs.append((ot[0:dh] * (1.0 / ot[dh:dh + 1])).astype(BF16))
            if g == ATT_GROUP - 1:
                rows = pl.ds(pl.multiple_of((i0 + u) * tq, tq), tq)
                ot_all = jnp.concatenate(outs, axis=0)
                o_ref[rows, :] = lax.dot_general(eye, ot_all, _NT, preferred_element_type=F32).astype(o_ref.dtype)
                outs = []
        return tuple(pending)

    first = tuple(issue_scores(0, n) for n in range(ahead))
    lax.fori_loop(0, n_iter, q_tiles, first)


def _attention(qt, kp, vt, expert_weights, layer):
    b, n_qt, nq, tq = qt.shape
    t = kp.shape[2]
    w = ATT_GROUP * ATT_HEAD_DIM
    grid = (b, ATT_KV_HEADS)
    c_in, c_out, c_shape = _expert_cast_specs(expert_weights, layer, grid)
    o, *cast = pl.pallas_call(
        _attn_kernel,
        grid=grid,
        in_specs=[pl.BlockSpec((None, n_qt, w, tq), lambda i, h: (i, 0, h, 0)),
                  pl.BlockSpec((None, None, t, LANES), lambda i, h: (i, h, 0, 0)),
                  pl.BlockSpec((None, None, ATT_VT_ROWS, t), lambda i, h: (i, h, 0, 0))] + c_in,
        out_specs=[pl.BlockSpec((None, n_qt * tq, w), lambda i, h: (i, 0, h))] + c_out,
        out_shape=[jax.ShapeDtypeStruct((b, n_qt * tq, nq), BF16)] + c_shape,
        scratch_shapes=[pltpu.VMEM((ATT_SCORE_SLOTS, t, tq), F32)],
        compiler_params=_params("parallel", "parallel"),
        name="gqa_attention",
    )(qt, kp, vt, *expert_weights)
    return o, cast


def _rope_tables(n_tokens, q_gain, k_gain):
    rows_n = n_tokens // GRID_W
    rows = jnp.repeat(jnp.arange(rows_n), GRID_W).astype(F32)
    cols = jnp.tile(jnp.arange(GRID_W), rows_n).astype(F32)
    axis_dims = ATT_HEAD_DIM // 2
    freqs = ROPE_THETA ** (-jnp.arange(0, axis_dims, 2, dtype=F32) / axis_dims)
    ang = jnp.concatenate([rows[:, None] * freqs, cols[:, None] * freqs], axis=-1)
    ang = jnp.concatenate([jnp.zeros((N_META, ang.shape[1]), F32), ang], axis=0)
    cos_e = jnp.tile(jnp.repeat(jnp.cos(ang), 2, axis=1), (1, LANES // ATT_HEAD_DIM))
    sign = jnp.tile(jnp.array([-1.0, 1.0], F32), LANES // 2)
    sin_e = jnp.tile(jnp.repeat(jnp.sin(ang), 2, axis=1), (1, LANES // ATT_HEAD_DIM)) * sign

    def tables(gain, scale):
        g = jnp.tile(gain.astype(F32), LANES // ATT_HEAD_DIM)
        g_partner = g.reshape(-1, 2)[:, ::-1].reshape(-1)
        return cos_e * (g * scale), sin_e * (g_partner * scale)

    return tables(q_gain, ATT_Q_SCALE) + tables(k_gain, 1.0)


def kernel(x, meta_tokens, norm_mix, norm_ffn, norm_final, sc_w_in, sc_conv_w, gla_w_gate_fwd, gla_b_gate_fwd, gla_w_gate_bwd, gla_b_gate_bwd, gla_norm_w, sc_w_out, att_w_in, att_q_norm, att_k_norm, att_w_out, moe_w_group, moe_b_group, moe_w_expert, moe_b_expert, moe_w_gate, moe_w_up, moe_w_down):
    bsz, s_len, d = x.shape
    t = s_len + N_META
    n0, n1 = bsz * t, bsz * s_len
    tm0 = t // 3
    tb0 = 3 * LANES
    tm1 = 4 * LANES
    assert t % tm0 == 0 and tm0 % 16 == 0 and n0 % tb0 == 0 and n1 % tm1 == 0 and s_len % tm1 == 0

    expert_f32 = (moe_w_gate, moe_w_up, moe_w_down)

    nk = GLA_HEADS * GLA_DK
    mix_in = sc_w_in.shape[2]
    mix_pad = -mix_in % MXU_WIDTH
    w_in = jnp.pad(sc_w_in[0], ((0, 0), (0, mix_pad))).astype(BF16)
    u, h = _embed_norm_matmul(x, meta_tokens.astype(x.dtype), norm_mix[0], w_in, tm0)
    h = h.reshape(n0, d)
    u3 = u.reshape(bsz, t, mix_in + mix_pad)
    y_a, gate_up0 = _gated_conv(u3, sc_conv_w[0], expert_f32[:2], 0)
    wc = jnp.zeros((LANES, 2 * nk), F32)
    wc = wc.at[:GLA_GATE_RANK, :nk].set(gla_w_gate_fwd[0])
    wc = wc.at[GLA_GATE_RANK:2 * GLA_GATE_RANK, nk:].set(gla_w_gate_bwd[0])
    bc = jnp.concatenate([gla_b_gate_fwd[0], gla_b_gate_bwd[0]]).reshape(1, 2 * nk)
    y_b, down0 = _gla(u3, wc.astype(BF16), bc, gla_norm_w[0].reshape(1, GLA_DV), expert_f32[2:], 0)
    experts0 = gate_up0 + down0
    w_out = sc_w_out[0].astype(BF16)
    router0 = _router_operands(norm_ffn[0], moe_w_group[0], moe_b_group[0], moe_w_expert[0], moe_b_expert[0])
    routed = _proj2_router(y_a.reshape(n0, SC_WIDTH), y_b.reshape(n0, -1), w_out[:SC_WIDTH], w_out[SC_WIDTH:],
                           h, router0, tb0)
    h, u = _moe_layer(routed, experts0, norm_mix[1], att_w_in[0].astype(BF16), tb0)
    u3 = u.reshape(bsz, t, -1)
    qt, kp, vt = _rope(u3, _rope_tables(s_len, att_q_norm[0], att_k_norm[0]))
    o, experts1 = _attention(qt, kp, vt, expert_f32, 1)
    router1 = _router_operands(norm_ffn[1], moe_w_group[1], moe_b_group[1], moe_w_expert[1], moe_b_expert[1])
    routed = _att_out_router(o.reshape(n1, d), att_w_out[0].astype(BF16), h.reshape(bsz, t, d), router1, tm1)
    out = _moe_layer(routed, experts1, norm_final, None, tm1)
    return out.reshape(bsz, s_len, d)
```
